```python
import math
import jax
import jax.numpy as jnp
from jax import lax
import numpy as np

D_MODEL = 1024
BATCH = 16
SEQ = 2048
DEPTH = 1

D_MIX = D_MODEL
ATT_HEADS = 4
ATT_QK_DIM = 64
ATT_V_DIM = 2 * ATT_QK_DIM
D_ATT = ATT_HEADS * ATT_V_DIM
M_HEADS = 4
D_MLSTM = D_MIX - D_ATT
M_DIM = D_MLSTM // M_HEADS
D_PROJ = 3 * D_ATT + 4 * D_MLSTM + 2 * M_HEADS
CONV_W = 4
CHUNK = 64
Q_BLOCK = 128
N_BUCKETS = 32
MAX_DIST = 128
N_KEYS = 128
N_EXPERTS = N_KEYS * N_KEYS
PEER_HEADS = 8
PEER_TOPK = 16
PEER_QDIM = 256
PEER_BLOCK = 128
EPS = 1e-6

kernel_name = 'hybrid_diffattn_mlstm_peer'


def _rms(x):
    xf = x.astype(jnp.float32)
    return xf * lax.rsqrt(jnp.mean(xf * xf, axis=-1, keepdims=True) + EPS)


def rmsnorm(x, g):
    return _rms(x).astype(x.dtype) * g


def modulate(h, shift, scale):
    return h * (1.0 + scale[:, None, :]) + shift[:, None, :]


def rel_bucket(n):
    max_exact = N_BUCKETS // 2
    nf = jnp.maximum(n, 1).astype(jnp.float32)
    large = max_exact + (jnp.log(nf / max_exact) / math.log(MAX_DIST / max_exact)
                         * (N_BUCKETS - max_exact)).astype(jnp.int32)
    large = jnp.minimum(large, N_BUCKETS - 1)
    return jnp.where(n < max_exact, n, large)


def causal_conv(x, w, b):
    S = x.shape[1]
    xp = jnp.pad(x, ((0, 0), (CONV_W - 1, 0), (0, 0)))
    out = xp[:, 0:S] * w[0]
    for j in range(1, CONV_W):
        out = out + xp[:, j:j + S] * w[j]
    return out + b


def diff_attention(q, k, v, rel_bias, lam_q1, lam_k1, lam_q2, lam_k2, sub_g, lambda_init):
    B, S = q.shape[0], q.shape[1]
    lam = (jnp.exp(jnp.sum(lam_q1 * lam_k1).astype(jnp.float32))
           - jnp.exp(jnp.sum(lam_q2 * lam_k2).astype(jnp.float32)) + lambda_init)
    scale = ATT_QK_DIM ** -0.5
    outs = []
    for qb in range(S // Q_BLOCK):
        s0 = qb * Q_BLOCK
        s1 = s0 + Q_BLOCK
        logits = jnp.einsum('bqhtd,bkhtd->bhtqk', q[:, s0:s1], k[:, :s1]).astype(jnp.float32) * scale
        rel = (s0 + jnp.arange(Q_BLOCK, dtype=jnp.int32))[:, None] - jnp.arange(s1, dtype=jnp.int32)[None, :]
        bias = jnp.transpose(rel_bias[rel_bucket(jnp.maximum(rel, 0))], (2, 0, 1)).astype(jnp.float32)
        logits = jnp.where(rel >= 0, logits + bias[None, :, None], jnp.finfo(jnp.float32).min)
        p = jax.nn.softmax(logits, axis=-1)
        a = p[:, :, 0] - lam * p[:, :, 1]
        outs.append(jnp.einsum('bhqk,bkhe->bqhe', a.astype(v.dtype), v[:, :s1]))
    o = jnp.concatenate(outs, axis=1)
    o = _rms(o).astype(v.dtype) * sub_g * (1.0 - lambda_init)
    return o.reshape(B, S, D_ATT)


def mlstm_chunkwise(q, k, v, i_pre, f_pre):
    B, S, H, d = q.shape
    NC = S // CHUNK

    def chunks(t):
        return t.reshape(B, NC, CHUNK, H, d).transpose(0, 3, 1, 2, 4)

    q, k, v = chunks(q), chunks(k), chunks(v)
    ig = i_pre.reshape(B, NC, CHUNK, H).transpose(0, 3, 1, 2)
    logf = jax.nn.log_sigmoid(f_pre).reshape(B, NC, CHUNK, H).transpose(0, 3, 1, 2)
    b = jnp.cumsum(logf, axis=-1)
    b_last = b[..., -1]
    a_end = b_last[..., None] - b + ig

    def step(carry, xs):
        C, n, m = carry
        k_c, v_c, a_c, bl = xs
        m_new = jnp.maximum(bl + m, jnp.max(a_c, axis=-1))
        decay = jnp.exp(bl + m - m_new)
        w = jnp.exp(a_c - m_new[..., None])
        C_new = decay[..., None, None] * C + jnp.einsum('bhl,bhld,bhle->bhde', w, k_c, v_c)
        n_new = decay[..., None] * n + jnp.einsum('bhl,bhld->bhd', w, k_c)
        return (C_new, n_new, m_new), (C, n, m)

    init = (jnp.zeros((B, H, d, d), jnp.float32), jnp.zeros((B, H, d), jnp.float32),
            jnp.zeros((B, H), jnp.float32))
    xs = (k.transpose(2, 0, 1, 3, 4), v.transpose(2, 0, 1, 3, 4),
          a_end.transpose(2, 0, 1, 3), b_last.transpose(2, 0, 1))
    _, (C_prev, n_prev, m_prev) = lax.scan(step, init, xs)
    C_prev = C_prev.transpose(1, 2, 0, 3, 4)
    n_prev = n_prev.transpose(1, 2, 0, 3)
    m_prev = m_prev.transpose(1, 2, 0)

    causal = jnp.tril(jnp.ones((CHUNK, CHUNK), dtype=bool))
    logD = jnp.where(causal, b[..., :, None] - b[..., None, :] + ig[..., None, :], -jnp.inf)
    m_inter = b + m_prev[..., None]
    m_j = jnp.maximum(jnp.max(logD, axis=-1), m_inter)
    W = jnp.exp(logD - m_j[..., None])
    Sqk = jnp.einsum('bhcjd,bhcsd->bhcjs', q, k) * W
    inter = jnp.exp(m_inter - m_j)
    num = (jnp.einsum('bhcjs,bhcse->bhcje', Sqk, v)
           + inter[..., None] * jnp.einsum('bhcjd,bhcde->bhcje', q, C_prev))
    den = jnp.sum(Sqk, axis=-1) + inter * jnp.einsum('bhcjd,bhcd->bhcj', q, n_prev)
    h = num / jnp.maximum(jnp.abs(den), jnp.exp(-m_j))[..., None]
    return h.transpose(0, 2, 3, 1, 4).reshape(B, S, H, d)


def hybrid_mixer(h, w_in, conv_w, conv_b, b_igate, b_fgate, lam_q1, lam_k1, lam_q2, lam_k2,
                 diff_sub_g, mlstm_norm_g, w_out, rel_bias, lambda_init):
    B, S, _ = h.shape
    proj = h @ w_in
    sizes = [D_ATT, D_ATT, D_ATT, D_MLSTM, D_MLSTM, D_MLSTM, D_MLSTM, M_HEADS, M_HEADS]
    splits = np.cumsum(sizes)[:-1].tolist()
    dq, dk, dv, mq, mk, mv, mo, mi, mf = jnp.split(proj, splits, axis=-1)
    att = diff_attention(dq.reshape(B, S, ATT_HEADS, 2, ATT_QK_DIM),
                         dk.reshape(B, S, ATT_HEADS, 2, ATT_QK_DIM),
                         dv.reshape(B, S, ATT_HEADS, ATT_V_DIM),
                         rel_bias, lam_q1, lam_k1, lam_q2, lam_k2, diff_sub_g, lambda_init)
    qk = jax.nn.silu(causal_conv(jnp.concatenate([mq, mk], axis=-1), conv_w, conv_b))
    mq, mk = jnp.split(qk, 2, axis=-1)
    f32 = jnp.float32
    hm = mlstm_chunkwise(mq.reshape(B, S, M_HEADS, M_DIM).astype(f32),
                         (mk * (M_DIM ** -0.5)).reshape(B, S, M_HEADS, M_DIM).astype(f32),
                         mv.reshape(B, S, M_HEADS, M_DIM).astype(f32),
                         (mi + b_igate).astype(f32), (mf + b_fgate).astype(f32))
    hm = jax.nn.sigmoid(mo.astype(f32)).reshape(B, S, M_HEADS, M_DIM) * hm
    hm = _rms(hm).reshape(B, S, D_MLSTM).astype(h.dtype) * mlstm_norm_g
    mix = jnp.concatenate([att, hm], axis=-1)
    return mix @ w_out


def peer(h, w_q, sub_keys, u, v):
    B, S, D = h.shape
    T = B * S
    ht = h.reshape(T, D)
    q = (ht @ w_q).reshape(T, PEER_HEADS, 2, PEER_QDIM // 2)
    s = jnp.einsum('thpd,hpkd->thpk', q, sub_keys).astype(jnp.float32)
    sv, si = lax.top_k(s, PEER_TOPK)
    cand = sv[:, :, 0, :, None] + sv[:, :, 1, None, :]
    cand_idx = si[:, :, 0, :, None] * N_KEYS + si[:, :, 1, None, :]
    top_s, pos = lax.top_k(cand.reshape(T, PEER_HEADS, PEER_TOPK * PEER_TOPK), PEER_TOPK)
    idx = jnp.take_along_axis(cand_idx.reshape(T, PEER_HEADS, PEER_TOPK * PEER_TOPK), pos, axis=-1)
    g = jax.nn.softmax(top_s, axis=-1)
    nb = T // PEER_BLOCK

    def block(args):
        hb, ib, gb = args
        act = jax.nn.gelu(jnp.einsum('thkd,td->thk', u[ib], hb))
        return jnp.einsum('thk,thkd->td', (gb * act).astype(hb.dtype), v[ib])

    out = lax.map(block, (ht.reshape(nb, PEER_BLOCK, D),
                          idx.reshape(nb, PEER_BLOCK, PEER_HEADS, PEER_TOPK),
                          g.reshape(nb, PEER_BLOCK, PEER_HEADS, PEER_TOPK)))
    return out.reshape(B, S, D)


def setup_inputs(seed: int = 0) -> dict:
    key = jax.random.key(seed)
    ks = jax.random.split(key, 24)
    L, D = DEPTH, D_MODEL

    def nrm(k, shape, s):
        return jax.random.normal(k, shape, jnp.float32) * s

    return {
        'x': nrm(ks[0], (BATCH, SEQ, D), 1.0),
        'c': nrm(ks[1], (BATCH, D), 1.0),
        'w_ada': nrm(ks[2], (L, D, 6 * D), 0.5 * D ** -0.5),
        'b_ada': nrm(ks[3], (L, 6 * D), 0.02),
        'norm1_g': 1.0 + nrm(ks[4], (L, D), 0.02),
        'norm2_g': 1.0 + nrm(ks[5], (L, D), 0.02),
        'w_in': nrm(ks[6], (L, D, D_PROJ), D ** -0.5),
        'conv_w': nrm(ks[7], (L, CONV_W, 2 * D_MLSTM), CONV_W ** -0.5),
        'conv_b': nrm(ks[8], (L, 2 * D_MLSTM), 0.02),
        'b_igate': nrm(ks[9], (L, M_HEADS), 0.1),
        'b_fgate': jnp.linspace(3.0, 6.0, M_HEADS, dtype=jnp.float32)[None, :] + nrm(ks[10], (L, M_HEADS), 0.1),
        'lam_q1': nrm(ks[11], (L, ATT_QK_DIM), 0.1),
        'lam_k1': nrm(ks[12], (L, ATT_QK_DIM), 0.1),
        'lam_q2': nrm(ks[13], (L, ATT_QK_DIM), 0.1),
        'lam_k2': nrm(ks[14], (L, ATT_QK_DIM), 0.1),
        'diff_sub_g': 1.0 + nrm(ks[15], (L, ATT_V_DIM), 0.02),
        'mlstm_norm_g': 1.0 + nrm(ks[16], (L, D_MLSTM), 0.02),
        'w_out': nrm(ks[17], (L, D_MIX, D), D_MIX ** -0.5),
        'peer_w_q': nrm(ks[18], (L, D, PEER_HEADS * PEER_QDIM), D ** -0.5),
        'peer_sub_keys': nrm(ks[19], (L, PEER_HEADS, 2, N_KEYS, PEER_QDIM // 2), (PEER_QDIM // 2) ** -0.5),
        'peer_u': nrm(ks[20], (L, N_EXPERTS, D), D ** -0.5),
        'peer_v': nrm(ks[21], (L, N_EXPERTS, D), PEER_HEADS ** -0.5),
        'rel_bias': nrm(ks[22], (N_BUCKETS, ATT_HEADS), 0.5),
        'final_g': 1.0 + nrm(ks[23], (D,), 0.02),
    }


def reference(x, c, w_ada, b_ada, norm1_g, norm2_g, w_in, conv_w, conv_b, b_igate, b_fgate,
              lam_q1, lam_k1, lam_q2, lam_k2, diff_sub_g, mlstm_norm_g, w_out,
              peer_w_q, peer_sub_keys, peer_u, peer_v, rel_bias, final_g):
    cond = jax.nn.silu(c)
    for l in range(DEPTH):
        lambda_init = 0.8 - 0.6 * math.exp(-0.3 * l)
        mod = cond @ w_ada[l] + b_ada[l]
        sh1, sc1, g1, sh2, sc2, g2 = jnp.split(mod, 6, axis=-1)
        h = modulate(rmsnorm(x, norm1_g[l]), sh1, sc1)
        y = hybrid_mixer(h, w_in[l], conv_w[l], conv_b[l], b_igate[l], b_fgate[l],
                         lam_q1[l], lam_k1[l], lam_q2[l], lam_k2[l], diff_sub_g[l],
                         mlstm_norm_g[l], w_out[l], rel_bias, lambda_init)
        x = x + g1[:, None, :] * y
        h = modulate(rmsnorm(x, norm2_g[l]), sh2, sc2)
        x = x + g2[:, None, :] * peer(h, peer_w_q[l], peer_sub_keys[l], peer_u[l], peer_v[l])
    return rmsnorm(x, final_g)
```

```python
import functools
import math

import numpy as np
import jax
import jax.numpy as jnp
from jax import lax
from jax.experimental import pallas as pl
from jax.experimental.pallas import tpu as pltpu

F32 = jnp.float32
BF16 = jnp.bfloat16

ATT_HEADS = 4
ATT_QK_DIM = 64
ATT_V_DIM = 128
D_ATT = ATT_HEADS * ATT_V_DIM
M_HEADS = 4
M_DIM = 128
D_MLSTM = M_HEADS * M_DIM
CONV_W = 4
N_BUCKETS = 32
MAX_DIST = 128
N_KEYS = 128
PEER_HEADS = 8
PEER_TOPK = 16
PEER_QDIM = 256
EPS = 1e-6
LAMBDA_INIT = 0.8 - 0.6 * math.exp(-0.3 * 0)

ATT_BLOCK = 256
M_CHUNK = 128
ROW_TILE = 512
ROUTE_TILE = 512
LANES = 128
VMEM_LIMIT = 48 * 1024 * 1024

_NT = (((1,), (1,)), ((), ()))


def _rms_rows(x):
    return x * lax.rsqrt(jnp.mean(x * x, axis=-1, keepdims=True) + EPS)


def _sigmoid(x):
    return 1.0 / (1.0 + jnp.exp(-x))


def _ada_kernel(c_ref, w_ref, b_ref, o_ref):
    c = c_ref[...]
    cond = c * _sigmoid(c)
    o_ref[...] = jnp.dot(cond, w_ref[...], preferred_element_type=F32) + b_ref[...]


def _ada(c, w, b):
    bsz, d = c.shape
    n = w.shape[1]
    return pl.pallas_call(
        _ada_kernel,
        grid=(n // d,),
        in_specs=[pl.BlockSpec((bsz, d), lambda j: (0, 0)),
                  pl.BlockSpec((d, d), lambda j: (0, j)),
                  pl.BlockSpec((1, d), lambda j: (0, j))],
        out_specs=pl.BlockSpec((bsz, d), lambda j: (0, j)),
        out_shape=jax.ShapeDtypeStruct((bsz, n), F32),
        name="ada",
    )(c, w, b.reshape(1, n))


def _inproj_kernel(x_ref, mod_ref, g_ref, wa_ref, wm_ref, wg_ref, oa_ref, om_ref, og_ref):
    x = x_ref[0]
    mod = mod_ref[0]
    h = _rms_rows(x) * g_ref[...]
    h = h * (1.0 + mod[1:2, :]) + mod[0:1, :]
    hb = h.astype(BF16)
    oa_ref[0] = jnp.dot(hb, wa_ref[...], preferred_element_type=F32).astype(BF16)
    om_ref[0] = jnp.dot(hb, wm_ref[...], preferred_element_type=F32).astype(BF16)
    og_ref[0] = lax.dot_general(wg_ref[...], hb, _NT, preferred_element_type=F32)


def _inproj(x, mod, g, w_att, w_m, w_gt):
    bsz, s, d = x.shape
    tm = min(ROW_TILE, s)
    na, nm, ng = w_att.shape[1], w_m.shape[1], w_gt.shape[0]
    return pl.pallas_call(
        _inproj_kernel,
        grid=(bsz, s // tm),
        in_specs=[pl.BlockSpec((1, tm, d), lambda b, i: (b, i, 0)),
                  pl.BlockSpec((1, 6, d), lambda b, i: (b, 0, 0)),
                  pl.BlockSpec((1, d), lambda b, i: (0, 0)),
                  pl.BlockSpec((d, na), lambda b, i: (0, 0)),
                  pl.BlockSpec((d, nm), lambda b, i: (0, 0)),
                  pl.BlockSpec((ng, d), lambda b, i: (0, 0))],
        out_specs=[pl.BlockSpec((1, tm, na), lambda b, i: (b, i, 0)),
                   pl.BlockSpec((1, tm, nm), lambda b, i: (b, i, 0)),
                   pl.BlockSpec((1, ng, tm), lambda b, i: (b, 0, i))],
        out_shape=[jax.ShapeDtypeStruct((bsz, s, na), BF16),
                   jax.ShapeDtypeStruct((bsz, s, nm), BF16),
                   jax.ShapeDtypeStruct((bsz, ng, s), F32)],
        compiler_params=pltpu.CompilerParams(vmem_limit_bytes=VMEM_LIMIT),
        name="inproj",
    )(x, mod, g.reshape(1, d), w_att, w_m, w_gt)


def _rel_buckets():
    n = np.arange(2 * ATT_BLOCK)
    max_exact = N_BUCKETS // 2
    nf = np.maximum(n, 1).astype(np.float64)
    large = max_exact + (np.log(nf / max_exact) / math.log(MAX_DIST / max_exact)
                         * (N_BUCKETS - max_exact)).astype(np.int64)
    large = np.minimum(large, N_BUCKETS - 1)
    bucket = np.where(n < max_exact, n, large)
    qk = np.arange(ATT_BLOCK)[:, None] - np.arange(ATT_BLOCK)[None, :]
    tiles = np.stack([bucket[np.maximum(qk, 0)], bucket[ATT_BLOCK + qk]])
    return tiles.astype(np.int32)


def _relbias_kernel(rb_ref, bk_ref, o_ref):
    h = pl.program_id(0)
    bk = bk_ref[...]
    acc = jnp.zeros(bk.shape, F32)
    for b in range(N_BUCKETS):
        acc = jnp.where(bk == b, rb_ref[b, h], acc)
    o_ref[0] = acc


def _relbias(rel_bias):
    tiles = jnp.asarray(_rel_buckets())
    return pl.pallas_call(
        _relbias_kernel,
        grid=(ATT_HEADS,),
        in_specs=[pl.BlockSpec(memory_space=pltpu.SMEM),
                  pl.BlockSpec((2, ATT_BLOCK, ATT_BLOCK), lambda h: (0, 0, 0))],
        out_specs=pl.BlockSpec((1, 2, ATT_BLOCK, ATT_BLOCK), lambda h: (h, 0, 0, 0)),
        out_shape=jax.ShapeDtypeStruct((ATT_HEADS, 2, ATT_BLOCK, ATT_BLOCK), F32),
        name="relbias",
    )(rel_bias, tiles)


def _attn_kernel(q_ref, k_ref, v_ref, bias_ref, lam_ref, subg_ref, o_ref,
                 qz_ref, m_ref, l_ref, acc_ref):
    tq = ATT_BLOCK
    qi = pl.program_id(2)
    scale = ATT_QK_DIM ** -0.5

    q = q_ref[0]
    lane = lax.broadcasted_iota(jnp.int32, q.shape, 1)
    zero = jnp.zeros_like(q)
    qz_ref[0:tq, :] = jnp.where(lane < ATT_QK_DIM, q, zero)
    qz_ref[tq:2 * tq, :] = jnp.where(lane >= ATT_QK_DIM, q, zero)
    m_ref[...] = jnp.full(m_ref.shape, -jnp.inf, F32)
    l_ref[...] = jnp.zeros(l_ref.shape, F32)
    acc_ref[...] = jnp.zeros(acc_ref.shape, F32)

    def step(j, bias, masked):
        start = pl.multiple_of(j * tq, tq)
        k = k_ref[0, pl.ds(start, tq), :]
        v = v_ref[0, pl.ds(start, tq), :]
        s = lax.dot_general(qz_ref[...], k, _NT, preferred_element_type=F32) * scale
        if isinstance(bias, tuple):
            s = s + jnp.concatenate([bias[0], bias[0]], axis=0)
        else:
            s = s + bias
        if masked:
            row = lax.broadcasted_iota(jnp.int32, (tq, tq), 0)
            col = lax.broadcasted_iota(jnp.int32, (tq, tq), 1)
            keep = jnp.concatenate([col <= row, col <= row], axis=0)
            s = jnp.where(keep, s, jnp.finfo(F32).min)
        m_old = m_ref[...]
        m_new = jnp.maximum(m_old, jnp.max(s, axis=-1, keepdims=True))
        alpha = jnp.exp(m_old - m_new)
        p = jnp.exp(s - m_new)
        l_ref[...] = alpha * l_ref[...] + jnp.sum(p, axis=-1, keepdims=True)
        acc_ref[...] = alpha * acc_ref[...] + jnp.dot(p.astype(BF16), v, preferred_element_type=F32)
        m_ref[...] = m_new

    far_bias = bias_ref[0, 1, tq - 1:tq, 0:1]

    def far_body(j, carry):
        step(j, far_bias, False)
        return carry

    lax.fori_loop(0, jnp.maximum(qi - 1, 0), far_body, 0)

    @pl.when(qi >= 1)
    def _():
        step(qi - 1, (bias_ref[0, 1],), False)

    step(qi, (bias_ref[0, 0],), True)

    lam = (jnp.exp(jnp.sum(lam_ref[0:1, :] * lam_ref[1:2, :], axis=-1, keepdims=True))
           - jnp.exp(jnp.sum(lam_ref[2:3, :] * lam_ref[3:4, :], axis=-1, keepdims=True))
           + LAMBDA_INIT)
    o = acc_ref[...] / l_ref[...]
    o = o[0:tq, :] - lam * o[tq:2 * tq, :]
    o = _rms_rows(o) * subg_ref[...] * (1.0 - LAMBDA_INIT)
    o_ref[0] = o.astype(o_ref.dtype)


def _attn(att_qkv, bias_tiles, lam4, sub_g):
    bsz, s, _ = att_qkv.shape
    tq = ATT_BLOCK
    nh = ATT_HEADS
    return pl.pallas_call(
        _attn_kernel,
        grid=(bsz, nh, s // tq),
        in_specs=[pl.BlockSpec((1, tq, ATT_V_DIM), lambda b, h, i: (b, i, h)),
                  pl.BlockSpec((1, s, ATT_V_DIM), lambda b, h, i: (b, 0, nh + h)),
                  pl.BlockSpec((1, s, ATT_V_DIM), lambda b, h, i: (b, 0, 2 * nh + h)),
                  pl.BlockSpec((1, 2, tq, tq), lambda b, h, i: (h, 0, 0, 0)),
                  pl.BlockSpec((4, ATT_QK_DIM), lambda b, h, i: (0, 0)),
                  pl.BlockSpec((1, ATT_V_DIM), lambda b, h, i: (0, 0))],
        out_specs=pl.BlockSpec((1, tq, ATT_V_DIM), lambda b, h, i: (b, i, h)),
        out_shape=jax.ShapeDtypeStruct((bsz, s, D_ATT), BF16),
        scratch_shapes=[pltpu.VMEM((2 * tq, ATT_V_DIM), BF16),
                        pltpu.VMEM((2 * tq, 1), F32),
                        pltpu.VMEM((2 * tq, 1), F32),
                        pltpu.VMEM((2 * tq, ATT_V_DIM), F32)],
        compiler_params=pltpu.CompilerParams(vmem_limit_bytes=VMEM_LIMIT),
        name="attn",
    )(att_qkv, att_qkv, att_qkv, bias_tiles, lam4, sub_g.reshape(1, ATT_V_DIM))


def _mlstm_kernel(q_ref, k_ref, v_ref, o_ref, gi_ref, gf_ref, bias_ref, cwq_ref, cwk_ref,
                  cbq_ref, cbk_ref, ng_ref, out_ref, qs_ref, ks_ref, b_ref, ig_ref):
    s = q_ref.shape[1]
    L = M_CHUNK
    nc = s // L
    h = pl.program_id(1)

    row = lax.broadcasted_iota(jnp.int32, (s, M_DIM), 0)

    def conv_silu(x_ref, w_ref, cb_ref):
        x = x_ref[0].astype(F32)
        w = w_ref[...]
        out = None
        for j in range(CONV_W):
            shift = CONV_W - 1 - j
            xs = x if shift == 0 else jnp.where(row >= shift, pltpu.roll(x, shift, 0), 0.0)
            term = xs * w[j:j + 1, :]
            out = term if out is None else out + term
        out = out + cb_ref[...]
        return out * _sigmoid(out)

    qs_ref[...] = conv_silu(q_ref, cwq_ref, cbq_ref).astype(BF16)
    ks_ref[...] = (conv_silu(k_ref, cwk_ref, cbk_ref) * (M_DIM ** -0.5)).astype(BF16)

    ig = gi_ref[0, 0] + bias_ref[h]
    f = gf_ref[0, 0] + bias_ref[M_HEADS + h]
    logf = jnp.minimum(f, 0.0) - jnp.log(1.0 + jnp.exp(-jnp.abs(f)))
    r = lax.broadcasted_iota(jnp.int32, (L, L), 0)
    c = lax.broadcasted_iota(jnp.int32, (L, L), 1)
    tri = (r <= c).astype(F32)
    b_ref[...] = jnp.dot(logf, tri, preferred_element_type=F32,
                         precision=lax.Precision.HIGHEST)
    ig_ref[...] = ig
    eye = r == c
    causal = c <= r

    def to_col(x_row):
        return jnp.sum(jnp.where(eye, x_row, 0.0), axis=1, keepdims=True)

    def chunk(ci, carry):
        C, n, m = carry
        start = pl.multiple_of(ci * L, L)
        qc = qs_ref[pl.ds(start, L), :]
        kc = ks_ref[pl.ds(start, L), :]
        vc = v_ref[0, pl.ds(start, L), :]
        b_r = b_ref[pl.ds(ci, 1), :]
        ig_r = ig_ref[pl.ds(ci, 1), :]
        b_last = b_r[:, L - 1:L]
        a_r = b_last - b_r + ig_r
        b_c = to_col(b_r)
        a_c = to_col(a_r)

        logd = jnp.where(causal, b_c - b_r + ig_r, -jnp.inf)
        m_inter = b_c + m
        m_j = jnp.maximum(jnp.max(logd, axis=1, keepdims=True), m_inter)
        w = jnp.exp(logd - m_j)
        sqk = lax.dot_general(qc, kc, _NT, preferred_element_type=F32) * w
        inter = jnp.exp(m_inter - m_j)
        num = (jnp.dot(sqk.astype(BF16), vc, preferred_element_type=F32)
               + inter * jnp.dot(qc, C.astype(BF16), preferred_element_type=F32))
        den = (jnp.sum(sqk, axis=1, keepdims=True)
               + inter * jnp.sum(qc.astype(F32) * n, axis=1, keepdims=True))
        hc = num / jnp.maximum(jnp.abs(den), jnp.exp(-m_j))

        og = _sigmoid(o_ref[0, pl.ds(start, L), :].astype(F32))
        out_ref[0, pl.ds(start, L), :] = (_rms_rows(og * hc) * ng_ref[...]).astype(out_ref.dtype)

        m_new = jnp.maximum(b_last + m, jnp.max(a_r, axis=1, keepdims=True))
        decay = jnp.exp(b_last + m - m_new)
        kw = kc.astype(F32) * jnp.exp(a_c - m_new)
        C_new = decay * C + jnp.dot(kw.T.astype(BF16), vc, preferred_element_type=F32)
        n_new = decay * n + jnp.sum(kw, axis=0, keepdims=True)
        return C_new, n_new, m_new

    init = (jnp.zeros((M_DIM, M_DIM), F32), jnp.zeros((1, M_DIM), F32), jnp.zeros((1, 1), F32))
    lax.fori_loop(0, nc, chunk, init)


def _mlstm(m_qkvo, gates, gate_bias, conv_w, conv_b, norm_g):
    bsz, s, _ = m_qkvo.shape
    L = M_CHUNK
    nc = s // L
    nh = M_HEADS
    d = M_DIM
    g4 = gates.reshape(bsz, 2 * nh, nc, L)
    seq = lambda off: pl.BlockSpec((1, s, d), lambda b, h: (b, 0, off + h))
    return pl.pallas_call(
        _mlstm_kernel,
        grid=(bsz, nh),
        in_specs=[seq(0), seq(nh), seq(2 * nh), seq(3 * nh),
                  pl.BlockSpec((1, 1, nc, L), lambda b, h: (b, h, 0, 0)),
                  pl.BlockSpec((1, 1, nc, L), lambda b, h: (b, nh + h, 0, 0)),
                  pl.BlockSpec(memory_space=pltpu.SMEM),
                  pl.BlockSpec((CONV_W, d), lambda b, h: (0, h)),
                  pl.BlockSpec((CONV_W, d), lambda b, h: (0, nh + h)),
                  pl.BlockSpec((1, d), lambda b, h: (0, h)),
                  pl.BlockSpec((1, d), lambda b, h: (0, nh + h)),
                  pl.BlockSpec((1, d), lambda b, h: (0, h))],
        out_specs=pl.BlockSpec((1, s, d), lambda b, h: (b, 0, h)),
        out_shape=jax.ShapeDtypeStruct((bsz, s, D_MLSTM), BF16),
        scratch_shapes=[pltpu.VMEM((s, d), BF16), pltpu.VMEM((s, d), BF16),
                        pltpu.VMEM((nc, L), F32), pltpu.VMEM((nc, L), F32)],
        compiler_params=pltpu.CompilerParams(vmem_limit_bytes=VMEM_LIMIT),
        name="mlstm",
    )(m_qkvo, m_qkvo, m_qkvo, m_qkvo, g4, g4, gate_bias, conv_w, conv_w,
      conv_b.reshape(1, -1), conv_b.reshape(1, -1), norm_g.reshape(1, -1))


def _outproj_kernel(att_ref, hm_ref, x_ref, mod_ref, g2_ref, wo_ref, wq_ref,
                    x1_ref, h2_ref, qp_ref):
    mod = mod_ref[0]
    y = (jnp.dot(att_ref[0], wo_ref[0:D_ATT, :], preferred_element_type=F32)
         + jnp.dot(hm_ref[0], wo_ref[D_ATT:, :], preferred_element_type=F32))
    x1 = x_ref[0] + mod[2:3, :] * y
    x1_ref[0] = x1
    h2 = _rms_rows(x1) * g2_ref[...]
    h2 = h2 * (1.0 + mod[4:5, :]) + mod[3:4, :]
    h2_ref[0] = h2
    qp_ref[0] = jnp.dot(h2.astype(BF16), wq_ref[...], preferred_element_type=F32).astype(BF16)


def _outproj(att, hm, x, mod, g2, w_out, w_q):
    bsz, s, d = x.shape
    tm = min(ROW_TILE, s)
    nq = w_q.shape[1]
    tile = lambda n: pl.BlockSpec((1, tm, n), lambda b, i: (b, i, 0))
    return pl.pallas_call(
        _outproj_kernel,
        grid=(bsz, s // tm),
        in_specs=[tile(D_ATT), tile(D_MLSTM), tile(d),
                  pl.BlockSpec((1, 6, d), lambda b, i: (b, 0, 0)),
                  pl.BlockSpec((1, d), lambda b, i: (0, 0)),
                  pl.BlockSpec((d, d), lambda b, i: (0, 0)),
                  pl.BlockSpec((d, nq), lambda b, i: (0, 0))],
        out_specs=[tile(d), tile(d), tile(nq)],
        out_shape=[jax.ShapeDtypeStruct((bsz, s, d), F32),
                   jax.ShapeDtypeStruct((bsz, s, d), F32),
                   jax.ShapeDtypeStruct((bsz, s, nq), BF16)],
        compiler_params=pltpu.CompilerParams(vmem_limit_bytes=VMEM_LIMIT),
        name="outproj",
    )(att, hm, x, mod, g2.reshape(1, d), w_out, w_q)


def _top16_rows(blocks, ids):
    big = jnp.int32(1 << 30)
    vals, pos = [], []
    for _ in range(PEER_TOPK):
        m = functools.reduce(jnp.maximum, blocks)
        m = jnp.max(m, axis=0, keepdims=True)
        cand = functools.reduce(jnp.minimum,
                                [jnp.where(b == m, i, big) for b, i in zip(blocks, ids)])
        p = jnp.min(cand, axis=0, keepdims=True)
        blocks = [jnp.where(i == p, -jnp.inf, b) for b, i in zip(blocks, ids)]
        vals.append(m)
        pos.append(p)
    return jnp.concatenate(vals, axis=0), jnp.concatenate(pos, axis=0)


def _pick_rows(table, sel):
    out = jnp.zeros(sel.shape, table.dtype)
    for r in range(PEER_TOPK):
        out = jnp.where(sel == r, table[r:r + 1, :], out)
    return out


def _route_kernel(q_ref, keys_ref, idx_ref, g_ref):
    k = PEER_TOPK
    half = PEER_QDIM // 2
    key_id = lax.broadcasted_iota(jnp.int32, (N_KEYS, LANES), 0)
    sub_id = lax.broadcasted_iota(jnp.int32, (k, LANES), 0)
    for t in range(q_ref.shape[0] // LANES):
        rows = pl.ds(t * LANES, LANES)
        sv, si = [], []
        for p in range(2):
            qh = q_ref[rows, p * half:(p + 1) * half]
            s = lax.dot_general(keys_ref[0, p], qh, _NT, preferred_element_type=F32)
            v, i = _top16_rows([s], [key_id])
            sv.append(v)
            si.append(i)
        blocks = [sv[0][i:i + 1, :] + sv[1] for i in range(k)]
        ids = [sub_id + i * k for i in range(k)]
        top_s, pos = _top16_rows(blocks, ids)
        idx = (_pick_rows(si[0], lax.shift_right_logical(pos, 4)) * N_KEYS
               + _pick_rows(si[1], lax.bitwise_and(pos, k - 1)))
        e = jnp.exp(top_s - top_s[0:1, :])
        idx_ref[:, t * LANES:(t + 1) * LANES] = idx
        g_ref[:, t * LANES:(t + 1) * LANES] = e / jnp.sum(e, axis=0, keepdims=True)


def _route(qp, sub_keys):
    t, _ = qp.shape
    tt = min(ROUTE_TILE, t)
    k = PEER_TOPK
    return pl.pallas_call(
        _route_kernel,
        grid=(t // tt, PEER_HEADS),
        in_specs=[pl.BlockSpec((tt, PEER_QDIM), lambda i, h: (i, h)),
                  pl.BlockSpec((1, 2, N_KEYS, PEER_QDIM // 2), lambda i, h: (h, 0, 0, 0))],
        out_specs=[pl.BlockSpec((k, tt), lambda i, h: (h, i)),
                   pl.BlockSpec((k, tt), lambda i, h: (h, i))],
        out_shape=[jax.ShapeDtypeStruct((PEER_HEADS * k, t), jnp.int32),
                   jax.ShapeDtypeStruct((PEER_HEADS * k, t), F32)],
        name="route",
    )(qp, sub_keys)


def _final_kernel(x1_ref, po_ref, mod_ref, g_ref, o_ref):
    x2 = x1_ref[0] + mod_ref[0][5:6, :] * po_ref[0]
    o_ref[0] = _rms_rows(x2) * g_ref[...]


def _final(x1, peer_out, mod, final_g):
    bsz, s, d = x1.shape
    tm = min(ROW_TILE, s)
    tile = pl.BlockSpec((1, tm, d), lambda b, i: (b, i, 0))
    return pl.pallas_call(
        _final_kernel,
        grid=(bsz, s // tm),
        in_specs=[tile, tile, pl.BlockSpec((1, 6, d), lambda b, i: (b, 0, 0)),
                  pl.BlockSpec((1, d), lambda b, i: (0, 0))],
        out_specs=tile,
        out_shape=jax.ShapeDtypeStruct((bsz, s, d), F32),
        name="final",
    )(x1, peer_out, mod, final_g.reshape(1, d))


def _experts(h2, idx_t, g_t, u, v):
    idx, g = idx_t.T, g_t.T

    def one(args):
        hb, ib, gb = args
        act = jax.nn.gelu(jnp.einsum('tkd,td->tk', u[ib], hb))
        return jnp.einsum('tk,tkd->td', gb * act, v[ib])
    t, d = h2.shape
    nb = t // 128
    out = lax.map(one, (h2.reshape(nb, 128, d), idx.reshape(nb, 128, -1), g.reshape(nb, 128, -1)))
    return out.reshape(t, d)


def _front(x, c, w_ada, b_ada, norm1_g, norm2_g, w_in, conv_w, conv_b, b_igate, b_fgate,
           lam_q1, lam_k1, lam_q2, lam_k2, diff_sub_g, mlstm_norm_g, w_out,
           peer_w_q, peer_sub_keys, rel_bias):
    bsz, s, d = x.shape
    mod = _ada(c, w_ada[0], b_ada[0]).reshape(bsz, 6, d)

    w = w_in[0]
    w_att = w[:, :3 * D_ATT].astype(BF16)
    w_m = w[:, 3 * D_ATT:3 * D_ATT + 4 * D_MLSTM].astype(BF16)
    w_gt = w[:, 3 * D_ATT + 4 * D_MLSTM:].T.astype(BF16)
    att_qkv, m_qkvo, gates = _inproj(x, mod, norm1_g[0], w_att, w_m, w_gt)

    bias_tiles = _relbias(rel_bias)
    lam4 = jnp.stack([lam_q1[0], lam_k1[0], lam_q2[0], lam_k2[0]])
    att = _attn(att_qkv, bias_tiles, lam4, diff_sub_g[0])

    gate_bias = jnp.concatenate([b_igate[0], b_fgate[0]])
    hm = _mlstm(m_qkvo, gates, gate_bias, conv_w[0], conv_b[0], mlstm_norm_g[0])

    x1, h2, qp = _outproj(att, hm, x, mod, norm2_g[0], w_out[0].astype(BF16),
                          peer_w_q[0].astype(BF16))
    idx_t, g_t = _route(qp.reshape(bsz * s, -1), peer_sub_keys[0].astype(BF16))
    return mod, x1, h2, idx_t, g_t


def kernel(x, c, w_ada, b_ada, norm1_g, norm2_g, w_in, conv_w, conv_b, b_igate, b_fgate, lam_q1, lam_k1, lam_q2, lam_k2, diff_sub_g, mlstm_norm_g, w_out, peer_w_q, peer_sub_keys, peer_u, peer_v, rel_bias, final_g):
    bsz, s, d = x.shape
    mod, x1, h2, idx_t, g_t = _front(x, c, w_ada, b_ada, norm1_g, norm2_g, w_in, conv_w, conv_b,
                                     b_igate, b_fgate, lam_q1, lam_k1, lam_q2, lam_k2,
                                     diff_sub_g, mlstm_norm_g, w_out, peer_w_q, peer_sub_keys,
                                     rel_bias)
    peer_out = _experts(h2.reshape(bsz * s, d), idx_t, g_t, peer_u[0], peer_v[0])
    return _final(x1, peer_out.reshape(bsz, s, d), mod, final_g)
```

```python
import functools
import math

import numpy as np
import jax
import jax.numpy as jnp
from jax import lax
from jax.experimental import pallas as pl
from jax.experimental.pallas import tpu as pltpu
from jax.experimental.pallas import tpu_sc as plsc

F32 = jnp.float32
BF16 = jnp.bfloat16

ATT_HEADS = 4
ATT_QK_DIM = 64
ATT_V_DIM = 128
D_ATT = ATT_HEADS * ATT_V_DIM
M_HEADS = 4
M_DIM = 128
D_MLSTM = M_HEADS * M_DIM
CONV_W = 4
N_BUCKETS = 32
MAX_DIST = 128
N_KEYS = 128
PEER_HEADS = 8
PEER_TOPK = 16
PEER_QDIM = 256
EPS = 1e-6
LAMBDA_INIT = 0.8 - 0.6 * math.exp(-0.3 * 0)

ATT_BLOCK = 256
M_CHUNK = 128
ROW_TILE = 512
ROUTE_TILE = 512
LANES = 128
VMEM_LIMIT = 48 * 1024 * 1024

_NT = (((1,), (1,)), ((), ()))


def _rms_rows(x):
    return x * lax.rsqrt(jnp.mean(x * x, axis=-1, keepdims=True) + EPS)


def _sigmoid(x):
    return 1.0 / (1.0 + jnp.exp(-x))


def _ada_kernel(c_ref, w_ref, b_ref, o_ref):
    c = c_ref[...]
    cond = c * _sigmoid(c)
    o_ref[...] = jnp.dot(cond, w_ref[...], preferred_element_type=F32) + b_ref[...]


def _ada(c, w, b):
    bsz, d = c.shape
    n = w.shape[1]
    return pl.pallas_call(
        _ada_kernel,
        grid=(n // d,),
        in_specs=[pl.BlockSpec((bsz, d), lambda j: (0, 0)),
                  pl.BlockSpec((d, d), lambda j: (0, j)),
                  pl.BlockSpec((1, d), lambda j: (0, j))],
        out_specs=pl.BlockSpec((bsz, d), lambda j: (0, j)),
        out_shape=jax.ShapeDtypeStruct((bsz, n), F32),
        name="ada",
    )(c, w, b.reshape(1, n))


def _inproj_kernel(x_ref, mod_ref, g_ref, wa_ref, wm_ref, wg_ref, oa_ref, om_ref, og_ref):
    x = x_ref[0]
    mod = mod_ref[0]
    h = _rms_rows(x) * g_ref[...]
    h = h * (1.0 + mod[1:2, :]) + mod[0:1, :]
    hb = h.astype(BF16)
    oa_ref[0] = jnp.dot(hb, wa_ref[...], preferred_element_type=F32).astype(BF16)
    om_ref[0] = jnp.dot(hb, wm_ref[...], preferred_element_type=F32).astype(BF16)
    og_ref[0] = lax.dot_general(wg_ref[...], hb, _NT, preferred_element_type=F32)


def _inproj(x, mod, g, w_att, w_m, w_gt):
    bsz, s, d = x.shape
    tm = min(ROW_TILE, s)
    na, nm, ng = w_att.shape[1], w_m.shape[1], w_gt.shape[0]
    return pl.pallas_call(
        _inproj_kernel,
        grid=(bsz, s // tm),
        in_specs=[pl.BlockSpec((1, tm, d), lambda b, i: (b, i, 0)),
                  pl.BlockSpec((1, 6, d), lambda b, i: (b, 0, 0)),
                  pl.BlockSpec((1, d), lambda b, i: (0, 0)),
                  pl.BlockSpec((d, na), lambda b, i: (0, 0)),
                  pl.BlockSpec((d, nm), lambda b, i: (0, 0)),
                  pl.BlockSpec((ng, d), lambda b, i: (0, 0))],
        out_specs=[pl.BlockSpec((1, tm, na), lambda b, i: (b, i, 0)),
                   pl.BlockSpec((1, tm, nm), lambda b, i: (b, i, 0)),
                   pl.BlockSpec((1, ng, tm), lambda b, i: (b, 0, i))],
        out_shape=[jax.ShapeDtypeStruct((bsz, s, na), BF16),
                   jax.ShapeDtypeStruct((bsz, s, nm), BF16),
                   jax.ShapeDtypeStruct((bsz, ng, s), F32)],
        compiler_params=pltpu.CompilerParams(vmem_limit_bytes=VMEM_LIMIT),
        name="inproj",
    )(x, mod, g.reshape(1, d), w_att, w_m, w_gt)


def _rel_buckets():
    n = np.arange(2 * ATT_BLOCK)
    max_exact = N_BUCKETS // 2
    nf = np.maximum(n, 1).astype(np.float64)
    large = max_exact + (np.log(nf / max_exact) / math.log(MAX_DIST / max_exact)
                         * (N_BUCKETS - max_exact)).astype(np.int64)
    large = np.minimum(large, N_BUCKETS - 1)
    bucket = np.where(n < max_exact, n, large)
    qk = np.arange(ATT_BLOCK)[:, None] - np.arange(ATT_BLOCK)[None, :]
    tiles = np.stack([bucket[np.maximum(qk, 0)], bucket[ATT_BLOCK + qk]])
    return tiles.astype(np.int32)


def _relbias_kernel(rb_ref, bk_ref, o_ref):
    h = pl.program_id(0)
    bk = bk_ref[...]
    acc = jnp.zeros(bk.shape, F32)
    for b in range(N_BUCKETS):
        acc = jnp.where(bk == b, rb_ref[b, h], acc)
    o_ref[0] = acc


def _relbias(rel_bias):
    tiles = jnp.asarray(_rel_buckets())
    return pl.pallas_call(
        _relbias_kernel,
        grid=(ATT_HEADS,),
        in_specs=[pl.BlockSpec(memory_space=pltpu.SMEM),
                  pl.BlockSpec((2, ATT_BLOCK, ATT_BLOCK), lambda h: (0, 0, 0))],
        out_specs=pl.BlockSpec((1, 2, ATT_BLOCK, ATT_BLOCK), lambda h: (h, 0, 0, 0)),
        out_shape=jax.ShapeDtypeStruct((ATT_HEADS, 2, ATT_BLOCK, ATT_BLOCK), F32),
        name="relbias",
    )(rel_bias, tiles)


def _attn_kernel(q_ref, k_ref, v_ref, bias_ref, lam_ref, subg_ref, o_ref,
                 qz_ref, m_ref, l_ref, acc_ref):
    tq = ATT_BLOCK
    qi = pl.program_id(2)
    scale = ATT_QK_DIM ** -0.5

    q = q_ref[0]
    lane = lax.broadcasted_iota(jnp.int32, q.shape, 1)
    zero = jnp.zeros_like(q)
    qz_ref[0:tq, :] = jnp.where(lane < ATT_QK_DIM, q, zero)
    qz_ref[tq:2 * tq, :] = jnp.where(lane >= ATT_QK_DIM, q, zero)
    m_ref[...] = jnp.full(m_ref.shape, -jnp.inf, F32)
    l_ref[...] = jnp.zeros(l_ref.shape, F32)
    acc_ref[...] = jnp.zeros(acc_ref.shape, F32)

    def step(j, bias, masked):
        start = pl.multiple_of(j * tq, tq)
        k = k_ref[0, pl.ds(start, tq), :]
        v = v_ref[0, pl.ds(start, tq), :]
        s = lax.dot_general(qz_ref[...], k, _NT, preferred_element_type=F32) * scale
        if isinstance(bias, tuple):
            s = s + jnp.concatenate([bias[0], bias[0]], axis=0)
        else:
            s = s + bias
        if masked:
            row = lax.broadcasted_iota(jnp.int32, (tq, tq), 0)
            col = lax.broadcasted_iota(jnp.int32, (tq, tq), 1)
            keep = jnp.concatenate([col <= row, col <= row], axis=0)
            s = jnp.where(keep, s, jnp.finfo(F32).min)
        m_old = m_ref[...]
        m_new = jnp.maximum(m_old, jnp.max(s, axis=-1, keepdims=True))
        alpha = jnp.exp(m_old - m_new)
        p = jnp.exp(s - m_new)
        l_ref[...] = alpha * l_ref[...] + jnp.sum(p, axis=-1, keepdims=True)
        acc_ref[...] = alpha * acc_ref[...] + jnp.dot(p.astype(BF16), v, preferred_element_type=F32)
        m_ref[...] = m_new

    far_bias = bias_ref[0, 1, tq - 1:tq, 0:1]

    def far_body(j, carry):
        step(j, far_bias, False)
        return carry

    lax.fori_loop(0, jnp.maximum(qi - 1, 0), far_body, 0)

    @pl.when(qi >= 1)
    def _():
        step(qi - 1, (bias_ref[0, 1],), False)

    step(qi, (bias_ref[0, 0],), True)

    lam = (jnp.exp(jnp.sum(lam_ref[0:1, :] * lam_ref[1:2, :], axis=-1, keepdims=True))
           - jnp.exp(jnp.sum(lam_ref[2:3, :] * lam_ref[3:4, :], axis=-1, keepdims=True))
           + LAMBDA_INIT)
    o = acc_ref[...] / l_ref[...]
    o = o[0:tq, :] - lam * o[tq:2 * tq, :]
    o = _rms_rows(o) * subg_ref[...] * (1.0 - LAMBDA_INIT)
    o_ref[0] = o.astype(o_ref.dtype)


def _attn(att_qkv, bias_tiles, lam4, sub_g):
    bsz, s, _ = att_qkv.shape
    tq = ATT_BLOCK
    nh = ATT_HEADS
    return pl.pallas_call(
        _attn_kernel,
        grid=(bsz, nh, s // tq),
        in_specs=[pl.BlockSpec((1, tq, ATT_V_DIM), lambda b, h, i: (b, i, h)),
                  pl.BlockSpec((1, s, ATT_V_DIM), lambda b, h, i: (b, 0, nh + h)),
                  pl.BlockSpec((1, s, ATT_V_DIM), lambda b, h, i: (b, 0, 2 * nh + h)),
                  pl.BlockSpec((1, 2, tq, tq), lambda b, h, i: (h, 0, 0, 0)),
                  pl.BlockSpec((4, ATT_QK_DIM), lambda b, h, i: (0, 0)),
                  pl.BlockSpec((1, ATT_V_DIM), lambda b, h, i: (0, 0))],
        out_specs=pl.BlockSpec((1, tq, ATT_V_DIM), lambda b, h, i: (b, i, h)),
        out_shape=jax.ShapeDtypeStruct((bsz, s, D_ATT), BF16),
        scratch_shapes=[pltpu.VMEM((2 * tq, ATT_V_DIM), BF16),
                        pltpu.VMEM((2 * tq, 1), F32),
                        pltpu.VMEM((2 * tq, 1), F32),
                        pltpu.VMEM((2 * tq, ATT_V_DIM), F32)],
        compiler_params=pltpu.CompilerParams(vmem_limit_bytes=VMEM_LIMIT),
        name="attn",
    )(att_qkv, att_qkv, att_qkv, bias_tiles, lam4, sub_g.reshape(1, ATT_V_DIM))


def _mlstm_kernel(q_ref, k_ref, v_ref, o_ref, gi_ref, gf_ref, bias_ref, cwq_ref, cwk_ref,
                  cbq_ref, cbk_ref, ng_ref, out_ref, qs_ref, ks_ref, b_ref, ig_ref):
    s = q_ref.shape[1]
    L = M_CHUNK
    nc = s // L
    h = pl.program_id(1)

    row = lax.broadcasted_iota(jnp.int32, (s, M_DIM), 0)

    def conv_silu(x_ref, w_ref, cb_ref):
        x = x_ref[0].astype(F32)
        w = w_ref[...]
        out = None
        for j in range(CONV_W):
            shift = CONV_W - 1 - j
            xs = x if shift == 0 else jnp.where(row >= shift, pltpu.roll(x, shift, 0), 0.0)
            term = xs * w[j:j + 1, :]
            out = term if out is None else out + term
        out = out + cb_ref[...]
        return out * _sigmoid(out)

    qs_ref[...] = conv_silu(q_ref, cwq_ref, cbq_ref).astype(BF16)
    ks_ref[...] = (conv_silu(k_ref, cwk_ref, cbk_ref) * (M_DIM ** -0.5)).astype(BF16)

    ig = gi_ref[0, 0] + bias_ref[h]
    f = gf_ref[0, 0] + bias_ref[M_HEADS + h]
    logf = jnp.minimum(f, 0.0) - jnp.log(1.0 + jnp.exp(-jnp.abs(f)))
    r = lax.broadcasted_iota(jnp.int32, (L, L), 0)
    c = lax.broadcasted_iota(jnp.int32, (L, L), 1)
    tri = (r <= c).astype(F32)
    b_ref[...] = jnp.dot(logf, tri, preferred_element_type=F32,
                         precision=lax.Precision.HIGHEST)
    ig_ref[...] = ig
    eye = r == c
    causal = c <= r

    def to_col(x_row):
        return jnp.sum(jnp.where(eye, x_row, 0.0), axis=1, keepdims=True)

    def chunk(ci, carry):
        C, n, m = carry
        start = pl.multiple_of(ci * L, L)
        qc = qs_ref[pl.ds(start, L), :]
        kc = ks_ref[pl.ds(start, L), :]
        vc = v_ref[0, pl.ds(start, L), :]
        b_r = b_ref[pl.ds(ci, 1), :]
        ig_r = ig_ref[pl.ds(ci, 1), :]
        b_last = b_r[:, L - 1:L]
        a_r = b_last - b_r + ig_r
        b_c = to_col(b_r)
        a_c = to_col(a_r)

        logd = jnp.where(causal, b_c - b_r + ig_r, -jnp.inf)
        m_inter = b_c + m
        m_j = jnp.maximum(jnp.max(logd, axis=1, keepdims=True), m_inter)
        w = jnp.exp(logd - m_j)
        sqk = lax.dot_general(qc, kc, _NT, preferred_element_type=F32) * w
        inter = jnp.exp(m_inter - m_j)
        num = (jnp.dot(sqk.astype(BF16), vc, preferred_element_type=F32)
               + inter * jnp.dot(qc, C.astype(BF16), preferred_element_type=F32))
        den = (jnp.sum(sqk, axis=1, keepdims=True)
               + inter * jnp.sum(qc.astype(F32) * n, axis=1, keepdims=True))
        hc = num / jnp.maximum(jnp.abs(den), jnp.exp(-m_j))

        og = _sigmoid(o_ref[0, pl.ds(start, L), :].astype(F32))
        out_ref[0, pl.ds(start, L), :] = (_rms_rows(og * hc) * ng_ref[...]).astype(out_ref.dtype)

        m_new = jnp.maximum(b_last + m, jnp.max(a_r, axis=1, keepdims=True))
        decay = jnp.exp(b_last + m - m_new)
        kw = kc.astype(F32) * jnp.exp(a_c - m_new)
        C_new = decay * C + jnp.dot(kw.T.astype(BF16), vc, preferred_element_type=F32)
        n_new = decay * n + jnp.sum(kw, axis=0, keepdims=True)
        return C_new, n_new, m_new

    init = (jnp.zeros((M_DIM, M_DIM), F32), jnp.zeros((1, M_DIM), F32), jnp.zeros((1, 1), F32))
    lax.fori_loop(0, nc, chunk, init)


def _mlstm(m_qkvo, gates, gate_bias, conv_w, conv_b, norm_g):
    bsz, s, _ = m_qkvo.shape
    L = M_CHUNK
    nc = s // L
    nh = M_HEADS
    d = M_DIM
    g4 = gates.reshape(bsz, 2 * nh, nc, L)
    seq = lambda off: pl.BlockSpec((1, s, d), lambda b, h: (b, 0, off + h))
    return pl.pallas_call(
        _mlstm_kernel,
        grid=(bsz, nh),
        in_specs=[seq(0), seq(nh), seq(2 * nh), seq(3 * nh),
                  pl.BlockSpec((1, 1, nc, L), lambda b, h: (b, h, 0, 0)),
                  pl.BlockSpec((1, 1, nc, L), lambda b, h: (b, nh + h, 0, 0)),
                  pl.BlockSpec(memory_space=pltpu.SMEM),
                  pl.BlockSpec((CONV_W, d), lambda b, h: (0, h)),
                  pl.BlockSpec((CONV_W, d), lambda b, h: (0, nh + h)),
                  pl.BlockSpec((1, d), lambda b, h: (0, h)),
                  pl.BlockSpec((1, d), lambda b, h: (0, nh + h)),
                  pl.BlockSpec((1, d), lambda b, h: (0, h))],
        out_specs=pl.BlockSpec((1, s, d), lambda b, h: (b, 0, h)),
        out_shape=jax.ShapeDtypeStruct((bsz, s, D_MLSTM), BF16),
        scratch_shapes=[pltpu.VMEM((s, d), BF16), pltpu.VMEM((s, d), BF16),
                        pltpu.VMEM((nc, L), F32), pltpu.VMEM((nc, L), F32)],
        compiler_params=pltpu.CompilerParams(vmem_limit_bytes=VMEM_LIMIT),
        name="mlstm",
    )(m_qkvo, m_qkvo, m_qkvo, m_qkvo, g4, g4, gate_bias, conv_w, conv_w,
      conv_b.reshape(1, -1), conv_b.reshape(1, -1), norm_g.reshape(1, -1))


def _outproj_kernel(att_ref, hm_ref, x_ref, mod_ref, g2_ref, wo_ref, wq_ref,
                    x1_ref, h2_ref, qp_ref):
    mod = mod_ref[0]
    y = (jnp.dot(att_ref[0], wo_ref[0:D_ATT, :], preferred_element_type=F32)
         + jnp.dot(hm_ref[0], wo_ref[D_ATT:, :], preferred_element_type=F32))
    x1 = x_ref[0] + mod[2:3, :] * y
    x1_ref[0] = x1
    h2 = _rms_rows(x1) * g2_ref[...]
    h2 = h2 * (1.0 + mod[4:5, :]) + mod[3:4, :]
    h2_ref[0] = h2
    qp_ref[0] = jnp.dot(h2.astype(BF16), wq_ref[...], preferred_element_type=F32).astype(BF16)


def _outproj(att, hm, x, mod, g2, w_out, w_q):
    bsz, s, d = x.shape
    tm = min(ROW_TILE, s)
    nq = w_q.shape[1]
    tile = lambda n: pl.BlockSpec((1, tm, n), lambda b, i: (b, i, 0))
    return pl.pallas_call(
        _outproj_kernel,
        grid=(bsz, s // tm),
        in_specs=[tile(D_ATT), tile(D_MLSTM), tile(d),
                  pl.BlockSpec((1, 6, d), lambda b, i: (b, 0, 0)),
                  pl.BlockSpec((1, d), lambda b, i: (0, 0)),
                  pl.BlockSpec((d, d), lambda b, i: (0, 0)),
                  pl.BlockSpec((d, nq), lambda b, i: (0, 0))],
        out_specs=[tile(d), tile(d), tile(nq)],
        out_shape=[jax.ShapeDtypeStruct((bsz, s, d), F32),
                   jax.ShapeDtypeStruct((bsz, s, d), F32),
                   jax.ShapeDtypeStruct((bsz, s, nq), BF16)],
        compiler_params=pltpu.CompilerParams(vmem_limit_bytes=VMEM_LIMIT),
        name="outproj",
    )(att, hm, x, mod, g2.reshape(1, d), w_out, w_q)


def _top16_rows(blocks, ids):
    big = jnp.int32(1 << 30)
    vals, pos = [], []
    for _ in range(PEER_TOPK):
        m = functools.reduce(jnp.maximum, blocks)
        m = jnp.max(m, axis=0, keepdims=True)
        cand = functools.reduce(jnp.minimum,
                                [jnp.where(b == m, i, big) for b, i in zip(blocks, ids)])
        p = jnp.min(cand, axis=0, keepdims=True)
        blocks = [jnp.where(i == p, -jnp.inf, b) for b, i in zip(blocks, ids)]
        vals.append(m)
        pos.append(p)
    return jnp.concatenate(vals, axis=0), jnp.concatenate(pos, axis=0)


def _pick_rows(table, sel):
    out = jnp.zeros(sel.shape, table.dtype)
    for r in range(PEER_TOPK):
        out = jnp.where(sel == r, table[r:r + 1, :], out)
    return out


def _route_kernel(q_ref, keys_ref, idx_ref, g_ref):
    k = PEER_TOPK
    half = PEER_QDIM // 2
    key_id = lax.broadcasted_iota(jnp.int32, (N_KEYS, LANES), 0)
    sub_id = lax.broadcasted_iota(jnp.int32, (k, LANES), 0)
    for t in range(q_ref.shape[0] // LANES):
        rows = pl.ds(t * LANES, LANES)
        sv, si = [], []
        for p in range(2):
            qh = q_ref[rows, p * half:(p + 1) * half]
            s = lax.dot_general(keys_ref[0, p], qh, _NT, preferred_element_type=F32)
            v, i = _top16_rows([s], [key_id])
            sv.append(v)
            si.append(i)
        blocks = [sv[0][i:i + 1, :] + sv[1] for i in range(k)]
        ids = [sub_id + i * k for i in range(k)]
        top_s, pos = _top16_rows(blocks, ids)
        idx = (_pick_rows(si[0], lax.shift_right_logical(pos, 4)) * N_KEYS
               + _pick_rows(si[1], lax.bitwise_and(pos, k - 1)))
        e = jnp.exp(top_s - top_s[0:1, :])
        idx_ref[:, t * LANES:(t + 1) * LANES] = idx
        g_ref[:, t * LANES:(t + 1) * LANES] = e / jnp.sum(e, axis=0, keepdims=True)


def _route(qp, sub_keys):
    t, _ = qp.shape
    tt = min(ROUTE_TILE, t)
    k = PEER_TOPK
    return pl.pallas_call(
        _route_kernel,
        grid=(t // tt, PEER_HEADS),
        in_specs=[pl.BlockSpec((tt, PEER_QDIM), lambda i, h: (i, h)),
                  pl.BlockSpec((1, 2, N_KEYS, PEER_QDIM // 2), lambda i, h: (h, 0, 0, 0))],
        out_specs=[pl.BlockSpec((k, tt), lambda i, h: (h, i)),
                   pl.BlockSpec((k, tt), lambda i, h: (h, i))],
        out_shape=[jax.ShapeDtypeStruct((PEER_HEADS * k, t), jnp.int32),
                   jax.ShapeDtypeStruct((PEER_HEADS * k, t), F32)],
        name="route",
    )(qp, sub_keys)


def _final_kernel(x1_ref, po_ref, mod_ref, g_ref, o_ref):
    x2 = x1_ref[0] + mod_ref[0][5:6, :] * po_ref[0]
    o_ref[0] = _rms_rows(x2) * g_ref[...]


def _final(x1, peer_out, mod, final_g):
    bsz, s, d = x1.shape
    tm = min(ROW_TILE, s)
    tile = pl.BlockSpec((1, tm, d), lambda b, i: (b, i, 0))
    return pl.pallas_call(
        _final_kernel,
        grid=(bsz, s // tm),
        in_specs=[tile, tile, pl.BlockSpec((1, 6, d), lambda b, i: (b, 0, 0)),
                  pl.BlockSpec((1, d), lambda b, i: (0, 0))],
        out_specs=tile,
        out_shape=jax.ShapeDtypeStruct((bsz, s, d), F32),
        name="final",
    )(x1, peer_out, mod, final_g.reshape(1, d))


GELU_C = math.sqrt(2.0 / math.pi)
SC_LANES = 16
SC_WORKERS = 32
SC_CORES = 2
SC_TOKEN_BLOCK = 8
N_PAIRS = PEER_HEADS * PEER_TOPK


def _gelu_tanh(x):
    z = GELU_C * (x + 0.044715 * (x * x * x))
    t = 1.0 - 2.0 / (jnp.exp(2.0 * z) + 1.0)
    return x * (0.5 * (1.0 + t))


def _experts(h2, idx, g, u, v):
    t_total, d = h2.shape
    tpw = t_total // SC_WORKERS
    tb = SC_TOKEN_BLOCK
    nchunk = d // SC_LANES
    k = PEER_TOPK
    nh = PEER_HEADS
    n_items = tb * nh
    mesh = plsc.VectorSubcoreMesh(core_axis_name="c", subcore_axis_name="s")

    @functools.partial(
        pl.kernel, mesh=mesh,
        out_type=jax.ShapeDtypeStruct((t_total, d), F32),
        scratch_types=[
            pltpu.VMEM((tb, N_PAIRS), jnp.int32),
            pltpu.VMEM((tb, N_PAIRS), F32),
            pltpu.VMEM((tb, d), F32),
            pltpu.VMEM((tb, d), F32),
            pltpu.VMEM((2, k, d), F32),
            pltpu.VMEM((2, k, d), F32),
            pltpu.SemaphoreType.DMA((2,)),
            pltpu.SemaphoreType.DMA((2,)),
        ],
        compiler_params=pltpu.CompilerParams(needs_layout_passes=False),
        name="experts",
    )
    def experts(h_hbm, idx_hbm, g_hbm, u_hbm, v_hbm, out_hbm,
                idx_b, g_b, h_b, out_b, urows, vrows, usem, vsem):
        wid = lax.axis_index("s") * SC_CORES + lax.axis_index("c")
        base = wid * tpw
        lane = lax.iota(jnp.int32, SC_LANES)

        def split(item):
            return lax.shift_right_logical(item, 3), lax.bitwise_and(item, nh - 1)

        def copies(item, b):
            tt, hd = split(item)
            ids = idx_b.at[tt, pl.ds(hd * k, k)]
            return (pltpu.make_async_copy(u_hbm.at[ids], urows.at[b], usem.at[b]),
                    pltpu.make_async_copy(v_hbm.at[ids], vrows.at[b], vsem.at[b]))

        def fetch(item, b):
            cu, cv = copies(item, b)
            cu.start()
            cv.start()

        def compute(item, b):
            tt, hd = split(item)
            cu, cv = copies(item, b)
            cu.wait()

            def ubody(j, accs):
                sl = pl.ds(j * SC_LANES, SC_LANES)
                hj = h_b[tt, sl]
                return tuple(a + urows[b, p, sl] * hj for p, a in enumerate(accs))

            accs = lax.fori_loop(0, nchunk, ubody,
                                 tuple(jnp.zeros((SC_LANES,), F32) for _ in range(k)))
            s = jnp.zeros((SC_LANES,), F32)
            for p in range(k):
                s = jnp.where(lane == p, jnp.sum(accs[p]), s)
            c = g_b[tt, pl.ds(hd * k, k)] * _gelu_tanh(s)
            cb = [jnp.full((SC_LANES,), c[p]) for p in range(k)]
            cv.wait()

            @pl.loop(0, nchunk)
            def _(j):
                sl = pl.ds(j * SC_LANES, SC_LANES)
                acc = out_b[tt, sl]
                for p in range(k):
                    acc = acc + cb[p] * vrows[b, p, sl]
                out_b[tt, sl] = acc

        @pl.loop(0, tpw // tb)
        def _(blk):
            t0 = base + blk * tb
            pltpu.sync_copy(idx_hbm.at[pl.ds(t0, tb)], idx_b)
            pltpu.sync_copy(g_hbm.at[pl.ds(t0, tb)], g_b)
            pltpu.sync_copy(h_hbm.at[pl.ds(t0, tb)], h_b)
            fetch(0, 0)

            @pl.loop(0, tb)
            def _(tt):
                @pl.loop(0, nchunk)
                def _(j):
                    out_b[tt, pl.ds(j * SC_LANES, SC_LANES)] = jnp.zeros((SC_LANES,), F32)

            @pl.loop(0, n_items, step=2)
            def _(it):
                fetch(it + 1, 1)
                compute(it, 0)

                @pl.when(it + 2 < n_items)
                def _():
                    fetch(it + 2, 0)

                compute(it + 1, 1)

            pltpu.sync_copy(out_b, out_hbm.at[pl.ds(t0, tb)])

    return experts(h2, idx, g, u, v)


def _front(x, c, w_ada, b_ada, norm1_g, norm2_g, w_in, conv_w, conv_b, b_igate, b_fgate,
           lam_q1, lam_k1, lam_q2, lam_k2, diff_sub_g, mlstm_norm_g, w_out,
           peer_w_q, peer_sub_keys, rel_bias):
    bsz, s, d = x.shape
    mod = _ada(c, w_ada[0], b_ada[0]).reshape(bsz, 6, d)

    w = w_in[0]
    w_att = w[:, :3 * D_ATT].astype(BF16)
    w_m = w[:, 3 * D_ATT:3 * D_ATT + 4 * D_MLSTM].astype(BF16)
    w_gt = w[:, 3 * D_ATT + 4 * D_MLSTM:].T.astype(BF16)
    att_qkv, m_qkvo, gates = _inproj(x, mod, norm1_g[0], w_att, w_m, w_gt)

    bias_tiles = _relbias(rel_bias)
    lam4 = jnp.stack([lam_q1[0], lam_k1[0], lam_q2[0], lam_k2[0]])
    att = _attn(att_qkv, bias_tiles, lam4, diff_sub_g[0])

    gate_bias = jnp.concatenate([b_igate[0], b_fgate[0]])
    hm = _mlstm(m_qkvo, gates, gate_bias, conv_w[0], conv_b[0], mlstm_norm_g[0])

    x1, h2, qp = _outproj(att, hm, x, mod, norm2_g[0], w_out[0].astype(BF16),
                          peer_w_q[0].astype(BF16))
    idx_t, g_t = _route(qp.reshape(bsz * s, -1), peer_sub_keys[0].astype(BF16))
    return mod, x1, h2, idx_t, g_t


def kernel(x, c, w_ada, b_ada, norm1_g, norm2_g, w_in, conv_w, conv_b, b_igate, b_fgate, lam_q1, lam_k1, lam_q2, lam_k2, diff_sub_g, mlstm_norm_g, w_out, peer_w_q, peer_sub_keys, peer_u, peer_v, rel_bias, final_g):
    bsz, s, d = x.shape
    mod, x1, h2, idx_t, g_t = _front(x, c, w_ada, b_ada, norm1_g, norm2_g, w_in, conv_w, conv_b,
                                     b_igate, b_fgate, lam_q1, lam_k1, lam_q2, lam_k2,
                                     diff_sub_g, mlstm_norm_g, w_out, peer_w_q, peer_sub_keys,
                                     rel_bias)
    peer_out = _experts(h2.reshape(bsz * s, d), idx_t.T, g_t.T, peer_u[0], peer_v[0])
    return _final(x1, peer_out.reshape(bsz, s, d), mod, final_g)
```

```python
import functools
import math

import numpy as np
import jax
import jax.numpy as jnp
from jax import lax
from jax.experimental import pallas as pl
from jax.experimental.pallas import tpu as pltpu
from jax.experimental.pallas import tpu_sc as plsc

F32 = jnp.float32
BF16 = jnp.bfloat16

ATT_HEADS = 4
ATT_QK_DIM = 64
ATT_V_DIM = 128
D_ATT = ATT_HEADS * ATT_V_DIM
M_HEADS = 4
M_DIM = 128
D_MLSTM = M_HEADS * M_DIM
CONV_W = 4
N_BUCKETS = 32
MAX_DIST = 128
N_KEYS = 128
PEER_HEADS = 8
PEER_TOPK = 16
PEER_QDIM = 256
EPS = 1e-6
LAMBDA_INIT = 0.8 - 0.6 * math.exp(-0.3 * 0)

ATT_BLOCK = 256
M_CHUNK = 128
ROW_TILE = 512
ROUTE_TILE = 512
LANES = 128
VMEM_LIMIT = 48 * 1024 * 1024

_NT = (((1,), (1,)), ((), ()))


def _rms_rows(x):
    return x * lax.rsqrt(jnp.mean(x * x, axis=-1, keepdims=True) + EPS)


def _sigmoid(x):
    return 1.0 / (1.0 + jnp.exp(-x))


def _ada_kernel(c_ref, w_ref, b_ref, o_ref):
    c = c_ref[...]
    cond = c * _sigmoid(c)
    o_ref[...] = jnp.dot(cond, w_ref[...], preferred_element_type=F32) + b_ref[...]


def _ada(c, w, b):
    bsz, d = c.shape
    n = w.shape[1]
    return pl.pallas_call(
        _ada_kernel,
        grid=(n // d,),
        in_specs=[pl.BlockSpec((bsz, d), lambda j: (0, 0)),
                  pl.BlockSpec((d, d), lambda j: (0, j)),
                  pl.BlockSpec((1, d), lambda j: (0, j))],
        out_specs=pl.BlockSpec((bsz, d), lambda j: (0, j)),
        out_shape=jax.ShapeDtypeStruct((bsz, n), F32),
        name="ada",
    )(c, w, b.reshape(1, n))


def _inproj_kernel(x_ref, mod_ref, g_ref, wa_ref, wm_ref, wg_ref, oa_ref, om_ref, og_ref):
    x = x_ref[0]
    mod = mod_ref[0]
    h = _rms_rows(x) * g_ref[...]
    h = h * (1.0 + mod[1:2, :]) + mod[0:1, :]
    hb = h.astype(BF16)
    oa_ref[0] = jnp.dot(hb, wa_ref[...], preferred_element_type=F32).astype(BF16)
    om_ref[0] = jnp.dot(hb, wm_ref[...], preferred_element_type=F32).astype(BF16)
    og_ref[0] = lax.dot_general(wg_ref[...], hb, _NT, preferred_element_type=F32)


def _inproj(x, mod, g, w_att, w_m, w_gt):
    bsz, s, d = x.shape
    tm = min(ROW_TILE, s)
    na, nm, ng = w_att.shape[1], w_m.shape[1], w_gt.shape[0]
    return pl.pallas_call(
        _inproj_kernel,
        grid=(bsz, s // tm),
        in_specs=[pl.BlockSpec((1, tm, d), lambda b, i: (b, i, 0)),
                  pl.BlockSpec((1, 6, d), lambda b, i: (b, 0, 0)),
                  pl.BlockSpec((1, d), lambda b, i: (0, 0)),
                  pl.BlockSpec((d, na), lambda b, i: (0, 0)),
                  pl.BlockSpec((d, nm), lambda b, i: (0, 0)),
                  pl.BlockSpec((ng, d), lambda b, i: (0, 0))],
        out_specs=[pl.BlockSpec((1, tm, na), lambda b, i: (b, i, 0)),
                   pl.BlockSpec((1, tm, nm), lambda b, i: (b, i, 0)),
                   pl.BlockSpec((1, ng, tm), lambda b, i: (b, 0, i))],
        out_shape=[jax.ShapeDtypeStruct((bsz, s, na), BF16),
                   jax.ShapeDtypeStruct((bsz, s, nm), BF16),
                   jax.ShapeDtypeStruct((bsz, ng, s), F32)],
        compiler_params=pltpu.CompilerParams(vmem_limit_bytes=VMEM_LIMIT),
        name="inproj",
    )(x, mod, g.reshape(1, d), w_att, w_m, w_gt)


def _rel_buckets():
    n = np.arange(2 * ATT_BLOCK)
    max_exact = N_BUCKETS // 2
    nf = np.maximum(n, 1).astype(np.float64)
    large = max_exact + (np.log(nf / max_exact) / math.log(MAX_DIST / max_exact)
                         * (N_BUCKETS - max_exact)).astype(np.int64)
    large = np.minimum(large, N_BUCKETS - 1)
    bucket = np.where(n < max_exact, n, large)
    qk = np.arange(ATT_BLOCK)[:, None] - np.arange(ATT_BLOCK)[None, :]
    tiles = np.stack([bucket[np.maximum(qk, 0)], bucket[ATT_BLOCK + qk]])
    return tiles.astype(np.int32)


def _relbias_kernel(rb_ref, bk_ref, o_ref):
    h = pl.program_id(0)
    bk = bk_ref[...]
    acc = jnp.zeros(bk.shape, F32)
    for b in range(N_BUCKETS):
        acc = jnp.where(bk == b, rb_ref[b, h], acc)
    o_ref[0] = acc


def _relbias(rel_bias):
    tiles = jnp.asarray(_rel_buckets())
    return pl.pallas_call(
        _relbias_kernel,
        grid=(ATT_HEADS,),
        in_specs=[pl.BlockSpec(memory_space=pltpu.SMEM),
                  pl.BlockSpec((2, ATT_BLOCK, ATT_BLOCK), lambda h: (0, 0, 0))],
        out_specs=pl.BlockSpec((1, 2, ATT_BLOCK, ATT_BLOCK), lambda h: (h, 0, 0, 0)),
        out_shape=jax.ShapeDtypeStruct((ATT_HEADS, 2, ATT_BLOCK, ATT_BLOCK), F32),
        name="relbias",
    )(rel_bias, tiles)


def _attn_kernel(q_ref, k_ref, v_ref, bias_ref, lam_ref, subg_ref, o_ref,
                 qz_ref, m_ref, l_ref, acc_ref):
    tq = ATT_BLOCK
    qi = pl.program_id(2)
    scale = ATT_QK_DIM ** -0.5

    q = q_ref[0]
    lane = lax.broadcasted_iota(jnp.int32, q.shape, 1)
    zero = jnp.zeros_like(q)
    qz_ref[0:tq, :] = jnp.where(lane < ATT_QK_DIM, q, zero)
    qz_ref[tq:2 * tq, :] = jnp.where(lane >= ATT_QK_DIM, q, zero)
    m_ref[...] = jnp.full(m_ref.shape, -jnp.inf, F32)
    l_ref[...] = jnp.zeros(l_ref.shape, F32)
    acc_ref[...] = jnp.zeros(acc_ref.shape, F32)

    def step(j, bias, masked):
        start = pl.multiple_of(j * tq, tq)
        k = k_ref[0, pl.ds(start, tq), :]
        v = v_ref[0, pl.ds(start, tq), :]
        s = lax.dot_general(qz_ref[...], k, _NT, preferred_element_type=F32) * scale
        if isinstance(bias, tuple):
            s = s + jnp.concatenate([bias[0], bias[0]], axis=0)
        else:
            s = s + bias
        if masked:
            row = lax.broadcasted_iota(jnp.int32, (tq, tq), 0)
            col = lax.broadcasted_iota(jnp.int32, (tq, tq), 1)
            keep = jnp.concatenate([col <= row, col <= row], axis=0)
            s = jnp.where(keep, s, jnp.finfo(F32).min)
        m_old = m_ref[...]
        m_new = jnp.maximum(m_old, jnp.max(s, axis=-1, keepdims=True))
        alpha = jnp.exp(m_old - m_new)
        p = jnp.exp(s - m_new)
        l_ref[...] = alpha * l_ref[...] + jnp.sum(p, axis=-1, keepdims=True)
        acc_ref[...] = alpha * acc_ref[...] + jnp.dot(p.astype(BF16), v, preferred_element_type=F32)
        m_ref[...] = m_new

    far_bias = bias_ref[0, 1, tq - 1:tq, 0:1]

    def far_body(j, carry):
        step(j, far_bias, False)
        return carry

    lax.fori_loop(0, jnp.maximum(qi - 1, 0), far_body, 0)

    @pl.when(qi >= 1)
    def _():
        step(qi - 1, (bias_ref[0, 1],), False)

    step(qi, (bias_ref[0, 0],), True)

    lam = (jnp.exp(jnp.sum(lam_ref[0:1, :] * lam_ref[1:2, :], axis=-1, keepdims=True))
           - jnp.exp(jnp.sum(lam_ref[2:3, :] * lam_ref[3:4, :], axis=-1, keepdims=True))
           + LAMBDA_INIT)
    o = acc_ref[...] / l_ref[...]
    o = o[0:tq, :] - lam * o[tq:2 * tq, :]
    o = _rms_rows(o) * subg_ref[...] * (1.0 - LAMBDA_INIT)
    o_ref[0] = o.astype(o_ref.dtype)


def _attn(att_qkv, bias_tiles, lam4, sub_g):
    bsz, s, _ = att_qkv.shape
    tq = ATT_BLOCK
    nh = ATT_HEADS
    return pl.pallas_call(
        _attn_kernel,
        grid=(bsz, nh, s // tq),
        in_specs=[pl.BlockSpec((1, tq, ATT_V_DIM), lambda b, h, i: (b, i, h)),
                  pl.BlockSpec((1, s, ATT_V_DIM), lambda b, h, i: (b, 0, nh + h)),
                  pl.BlockSpec((1, s, ATT_V_DIM), lambda b, h, i: (b, 0, 2 * nh + h)),
                  pl.BlockSpec((1, 2, tq, tq), lambda b, h, i: (h, 0, 0, 0)),
                  pl.BlockSpec((4, ATT_QK_DIM), lambda b, h, i: (0, 0)),
                  pl.BlockSpec((1, ATT_V_DIM), lambda b, h, i: (0, 0))],
        out_specs=pl.BlockSpec((1, tq, ATT_V_DIM), lambda b, h, i: (b, i, h)),
        out_shape=jax.ShapeDtypeStruct((bsz, s, D_ATT), BF16),
        scratch_shapes=[pltpu.VMEM((2 * tq, ATT_V_DIM), BF16),
                        pltpu.VMEM((2 * tq, 1), F32),
                        pltpu.VMEM((2 * tq, 1), F32),
                        pltpu.VMEM((2 * tq, ATT_V_DIM), F32)],
        compiler_params=pltpu.CompilerParams(vmem_limit_bytes=VMEM_LIMIT),
        name="attn",
    )(att_qkv, att_qkv, att_qkv, bias_tiles, lam4, sub_g.reshape(1, ATT_V_DIM))


def _mlstm_kernel(q_ref, k_ref, v_ref, o_ref, gi_ref, gf_ref, bias_ref, cwq_ref, cwk_ref,
                  cbq_ref, cbk_ref, ng_ref, out_ref, qs_ref, ks_ref, b_ref, ig_ref):
    s = q_ref.shape[1]
    L = M_CHUNK
    nc = s // L
    h = pl.program_id(1)

    row = lax.broadcasted_iota(jnp.int32, (s, M_DIM), 0)

    def conv_silu(x_ref, w_ref, cb_ref):
        x = x_ref[0].astype(F32)
        w = w_ref[...]
        out = None
        for j in range(CONV_W):
            shift = CONV_W - 1 - j
            xs = x if shift == 0 else jnp.where(row >= shift, pltpu.roll(x, shift, 0), 0.0)
            term = xs * w[j:j + 1, :]
            out = term if out is None else out + term
        out = out + cb_ref[...]
        return out * _sigmoid(out)

    qs_ref[...] = conv_silu(q_ref, cwq_ref, cbq_ref).astype(BF16)
    ks_ref[...] = (conv_silu(k_ref, cwk_ref, cbk_ref) * (M_DIM ** -0.5)).astype(BF16)

    ig = gi_ref[0, 0] + bias_ref[h]
    f = gf_ref[0, 0] + bias_ref[M_HEADS + h]
    logf = jnp.minimum(f, 0.0) - jnp.log(1.0 + jnp.exp(-jnp.abs(f)))
    r = lax.broadcasted_iota(jnp.int32, (L, L), 0)
    c = lax.broadcasted_iota(jnp.int32, (L, L), 1)
    tri = (r <= c).astype(F32)
    b_ref[...] = jnp.dot(logf, tri, preferred_element_type=F32,
                         precision=lax.Precision.HIGHEST)
    ig_ref[...] = ig
    eye = r == c
    causal = c <= r

    def to_col(x_row):
        return jnp.sum(jnp.where(eye, x_row, 0.0), axis=1, keepdims=True)

    def chunk(ci, carry):
        C, n, m = carry
        start = pl.multiple_of(ci * L, L)
        qc = qs_ref[pl.ds(start, L), :]
        kc = ks_ref[pl.ds(start, L), :]
        vc = v_ref[0, pl.ds(start, L), :]
        b_r = b_ref[pl.ds(ci, 1), :]
        ig_r = ig_ref[pl.ds(ci, 1), :]
        b_last = b_r[:, L - 1:L]
        a_r = b_last - b_r + ig_r
        b_c = to_col(b_r)
        a_c = to_col(a_r)

        logd = jnp.where(causal, b_c - b_r + ig_r, -jnp.inf)
        m_inter = b_c + m
        m_j = jnp.maximum(jnp.max(logd, axis=1, keepdims=True), m_inter)
        w = jnp.exp(logd - m_j)
        sqk = lax.dot_general(qc, kc, _NT, preferred_element_type=F32) * w
        inter = jnp.exp(m_inter - m_j)
        num = (jnp.dot(sqk.astype(BF16), vc, preferred_element_type=F32)
               + inter * jnp.dot(qc, C.astype(BF16), preferred_element_type=F32))
        den = (jnp.sum(sqk, axis=1, keepdims=True)
               + inter * jnp.sum(qc.astype(F32) * n, axis=1, keepdims=True))
        hc = num / jnp.maximum(jnp.abs(den), jnp.exp(-m_j))

        og = _sigmoid(o_ref[0, pl.ds(start, L), :].astype(F32))
        out_ref[0, pl.ds(start, L), :] = (_rms_rows(og * hc) * ng_ref[...]).astype(out_ref.dtype)

        m_new = jnp.maximum(b_last + m, jnp.max(a_r, axis=1, keepdims=True))
        decay = jnp.exp(b_last + m - m_new)
        kw = kc.astype(F32) * jnp.exp(a_c - m_new)
        C_new = decay * C + jnp.dot(kw.T.astype(BF16), vc, preferred_element_type=F32)
        n_new = decay * n + jnp.sum(kw, axis=0, keepdims=True)
        return C_new, n_new, m_new

    init = (jnp.zeros((M_DIM, M_DIM), F32), jnp.zeros((1, M_DIM), F32), jnp.zeros((1, 1), F32))
    lax.fori_loop(0, nc, chunk, init)


def _mlstm(m_qkvo, gates, gate_bias, conv_w, conv_b, norm_g):
    bsz, s, _ = m_qkvo.shape
    L = M_CHUNK
    nc = s // L
    nh = M_HEADS
    d = M_DIM
    g4 = gates.reshape(bsz, 2 * nh, nc, L)
    seq = lambda off: pl.BlockSpec((1, s, d), lambda b, h: (b, 0, off + h))
    return pl.pallas_call(
        _mlstm_kernel,
        grid=(bsz, nh),
        in_specs=[seq(0), seq(nh), seq(2 * nh), seq(3 * nh),
                  pl.BlockSpec((1, 1, nc, L), lambda b, h: (b, h, 0, 0)),
                  pl.BlockSpec((1, 1, nc, L), lambda b, h: (b, nh + h, 0, 0)),
                  pl.BlockSpec(memory_space=pltpu.SMEM),
                  pl.BlockSpec((CONV_W, d), lambda b, h: (0, h)),
                  pl.BlockSpec((CONV_W, d), lambda b, h: (0, nh + h)),
                  pl.BlockSpec((1, d), lambda b, h: (0, h)),
                  pl.BlockSpec((1, d), lambda b, h: (0, nh + h)),
                  pl.BlockSpec((1, d), lambda b, h: (0, h))],
        out_specs=pl.BlockSpec((1, s, d), lambda b, h: (b, 0, h)),
        out_shape=jax.ShapeDtypeStruct((bsz, s, D_MLSTM), BF16),
        scratch_shapes=[pltpu.VMEM((s, d), BF16), pltpu.VMEM((s, d), BF16),
                        pltpu.VMEM((nc, L), F32), pltpu.VMEM((nc, L), F32)],
        compiler_params=pltpu.CompilerParams(vmem_limit_bytes=VMEM_LIMIT),
        name="mlstm",
    )(m_qkvo, m_qkvo, m_qkvo, m_qkvo, g4, g4, gate_bias, conv_w, conv_w,
      conv_b.reshape(1, -1), conv_b.reshape(1, -1), norm_g.reshape(1, -1))


def _outproj_kernel(att_ref, hm_ref, x_ref, mod_ref, g2_ref, wo_ref, wq_ref,
                    x1_ref, h2_ref, qp_ref):
    mod = mod_ref[0]
    y = (jnp.dot(att_ref[0], wo_ref[0:D_ATT, :], preferred_element_type=F32)
         + jnp.dot(hm_ref[0], wo_ref[D_ATT:, :], preferred_element_type=F32))
    x1 = x_ref[0] + mod[2:3, :] * y
    x1_ref[0] = x1
    h2 = _rms_rows(x1) * g2_ref[...]
    h2 = h2 * (1.0 + mod[4:5, :]) + mod[3:4, :]
    hb = h2.astype(BF16)
    h2_ref[0] = hb
    qp_ref[0] = jnp.dot(hb, wq_ref[...], preferred_element_type=F32).astype(BF16)


def _outproj(att, hm, x, mod, g2, w_out, w_q):
    bsz, s, d = x.shape
    tm = min(ROW_TILE, s)
    nq = w_q.shape[1]
    tile = lambda n: pl.BlockSpec((1, tm, n), lambda b, i: (b, i, 0))
    return pl.pallas_call(
        _outproj_kernel,
        grid=(bsz, s // tm),
        in_specs=[tile(D_ATT), tile(D_MLSTM), tile(d),
                  pl.BlockSpec((1, 6, d), lambda b, i: (b, 0, 0)),
                  pl.BlockSpec((1, d), lambda b, i: (0, 0)),
                  pl.BlockSpec((d, d), lambda b, i: (0, 0)),
                  pl.BlockSpec((d, nq), lambda b, i: (0, 0))],
        out_specs=[tile(d), tile(d), tile(nq)],
        out_shape=[jax.ShapeDtypeStruct((bsz, s, d), F32),
                   jax.ShapeDtypeStruct((bsz, s, d), BF16),
                   jax.ShapeDtypeStruct((bsz, s, nq), BF16)],
        compiler_params=pltpu.CompilerParams(vmem_limit_bytes=VMEM_LIMIT),
        name="outproj",
    )(att, hm, x, mod, g2.reshape(1, d), w_out, w_q)


def _top16_rows(blocks, ids):
    big = jnp.int32(1 << 30)
    vals, pos = [], []
    for _ in range(PEER_TOPK):
        m = functools.reduce(jnp.maximum, blocks)
        m = jnp.max(m, axis=0, keepdims=True)
        cand = functools.reduce(jnp.minimum,
                                [jnp.where(b == m, i, big) for b, i in zip(blocks, ids)])
        p = jnp.min(cand, axis=0, keepdims=True)
        blocks = [jnp.where(i == p, -jnp.inf, b) for b, i in zip(blocks, ids)]
        vals.append(m)
        pos.append(p)
    return jnp.concatenate(vals, axis=0), jnp.concatenate(pos, axis=0)


def _pick_rows(table, sel):
    out = jnp.zeros(sel.shape, table.dtype)
    for r in range(PEER_TOPK):
        out = jnp.where(sel == r, table[r:r + 1, :], out)
    return out


def _route_kernel(q_ref, keys_ref, idx_ref, g_ref):
    k = PEER_TOPK
    half = PEER_QDIM // 2
    key_id = lax.broadcasted_iota(jnp.int32, (N_KEYS, LANES), 0)
    sub_id = lax.broadcasted_iota(jnp.int32, (k, LANES), 0)
    for t in range(q_ref.shape[0] // LANES):
        rows = pl.ds(t * LANES, LANES)
        sv, si = [], []
        for p in range(2):
            qh = q_ref[rows, p * half:(p + 1) * half]
            s = lax.dot_general(keys_ref[0, p], qh, _NT, preferred_element_type=F32)
            v, i = _top16_rows([s], [key_id])
            sv.append(v)
            si.append(i)
        blocks = [sv[0][i:i + 1, :] + sv[1] for i in range(k)]
        ids = [sub_id + i * k for i in range(k)]
        top_s, pos = _top16_rows(blocks, ids)
        idx = (_pick_rows(si[0], lax.shift_right_logical(pos, 4)) * N_KEYS
               + _pick_rows(si[1], lax.bitwise_and(pos, k - 1)))
        e = jnp.exp(top_s - top_s[0:1, :])
        idx_ref[:, t * LANES:(t + 1) * LANES] = idx
        g_ref[:, t * LANES:(t + 1) * LANES] = e / jnp.sum(e, axis=0, keepdims=True)


def _route(qp, sub_keys):
    t, _ = qp.shape
    tt = min(ROUTE_TILE, t)
    k = PEER_TOPK
    return pl.pallas_call(
        _route_kernel,
        grid=(t // tt, PEER_HEADS),
        in_specs=[pl.BlockSpec((tt, PEER_QDIM), lambda i, h: (i, h)),
                  pl.BlockSpec((1, 2, N_KEYS, PEER_QDIM // 2), lambda i, h: (h, 0, 0, 0))],
        out_specs=[pl.BlockSpec((k, tt), lambda i, h: (h, i)),
                   pl.BlockSpec((k, tt), lambda i, h: (h, i))],
        out_shape=[jax.ShapeDtypeStruct((PEER_HEADS * k, t), jnp.int32),
                   jax.ShapeDtypeStruct((PEER_HEADS * k, t), F32)],
        name="route",
    )(qp, sub_keys)


def _final_kernel(x1_ref, po_ref, mod_ref, g_ref, o_ref):
    x2 = x1_ref[0] + mod_ref[0][5:6, :] * po_ref[0]
    o_ref[0] = _rms_rows(x2) * g_ref[...]


def _final(x1, peer_out, mod, final_g):
    bsz, s, d = x1.shape
    tm = min(ROW_TILE, s)
    tile = pl.BlockSpec((1, tm, d), lambda b, i: (b, i, 0))
    return pl.pallas_call(
        _final_kernel,
        grid=(bsz, s // tm),
        in_specs=[tile, tile, pl.BlockSpec((1, 6, d), lambda b, i: (b, 0, 0)),
                  pl.BlockSpec((1, d), lambda b, i: (0, 0))],
        out_specs=tile,
        out_shape=jax.ShapeDtypeStruct((bsz, s, d), F32),
        name="final",
    )(x1, peer_out, mod, final_g.reshape(1, d))


GELU_C = math.sqrt(2.0 / math.pi)
SC_LANES = 16
SC_WORKERS = 32
SC_CORES = 2
SC_TOKEN_BLOCK = 8
SC_ILV = plsc.PackFormat.INTERLEAVED
N_PAIRS = PEER_HEADS * PEER_TOPK


def _gelu_tanh(x):
    z = GELU_C * (x + 0.044715 * (x * x * x))
    t = 1.0 - 2.0 / (jnp.exp(2.0 * z) + 1.0)
    return x * (0.5 * (1.0 + t))


def _pack_pairs(x):
    xb = x.astype(BF16).reshape(x.shape[:-1] + (x.shape[-1] // 2, 2))
    return lax.bitcast_convert_type(xb, jnp.int32)


def _pack_halves(x):
    n = x.shape[-1]
    xr = x.reshape(x.shape[:-1] + (n // (2 * SC_LANES), 2, SC_LANES))
    return _pack_pairs(jnp.swapaxes(xr, -1, -2).reshape(x.shape))


def _experts(h2, idx, g, u, v):
    t_total, dw = h2.shape
    d = 2 * dw
    tpw = t_total // SC_WORKERS
    tb = SC_TOKEN_BLOCK
    k = PEER_TOPK
    nh = PEER_HEADS
    n_items = tb * nh
    mesh = plsc.VectorSubcoreMesh(core_axis_name="c", subcore_axis_name="s")

    @functools.partial(
        pl.kernel, mesh=mesh,
        out_type=jax.ShapeDtypeStruct((t_total, d), F32),
        scratch_types=[
            pltpu.VMEM((tb, N_PAIRS), jnp.int32),
            pltpu.VMEM((tb, N_PAIRS), F32),
            pltpu.VMEM((tb, dw), jnp.int32),
            pltpu.VMEM((tb, d), F32),
            pltpu.VMEM((2, k, dw), jnp.int32),
            pltpu.VMEM((2, k, dw), jnp.int32),
            pltpu.SemaphoreType.DMA((2,)),
            pltpu.SemaphoreType.DMA((2,)),
        ],
        compiler_params=pltpu.CompilerParams(needs_layout_passes=False),
        name="experts",
    )
    def experts(h_hbm, idx_hbm, g_hbm, u_hbm, v_hbm, out_hbm,
                idx_b, g_b, h_b, out_b, urows, vrows, usem, vsem):
        wid = lax.axis_index("s") * SC_CORES + lax.axis_index("c")
        base = wid * tpw
        lane = lax.iota(jnp.int32, SC_LANES)

        def split(item):
            return lax.shift_right_logical(item, 3), lax.bitwise_and(item, nh - 1)

        def copies(item, b):
            tt, hd = split(item)
            ids = idx_b.at[tt, pl.ds(hd * k, k)]
            return (pltpu.make_async_copy(u_hbm.at[ids], urows.at[b], usem.at[b]),
                    pltpu.make_async_copy(v_hbm.at[ids], vrows.at[b], vsem.at[b]))

        def fetch(item, b):
            cu, cv = copies(item, b)
            cu.start()
            cv.start()

        def words(ref, *lead, off):
            return plsc.bitcast(ref[(*lead, pl.ds(off, SC_LANES))], BF16)

        def sum4_unpack(pr):
            return plsc.unpack((pr[0] + pr[1]) + (pr[2] + pr[3]), format=SC_ILV)

        def compute(item, b):
            tt, hd = split(item)
            cu, cv = copies(item, b)
            cu.wait()

            def ubody(j, accs):
                hs = [words(h_b, tt, off=(j * 4 + q) * SC_LANES) for q in range(4)]
                new = []
                for p, a in enumerate(accs):
                    lo, hi = sum4_unpack([words(urows, b, p, off=(j * 4 + q) * SC_LANES) * hs[q]
                                          for q in range(4)])
                    new.append(a + (lo + hi))
                return tuple(new)

            accs = plsc.parallel_loop(
                0, dw // (4 * SC_LANES), 1,
                carry=tuple(jnp.zeros((SC_LANES,), F32) for _ in range(k)))(ubody)
            s = jnp.zeros((SC_LANES,), F32)
            for p in range(k):
                s = jnp.where(lane == p, jnp.sum(accs[p]), s)
            c = g_b[tt, pl.ds(hd * k, k)] * _gelu_tanh(s)
            cbb = []
            for p in range(k):
                cp = jnp.full((SC_LANES,), c[p])
                cbb.append(plsc.pack(cp, cp, format=SC_ILV))
            cv.wait()

            @plsc.parallel_loop(0, dw // SC_LANES, 1, unroll=2)
            def _(ch):
                sa = pl.ds(ch * 2 * SC_LANES, SC_LANES)
                sb = pl.ds(ch * 2 * SC_LANES + SC_LANES, SC_LANES)
                los, his = [], []
                for grp in range(k // 4):
                    lo, hi = sum4_unpack([cbb[grp * 4 + q]
                                          * words(vrows, b, grp * 4 + q, off=ch * SC_LANES)
                                          for q in range(4)])
                    los.append(lo)
                    his.append(hi)
                out_b[tt, sa] = out_b[tt, sa] + ((los[0] + los[1]) + (los[2] + los[3]))
                out_b[tt, sb] = out_b[tt, sb] + ((his[0] + his[1]) + (his[2] + his[3]))

        @pl.loop(0, tpw // tb)
        def _(blk):
            t0 = base + blk * tb
            pltpu.sync_copy(idx_hbm.at[pl.ds(t0, tb)], idx_b)
            pltpu.sync_copy(g_hbm.at[pl.ds(t0, tb)], g_b)
            pltpu.sync_copy(h_hbm.at[pl.ds(t0, tb)], h_b)
            fetch(0, 0)

            @pl.loop(0, tb)
            def _(tt):
                @pl.loop(0, d // SC_LANES)
                def _(j):
                    out_b[tt, pl.ds(j * SC_LANES, SC_LANES)] = jnp.zeros((SC_LANES,), F32)

            @pl.loop(0, n_items, step=2)
            def _(it):
                fetch(it + 1, 1)
                compute(it, 0)

                @pl.when(it + 2 < n_items)
                def _():
                    fetch(it + 2, 0)

                compute(it + 1, 1)

            pltpu.sync_copy(out_b, out_hbm.at[pl.ds(t0, tb)])

    return experts(h2, idx, g, u, v)


def _front(x, c, w_ada, b_ada, norm1_g, norm2_g, w_in, conv_w, conv_b, b_igate, b_fgate,
           lam_q1, lam_k1, lam_q2, lam_k2, diff_sub_g, mlstm_norm_g, w_out,
           peer_w_q, peer_sub_keys, rel_bias):
    bsz, s, d = x.shape
    mod = _ada(c, w_ada[0], b_ada[0]).reshape(bsz, 6, d)

    w = w_in[0]
    w_att = w[:, :3 * D_ATT].astype(BF16)
    w_m = w[:, 3 * D_ATT:3 * D_ATT + 4 * D_MLSTM].astype(BF16)
    w_gt = w[:, 3 * D_ATT + 4 * D_MLSTM:].T.astype(BF16)
    att_qkv, m_qkvo, gates = _inproj(x, mod, norm1_g[0], w_att, w_m, w_gt)

    bias_tiles = _relbias(rel_bias)
    lam4 = jnp.stack([lam_q1[0], lam_k1[0], lam_q2[0], lam_k2[0]])
    att = _attn(att_qkv, bias_tiles, lam4, diff_sub_g[0])

    gate_bias = jnp.concatenate([b_igate[0], b_fgate[0]])
    hm = _mlstm(m_qkvo, gates, gate_bias, conv_w[0], conv_b[0], mlstm_norm_g[0])

    x1, h2, qp = _outproj(att, hm, x, mod, norm2_g[0], w_out[0].astype(BF16),
                          peer_w_q[0].astype(BF16))
    idx_t, g_t = _route(qp.reshape(bsz * s, -1), peer_sub_keys[0].astype(BF16))
    return mod, x1, h2, idx_t, g_t


def kernel(x, c, w_ada, b_ada, norm1_g, norm2_g, w_in, conv_w, conv_b, b_igate, b_fgate, lam_q1, lam_k1, lam_q2, lam_k2, diff_sub_g, mlstm_norm_g, w_out, peer_w_q, peer_sub_keys, peer_u, peer_v, rel_bias, final_g):
    bsz, s, d = x.shape
    mod, x1, h2, idx_t, g_t = _front(x, c, w_ada, b_ada, norm1_g, norm2_g, w_in, conv_w, conv_b,
                                     b_igate, b_fgate, lam_q1, lam_k1, lam_q2, lam_k2,
                                     diff_sub_g, mlstm_norm_g, w_out, peer_w_q, peer_sub_keys,
                                     rel_bias)
    peer_out = _experts(_pack_pairs(h2.reshape(bsz * s, d)), idx_t.T, g_t.T,
                        _pack_pairs(peer_u[0]), _pack_halves(peer_v[0]))
    return _final(x1, peer_out.reshape(bsz, s, d), mod, final_g)
```

```python
import functools
import math

import numpy as np
import jax
import jax.numpy as jnp
from jax import lax
from jax.experimental import pallas as pl
from jax.experimental.pallas import tpu as pltpu
from jax.experimental.pallas import tpu_sc as plsc

F32 = jnp.float32
BF16 = jnp.bfloat16

ATT_HEADS = 4
ATT_QK_DIM = 64
ATT_V_DIM = 128
D_ATT = ATT_HEADS * ATT_V_DIM
M_HEADS = 4
M_DIM = 128
D_MLSTM = M_HEADS * M_DIM
CONV_W = 4
N_BUCKETS = 32
MAX_DIST = 128
N_KEYS = 128
PEER_HEADS = 8
PEER_TOPK = 16
PEER_QDIM = 256
EPS = 1e-6
LAMBDA_INIT = 0.8 - 0.6 * math.exp(-0.3 * 0)

ATT_BLOCK = 256
M_CHUNK = 128
ROW_TILE = 512
ROUTE_TILE = 512
LANES = 128
VMEM_LIMIT = 48 * 1024 * 1024

_NT = (((1,), (1,)), ((), ()))


def _rms_rows(x):
    return x * lax.rsqrt(jnp.mean(x * x, axis=-1, keepdims=True) + EPS)


def _sigmoid(x):
    return 1.0 / (1.0 + jnp.exp(-x))


def _pack_bf16_halves(x):
    bits = lax.bitcast_convert_type(x.astype(BF16).astype(F32), jnp.int32)
    n = x.shape[-1] // 2
    return lax.bitwise_or(lax.shift_right_logical(bits[..., :n], 16),
                          lax.bitwise_and(bits[..., n:], jnp.int32(-65536)))


def _ada_kernel(c_ref, w_ref, b_ref, o_ref):
    c = c_ref[...]
    cond = c * _sigmoid(c)
    o_ref[...] = jnp.dot(cond, w_ref[...], preferred_element_type=F32) + b_ref[...]


def _ada(c, w, b):
    bsz, d = c.shape
    n = w.shape[1]
    return pl.pallas_call(
        _ada_kernel,
        grid=(n // d,),
        in_specs=[pl.BlockSpec((bsz, d), lambda j: (0, 0)),
                  pl.BlockSpec((d, d), lambda j: (0, j)),
                  pl.BlockSpec((1, d), lambda j: (0, j))],
        out_specs=pl.BlockSpec((bsz, d), lambda j: (0, j)),
        out_shape=jax.ShapeDtypeStruct((bsz, n), F32),
        name="ada",
    )(c, w, b.reshape(1, n))


def _inproj_kernel(x_ref, mod_ref, g_ref, wa_ref, wm_ref, wg_ref, oa_ref, om_ref, og_ref):
    x = x_ref[0]
    mod = mod_ref[0]
    h = _rms_rows(x) * g_ref[...]
    h = h * (1.0 + mod[1:2, :]) + mod[0:1, :]
    hb = h.astype(BF16)
    oa_ref[0] = jnp.dot(hb, wa_ref[...], preferred_element_type=F32).astype(BF16)
    om_ref[0] = jnp.dot(hb, wm_ref[...], preferred_element_type=F32).astype(BF16)
    og_ref[0] = lax.dot_general(wg_ref[...], hb, _NT, preferred_element_type=F32)


def _inproj(x, mod, g, w_att, w_m, w_gt, b0, bsz):
    _, s, d = x.shape
    tm = min(ROW_TILE, s)
    na, nm, ng = w_att.shape[1], w_m.shape[1], w_gt.shape[0]
    return pl.pallas_call(
        _inproj_kernel,
        grid=(bsz, s // tm),
        in_specs=[pl.BlockSpec((1, tm, d), lambda b, i: (b + b0, i, 0)),
                  pl.BlockSpec((1, 6, d), lambda b, i: (b + b0, 0, 0)),
                  pl.BlockSpec((1, d), lambda b, i: (0, 0)),
                  pl.BlockSpec((d, na), lambda b, i: (0, 0)),
                  pl.BlockSpec((d, nm), lambda b, i: (0, 0)),
                  pl.BlockSpec((ng, d), lambda b, i: (0, 0))],
        out_specs=[pl.BlockSpec((1, tm, na), lambda b, i: (b, i, 0)),
                   pl.BlockSpec((1, tm, nm), lambda b, i: (b, i, 0)),
                   pl.BlockSpec((1, ng, tm), lambda b, i: (b, 0, i))],
        out_shape=[jax.ShapeDtypeStruct((bsz, s, na), BF16),
                   jax.ShapeDtypeStruct((bsz, s, nm), BF16),
                   jax.ShapeDtypeStruct((bsz, ng, s), F32)],
        compiler_params=pltpu.CompilerParams(vmem_limit_bytes=VMEM_LIMIT),
        name="inproj",
    )(x, mod, g.reshape(1, d), w_att, w_m, w_gt)


def _rel_buckets():
    n = np.arange(2 * ATT_BLOCK)
    max_exact = N_BUCKETS // 2
    nf = np.maximum(n, 1).astype(np.float64)
    large = max_exact + (np.log(nf / max_exact) / math.log(MAX_DIST / max_exact)
                         * (N_BUCKETS - max_exact)).astype(np.int64)
    large = np.minimum(large, N_BUCKETS - 1)
    bucket = np.where(n < max_exact, n, large)
    qk = np.arange(ATT_BLOCK)[:, None] - np.arange(ATT_BLOCK)[None, :]
    tiles = np.stack([bucket[np.maximum(qk, 0)], bucket[ATT_BLOCK + qk]])
    return tiles.astype(np.int32)


def _relbias_kernel(rb_ref, bk_ref, o_ref):
    h = pl.program_id(0)
    bk = bk_ref[...]
    acc = jnp.zeros(bk.shape, F32)
    for b in range(N_BUCKETS):
        acc = jnp.where(bk == b, rb_ref[b, h], acc)
    o_ref[0] = acc


def _relbias(rel_bias):
    tiles = jnp.asarray(_rel_buckets())
    return pl.pallas_call(
        _relbias_kernel,
        grid=(ATT_HEADS,),
        in_specs=[pl.BlockSpec(memory_space=pltpu.SMEM),
                  pl.BlockSpec((2, ATT_BLOCK, ATT_BLOCK), lambda h: (0, 0, 0))],
        out_specs=pl.BlockSpec((1, 2, ATT_BLOCK, ATT_BLOCK), lambda h: (h, 0, 0, 0)),
        out_shape=jax.ShapeDtypeStruct((ATT_HEADS, 2, ATT_BLOCK, ATT_BLOCK), F32),
        name="relbias",
    )(rel_bias, tiles)


def _attn_kernel(q_ref, k_ref, v_ref, bias_ref, lam_ref, subg_ref, o_ref,
                 qz_ref, m_ref, l_ref, acc_ref):
    tq = ATT_BLOCK
    qi = pl.program_id(2)
    scale = ATT_QK_DIM ** -0.5

    q = q_ref[0]
    lane = lax.broadcasted_iota(jnp.int32, q.shape, 1)
    zero = jnp.zeros_like(q)
    qz_ref[0:tq, :] = jnp.where(lane < ATT_QK_DIM, q, zero)
    qz_ref[tq:2 * tq, :] = jnp.where(lane >= ATT_QK_DIM, q, zero)
    m_ref[...] = jnp.full(m_ref.shape, -jnp.inf, F32)
    l_ref[...] = jnp.zeros(l_ref.shape, F32)
    acc_ref[...] = jnp.zeros(acc_ref.shape, F32)

    def step(j, bias, masked):
        start = pl.multiple_of(j * tq, tq)
        k = k_ref[0, pl.ds(start, tq), :]
        v = v_ref[0, pl.ds(start, tq), :]
        s = lax.dot_general(qz_ref[...], k, _NT, preferred_element_type=F32) * scale
        if isinstance(bias, tuple):
            s = s + jnp.concatenate([bias[0], bias[0]], axis=0)
        else:
            s = s + bias
        if masked:
            row = lax.broadcasted_iota(jnp.int32, (tq, tq), 0)
            col = lax.broadcasted_iota(jnp.int32, (tq, tq), 1)
            keep = jnp.concatenate([col <= row, col <= row], axis=0)
            s = jnp.where(keep, s, jnp.finfo(F32).min)
        m_old = m_ref[...]
        m_new = jnp.maximum(m_old, jnp.max(s, axis=-1, keepdims=True))
        alpha = jnp.exp(m_old - m_new)
        p = jnp.exp(s - m_new)
        l_ref[...] = alpha * l_ref[...] + jnp.sum(p, axis=-1, keepdims=True)
        acc_ref[...] = alpha * acc_ref[...] + jnp.dot(p.astype(BF16), v, preferred_element_type=F32)
        m_ref[...] = m_new

    far_bias = bias_ref[0, 1, tq - 1:tq, 0:1]

    def far_body(j, carry):
        step(j, far_bias, False)
        return carry

    lax.fori_loop(0, jnp.maximum(qi - 1, 0), far_body, 0)

    @pl.when(qi >= 1)
    def _():
        step(qi - 1, (bias_ref[0, 1],), False)

    step(qi, (bias_ref[0, 0],), True)

    lam = (jnp.exp(jnp.sum(lam_ref[0:1, :] * lam_ref[1:2, :], axis=-1, keepdims=True))
           - jnp.exp(jnp.sum(lam_ref[2:3, :] * lam_ref[3:4, :], axis=-1, keepdims=True))
           + LAMBDA_INIT)
    o = acc_ref[...] / l_ref[...]
    o = o[0:tq, :] - lam * o[tq:2 * tq, :]
    o = _rms_rows(o) * subg_ref[...] * (1.0 - LAMBDA_INIT)
    o_ref[0] = o.astype(o_ref.dtype)


def _attn(att_qkv, bias_tiles, lam4, sub_g):
    bsz, s, _ = att_qkv.shape
    tq = ATT_BLOCK
    nh = ATT_HEADS
    return pl.pallas_call(
        _attn_kernel,
        grid=(bsz, nh, s // tq),
        in_specs=[pl.BlockSpec((1, tq, ATT_V_DIM), lambda b, h, i: (b, i, h)),
                  pl.BlockSpec((1, s, ATT_V_DIM), lambda b, h, i: (b, 0, nh + h)),
                  pl.BlockSpec((1, s, ATT_V_DIM), lambda b, h, i: (b, 0, 2 * nh + h)),
                  pl.BlockSpec((1, 2, tq, tq), lambda b, h, i: (h, 0, 0, 0)),
                  pl.BlockSpec((4, ATT_QK_DIM), lambda b, h, i: (0, 0)),
                  pl.BlockSpec((1, ATT_V_DIM), lambda b, h, i: (0, 0))],
        out_specs=pl.BlockSpec((1, tq, ATT_V_DIM), lambda b, h, i: (b, i, h)),
        out_shape=jax.ShapeDtypeStruct((bsz, s, D_ATT), BF16),
        scratch_shapes=[pltpu.VMEM((2 * tq, ATT_V_DIM), BF16),
                        pltpu.VMEM((2 * tq, 1), F32),
                        pltpu.VMEM((2 * tq, 1), F32),
                        pltpu.VMEM((2 * tq, ATT_V_DIM), F32)],
        compiler_params=pltpu.CompilerParams(vmem_limit_bytes=VMEM_LIMIT),
        name="attn",
    )(att_qkv, att_qkv, att_qkv, bias_tiles, lam4, sub_g.reshape(1, ATT_V_DIM))


def _mlstm_kernel(q_ref, k_ref, v_ref, o_ref, gi_ref, gf_ref, bias_ref, cwq_ref, cwk_ref,
                  cbq_ref, cbk_ref, ng_ref, out_ref, qs_ref, ks_ref, b_ref, ig_ref):
    s = q_ref.shape[1]
    L = M_CHUNK
    nc = s // L
    h = pl.program_id(1)

    row = lax.broadcasted_iota(jnp.int32, (s, M_DIM), 0)

    def conv_silu(x_ref, w_ref, cb_ref):
        x = x_ref[0].astype(F32)
        w = w_ref[...]
        out = None
        for j in range(CONV_W):
            shift = CONV_W - 1 - j
            xs = x if shift == 0 else jnp.where(row >= shift, pltpu.roll(x, shift, 0), 0.0)
            term = xs * w[j:j + 1, :]
            out = term if out is None else out + term
        out = out + cb_ref[...]
        return out * _sigmoid(out)

    qs_ref[...] = conv_silu(q_ref, cwq_ref, cbq_ref).astype(BF16)
    ks_ref[...] = (conv_silu(k_ref, cwk_ref, cbk_ref) * (M_DIM ** -0.5)).astype(BF16)

    ig = gi_ref[0, 0] + bias_ref[h]
    f = gf_ref[0, 0] + bias_ref[M_HEADS + h]
    logf = jnp.minimum(f, 0.0) - jnp.log(1.0 + jnp.exp(-jnp.abs(f)))
    r = lax.broadcasted_iota(jnp.int32, (L, L), 0)
    c = lax.broadcasted_iota(jnp.int32, (L, L), 1)
    tri = (r <= c).astype(F32)
    b_ref[...] = jnp.dot(logf, tri, preferred_element_type=F32,
                         precision=lax.Precision.HIGHEST)
    ig_ref[...] = ig
    eye = r == c
    causal = c <= r

    def to_col(x_row):
        return jnp.sum(jnp.where(eye, x_row, 0.0), axis=1, keepdims=True)

    def chunk(ci, carry):
        C, n, m = carry
        start = pl.multiple_of(ci * L, L)
        qc = qs_ref[pl.ds(start, L), :]
        kc = ks_ref[pl.ds(start, L), :]
        vc = v_ref[0, pl.ds(start, L), :]
        b_r = b_ref[pl.ds(ci, 1), :]
        ig_r = ig_ref[pl.ds(ci, 1), :]
        b_last = b_r[:, L - 1:L]
        a_r = b_last - b_r + ig_r
        b_c = to_col(b_r)
        a_c = to_col(a_r)

        logd = jnp.where(causal, b_c - b_r + ig_r, -jnp.inf)
        m_inter = b_c + m
        m_j = jnp.maximum(jnp.max(logd, axis=1, keepdims=True), m_inter)
        w = jnp.exp(logd - m_j)
        sqk = lax.dot_general(qc, kc, _NT, preferred_element_type=F32) * w
        inter = jnp.exp(m_inter - m_j)
        num = (jnp.dot(sqk.astype(BF16), vc, preferred_element_type=F32)
               + inter * jnp.dot(qc, C.astype(BF16), preferred_element_type=F32))
        den = (jnp.sum(sqk, axis=1, keepdims=True)
               + inter * jnp.sum(qc.astype(F32) * n, axis=1, keepdims=True))
        hc = num / jnp.maximum(jnp.abs(den), jnp.exp(-m_j))

        og = _sigmoid(o_ref[0, pl.ds(start, L), :].astype(F32))
        out_ref[0, pl.ds(start, L), :] = (_rms_rows(og * hc) * ng_ref[...]).astype(out_ref.dtype)

        m_new = jnp.maximum(b_last + m, jnp.max(a_r, axis=1, keepdims=True))
        decay = jnp.exp(b_last + m - m_new)
        kw = kc.astype(F32) * jnp.exp(a_c - m_new)
        C_new = decay * C + jnp.dot(kw.T.astype(BF16), vc, preferred_element_type=F32)
        n_new = decay * n + jnp.sum(kw, axis=0, keepdims=True)
        return C_new, n_new, m_new

    init = (jnp.zeros((M_DIM, M_DIM), F32), jnp.zeros((1, M_DIM), F32), jnp.zeros((1, 1), F32))
    lax.fori_loop(0, nc, chunk, init)


def _mlstm(m_qkvo, gates, gate_bias, conv_w, conv_b, norm_g):
    bsz, s, _ = m_qkvo.shape
    L = M_CHUNK
    nc = s // L
    nh = M_HEADS
    d = M_DIM
    g4 = gates.reshape(bsz, 2 * nh, nc, L)
    seq = lambda off: pl.BlockSpec((1, s, d), lambda b, h: (b, 0, off + h))
    return pl.pallas_call(
        _mlstm_kernel,
        grid=(bsz, nh),
        in_specs=[seq(0), seq(nh), seq(2 * nh), seq(3 * nh),
                  pl.BlockSpec((1, 1, nc, L), lambda b, h: (b, h, 0, 0)),
                  pl.BlockSpec((1, 1, nc, L), lambda b, h: (b, nh + h, 0, 0)),
                  pl.BlockSpec(memory_space=pltpu.SMEM),
                  pl.BlockSpec((CONV_W, d), lambda b, h: (0, h)),
                  pl.BlockSpec((CONV_W, d), lambda b, h: (0, nh + h)),
                  pl.BlockSpec((1, d), lambda b, h: (0, h)),
                  pl.BlockSpec((1, d), lambda b, h: (0, nh + h)),
                  pl.BlockSpec((1, d), lambda b, h: (0, h))],
        out_specs=pl.BlockSpec((1, s, d), lambda b, h: (b, 0, h)),
        out_shape=jax.ShapeDtypeStruct((bsz, s, D_MLSTM), BF16),
        scratch_shapes=[pltpu.VMEM((s, d), BF16), pltpu.VMEM((s, d), BF16),
                        pltpu.VMEM((nc, L), F32), pltpu.VMEM((nc, L), F32)],
        compiler_params=pltpu.CompilerParams(vmem_limit_bytes=VMEM_LIMIT),
        name="mlstm",
    )(m_qkvo, m_qkvo, m_qkvo, m_qkvo, g4, g4, gate_bias, conv_w, conv_w,
      conv_b.reshape(1, -1), conv_b.reshape(1, -1), norm_g.reshape(1, -1))


def _outproj_kernel(att_ref, hm_ref, x_ref, mod_ref, g2_ref, wo_ref, wq_ref,
                    x1_ref, h2_ref, qp_ref):
    mod = mod_ref[0]
    y = (jnp.dot(att_ref[0], wo_ref[0:D_ATT, :], preferred_element_type=F32)
         + jnp.dot(hm_ref[0], wo_ref[D_ATT:, :], preferred_element_type=F32))
    x1 = x_ref[0] + mod[2:3, :] * y
    x1_ref[0] = x1
    h2 = _rms_rows(x1) * g2_ref[...]
    h2 = h2 * (1.0 + mod[4:5, :]) + mod[3:4, :]
    hb = h2.astype(BF16)
    h2_ref[0] = _pack_bf16_halves(hb)
    qp_ref[0] = jnp.dot(hb, wq_ref[...], preferred_element_type=F32).astype(BF16)


def _outproj(att, hm, x, mod, g2, w_out, w_q, b0):
    bsz, s, _ = att.shape
    d = x.shape[-1]
    tm = min(ROW_TILE, s)
    nq = w_q.shape[1]
    tile = lambda n: pl.BlockSpec((1, tm, n), lambda b, i: (b, i, 0))
    return pl.pallas_call(
        _outproj_kernel,
        grid=(bsz, s // tm),
        in_specs=[tile(D_ATT), tile(D_MLSTM),
                  pl.BlockSpec((1, tm, d), lambda b, i: (b + b0, i, 0)),
                  pl.BlockSpec((1, 6, d), lambda b, i: (b + b0, 0, 0)),
                  pl.BlockSpec((1, d), lambda b, i: (0, 0)),
                  pl.BlockSpec((d, d), lambda b, i: (0, 0)),
                  pl.BlockSpec((d, nq), lambda b, i: (0, 0))],
        out_specs=[tile(d), tile(d // 2), tile(nq)],
        out_shape=[jax.ShapeDtypeStruct((bsz, s, d), F32),
                   jax.ShapeDtypeStruct((bsz, s, d // 2), jnp.int32),
                   jax.ShapeDtypeStruct((bsz, s, nq), BF16)],
        compiler_params=pltpu.CompilerParams(vmem_limit_bytes=VMEM_LIMIT),
        name="outproj",
    )(att, hm, x, mod, g2.reshape(1, d), w_out, w_q)


def _top16_rows(blocks, ids):
    big = jnp.int32(1 << 30)
    vals, pos = [], []
    for _ in range(PEER_TOPK):
        m = functools.reduce(jnp.maximum, blocks)
        m = jnp.max(m, axis=0, keepdims=True)
        cand = functools.reduce(jnp.minimum,
                                [jnp.where(b == m, i, big) for b, i in zip(blocks, ids)])
        p = jnp.min(cand, axis=0, keepdims=True)
        blocks = [jnp.where(i == p, -jnp.inf, b) for b, i in zip(blocks, ids)]
        vals.append(m)
        pos.append(p)
    return jnp.concatenate(vals, axis=0), jnp.concatenate(pos, axis=0)


def _pick_rows(table, sel):
    out = jnp.zeros(sel.shape, table.dtype)
    for r in range(PEER_TOPK):
        out = jnp.where(sel == r, table[r:r + 1, :], out)
    return out


def _route_kernel(q_ref, keys_ref, idx_ref, g_ref):
    k = PEER_TOPK
    half = PEER_QDIM // 2
    key_id = lax.broadcasted_iota(jnp.int32, (N_KEYS, LANES), 0)
    sub_id = lax.broadcasted_iota(jnp.int32, (k, LANES), 0)
    for t in range(q_ref.shape[0] // LANES):
        rows = pl.ds(t * LANES, LANES)
        sv, si = [], []
        for p in range(2):
            qh = q_ref[rows, p * half:(p + 1) * half]
            s = lax.dot_general(keys_ref[0, p], qh, _NT, preferred_element_type=F32)
            v, i = _top16_rows([s], [key_id])
            sv.append(v)
            si.append(i)
        blocks = [sv[0][i:i + 1, :] + sv[1] for i in range(k)]
        ids = [sub_id + i * k for i in range(k)]
        top_s, pos = _top16_rows(blocks, ids)
        idx = (_pick_rows(si[0], lax.shift_right_logical(pos, 4)) * N_KEYS
               + _pick_rows(si[1], lax.bitwise_and(pos, k - 1)))
        e = jnp.exp(top_s - top_s[0:1, :])
        idx_ref[:, t * LANES:(t + 1) * LANES] = idx
        g_ref[:, t * LANES:(t + 1) * LANES] = e / jnp.sum(e, axis=0, keepdims=True)


def _route(qp, sub_keys):
    t, _ = qp.shape
    tt = min(ROUTE_TILE, t)
    k = PEER_TOPK
    return pl.pallas_call(
        _route_kernel,
        grid=(t // tt, PEER_HEADS),
        in_specs=[pl.BlockSpec((tt, PEER_QDIM), lambda i, h: (i, h)),
                  pl.BlockSpec((1, 2, N_KEYS, PEER_QDIM // 2), lambda i, h: (h, 0, 0, 0))],
        out_specs=[pl.BlockSpec((k, tt), lambda i, h: (h, i)),
                   pl.BlockSpec((k, tt), lambda i, h: (h, i))],
        out_shape=[jax.ShapeDtypeStruct((PEER_HEADS * k, t), jnp.int32),
                   jax.ShapeDtypeStruct((PEER_HEADS * k, t), F32)],
        name="route",
    )(qp, sub_keys)


def _final_kernel(x1_ref, po_ref, mod_ref, g_ref, o_ref):
    x2 = x1_ref[0] + mod_ref[0][5:6, :] * po_ref[0]
    o_ref[0] = _rms_rows(x2) * g_ref[...]


def _final(x1, peer_out, mod, final_g, b0):
    bsz, s, d = x1.shape
    tm = min(ROW_TILE, s)
    tile = pl.BlockSpec((1, tm, d), lambda b, i: (b, i, 0))
    return pl.pallas_call(
        _final_kernel,
        grid=(bsz, s // tm),
        in_specs=[tile, tile, pl.BlockSpec((1, 6, d), lambda b, i: (b + b0, 0, 0)),
                  pl.BlockSpec((1, d), lambda b, i: (0, 0))],
        out_specs=tile,
        out_shape=jax.ShapeDtypeStruct((bsz, s, d), F32),
        name="final",
    )(x1, peer_out, mod, final_g.reshape(1, d))


GELU_C = math.sqrt(2.0 / math.pi)
SC_LANES = 16
SC_WORKERS = 32
SC_CORES = 2
SC_TOKEN_BLOCK = 8
SC_ILV = plsc.PackFormat.INTERLEAVED
N_PAIRS = PEER_HEADS * PEER_TOPK


def _gelu_tanh(x):
    z = GELU_C * (x + 0.044715 * (x * x * x))
    t = 1.0 - 2.0 / (jnp.exp(2.0 * z) + 1.0)
    return x * (0.5 * (1.0 + t))


def _experts(h2, idx, g, u, v):
    t_total, dw = h2.shape
    d = 2 * dw
    tpw = t_total // SC_WORKERS
    tb = SC_TOKEN_BLOCK
    k = PEER_TOPK
    nh = PEER_HEADS
    n_items = tb * nh
    mesh = plsc.VectorSubcoreMesh(core_axis_name="c", subcore_axis_name="s")

    @functools.partial(
        pl.kernel, mesh=mesh,
        out_type=jax.ShapeDtypeStruct((t_total, d), F32),
        scratch_types=[
            pltpu.VMEM((tb, N_PAIRS), jnp.int32),
            pltpu.VMEM((tb, N_PAIRS), F32),
            pltpu.VMEM((tb, dw), jnp.int32),
            pltpu.VMEM((tb, d), F32),
            pltpu.VMEM((2, k, dw), jnp.int32),
            pltpu.VMEM((2, k, dw), jnp.int32),
            pltpu.SemaphoreType.DMA((2,)),
            pltpu.SemaphoreType.DMA((2,)),
        ],
        compiler_params=pltpu.CompilerParams(needs_layout_passes=False),
        name="experts",
    )
    def experts(h_hbm, idx_hbm, g_hbm, u_hbm, v_hbm, out_hbm,
                idx_b, g_b, h_b, out_b, urows, vrows, usem, vsem):
        wid = lax.axis_index("s") * SC_CORES + lax.axis_index("c")
        base = wid * tpw
        lane = lax.iota(jnp.int32, SC_LANES)

        def split(item):
            return lax.shift_right_logical(item, 3), lax.bitwise_and(item, nh - 1)

        def copies(item, b):
            tt, hd = split(item)
            ids = idx_b.at[tt, pl.ds(hd * k, k)]
            return (pltpu.make_async_copy(u_hbm.at[ids], urows.at[b], usem.at[b]),
                    pltpu.make_async_copy(v_hbm.at[ids], vrows.at[b], vsem.at[b]))

        def fetch(item, b):
            cu, cv = copies(item, b)
            cu.start()
            cv.start()

        def words(ref, *lead, off):
            return plsc.bitcast(ref[(*lead, pl.ds(off, SC_LANES))], BF16)

        def sum4_unpack(pr):
            return plsc.unpack((pr[0] + pr[1]) + (pr[2] + pr[3]), format=SC_ILV)

        def compute(item, b):
            tt, hd = split(item)
            cu, cv = copies(item, b)
            cu.wait()

            def ubody(j, accs):
                hs = [words(h_b, tt, off=(j * 4 + q) * SC_LANES) for q in range(4)]
                new = []
                for p, a in enumerate(accs):
                    lo, hi = sum4_unpack([words(urows, b, p, off=(j * 4 + q) * SC_LANES) * hs[q]
                                          for q in range(4)])
                    new.append(a + (lo + hi))
                return tuple(new)

            accs = plsc.parallel_loop(
                0, dw // (4 * SC_LANES), 1,
                carry=tuple(jnp.zeros((SC_LANES,), F32) for _ in range(k)))(ubody)
            s = jnp.zeros((SC_LANES,), F32)
            for p in range(k):
                s = jnp.where(lane == p, jnp.sum(accs[p]), s)
            c = g_b[tt, pl.ds(hd * k, k)] * _gelu_tanh(s)
            cbb = []
            for p in range(k):
                cp = jnp.full((SC_LANES,), c[p])
                cbb.append(plsc.pack(cp, cp, format=SC_ILV))
            cv.wait()

            @plsc.parallel_loop(0, dw // SC_LANES, 1, unroll=2)
            def _(ch):
                sa = pl.ds(ch * SC_LANES, SC_LANES)
                sb = pl.ds(dw + ch * SC_LANES, SC_LANES)
                los, his = [], []
                for grp in range(k // 4):
                    lo, hi = sum4_unpack([cbb[grp * 4 + q]
                                          * words(vrows, b, grp * 4 + q, off=ch * SC_LANES)
                                          for q in range(4)])
                    los.append(lo)
                    his.append(hi)
                out_b[tt, sa] = out_b[tt, sa] + ((los[0] + los[1]) + (los[2] + los[3]))
                out_b[tt, sb] = out_b[tt, sb] + ((his[0] + his[1]) + (his[2] + his[3]))

        @pl.loop(0, tpw // tb)
        def _(blk):
            t0 = base + blk * tb
            pltpu.sync_copy(idx_hbm.at[pl.ds(t0, tb)], idx_b)
            pltpu.sync_copy(g_hbm.at[pl.ds(t0, tb)], g_b)
            pltpu.sync_copy(h_hbm.at[pl.ds(t0, tb)], h_b)
            fetch(0, 0)

            @pl.loop(0, tb)
            def _(tt):
                @pl.loop(0, d // SC_LANES)
                def _(j):
                    out_b[tt, pl.ds(j * SC_LANES, SC_LANES)] = jnp.zeros((SC_LANES,), F32)

            @pl.loop(0, n_items, step=2)
            def _(it):
                fetch(it + 1, 1)
                compute(it, 0)

                @pl.when(it + 2 < n_items)
                def _():
                    fetch(it + 2, 0)

                compute(it + 1, 1)

            pltpu.sync_copy(out_b, out_hbm.at[pl.ds(t0, tb)])

    return experts(h2, idx, g, u, v)


BATCH_CHUNKS = 4


def kernel(x, c, w_ada, b_ada, norm1_g, norm2_g, w_in, conv_w, conv_b, b_igate, b_fgate, lam_q1, lam_k1, lam_q2, lam_k2, diff_sub_g, mlstm_norm_g, w_out, peer_w_q, peer_sub_keys, peer_u, peer_v, rel_bias, final_g):
    bsz, s, d = x.shape
    mod = _ada(c, w_ada[0], b_ada[0]).reshape(bsz, 6, d)

    w = w_in[0]
    w_att = w[:, :3 * D_ATT].astype(BF16)
    w_m = w[:, 3 * D_ATT:3 * D_ATT + 4 * D_MLSTM].astype(BF16)
    w_gt = w[:, 3 * D_ATT + 4 * D_MLSTM:].T.astype(BF16)
    w_o = w_out[0].astype(BF16)
    w_q = peer_w_q[0].astype(BF16)
    sub_keys = peer_sub_keys[0].astype(BF16)
    u_packed = _pack_bf16_halves(peer_u[0])
    v_packed = _pack_bf16_halves(peer_v[0])
    bias_tiles = _relbias(rel_bias)
    lam4 = jnp.stack([lam_q1[0], lam_k1[0], lam_q2[0], lam_k2[0]])
    gate_bias = jnp.concatenate([b_igate[0], b_fgate[0]])

    nb = bsz // BATCH_CHUNKS
    outs = []
    for ci in range(BATCH_CHUNKS):
        b0 = ci * nb
        att_qkv, m_qkvo, gates = _inproj(x, mod, norm1_g[0], w_att, w_m, w_gt, b0, nb)
        att = _attn(att_qkv, bias_tiles, lam4, diff_sub_g[0])
        hm = _mlstm(m_qkvo, gates, gate_bias, conv_w[0], conv_b[0], mlstm_norm_g[0])
        x1, h2, qp = _outproj(att, hm, x, mod, norm2_g[0], w_o, w_q, b0)
        idx_t, g_t = _route(qp.reshape(nb * s, -1), sub_keys)
        peer_out = _experts(h2.reshape(nb * s, d // 2), idx_t.T, g_t.T, u_packed, v_packed)
        outs.append(_final(x1, peer_out.reshape(nb, s, d), mod, final_g, b0))
    return jnp.concatenate(outs, axis=0)
```

```python
import functools
import math

import numpy as np
import jax
import jax.numpy as jnp
from jax import lax
from jax.experimental import pallas as pl
from jax.experimental.pallas import tpu as pltpu
from jax.experimental.pallas import tpu_sc as plsc

F32 = jnp.float32
BF16 = jnp.bfloat16

ATT_HEADS = 4
ATT_QK_DIM = 64
ATT_V_DIM = 128
D_ATT = ATT_HEADS * ATT_V_DIM
M_HEADS = 4
M_DIM = 128
D_MLSTM = M_HEADS * M_DIM
CONV_W = 4
N_BUCKETS = 32
MAX_DIST = 128
N_KEYS = 128
PEER_HEADS = 8
PEER_TOPK = 16
PEER_QDIM = 256
EPS = 1e-6
LAMBDA_INIT = 0.8 - 0.6 * math.exp(-0.3 * 0)

ATT_BLOCK = 256
M_CHUNK = 128
ROW_TILE = 512
ROUTE_TILE = 512
LANES = 128
VMEM_LIMIT = 48 * 1024 * 1024

_NT = (((1,), (1,)), ((), ()))


def _rms_rows(x):
    return x * lax.rsqrt(jnp.mean(x * x, axis=-1, keepdims=True) + EPS)


def _sigmoid(x):
    return 1.0 / (1.0 + jnp.exp(-x))


def _pack_bf16_halves(x):
    bits = lax.bitcast_convert_type(x.astype(BF16).astype(F32), jnp.int32)
    n = x.shape[-1] // 2
    return lax.bitwise_or(lax.shift_right_logical(bits[..., :n], 16),
                          lax.bitwise_and(bits[..., n:], jnp.int32(-65536)))


def _ada_kernel(c_ref, w_ref, b_ref, o_ref):
    c = c_ref[...]
    cond = c * _sigmoid(c)
    o_ref[...] = jnp.dot(cond, w_ref[...], preferred_element_type=F32) + b_ref[...]


def _ada(c, w, b):
    bsz, d = c.shape
    n = w.shape[1]
    return pl.pallas_call(
        _ada_kernel,
        grid=(n // d,),
        in_specs=[pl.BlockSpec((bsz, d), lambda j: (0, 0)),
                  pl.BlockSpec((d, d), lambda j: (0, j)),
                  pl.BlockSpec((1, d), lambda j: (0, j))],
        out_specs=pl.BlockSpec((bsz, d), lambda j: (0, j)),
        out_shape=jax.ShapeDtypeStruct((bsz, n), F32),
        name="ada",
    )(c, w, b.reshape(1, n))


def _inproj_kernel(x_ref, mod_ref, g_ref, wa_ref, wm_ref, wg_ref, oa_ref, om_ref, og_ref):
    x = x_ref[0]
    mod = mod_ref[0]
    h = _rms_rows(x) * g_ref[...]
    h = h * (1.0 + mod[1:2, :]) + mod[0:1, :]
    hb = h.astype(BF16)
    oa_ref[0] = jnp.dot(hb, wa_ref[...], preferred_element_type=F32).astype(BF16)
    om_ref[0] = jnp.dot(hb, wm_ref[...], preferred_element_type=F32).astype(BF16)
    og_ref[0] = lax.dot_general(wg_ref[...], hb, _NT, preferred_element_type=F32)


def _inproj(x, mod, g, w_att, w_m, w_gt, b0, bsz):
    _, s, d = x.shape
    tm = min(ROW_TILE, s)
    na, nm, ng = w_att.shape[1], w_m.shape[1], w_gt.shape[0]
    return pl.pallas_call(
        _inproj_kernel,
        grid=(bsz, s // tm),
        in_specs=[pl.BlockSpec((1, tm, d), lambda b, i: (b + b0, i, 0)),
                  pl.BlockSpec((1, 6, d), lambda b, i: (b + b0, 0, 0)),
                  pl.BlockSpec((1, d), lambda b, i: (0, 0)),
                  pl.BlockSpec((d, na), lambda b, i: (0, 0)),
                  pl.BlockSpec((d, nm), lambda b, i: (0, 0)),
                  pl.BlockSpec((ng, d), lambda b, i: (0, 0))],
        out_specs=[pl.BlockSpec((1, tm, na), lambda b, i: (b, i, 0)),
                   pl.BlockSpec((1, tm, nm), lambda b, i: (b, i, 0)),
                   pl.BlockSpec((1, ng, tm), lambda b, i: (b, 0, i))],
        out_shape=[jax.ShapeDtypeStruct((bsz, s, na), BF16),
                   jax.ShapeDtypeStruct((bsz, s, nm), BF16),
                   jax.ShapeDtypeStruct((bsz, ng, s), F32)],
        compiler_params=pltpu.CompilerParams(vmem_limit_bytes=VMEM_LIMIT),
        name="inproj",
    )(x, mod, g.reshape(1, d), w_att, w_m, w_gt)


def _rel_buckets():
    n = np.arange(2 * ATT_BLOCK)
    max_exact = N_BUCKETS // 2
    nf = np.maximum(n, 1).astype(np.float64)
    large = max_exact + (np.log(nf / max_exact) / math.log(MAX_DIST / max_exact)
                         * (N_BUCKETS - max_exact)).astype(np.int64)
    large = np.minimum(large, N_BUCKETS - 1)
    bucket = np.where(n < max_exact, n, large)
    qk = np.arange(ATT_BLOCK)[:, None] - np.arange(ATT_BLOCK)[None, :]
    tiles = np.stack([bucket[np.maximum(qk, 0)], bucket[ATT_BLOCK + qk]])
    return tiles.astype(np.int32)


def _relbias_kernel(rb_ref, bk_ref, o_ref):
    h = pl.program_id(0)
    bk = bk_ref[...]
    acc = jnp.zeros(bk.shape, F32)
    for b in range(N_BUCKETS):
        acc = jnp.where(bk == b, rb_ref[b, h], acc)
    o_ref[0] = acc


def _relbias(rel_bias):
    tiles = jnp.asarray(_rel_buckets())
    return pl.pallas_call(
        _relbias_kernel,
        grid=(ATT_HEADS,),
        in_specs=[pl.BlockSpec(memory_space=pltpu.SMEM),
                  pl.BlockSpec((2, ATT_BLOCK, ATT_BLOCK), lambda h: (0, 0, 0))],
        out_specs=pl.BlockSpec((1, 2, ATT_BLOCK, ATT_BLOCK), lambda h: (h, 0, 0, 0)),
        out_shape=jax.ShapeDtypeStruct((ATT_HEADS, 2, ATT_BLOCK, ATT_BLOCK), F32),
        name="relbias",
    )(rel_bias, tiles)


def _attn_kernel(q_ref, k_ref, v_ref, bias_ref, lam_ref, subg_ref, o_ref,
                 qz_ref, m_ref, l_ref, acc_ref):
    tq = ATT_BLOCK
    qi = pl.program_id(2)
    scale = ATT_QK_DIM ** -0.5

    q = q_ref[0]
    lane = lax.broadcasted_iota(jnp.int32, q.shape, 1)
    zero = jnp.zeros_like(q)
    qz_ref[0:tq, :] = jnp.where(lane < ATT_QK_DIM, q, zero)
    qz_ref[tq:2 * tq, :] = jnp.where(lane >= ATT_QK_DIM, q, zero)
    m_ref[...] = jnp.full(m_ref.shape, -jnp.inf, F32)
    l_ref[...] = jnp.zeros(l_ref.shape, F32)
    acc_ref[...] = jnp.zeros(acc_ref.shape, F32)

    def step(j, bias, masked):
        start = pl.multiple_of(j * tq, tq)
        k = k_ref[0, pl.ds(start, tq), :]
        v = v_ref[0, pl.ds(start, tq), :]
        s = lax.dot_general(qz_ref[...], k, _NT, preferred_element_type=F32) * scale
        if isinstance(bias, tuple):
            s = s + jnp.concatenate([bias[0], bias[0]], axis=0)
        else:
            s = s + bias
        if masked:
            row = lax.broadcasted_iota(jnp.int32, (tq, tq), 0)
            col = lax.broadcasted_iota(jnp.int32, (tq, tq), 1)
            keep = jnp.concatenate([col <= row, col <= row], axis=0)
            s = jnp.where(keep, s, jnp.finfo(F32).min)
        m_old = m_ref[...]
        m_new = jnp.maximum(m_old, jnp.max(s, axis=-1, keepdims=True))
        alpha = jnp.exp(m_old - m_new)
        p = jnp.exp(s - m_new)
        l_ref[...] = alpha * l_ref[...] + jnp.sum(p, axis=-1, keepdims=True)
        acc_ref[...] = alpha * acc_ref[...] + jnp.dot(p.astype(BF16), v, preferred_element_type=F32)
        m_ref[...] = m_new

    far_bias = bias_ref[0, 1, tq - 1:tq, 0:1]

    def far_body(j, carry):
        step(j, far_bias, False)
        return carry

    lax.fori_loop(0, jnp.maximum(qi - 1, 0), far_body, 0)

    @pl.when(qi >= 1)
    def _():
        step(qi - 1, (bias_ref[0, 1],), False)

    step(qi, (bias_ref[0, 0],), True)

    lam = (jnp.exp(jnp.sum(lam_ref[0:1, :] * lam_ref[1:2, :], axis=-1, keepdims=True))
           - jnp.exp(jnp.sum(lam_ref[2:3, :] * lam_ref[3:4, :], axis=-1, keepdims=True))
           + LAMBDA_INIT)
    o = acc_ref[...] / l_ref[...]
    o = o[0:tq, :] - lam * o[tq:2 * tq, :]
    o = _rms_rows(o) * subg_ref[...] * (1.0 - LAMBDA_INIT)
    o_ref[0] = o.astype(o_ref.dtype)


def _attn(att_qkv, bias_tiles, lam4, sub_g):
    bsz, s, _ = att_qkv.shape
    tq = ATT_BLOCK
    nh = ATT_HEADS
    return pl.pallas_call(
        _attn_kernel,
        grid=(bsz, nh, s // tq),
        in_specs=[pl.BlockSpec((1, tq, ATT_V_DIM), lambda b, h, i: (b, i, h)),
                  pl.BlockSpec((1, s, ATT_V_DIM), lambda b, h, i: (b, 0, nh + h)),
                  pl.BlockSpec((1, s, ATT_V_DIM), lambda b, h, i: (b, 0, 2 * nh + h)),
                  pl.BlockSpec((1, 2, tq, tq), lambda b, h, i: (h, 0, 0, 0)),
                  pl.BlockSpec((4, ATT_QK_DIM), lambda b, h, i: (0, 0)),
                  pl.BlockSpec((1, ATT_V_DIM), lambda b, h, i: (0, 0))],
        out_specs=pl.BlockSpec((1, tq, ATT_V_DIM), lambda b, h, i: (b, i, h)),
        out_shape=jax.ShapeDtypeStruct((bsz, s, D_ATT), BF16),
        scratch_shapes=[pltpu.VMEM((2 * tq, ATT_V_DIM), BF16),
                        pltpu.VMEM((2 * tq, 1), F32),
                        pltpu.VMEM((2 * tq, 1), F32),
                        pltpu.VMEM((2 * tq, ATT_V_DIM), F32)],
        compiler_params=pltpu.CompilerParams(vmem_limit_bytes=VMEM_LIMIT),
        name="attn",
    )(att_qkv, att_qkv, att_qkv, bias_tiles, lam4, sub_g.reshape(1, ATT_V_DIM))


def _mlstm_kernel(q_ref, k_ref, v_ref, o_ref, gi_ref, gf_ref, bias_ref, cwq_ref, cwk_ref,
                  cbq_ref, cbk_ref, ng_ref, out_ref, qs_ref, ks_ref, b_ref, ig_ref):
    s = q_ref.shape[1]
    L = M_CHUNK
    nc = s // L
    h = pl.program_id(1)

    row = lax.broadcasted_iota(jnp.int32, (s, M_DIM), 0)

    def conv_silu(x_ref, w_ref, cb_ref):
        x = x_ref[0].astype(F32)
        w = w_ref[...]
        out = None
        for j in range(CONV_W):
            shift = CONV_W - 1 - j
            xs = x if shift == 0 else jnp.where(row >= shift, pltpu.roll(x, shift, 0), 0.0)
            term = xs * w[j:j + 1, :]
            out = term if out is None else out + term
        out = out + cb_ref[...]
        return out * _sigmoid(out)

    qs_ref[...] = conv_silu(q_ref, cwq_ref, cbq_ref).astype(BF16)
    ks_ref[...] = (conv_silu(k_ref, cwk_ref, cbk_ref) * (M_DIM ** -0.5)).astype(BF16)

    ig = gi_ref[0, 0] + bias_ref[h]
    f = gf_ref[0, 0] + bias_ref[M_HEADS + h]
    logf = jnp.minimum(f, 0.0) - jnp.log(1.0 + jnp.exp(-jnp.abs(f)))
    r = lax.broadcasted_iota(jnp.int32, (L, L), 0)
    c = lax.broadcasted_iota(jnp.int32, (L, L), 1)
    tri = (r <= c).astype(F32)
    b_ref[...] = jnp.dot(logf, tri, preferred_element_type=F32,
                         precision=lax.Precision.HIGHEST)
    ig_ref[...] = ig
    eye = r == c
    causal = c <= r

    def to_col(x_row):
        return jnp.sum(jnp.where(eye, x_row, 0.0), axis=1, keepdims=True)

    def chunk(ci, carry):
        C, n, m = carry
        start = pl.multiple_of(ci * L, L)
        qc = qs_ref[pl.ds(start, L), :]
        kc = ks_ref[pl.ds(start, L), :]
        vc = v_ref[0, pl.ds(start, L), :]
        b_r = b_ref[pl.ds(ci, 1), :]
        ig_r = ig_ref[pl.ds(ci, 1), :]
        b_last = b_r[:, L - 1:L]
        a_r = b_last - b_r + ig_r
        b_c = to_col(b_r)
        a_c = to_col(a_r)

        logd = jnp.where(causal, b_c - b_r + ig_r, -jnp.inf)
        m_inter = b_c + m
        m_j = jnp.maximum(jnp.max(logd, axis=1, keepdims=True), m_inter)
        w = jnp.exp(logd - m_j)
        sqk = lax.dot_general(qc, kc, _NT, preferred_element_type=F32) * w
        inter = jnp.exp(m_inter - m_j)
        num = (jnp.dot(sqk.astype(BF16), vc, preferred_element_type=F32)
               + inter * jnp.dot(qc, C.astype(BF16), preferred_element_type=F32))
        den = (jnp.sum(sqk, axis=1, keepdims=True)
               + inter * jnp.sum(qc.astype(F32) * n, axis=1, keepdims=True))
        hc = num / jnp.maximum(jnp.abs(den), jnp.exp(-m_j))

        og = _sigmoid(o_ref[0, pl.ds(start, L), :].astype(F32))
        out_ref[0, pl.ds(start, L), :] = (_rms_rows(og * hc) * ng_ref[...]).astype(out_ref.dtype)

        m_new = jnp.maximum(b_last + m, jnp.max(a_r, axis=1, keepdims=True))
        decay = jnp.exp(b_last + m - m_new)
        kw = kc.astype(F32) * jnp.exp(a_c - m_new)
        C_new = decay * C + jnp.dot(kw.T.astype(BF16), vc, preferred_element_type=F32)
        n_new = decay * n + jnp.sum(kw, axis=0, keepdims=True)
        return C_new, n_new, m_new

    init = (jnp.zeros((M_DIM, M_DIM), F32), jnp.zeros((1, M_DIM), F32), jnp.zeros((1, 1), F32))
    lax.fori_loop(0, nc, chunk, init)


def _mlstm(m_qkvo, gates, gate_bias, conv_w, conv_b, norm_g):
    bsz, s, _ = m_qkvo.shape
    L = M_CHUNK
    nc = s // L
    nh = M_HEADS
    d = M_DIM
    g4 = gates.reshape(bsz, 2 * nh, nc, L)
    seq = lambda off: pl.BlockSpec((1, s, d), lambda b, h: (b, 0, off + h))
    return pl.pallas_call(
        _mlstm_kernel,
        grid=(bsz, nh),
        in_specs=[seq(0), seq(nh), seq(2 * nh), seq(3 * nh),
                  pl.BlockSpec((1, 1, nc, L), lambda b, h: (b, h, 0, 0)),
                  pl.BlockSpec((1, 1, nc, L), lambda b, h: (b, nh + h, 0, 0)),
                  pl.BlockSpec(memory_space=pltpu.SMEM),
                  pl.BlockSpec((CONV_W, d), lambda b, h: (0, h)),
                  pl.BlockSpec((CONV_W, d), lambda b, h: (0, nh + h)),
                  pl.BlockSpec((1, d), lambda b, h: (0, h)),
                  pl.BlockSpec((1, d), lambda b, h: (0, nh + h)),
                  pl.BlockSpec((1, d), lambda b, h: (0, h))],
        out_specs=pl.BlockSpec((1, s, d), lambda b, h: (b, 0, h)),
        out_shape=jax.ShapeDtypeStruct((bsz, s, D_MLSTM), BF16),
        scratch_shapes=[pltpu.VMEM((s, d), BF16), pltpu.VMEM((s, d), BF16),
                        pltpu.VMEM((nc, L), F32), pltpu.VMEM((nc, L), F32)],
        compiler_params=pltpu.CompilerParams(vmem_limit_bytes=VMEM_LIMIT),
        name="mlstm",
    )(m_qkvo, m_qkvo, m_qkvo, m_qkvo, g4, g4, gate_bias, conv_w, conv_w,
      conv_b.reshape(1, -1), conv_b.reshape(1, -1), norm_g.reshape(1, -1))


def _outproj_kernel(att_ref, hm_ref, x_ref, mod_ref, g2_ref, wo_ref, wq_ref,
                    x1_ref, h2_ref, qp_ref):
    mod = mod_ref[0]
    y = (jnp.dot(att_ref[0], wo_ref[0:D_ATT, :], preferred_element_type=F32)
         + jnp.dot(hm_ref[0], wo_ref[D_ATT:, :], preferred_element_type=F32))
    x1 = x_ref[0] + mod[2:3, :] * y
    x1_ref[0] = x1
    h2 = _rms_rows(x1) * g2_ref[...]
    h2 = h2 * (1.0 + mod[4:5, :]) + mod[3:4, :]
    hb = h2.astype(BF16)
    h2_ref[0] = _pack_bf16_halves(hb)
    qp_ref[0] = jnp.dot(hb, wq_ref[...], preferred_element_type=F32).astype(BF16)


def _outproj(att, hm, x, mod, g2, w_out, w_q, b0):
    bsz, s, _ = att.shape
    d = x.shape[-1]
    tm = min(ROW_TILE, s)
    nq = w_q.shape[1]
    tile = lambda n: pl.BlockSpec((1, tm, n), lambda b, i: (b, i, 0))
    return pl.pallas_call(
        _outproj_kernel,
        grid=(bsz, s // tm),
        in_specs=[tile(D_ATT), tile(D_MLSTM),
                  pl.BlockSpec((1, tm, d), lambda b, i: (b + b0, i, 0)),
                  pl.BlockSpec((1, 6, d), lambda b, i: (b + b0, 0, 0)),
                  pl.BlockSpec((1, d), lambda b, i: (0, 0)),
                  pl.BlockSpec((d, d), lambda b, i: (0, 0)),
                  pl.BlockSpec((d, nq), lambda b, i: (0, 0))],
        out_specs=[tile(d), tile(d // 2), tile(nq)],
        out_shape=[jax.ShapeDtypeStruct((bsz, s, d), F32),
                   jax.ShapeDtypeStruct((bsz, s, d // 2), jnp.int32),
                   jax.ShapeDtypeStruct((bsz, s, nq), BF16)],
        compiler_params=pltpu.CompilerParams(vmem_limit_bytes=VMEM_LIMIT),
        name="outproj",
    )(att, hm, x, mod, g2.reshape(1, d), w_out, w_q)


def _top16_rows(blocks, ids):
    big = jnp.int32(1 << 30)
    vals, pos = [], []
    for _ in range(PEER_TOPK):
        m = functools.reduce(jnp.maximum, blocks)
        m = jnp.max(m, axis=0, keepdims=True)
        cand = functools.reduce(jnp.minimum,
                                [jnp.where(b == m, i, big) for b, i in zip(blocks, ids)])
        p = jnp.min(cand, axis=0, keepdims=True)
        blocks = [jnp.where(i == p, -jnp.inf, b) for b, i in zip(blocks, ids)]
        vals.append(m)
        pos.append(p)
    return jnp.concatenate(vals, axis=0), jnp.concatenate(pos, axis=0)


def _pick_rows(table, sel):
    out = jnp.zeros(sel.shape, table.dtype)
    for r in range(PEER_TOPK):
        out = jnp.where(sel == r, table[r:r + 1, :], out)
    return out


def _route_kernel(q_ref, keys_ref, idx_ref, g_ref):
    k = PEER_TOPK
    half = PEER_QDIM // 2
    key_id = lax.broadcasted_iota(jnp.int32, (N_KEYS, LANES), 0)
    sub_id = lax.broadcasted_iota(jnp.int32, (k, LANES), 0)
    for t in range(q_ref.shape[0] // LANES):
        rows = pl.ds(t * LANES, LANES)
        sv, si = [], []
        for p in range(2):
            qh = q_ref[rows, p * half:(p + 1) * half]
            s = lax.dot_general(keys_ref[0, p], qh, _NT, preferred_element_type=F32)
            v, i = _top16_rows([s], [key_id])
            sv.append(v)
            si.append(i)
        blocks = [sv[0][i:i + 1, :] + sv[1] for i in range(k)]
        ids = [sub_id + i * k for i in range(k)]
        top_s, pos = _top16_rows(blocks, ids)
        idx = (_pick_rows(si[0], lax.shift_right_logical(pos, 4)) * N_KEYS
               + _pick_rows(si[1], lax.bitwise_and(pos, k - 1)))
        e = jnp.exp(top_s - top_s[0:1, :])
        idx_ref[:, t * LANES:(t + 1) * LANES] = idx
        g_ref[:, t * LANES:(t + 1) * LANES] = e / jnp.sum(e, axis=0, keepdims=True)


def _route(qp, sub_keys):
    t, _ = qp.shape
    tt = min(ROUTE_TILE, t)
    k = PEER_TOPK
    return pl.pallas_call(
        _route_kernel,
        grid=(t // tt, PEER_HEADS),
        in_specs=[pl.BlockSpec((tt, PEER_QDIM), lambda i, h: (i, h)),
                  pl.BlockSpec((1, 2, N_KEYS, PEER_QDIM // 2), lambda i, h: (h, 0, 0, 0))],
        out_specs=[pl.BlockSpec((k, tt), lambda i, h: (h, i)),
                   pl.BlockSpec((k, tt), lambda i, h: (h, i))],
        out_shape=[jax.ShapeDtypeStruct((PEER_HEADS * k, t), jnp.int32),
                   jax.ShapeDtypeStruct((PEER_HEADS * k, t), F32)],
        name="route",
    )(qp, sub_keys)


def _final_kernel(x1_ref, po_ref, mod_ref, g_ref, o_ref):
    x2 = x1_ref[0] + mod_ref[0][5:6, :] * po_ref[0]
    o_ref[0] = _rms_rows(x2) * g_ref[...]


def _final_kernel_into(x1_ref, po_ref, mod_ref, g_ref, prev_ref, o_ref):
    del prev_ref
    _final_kernel(x1_ref, po_ref, mod_ref, g_ref, o_ref)


def _final(x1, peer_out, mod, final_g, b0, out):
    bsz, s, d = x1.shape
    tm = min(ROW_TILE, s)
    tile = pl.BlockSpec((1, tm, d), lambda b, i: (b, i, 0))
    in_specs = [tile, tile, pl.BlockSpec((1, 6, d), lambda b, i: (b + b0, 0, 0)),
                pl.BlockSpec((1, d), lambda b, i: (0, 0))]
    args = (x1, peer_out, mod, final_g.reshape(1, d))
    if out is not None:
        in_specs.append(pl.BlockSpec(memory_space=pl.ANY))
        args += (out,)
    return pl.pallas_call(
        _final_kernel if out is None else _final_kernel_into,
        grid=(bsz, s // tm),
        in_specs=in_specs,
        out_specs=pl.BlockSpec((1, tm, d), lambda b, i: (b + b0, i, 0)),
        out_shape=jax.ShapeDtypeStruct((mod.shape[0], s, d), F32),
        input_output_aliases={} if out is None else {4: 0},
        name="final",
    )(*args)


GELU_C = math.sqrt(2.0 / math.pi)
SC_LANES = 16
SC_WORKERS = 32
SC_CORES = 2
SC_TOKEN_BLOCK = 8
SC_ILV = plsc.PackFormat.INTERLEAVED
N_PAIRS = PEER_HEADS * PEER_TOPK


def _gelu_tanh(x):
    z = GELU_C * (x + 0.044715 * (x * x * x))
    t = 1.0 - 2.0 / (jnp.exp(2.0 * z) + 1.0)
    return x * (0.5 * (1.0 + t))


def _experts(h2, idx, g, u, v):
    t_total, dw = h2.shape
    d = 2 * dw
    tpw = t_total // SC_WORKERS
    tb = SC_TOKEN_BLOCK
    k = PEER_TOPK
    nh = PEER_HEADS
    n_items = tb * nh
    mesh = plsc.VectorSubcoreMesh(core_axis_name="c", subcore_axis_name="s")

    @functools.partial(
        pl.kernel, mesh=mesh,
        out_type=jax.ShapeDtypeStruct((t_total, d), F32),
        scratch_types=[
            pltpu.VMEM((tb, N_PAIRS), jnp.int32),
            pltpu.VMEM((tb, N_PAIRS), F32),
            pltpu.VMEM((tb, dw), jnp.int32),
            pltpu.VMEM((tb, d), F32),
            pltpu.VMEM((2, k, dw), jnp.int32),
            pltpu.VMEM((2, k, dw), jnp.int32),
            pltpu.SemaphoreType.DMA((2,)),
            pltpu.SemaphoreType.DMA((2,)),
        ],
        compiler_params=pltpu.CompilerParams(needs_layout_passes=False),
        name="experts",
    )
    def experts(h_hbm, idx_hbm, g_hbm, u_hbm, v_hbm, out_hbm,
                idx_b, g_b, h_b, out_b, urows, vrows, usem, vsem):
        wid = lax.axis_index("s") * SC_CORES + lax.axis_index("c")
        base = wid * tpw
        lane = lax.iota(jnp.int32, SC_LANES)

        def split(item):
            return lax.shift_right_logical(item, 3), lax.bitwise_and(item, nh - 1)

        def copies(item, b):
            tt, hd = split(item)
            ids = idx_b.at[tt, pl.ds(hd * k, k)]
            return (pltpu.make_async_copy(u_hbm.at[ids], urows.at[b], usem.at[b]),
                    pltpu.make_async_copy(v_hbm.at[ids], vrows.at[b], vsem.at[b]))

        def fetch(item, b):
            cu, cv = copies(item, b)
            cu.start()
            cv.start()

        def words(ref, *lead, off):
            return plsc.bitcast(ref[(*lead, pl.ds(off, SC_LANES))], BF16)

        def sum4_unpack(pr):
            return plsc.unpack((pr[0] + pr[1]) + (pr[2] + pr[3]), format=SC_ILV)

        def compute(item, b):
            tt, hd = split(item)
            cu, cv = copies(item, b)
            cu.wait()

            def ubody(j, accs):
                hs = [words(h_b, tt, off=(j * 4 + q) * SC_LANES) for q in range(4)]
                new = []
                for p, a in enumerate(accs):
                    lo, hi = sum4_unpack([words(urows, b, p, off=(j * 4 + q) * SC_LANES) * hs[q]
                                          for q in range(4)])
                    new.append(a + (lo + hi))
                return tuple(new)

            accs = plsc.parallel_loop(
                0, dw // (4 * SC_LANES), 1,
                carry=tuple(jnp.zeros((SC_LANES,), F32) for _ in range(k)))(ubody)
            s = jnp.zeros((SC_LANES,), F32)
            for p in range(k):
                s = jnp.where(lane == p, jnp.sum(accs[p]), s)
            c = g_b[tt, pl.ds(hd * k, k)] * _gelu_tanh(s)
            cbb = []
            for p in range(k):
                cp = jnp.full((SC_LANES,), c[p])
                cbb.append(plsc.pack(cp, cp, format=SC_ILV))
            cv.wait()

            @plsc.parallel_loop(0, dw // SC_LANES, 1, unroll=2)
            def _(ch):
                sa = pl.ds(ch * SC_LANES, SC_LANES)
                sb = pl.ds(dw + ch * SC_LANES, SC_LANES)
                los, his = [], []
                for grp in range(k // 4):
                    lo, hi = sum4_unpack([cbb[grp * 4 + q]
                                          * words(vrows, b, grp * 4 + q, off=ch * SC_LANES)
                                          for q in range(4)])
                    los.append(lo)
                    his.append(hi)
                out_b[tt, sa] = out_b[tt, sa] + ((los[0] + los[1]) + (los[2] + los[3]))
                out_b[tt, sb] = out_b[tt, sb] + ((his[0] + his[1]) + (his[2] + his[3]))

        @pl.loop(0, tpw // tb)
        def _(blk):
            t0 = base + blk * tb
            pltpu.sync_copy(idx_hbm.at[pl.ds(t0, tb)], idx_b)
            pltpu.sync_copy(g_hbm.at[pl.ds(t0, tb)], g_b)
            pltpu.sync_copy(h_hbm.at[pl.ds(t0, tb)], h_b)
            fetch(0, 0)

            @pl.loop(0, tb)
            def _(tt):
                @pl.loop(0, d // SC_LANES)
                def _(j):
                    out_b[tt, pl.ds(j * SC_LANES, SC_LANES)] = jnp.zeros((SC_LANES,), F32)

            @pl.loop(0, n_items, step=2)
            def _(it):
                fetch(it + 1, 1)
                compute(it, 0)

                @pl.when(it + 2 < n_items)
                def _():
                    fetch(it + 2, 0)

                compute(it + 1, 1)

            pltpu.sync_copy(out_b, out_hbm.at[pl.ds(t0, tb)])

    return experts(h2, idx, g, u, v)


BATCH_CHUNKS = 16


def kernel(x, c, w_ada, b_ada, norm1_g, norm2_g, w_in, conv_w, conv_b, b_igate, b_fgate, lam_q1, lam_k1, lam_q2, lam_k2, diff_sub_g, mlstm_norm_g, w_out, peer_w_q, peer_sub_keys, peer_u, peer_v, rel_bias, final_g):
    bsz, s, d = x.shape
    mod = _ada(c, w_ada[0], b_ada[0]).reshape(bsz, 6, d)

    w = w_in[0]
    w_att = w[:, :3 * D_ATT].astype(BF16)
    w_m = w[:, 3 * D_ATT:3 * D_ATT + 4 * D_MLSTM].astype(BF16)
    w_gt = w[:, 3 * D_ATT + 4 * D_MLSTM:].T.astype(BF16)
    w_o = w_out[0].astype(BF16)
    w_q = peer_w_q[0].astype(BF16)
    sub_keys = peer_sub_keys[0].astype(BF16)
    u_packed = _pack_bf16_halves(peer_u[0])
    v_packed = _pack_bf16_halves(peer_v[0])
    bias_tiles = _relbias(rel_bias)
    lam4 = jnp.stack([lam_q1[0], lam_k1[0], lam_q2[0], lam_k2[0]])
    gate_bias = jnp.concatenate([b_igate[0], b_fgate[0]])

    nb = bsz // BATCH_CHUNKS
    out = None
    for ci in range(BATCH_CHUNKS):
        b0 = ci * nb
        att_qkv, m_qkvo, gates = _inproj(x, mod, norm1_g[0], w_att, w_m, w_gt, b0, nb)
        att = _attn(att_qkv, bias_tiles, lam4, diff_sub_g[0])
        hm = _mlstm(m_qkvo, gates, gate_bias, conv_w[0], conv_b[0], mlstm_norm_g[0])
        x1, h2, qp = _outproj(att, hm, x, mod, norm2_g[0], w_o, w_q, b0)
        idx_t, g_t = _route(qp.reshape(nb * s, -1), sub_keys)
        peer_out = _experts(h2.reshape(nb * s, d // 2), idx_t.T, g_t.T, u_packed, v_packed)
        out = _final(x1, peer_out.reshape(nb, s, d), mod, final_g, b0, out)
    return out
```

```python
import functools
import math

import numpy as np
import jax
import jax.numpy as jnp
from jax import lax
from jax.experimental import pallas as pl
from jax.experimental.pallas import tpu as pltpu
from jax.experimental.pallas import tpu_sc as plsc

F32 = jnp.float32
BF16 = jnp.bfloat16

ATT_HEADS = 4
ATT_QK_DIM = 64
ATT_V_DIM = 128
D_ATT = ATT_HEADS * ATT_V_DIM
M_HEADS = 4
M_DIM = 128
D_MLSTM = M_HEADS * M_DIM
CONV_W = 4
N_BUCKETS = 32
MAX_DIST = 128
N_KEYS = 128
PEER_HEADS = 8
PEER_TOPK = 16
PEER_QDIM = 256
EPS = 1e-6
LAMBDA_INIT = 0.8 - 0.6 * math.exp(-0.3 * 0)

ATT_BLOCK = 256
M_CHUNK = 128
ROW_TILE = 512
ROUTE_TILE = 512
LANES = 128
VMEM_LIMIT = 48 * 1024 * 1024

_NT = (((1,), (1,)), ((), ()))


def _rms_rows(x):
    return x * lax.rsqrt(jnp.mean(x * x, axis=-1, keepdims=True) + EPS)


def _sigmoid(x):
    return 1.0 / (1.0 + jnp.exp(-x))


def _pack_bf16_halves(x):
    bits = lax.bitcast_convert_type(x.astype(BF16).astype(F32), jnp.int32)
    n = x.shape[-1] // 2
    return lax.bitwise_or(lax.shift_right_logical(bits[..., :n], 16),
                          lax.bitwise_and(bits[..., n:], jnp.int32(-65536)))


def _ada_kernel(c_ref, w_ref, b_ref, o_ref):
    c = c_ref[...]
    cond = c * _sigmoid(c)
    o_ref[...] = jnp.dot(cond, w_ref[...], preferred_element_type=F32) + b_ref[...]


def _ada(c, w, b):
    bsz, d = c.shape
    n = w.shape[1]
    return pl.pallas_call(
        _ada_kernel,
        grid=(n // d,),
        in_specs=[pl.BlockSpec((bsz, d), lambda j: (0, 0)),
                  pl.BlockSpec((d, d), lambda j: (0, j)),
                  pl.BlockSpec((1, d), lambda j: (0, j))],
        out_specs=pl.BlockSpec((bsz, d), lambda j: (0, j)),
        out_shape=jax.ShapeDtypeStruct((bsz, n), F32),
        name="ada",
    )(c, w, b.reshape(1, n))


def _inproj_kernel(x_ref, mod_ref, g_ref, wa_ref, wm_ref, wg_ref, oa_ref, om_ref, og_ref):
    x = x_ref[0]
    mod = mod_ref[0]
    h = _rms_rows(x) * g_ref[...]
    h = h * (1.0 + mod[1:2, :]) + mod[0:1, :]
    hb = h.astype(BF16)
    oa_ref[0] = jnp.dot(hb, wa_ref[...], preferred_element_type=F32).astype(BF16)
    om_ref[0] = jnp.dot(hb, wm_ref[...], preferred_element_type=F32).astype(BF16)
    og_ref[0] = lax.dot_general(wg_ref[...], hb, _NT, preferred_element_type=F32)


def _inproj(x, mod, g, w_att, w_m, w_gt, b0, bsz):
    _, s, d = x.shape
    tm = min(ROW_TILE, s)
    na, nm, ng = w_att.shape[1], w_m.shape[1], w_gt.shape[0]
    return pl.pallas_call(
        _inproj_kernel,
        grid=(bsz, s // tm),
        in_specs=[pl.BlockSpec((1, tm, d), lambda b, i: (b + b0, i, 0)),
                  pl.BlockSpec((1, 6, d), lambda b, i: (b + b0, 0, 0)),
                  pl.BlockSpec((1, d), lambda b, i: (0, 0)),
                  pl.BlockSpec((d, na), lambda b, i: (0, 0)),
                  pl.BlockSpec((d, nm), lambda b, i: (0, 0)),
                  pl.BlockSpec((ng, d), lambda b, i: (0, 0))],
        out_specs=[pl.BlockSpec((1, tm, na), lambda b, i: (b, i, 0)),
                   pl.BlockSpec((1, tm, nm), lambda b, i: (b, i, 0)),
                   pl.BlockSpec((1, ng, tm), lambda b, i: (b, 0, i))],
        out_shape=[jax.ShapeDtypeStruct((bsz, s, na), BF16),
                   jax.ShapeDtypeStruct((bsz, s, nm), BF16),
                   jax.ShapeDtypeStruct((bsz, ng, s), F32)],
        compiler_params=pltpu.CompilerParams(vmem_limit_bytes=VMEM_LIMIT),
        name="inproj",
    )(x, mod, g.reshape(1, d), w_att, w_m, w_gt)


def _rel_buckets():
    n = np.arange(2 * ATT_BLOCK)
    max_exact = N_BUCKETS // 2
    nf = np.maximum(n, 1).astype(np.float64)
    large = max_exact + (np.log(nf / max_exact) / math.log(MAX_DIST / max_exact)
                         * (N_BUCKETS - max_exact)).astype(np.int64)
    large = np.minimum(large, N_BUCKETS - 1)
    bucket = np.where(n < max_exact, n, large)
    qk = np.arange(ATT_BLOCK)[:, None] - np.arange(ATT_BLOCK)[None, :]
    tiles = np.stack([bucket[np.maximum(qk, 0)], bucket[ATT_BLOCK + qk]])
    return tiles.astype(np.int32)


def _relbias_kernel(rb_ref, bk_ref, o_ref):
    h = pl.program_id(0)
    bk = bk_ref[...]
    acc = jnp.zeros(bk.shape, F32)
    for b in range(N_BUCKETS):
        acc = jnp.where(bk == b, rb_ref[b, h], acc)
    o_ref[0] = acc


def _relbias(rel_bias):
    tiles = jnp.asarray(_rel_buckets())
    return pl.pallas_call(
        _relbias_kernel,
        grid=(ATT_HEADS,),
        in_specs=[pl.BlockSpec(memory_space=pltpu.SMEM),
                  pl.BlockSpec((2, ATT_BLOCK, ATT_BLOCK), lambda h: (0, 0, 0))],
        out_specs=pl.BlockSpec((1, 2, ATT_BLOCK, ATT_BLOCK), lambda h: (h, 0, 0, 0)),
        out_shape=jax.ShapeDtypeStruct((ATT_HEADS, 2, ATT_BLOCK, ATT_BLOCK), F32),
        name="relbias",
    )(rel_bias, tiles)


def _attn_kernel(q_ref, k_ref, v_ref, bias_ref, lam_ref, subg_ref, o_ref,
                 qz_ref, m_ref, l_ref, acc_ref):
    tq = ATT_BLOCK
    qi = pl.program_id(2)
    scale = ATT_QK_DIM ** -0.5

    q = q_ref[0]
    lane = lax.broadcasted_iota(jnp.int32, q.shape, 1)
    zero = jnp.zeros_like(q)
    qz_ref[0:tq, :] = jnp.where(lane < ATT_QK_DIM, q, zero)
    qz_ref[tq:2 * tq, :] = jnp.where(lane >= ATT_QK_DIM, q, zero)
    m_ref[...] = jnp.full(m_ref.shape, -jnp.inf, F32)
    l_ref[...] = jnp.zeros(l_ref.shape, F32)
    acc_ref[...] = jnp.zeros(acc_ref.shape, F32)

    def step(j, bias, masked):
        start = pl.multiple_of(j * tq, tq)
        k = k_ref[0, pl.ds(start, tq), :]
        v = v_ref[0, pl.ds(start, tq), :]
        s = lax.dot_general(qz_ref[...], k, _NT, preferred_element_type=F32) * scale
        if isinstance(bias, tuple):
            s = s + jnp.concatenate([bias[0], bias[0]], axis=0)
        else:
            s = s + bias
        if masked:
            row = lax.broadcasted_iota(jnp.int32, (tq, tq), 0)
            col = lax.broadcasted_iota(jnp.int32, (tq, tq), 1)
            keep = jnp.concatenate([col <= row, col <= row], axis=0)
            s = jnp.where(keep, s, jnp.finfo(F32).min)
        m_old = m_ref[...]
        m_new = jnp.maximum(m_old, jnp.max(s, axis=-1, keepdims=True))
        alpha = jnp.exp(m_old - m_new)
        p = jnp.exp(s - m_new)
        l_ref[...] = alpha * l_ref[...] + jnp.sum(p, axis=-1, keepdims=True)
        acc_ref[...] = alpha * acc_ref[...] + jnp.dot(p.astype(BF16), v, preferred_element_type=F32)
        m_ref[...] = m_new

    far_bias = bias_ref[0, 1, tq - 1:tq, 0:1]

    def far_body(j, carry):
        step(j, far_bias, False)
        return carry

    lax.fori_loop(0, jnp.maximum(qi - 1, 0), far_body, 0)

    @pl.when(qi >= 1)
    def _():
        step(qi - 1, (bias_ref[0, 1],), False)

    step(qi, (bias_ref[0, 0],), True)

    lam = (jnp.exp(jnp.sum(lam_ref[0:1, :] * lam_ref[1:2, :], axis=-1, keepdims=True))
           - jnp.exp(jnp.sum(lam_ref[2:3, :] * lam_ref[3:4, :], axis=-1, keepdims=True))
           + LAMBDA_INIT)
    o = acc_ref[...] / l_ref[...]
    o = o[0:tq, :] - lam * o[tq:2 * tq, :]
    o = _rms_rows(o) * subg_ref[...] * (1.0 - LAMBDA_INIT)
    o_ref[0] = o.astype(o_ref.dtype)


def _attn(att_qkv, bias_tiles, lam4, sub_g):
    bsz, s, _ = att_qkv.shape
    tq = ATT_BLOCK
    nh = ATT_HEADS
    return pl.pallas_call(
        _attn_kernel,
        grid=(bsz, nh, s // tq),
        in_specs=[pl.BlockSpec((1, tq, ATT_V_DIM), lambda b, h, i: (b, i, h)),
                  pl.BlockSpec((1, s, ATT_V_DIM), lambda b, h, i: (b, 0, nh + h)),
                  pl.BlockSpec((1, s, ATT_V_DIM), lambda b, h, i: (b, 0, 2 * nh + h)),
                  pl.BlockSpec((1, 2, tq, tq), lambda b, h, i: (h, 0, 0, 0)),
                  pl.BlockSpec((4, ATT_QK_DIM), lambda b, h, i: (0, 0)),
                  pl.BlockSpec((1, ATT_V_DIM), lambda b, h, i: (0, 0))],
        out_specs=pl.BlockSpec((1, tq, ATT_V_DIM), lambda b, h, i: (b, i, h)),
        out_shape=jax.ShapeDtypeStruct((bsz, s, D_ATT), BF16),
        scratch_shapes=[pltpu.VMEM((2 * tq, ATT_V_DIM), BF16),
                        pltpu.VMEM((2 * tq, 1), F32),
                        pltpu.VMEM((2 * tq, 1), F32),
                        pltpu.VMEM((2 * tq, ATT_V_DIM), F32)],
        compiler_params=pltpu.CompilerParams(vmem_limit_bytes=VMEM_LIMIT),
        name="attn",
    )(att_qkv, att_qkv, att_qkv, bias_tiles, lam4, sub_g.reshape(1, ATT_V_DIM))


def _mlstm_kernel(q_ref, k_ref, v_ref, o_ref, gi_ref, gf_ref, bias_ref, cwq_ref, cwk_ref,
                  cbq_ref, cbk_ref, ng_ref, out_ref, qs_ref, ks_ref, b_ref, ig_ref):
    s = q_ref.shape[1]
    L = M_CHUNK
    nc = s // L
    h = pl.program_id(1)

    row = lax.broadcasted_iota(jnp.int32, (s, M_DIM), 0)

    def conv_silu(x_ref, w_ref, cb_ref):
        x = x_ref[0].astype(F32)
        w = w_ref[...]
        out = None
        for j in range(CONV_W):
            shift = CONV_W - 1 - j
            xs = x if shift == 0 else jnp.where(row >= shift, pltpu.roll(x, shift, 0), 0.0)
            term = xs * w[j:j + 1, :]
            out = term if out is None else out + term
        out = out + cb_ref[...]
        return out * _sigmoid(out)

    qs_ref[...] = conv_silu(q_ref, cwq_ref, cbq_ref).astype(BF16)
    ks_ref[...] = (conv_silu(k_ref, cwk_ref, cbk_ref) * (M_DIM ** -0.5)).astype(BF16)

    ig = gi_ref[0, 0] + bias_ref[h]
    f = gf_ref[0, 0] + bias_ref[M_HEADS + h]
    logf = jnp.minimum(f, 0.0) - jnp.log(1.0 + jnp.exp(-jnp.abs(f)))
    r = lax.broadcasted_iota(jnp.int32, (L, L), 0)
    c = lax.broadcasted_iota(jnp.int32, (L, L), 1)
    tri = (r <= c).astype(F32)
    b_ref[...] = jnp.dot(logf, tri, preferred_element_type=F32,
                         precision=lax.Precision.HIGHEST)
    ig_ref[...] = ig
    eye = r == c
    causal = c <= r

    def to_col(x_row):
        return jnp.sum(jnp.where(eye, x_row, 0.0), axis=1, keepdims=True)

    def chunk(ci, carry):
        C, n, m = carry
        start = pl.multiple_of(ci * L, L)
        qc = qs_ref[pl.ds(start, L), :]
        kc = ks_ref[pl.ds(start, L), :]
        vc = v_ref[0, pl.ds(start, L), :]
        b_r = b_ref[pl.ds(ci, 1), :]
        ig_r = ig_ref[pl.ds(ci, 1), :]
        b_last = b_r[:, L - 1:L]
        a_r = b_last - b_r + ig_r
        b_c = to_col(b_r)
        a_c = to_col(a_r)

        logd = jnp.where(causal, b_c - b_r + ig_r, -jnp.inf)
        m_inter = b_c + m
        m_j = jnp.maximum(jnp.max(logd, axis=1, keepdims=True), m_inter)
        w = jnp.exp(logd - m_j)
        sqk = lax.dot_general(qc, kc, _NT, preferred_element_type=F32) * w
        inter = jnp.exp(m_inter - m_j)
        num = (jnp.dot(sqk.astype(BF16), vc, preferred_element_type=F32)
               + inter * jnp.dot(qc, C.astype(BF16), preferred_element_type=F32))
        den = (jnp.sum(sqk, axis=1, keepdims=True)
               + inter * jnp.sum(qc.astype(F32) * n, axis=1, keepdims=True))
        hc = num / jnp.maximum(jnp.abs(den), jnp.exp(-m_j))

        og = _sigmoid(o_ref[0, pl.ds(start, L), :].astype(F32))
        out_ref[0, pl.ds(start, L), :] = (_rms_rows(og * hc) * ng_ref[...]).astype(out_ref.dtype)

        m_new = jnp.maximum(b_last + m, jnp.max(a_r, axis=1, keepdims=True))
        decay = jnp.exp(b_last + m - m_new)
        kw = kc.astype(F32) * jnp.exp(a_c - m_new)
        C_new = decay * C + jnp.dot(kw.T.astype(BF16), vc, preferred_element_type=F32)
        n_new = decay * n + jnp.sum(kw, axis=0, keepdims=True)
        return C_new, n_new, m_new

    init = (jnp.zeros((M_DIM, M_DIM), F32), jnp.zeros((1, M_DIM), F32), jnp.zeros((1, 1), F32))
    lax.fori_loop(0, nc, chunk, init)


def _mlstm(m_qkvo, gates, gate_bias, conv_w, conv_b, norm_g):
    bsz, s, _ = m_qkvo.shape
    L = M_CHUNK
    nc = s // L
    nh = M_HEADS
    d = M_DIM
    g4 = gates.reshape(bsz, 2 * nh, nc, L)
    seq = lambda off: pl.BlockSpec((1, s, d), lambda b, h: (b, 0, off + h))
    return pl.pallas_call(
        _mlstm_kernel,
        grid=(bsz, nh),
        in_specs=[seq(0), seq(nh), seq(2 * nh), seq(3 * nh),
                  pl.BlockSpec((1, 1, nc, L), lambda b, h: (b, h, 0, 0)),
                  pl.BlockSpec((1, 1, nc, L), lambda b, h: (b, nh + h, 0, 0)),
                  pl.BlockSpec(memory_space=pltpu.SMEM),
                  pl.BlockSpec((CONV_W, d), lambda b, h: (0, h)),
                  pl.BlockSpec((CONV_W, d), lambda b, h: (0, nh + h)),
                  pl.BlockSpec((1, d), lambda b, h: (0, h)),
                  pl.BlockSpec((1, d), lambda b, h: (0, nh + h)),
                  pl.BlockSpec((1, d), lambda b, h: (0, h))],
        out_specs=pl.BlockSpec((1, s, d), lambda b, h: (b, 0, h)),
        out_shape=jax.ShapeDtypeStruct((bsz, s, D_MLSTM), BF16),
        scratch_shapes=[pltpu.VMEM((s, d), BF16), pltpu.VMEM((s, d), BF16),
                        pltpu.VMEM((nc, L), F32), pltpu.VMEM((nc, L), F32)],
        compiler_params=pltpu.CompilerParams(vmem_limit_bytes=VMEM_LIMIT),
        name="mlstm",
    )(m_qkvo, m_qkvo, m_qkvo, m_qkvo, g4, g4, gate_bias, conv_w, conv_w,
      conv_b.reshape(1, -1), conv_b.reshape(1, -1), norm_g.reshape(1, -1))


def _outproj_kernel(att_ref, hm_ref, x_ref, mod_ref, g2_ref, wo_ref, wq_ref,
                    x1_ref, h2_ref, qp_ref):
    mod = mod_ref[0]
    y = (jnp.dot(att_ref[0], wo_ref[0:D_ATT, :], preferred_element_type=F32)
         + jnp.dot(hm_ref[0], wo_ref[D_ATT:, :], preferred_element_type=F32))
    x1 = x_ref[0] + mod[2:3, :] * y
    x1_ref[0] = x1
    h2 = _rms_rows(x1) * g2_ref[...]
    h2 = h2 * (1.0 + mod[4:5, :]) + mod[3:4, :]
    hb = h2.astype(BF16)
    h2_ref[0] = _pack_bf16_halves(hb)
    qp_ref[0] = jnp.dot(hb, wq_ref[...], preferred_element_type=F32).astype(BF16)


def _outproj(att, hm, x, mod, g2, w_out, w_q, b0):
    bsz, s, _ = att.shape
    d = x.shape[-1]
    tm = min(ROW_TILE, s)
    nq = w_q.shape[1]
    tile = lambda n: pl.BlockSpec((1, tm, n), lambda b, i: (b, i, 0))
    return pl.pallas_call(
        _outproj_kernel,
        grid=(bsz, s // tm),
        in_specs=[tile(D_ATT), tile(D_MLSTM),
                  pl.BlockSpec((1, tm, d), lambda b, i: (b + b0, i, 0)),
                  pl.BlockSpec((1, 6, d), lambda b, i: (b + b0, 0, 0)),
                  pl.BlockSpec((1, d), lambda b, i: (0, 0)),
                  pl.BlockSpec((d, d), lambda b, i: (0, 0)),
                  pl.BlockSpec((d, nq), lambda b, i: (0, 0))],
        out_specs=[tile(d), tile(d // 2), tile(nq)],
        out_shape=[jax.ShapeDtypeStruct((bsz, s, d), F32),
                   jax.ShapeDtypeStruct((bsz, s, d // 2), jnp.int32),
                   jax.ShapeDtypeStruct((bsz, s, nq), BF16)],
        compiler_params=pltpu.CompilerParams(vmem_limit_bytes=VMEM_LIMIT),
        name="outproj",
    )(att, hm, x, mod, g2.reshape(1, d), w_out, w_q)


def _top16_rows(blocks, ids):
    big = jnp.int32(1 << 30)
    vals, pos = [], []
    for _ in range(PEER_TOPK):
        m = functools.reduce(jnp.maximum, blocks)
        m = jnp.max(m, axis=0, keepdims=True)
        cand = functools.reduce(jnp.minimum,
                                [jnp.where(b == m, i, big) for b, i in zip(blocks, ids)])
        p = jnp.min(cand, axis=0, keepdims=True)
        blocks = [jnp.where(i == p, -jnp.inf, b) for b, i in zip(blocks, ids)]
        vals.append(m)
        pos.append(p)
    return jnp.concatenate(vals, axis=0), jnp.concatenate(pos, axis=0)


def _pick_rows(table, sel):
    out = jnp.zeros(sel.shape, table.dtype)
    for r in range(PEER_TOPK):
        out = jnp.where(sel == r, table[r:r + 1, :], out)
    return out


def _route_kernel(q_ref, keys_ref, idx_ref, g_ref):
    k = PEER_TOPK
    half = PEER_QDIM // 2
    key_id = lax.broadcasted_iota(jnp.int32, (N_KEYS, LANES), 0)
    sub_id = lax.broadcasted_iota(jnp.int32, (k, LANES), 0)
    for t in range(q_ref.shape[0] // LANES):
        rows = pl.ds(t * LANES, LANES)
        sv, si = [], []
        for p in range(2):
            qh = q_ref[rows, p * half:(p + 1) * half]
            s = lax.dot_general(keys_ref[0, p], qh, _NT, preferred_element_type=F32)
            v, i = _top16_rows([s], [key_id])
            sv.append(v)
            si.append(i)
        blocks = [sv[0][i:i + 1, :] + sv[1] for i in range(k)]
        ids = [sub_id + i * k for i in range(k)]
        top_s, pos = _top16_rows(blocks, ids)
        idx = (_pick_rows(si[0], lax.shift_right_logical(pos, 4)) * N_KEYS
               + _pick_rows(si[1], lax.bitwise_and(pos, k - 1)))
        e = jnp.exp(top_s - top_s[0:1, :])
        idx_ref[:, t * LANES:(t + 1) * LANES] = idx
        g_ref[:, t * LANES:(t + 1) * LANES] = e / jnp.sum(e, axis=0, keepdims=True)


def _route(qp, sub_keys):
    t, _ = qp.shape
    tt = min(ROUTE_TILE, t)
    k = PEER_TOPK
    return pl.pallas_call(
        _route_kernel,
        grid=(t // tt, PEER_HEADS),
        in_specs=[pl.BlockSpec((tt, PEER_QDIM), lambda i, h: (i, h)),
                  pl.BlockSpec((1, 2, N_KEYS, PEER_QDIM // 2), lambda i, h: (h, 0, 0, 0))],
        out_specs=[pl.BlockSpec((k, tt), lambda i, h: (h, i)),
                   pl.BlockSpec((k, tt), lambda i, h: (h, i))],
        out_shape=[jax.ShapeDtypeStruct((PEER_HEADS * k, t), jnp.int32),
                   jax.ShapeDtypeStruct((PEER_HEADS * k, t), F32)],
        name="route",
    )(qp, sub_keys)


def _final_kernel(x1_ref, po_ref, mod_ref, g_ref, o_ref):
    x2 = x1_ref[0] + mod_ref[0][5:6, :] * po_ref[0]
    o_ref[0] = _rms_rows(x2) * g_ref[...]


def _final_kernel_into(x1_ref, po_ref, mod_ref, g_ref, prev_ref, o_ref):
    del prev_ref
    _final_kernel(x1_ref, po_ref, mod_ref, g_ref, o_ref)


def _final(x1, peer_out, mod, final_g, b0, out):
    bsz, s, d = x1.shape
    tm = min(ROW_TILE, s)
    tile = pl.BlockSpec((1, tm, d), lambda b, i: (b, i, 0))
    in_specs = [tile, tile, pl.BlockSpec((1, 6, d), lambda b, i: (b + b0, 0, 0)),
                pl.BlockSpec((1, d), lambda b, i: (0, 0))]
    args = (x1, peer_out, mod, final_g.reshape(1, d))
    if out is not None:
        in_specs.append(pl.BlockSpec(memory_space=pl.ANY))
        args += (out,)
    return pl.pallas_call(
        _final_kernel if out is None else _final_kernel_into,
        grid=(bsz, s // tm),
        in_specs=in_specs,
        out_specs=pl.BlockSpec((1, tm, d), lambda b, i: (b + b0, i, 0)),
        out_shape=jax.ShapeDtypeStruct((mod.shape[0], s, d), F32),
        input_output_aliases={} if out is None else {4: 0},
        name="final",
    )(*args)


GELU_C = math.sqrt(2.0 / math.pi)
SC_LANES = 16
SC_WORKERS = 32
SC_CORES = 2
SC_TOKEN_BLOCK = 16
SC_GATHER_DEPTH = 4
SC_ILV = plsc.PackFormat.INTERLEAVED
N_PAIRS = PEER_HEADS * PEER_TOPK


def _gelu_tanh(x):
    z = GELU_C * (x + 0.044715 * (x * x * x))
    t = 1.0 - 2.0 / (jnp.exp(2.0 * z) + 1.0)
    return x * (0.5 * (1.0 + t))


def _experts(h2, idx, g, u, v):
    t_total, dw = h2.shape
    d = 2 * dw
    tpw = t_total // SC_WORKERS
    tb = SC_TOKEN_BLOCK
    k = PEER_TOPK
    nh = PEER_HEADS
    n_items = tb * nh
    nbuf = SC_GATHER_DEPTH
    assert tpw % tb == 0 and n_items % nbuf == 0
    mesh = plsc.VectorSubcoreMesh(core_axis_name="c", subcore_axis_name="s")

    @functools.partial(
        pl.kernel, mesh=mesh,
        out_type=jax.ShapeDtypeStruct((t_total, d), F32),
        scratch_types=[
            pltpu.VMEM((tb, N_PAIRS), jnp.int32),
            pltpu.VMEM((tb, N_PAIRS), F32),
            pltpu.VMEM((tb, dw), jnp.int32),
            pltpu.VMEM((tb, d), F32),
            pltpu.VMEM((nbuf, k, dw), jnp.int32),
            pltpu.VMEM((nbuf, k, dw), jnp.int32),
            pltpu.SemaphoreType.DMA((nbuf,)),
            pltpu.SemaphoreType.DMA((nbuf,)),
        ],
        compiler_params=pltpu.CompilerParams(needs_layout_passes=False),
        name="experts",
    )
    def experts(h_hbm, idx_hbm, g_hbm, u_hbm, v_hbm, out_hbm,
                idx_b, g_b, h_b, out_b, urows, vrows, usem, vsem):
        wid = lax.axis_index("s") * SC_CORES + lax.axis_index("c")
        base = wid * tpw
        lane = lax.iota(jnp.int32, SC_LANES)

        def split(item):
            return lax.shift_right_logical(item, 3), lax.bitwise_and(item, nh - 1)

        def copies(item, b):
            tt, hd = split(item)
            ids = idx_b.at[tt, pl.ds(hd * k, k)]
            return (pltpu.make_async_copy(u_hbm.at[ids], urows.at[b], usem.at[b]),
                    pltpu.make_async_copy(v_hbm.at[ids], vrows.at[b], vsem.at[b]))

        def fetch(item, b):
            cu, cv = copies(item, b)
            cu.start()
            cv.start()

        def words(ref, *lead, off):
            return plsc.bitcast(ref[(*lead, pl.ds(off, SC_LANES))], BF16)

        def sum4_unpack(pr):
            return plsc.unpack((pr[0] + pr[1]) + (pr[2] + pr[3]), format=SC_ILV)

        def compute(item, b):
            tt, hd = split(item)
            cu, cv = copies(item, b)
            cu.wait()

            def ubody(j, accs):
                hs = [words(h_b, tt, off=(j * 4 + q) * SC_LANES) for q in range(4)]
                new = []
                for p, a in enumerate(accs):
                    lo, hi = sum4_unpack([words(urows, b, p, off=(j * 4 + q) * SC_LANES) * hs[q]
                                          for q in range(4)])
                    new.append(a + (lo + hi))
                return tuple(new)

            accs = plsc.parallel_loop(
                0, dw // (4 * SC_LANES), 1,
                carry=tuple(jnp.zeros((SC_LANES,), F32) for _ in range(k)))(ubody)
            s = jnp.zeros((SC_LANES,), F32)
            for p in range(k):
                s = jnp.where(lane == p, jnp.sum(accs[p]), s)
            c = g_b[tt, pl.ds(hd * k, k)] * _gelu_tanh(s)
            cbb = []
            for p in range(k):
                cp = jnp.full((SC_LANES,), c[p])
                cbb.append(plsc.pack(cp, cp, format=SC_ILV))
            cv.wait()

            @plsc.parallel_loop(0, dw // SC_LANES, 1, unroll=2)
            def _(ch):
                sa = pl.ds(ch * SC_LANES, SC_LANES)
                sb = pl.ds(dw + ch * SC_LANES, SC_LANES)
                los, his = [], []
                for grp in range(k // 4):
                    lo, hi = sum4_unpack([cbb[grp * 4 + q]
                                          * words(vrows, b, grp * 4 + q, off=ch * SC_LANES)
                                          for q in range(4)])
                    los.append(lo)
                    his.append(hi)
                out_b[tt, sa] = out_b[tt, sa] + ((los[0] + los[1]) + (los[2] + los[3]))
                out_b[tt, sb] = out_b[tt, sb] + ((his[0] + his[1]) + (his[2] + his[3]))

        @pl.loop(0, tpw // tb)
        def _(blk):
            t0 = base + blk * tb
            pltpu.sync_copy(idx_hbm.at[pl.ds(t0, tb)], idx_b)
            pltpu.sync_copy(g_hbm.at[pl.ds(t0, tb)], g_b)
            pltpu.sync_copy(h_hbm.at[pl.ds(t0, tb)], h_b)
            for b in range(nbuf - 1):
                fetch(b, b)

            @pl.loop(0, tb)
            def _(tt):
                @pl.loop(0, d // SC_LANES)
                def _(j):
                    out_b[tt, pl.ds(j * SC_LANES, SC_LANES)] = jnp.zeros((SC_LANES,), F32)

            @pl.loop(0, n_items, step=nbuf)
            def _(it):
                for b in range(nbuf):
                    ahead = it + b + nbuf - 1

                    @pl.when(ahead < n_items)
                    def _():
                        fetch(ahead, (b + nbuf - 1) % nbuf)

                    compute(it + b, b)

            pltpu.sync_copy(out_b, out_hbm.at[pl.ds(t0, tb)])

    return experts(h2, idx, g, u, v)


BATCH_CHUNKS = 16


def kernel(x, c, w_ada, b_ada, norm1_g, norm2_g, w_in, conv_w, conv_b, b_igate, b_fgate, lam_q1, lam_k1, lam_q2, lam_k2, diff_sub_g, mlstm_norm_g, w_out, peer_w_q, peer_sub_keys, peer_u, peer_v, rel_bias, final_g):
    bsz, s, d = x.shape
    mod = _ada(c, w_ada[0], b_ada[0]).reshape(bsz, 6, d)

    w = w_in[0]
    w_att = w[:, :3 * D_ATT].astype(BF16)
    w_m = w[:, 3 * D_ATT:3 * D_ATT + 4 * D_MLSTM].astype(BF16)
    w_gt = w[:, 3 * D_ATT + 4 * D_MLSTM:].T.astype(BF16)
    w_o = w_out[0].astype(BF16)
    w_q = peer_w_q[0].astype(BF16)
    sub_keys = peer_sub_keys[0].astype(BF16)
    u_packed = _pack_bf16_halves(peer_u[0])
    v_packed = _pack_bf16_halves(peer_v[0])
    bias_tiles = _relbias(rel_bias)
    lam4 = jnp.stack([lam_q1[0], lam_k1[0], lam_q2[0], lam_k2[0]])
    gate_bias = jnp.concatenate([b_igate[0], b_fgate[0]])

    nb = bsz // BATCH_CHUNKS
    out = None
    for ci in range(BATCH_CHUNKS):
        b0 = ci * nb
        att_qkv, m_qkvo, gates = _inproj(x, mod, norm1_g[0], w_att, w_m, w_gt, b0, nb)
        att = _attn(att_qkv, bias_tiles, lam4, diff_sub_g[0])
        hm = _mlstm(m_qkvo, gates, gate_bias, conv_w[0], conv_b[0], mlstm_norm_g[0])
        x1, h2, qp = _outproj(att, hm, x, mod, norm2_g[0], w_o, w_q, b0)
        idx_t, g_t = _route(qp.reshape(nb * s, -1), sub_keys)
        peer_out = _experts(h2.reshape(nb * s, d // 2), idx_t.T, g_t.T, u_packed, v_packed)
        out = _final(x1, peer_out.reshape(nb, s, d), mod, final_g, b0, out)
    return out
```

```python
import functools
import math

import numpy as np
import jax
import jax.numpy as jnp
from jax import lax
from jax.experimental import pallas as pl
from jax.experimental.pallas import tpu as pltpu
from jax.experimental.pallas import tpu_sc as plsc

F32 = jnp.float32
BF16 = jnp.bfloat16

ATT_HEADS = 4
ATT_QK_DIM = 64
ATT_V_DIM = 128
D_ATT = ATT_HEADS * ATT_V_DIM
M_HEADS = 4
M_DIM = 128
D_MLSTM = M_HEADS * M_DIM
CONV_W = 4
N_BUCKETS = 32
MAX_DIST = 128
N_KEYS = 128
PEER_HEADS = 8
PEER_TOPK = 16
PEER_QDIM = 256
EPS = 1e-6
LAMBDA_INIT = 0.8 - 0.6 * math.exp(-0.3 * 0)

ATT_BLOCK = 256
M_CHUNK = 128
ROW_TILE = 512
ROUTE_TILE = 512
LANES = 128
VMEM_LIMIT = 48 * 1024 * 1024

_NT = (((1,), (1,)), ((), ()))


def _rms_rows(x):
    return x * lax.rsqrt(jnp.mean(x * x, axis=-1, keepdims=True) + EPS)


def _sigmoid(x):
    return 1.0 / (1.0 + jnp.exp(-x))


def _pack_bf16_halves(x):
    bits = lax.bitcast_convert_type(x.astype(BF16).astype(F32), jnp.int32)
    n = x.shape[-1] // 2
    return lax.bitwise_or(lax.shift_right_logical(bits[..., :n], 16),
                          lax.bitwise_and(bits[..., n:], jnp.int32(-65536)))


def _ada_kernel(c_ref, w_ref, b_ref, o_ref):
    c = c_ref[...]
    cond = c * _sigmoid(c)
    o_ref[...] = jnp.dot(cond, w_ref[...], preferred_element_type=F32) + b_ref[...]


def _ada(c, w, b):
    bsz, d = c.shape
    n = w.shape[1]
    return pl.pallas_call(
        _ada_kernel,
        grid=(n // d,),
        in_specs=[pl.BlockSpec((bsz, d), lambda j: (0, 0)),
                  pl.BlockSpec((d, d), lambda j: (0, j)),
                  pl.BlockSpec((1, d), lambda j: (0, j))],
        out_specs=pl.BlockSpec((bsz, d), lambda j: (0, j)),
        out_shape=jax.ShapeDtypeStruct((bsz, n), F32),
        name="ada",
    )(c, w, b.reshape(1, n))


def _inproj_kernel(x_ref, mod_ref, g_ref, wa_ref, wm_ref, wg_ref, oa_ref, om_ref, og_ref):
    x = x_ref[0]
    mod = mod_ref[0]
    h = _rms_rows(x) * g_ref[...]
    h = h * (1.0 + mod[1:2, :]) + mod[0:1, :]
    hb = h.astype(BF16)
    oa_ref[0] = jnp.dot(hb, wa_ref[...], preferred_element_type=F32).astype(BF16)
    om_ref[0] = jnp.dot(hb, wm_ref[...], preferred_element_type=F32).astype(BF16)
    og_ref[0] = lax.dot_general(wg_ref[...], hb, _NT, preferred_element_type=F32)


def _inproj(x, mod, g, w_att, w_m, w_gt, b0, bsz):
    _, s, d = x.shape
    tm = min(ROW_TILE, s)
    na, nm, ng = w_att.shape[1], w_m.shape[1], w_gt.shape[0]
    return pl.pallas_call(
        _inproj_kernel,
        grid=(bsz, s // tm),
        in_specs=[pl.BlockSpec((1, tm, d), lambda b, i: (b + b0, i, 0)),
                  pl.BlockSpec((1, 6, d), lambda b, i: (b + b0, 0, 0)),
                  pl.BlockSpec((1, d), lambda b, i: (0, 0)),
                  pl.BlockSpec((d, na), lambda b, i: (0, 0)),
                  pl.BlockSpec((d, nm), lambda b, i: (0, 0)),
                  pl.BlockSpec((ng, d), lambda b, i: (0, 0))],
        out_specs=[pl.BlockSpec((1, tm, na), lambda b, i: (b, i, 0)),
                   pl.BlockSpec((1, tm, nm), lambda b, i: (b, i, 0)),
                   pl.BlockSpec((1, ng, tm), lambda b, i: (b, 0, i))],
        out_shape=[jax.ShapeDtypeStruct((bsz, s, na), BF16),
                   jax.ShapeDtypeStruct((bsz, s, nm), BF16),
                   jax.ShapeDtypeStruct((bsz, ng, s), F32)],
        compiler_params=pltpu.CompilerParams(vmem_limit_bytes=VMEM_LIMIT),
        name="inproj",
    )(x, mod, g.reshape(1, d), w_att, w_m, w_gt)


def _rel_buckets():
    n = np.arange(2 * ATT_BLOCK)
    max_exact = N_BUCKETS // 2
    nf = np.maximum(n, 1).astype(np.float64)
    large = max_exact + (np.log(nf / max_exact) / math.log(MAX_DIST / max_exact)
                         * (N_BUCKETS - max_exact)).astype(np.int64)
    large = np.minimum(large, N_BUCKETS - 1)
    bucket = np.where(n < max_exact, n, large)
    qk = np.arange(ATT_BLOCK)[:, None] - np.arange(ATT_BLOCK)[None, :]
    tiles = np.stack([bucket[np.maximum(qk, 0)], bucket[ATT_BLOCK + qk]])
    return tiles.astype(np.int32)


def _relbias_kernel(rb_ref, bk_ref, o_ref):
    h = pl.program_id(0)
    bk = bk_ref[...]
    acc = jnp.zeros(bk.shape, F32)
    for b in range(N_BUCKETS):
        acc = jnp.where(bk == b, rb_ref[b, h], acc)
    o_ref[0] = acc


def _relbias(rel_bias):
    tiles = jnp.asarray(_rel_buckets())
    return pl.pallas_call(
        _relbias_kernel,
        grid=(ATT_HEADS,),
        in_specs=[pl.BlockSpec(memory_space=pltpu.SMEM),
                  pl.BlockSpec((2, ATT_BLOCK, ATT_BLOCK), lambda h: (0, 0, 0))],
        out_specs=pl.BlockSpec((1, 2, ATT_BLOCK, ATT_BLOCK), lambda h: (h, 0, 0, 0)),
        out_shape=jax.ShapeDtypeStruct((ATT_HEADS, 2, ATT_BLOCK, ATT_BLOCK), F32),
        name="relbias",
    )(rel_bias, tiles)


def _attn_kernel(q_ref, k_ref, v_ref, bias_ref, lam_ref, subg_ref, o_ref,
                 qz_ref, m_ref, l_ref, acc_ref):
    tq = ATT_BLOCK
    qi = pl.program_id(2)
    scale = ATT_QK_DIM ** -0.5

    q = q_ref[0]
    lane = lax.broadcasted_iota(jnp.int32, q.shape, 1)
    zero = jnp.zeros_like(q)
    qz_ref[0:tq, :] = jnp.where(lane < ATT_QK_DIM, q, zero)
    qz_ref[tq:2 * tq, :] = jnp.where(lane >= ATT_QK_DIM, q, zero)
    m_ref[...] = jnp.full(m_ref.shape, -jnp.inf, F32)
    l_ref[...] = jnp.zeros(l_ref.shape, F32)
    acc_ref[...] = jnp.zeros(acc_ref.shape, F32)

    def step(j, bias, masked):
        start = pl.multiple_of(j * tq, tq)
        k = k_ref[0, pl.ds(start, tq), :]
        v = v_ref[0, pl.ds(start, tq), :]
        s = lax.dot_general(qz_ref[...], k, _NT, preferred_element_type=F32) * scale
        if isinstance(bias, tuple):
            s = s + jnp.concatenate([bias[0], bias[0]], axis=0)
        else:
            s = s + bias
        if masked:
            row = lax.broadcasted_iota(jnp.int32, (tq, tq), 0)
            col = lax.broadcasted_iota(jnp.int32, (tq, tq), 1)
            keep = jnp.concatenate([col <= row, col <= row], axis=0)
            s = jnp.where(keep, s, jnp.finfo(F32).min)
        m_old = m_ref[...]
        m_new = jnp.maximum(m_old, jnp.max(s, axis=-1, keepdims=True))
        alpha = jnp.exp(m_old - m_new)
        p = jnp.exp(s - m_new)
        l_ref[...] = alpha * l_ref[...] + jnp.sum(p, axis=-1, keepdims=True)
        acc_ref[...] = alpha * acc_ref[...] + jnp.dot(p.astype(BF16), v, preferred_element_type=F32)
        m_ref[...] = m_new

    far_bias = bias_ref[0, 1, tq - 1:tq, 0:1]

    def far_body(j, carry):
        step(j, far_bias, False)
        return carry

    lax.fori_loop(0, jnp.maximum(qi - 1, 0), far_body, 0)

    @pl.when(qi >= 1)
    def _():
        step(qi - 1, (bias_ref[0, 1],), False)

    step(qi, (bias_ref[0, 0],), True)

    lam = (jnp.exp(jnp.sum(lam_ref[0:1, :] * lam_ref[1:2, :], axis=-1, keepdims=True))
           - jnp.exp(jnp.sum(lam_ref[2:3, :] * lam_ref[3:4, :], axis=-1, keepdims=True))
           + LAMBDA_INIT)
    o = acc_ref[...] / l_ref[...]
    o = o[0:tq, :] - lam * o[tq:2 * tq, :]
    o = _rms_rows(o) * subg_ref[...] * (1.0 - LAMBDA_INIT)
    o_ref[0] = o.astype(o_ref.dtype)


def _attn(att_qkv, bias_tiles, lam4, sub_g):
    bsz, s, _ = att_qkv.shape
    tq = ATT_BLOCK
    nh = ATT_HEADS
    return pl.pallas_call(
        _attn_kernel,
        grid=(bsz, nh, s // tq),
        in_specs=[pl.BlockSpec((1, tq, ATT_V_DIM), lambda b, h, i: (b, i, h)),
                  pl.BlockSpec((1, s, ATT_V_DIM), lambda b, h, i: (b, 0, nh + h)),
                  pl.BlockSpec((1, s, ATT_V_DIM), lambda b, h, i: (b, 0, 2 * nh + h)),
                  pl.BlockSpec((1, 2, tq, tq), lambda b, h, i: (h, 0, 0, 0)),
                  pl.BlockSpec((4, ATT_QK_DIM), lambda b, h, i: (0, 0)),
                  pl.BlockSpec((1, ATT_V_DIM), lambda b, h, i: (0, 0))],
        out_specs=pl.BlockSpec((1, tq, ATT_V_DIM), lambda b, h, i: (b, i, h)),
        out_shape=jax.ShapeDtypeStruct((bsz, s, D_ATT), BF16),
        scratch_shapes=[pltpu.VMEM((2 * tq, ATT_V_DIM), BF16),
                        pltpu.VMEM((2 * tq, 1), F32),
                        pltpu.VMEM((2 * tq, 1), F32),
                        pltpu.VMEM((2 * tq, ATT_V_DIM), F32)],
        compiler_params=pltpu.CompilerParams(vmem_limit_bytes=VMEM_LIMIT),
        name="attn",
    )(att_qkv, att_qkv, att_qkv, bias_tiles, lam4, sub_g.reshape(1, ATT_V_DIM))


def _mlstm_kernel(q_ref, k_ref, v_ref, o_ref, gi_ref, gf_ref, bias_ref, cwq_ref, cwk_ref,
                  cbq_ref, cbk_ref, ng_ref, out_ref, qs_ref, ks_ref, b_ref, ig_ref):
    s = q_ref.shape[1]
    L = M_CHUNK
    nc = s // L
    h = pl.program_id(1)

    row = lax.broadcasted_iota(jnp.int32, (s, M_DIM), 0)

    def conv_silu(x_ref, w_ref, cb_ref):
        x = x_ref[0].astype(F32)
        w = w_ref[...]
        out = None
        for j in range(CONV_W):
            shift = CONV_W - 1 - j
            xs = x if shift == 0 else jnp.where(row >= shift, pltpu.roll(x, shift, 0), 0.0)
            term = xs * w[j:j + 1, :]
            out = term if out is None else out + term
        out = out + cb_ref[...]
        return out * _sigmoid(out)

    qs_ref[...] = conv_silu(q_ref, cwq_ref, cbq_ref).astype(BF16)
    ks_ref[...] = (conv_silu(k_ref, cwk_ref, cbk_ref) * (M_DIM ** -0.5)).astype(BF16)

    ig = gi_ref[0, 0] + bias_ref[h]
    f = gf_ref[0, 0] + bias_ref[M_HEADS + h]
    logf = jnp.minimum(f, 0.0) - jnp.log(1.0 + jnp.exp(-jnp.abs(f)))
    r = lax.broadcasted_iota(jnp.int32, (L, L), 0)
    c = lax.broadcasted_iota(jnp.int32, (L, L), 1)
    tri = (r <= c).astype(F32)
    b_ref[...] = jnp.dot(logf, tri, preferred_element_type=F32,
                         precision=lax.Precision.HIGHEST)
    ig_ref[...] = ig
    eye = r == c
    causal = c <= r

    def to_col(x_row):
        return jnp.sum(jnp.where(eye, x_row, 0.0), axis=1, keepdims=True)

    def chunk(ci, carry):
        C, n, m = carry
        start = pl.multiple_of(ci * L, L)
        qc = qs_ref[pl.ds(start, L), :]
        kc = ks_ref[pl.ds(start, L), :]
        vc = v_ref[0, pl.ds(start, L), :]
        b_r = b_ref[pl.ds(ci, 1), :]
        ig_r = ig_ref[pl.ds(ci, 1), :]
        b_last = b_r[:, L - 1:L]
        a_r = b_last - b_r + ig_r
        b_c = to_col(b_r)
        a_c = to_col(a_r)

        logd = jnp.where(causal, b_c - b_r + ig_r, -jnp.inf)
        m_inter = b_c + m
        m_j = jnp.maximum(jnp.max(logd, axis=1, keepdims=True), m_inter)
        w = jnp.exp(logd - m_j)
        sqk = lax.dot_general(qc, kc, _NT, preferred_element_type=F32) * w
        inter = jnp.exp(m_inter - m_j)
        num = (jnp.dot(sqk.astype(BF16), vc, preferred_element_type=F32)
               + inter * jnp.dot(qc, C.astype(BF16), preferred_element_type=F32))
        den = (jnp.sum(sqk, axis=1, keepdims=True)
               + inter * jnp.sum(qc.astype(F32) * n, axis=1, keepdims=True))
        hc = num / jnp.maximum(jnp.abs(den), jnp.exp(-m_j))

        og = _sigmoid(o_ref[0, pl.ds(start, L), :].astype(F32))
        out_ref[0, pl.ds(start, L), :] = (_rms_rows(og * hc) * ng_ref[...]).astype(out_ref.dtype)

        m_new = jnp.maximum(b_last + m, jnp.max(a_r, axis=1, keepdims=True))
        decay = jnp.exp(b_last + m - m_new)
        kw = kc.astype(F32) * jnp.exp(a_c - m_new)
        C_new = decay * C + jnp.dot(kw.T.astype(BF16), vc, preferred_element_type=F32)
        n_new = decay * n + jnp.sum(kw, axis=0, keepdims=True)
        return C_new, n_new, m_new

    init = (jnp.zeros((M_DIM, M_DIM), F32), jnp.zeros((1, M_DIM), F32), jnp.zeros((1, 1), F32))
    lax.fori_loop(0, nc, chunk, init)


def _mlstm(m_qkvo, gates, gate_bias, conv_w, conv_b, norm_g):
    bsz, s, _ = m_qkvo.shape
    L = M_CHUNK
    nc = s // L
    nh = M_HEADS
    d = M_DIM
    g4 = gates.reshape(bsz, 2 * nh, nc, L)
    seq = lambda off: pl.BlockSpec((1, s, d), lambda b, h: (b, 0, off + h))
    return pl.pallas_call(
        _mlstm_kernel,
        grid=(bsz, nh),
        in_specs=[seq(0), seq(nh), seq(2 * nh), seq(3 * nh),
                  pl.BlockSpec((1, 1, nc, L), lambda b, h: (b, h, 0, 0)),
                  pl.BlockSpec((1, 1, nc, L), lambda b, h: (b, nh + h, 0, 0)),
                  pl.BlockSpec(memory_space=pltpu.SMEM),
                  pl.BlockSpec((CONV_W, d), lambda b, h: (0, h)),
                  pl.BlockSpec((CONV_W, d), lambda b, h: (0, nh + h)),
                  pl.BlockSpec((1, d), lambda b, h: (0, h)),
                  pl.BlockSpec((1, d), lambda b, h: (0, nh + h)),
                  pl.BlockSpec((1, d), lambda b, h: (0, h))],
        out_specs=pl.BlockSpec((1, s, d), lambda b, h: (b, 0, h)),
        out_shape=jax.ShapeDtypeStruct((bsz, s, D_MLSTM), BF16),
        scratch_shapes=[pltpu.VMEM((s, d), BF16), pltpu.VMEM((s, d), BF16),
                        pltpu.VMEM((nc, L), F32), pltpu.VMEM((nc, L), F32)],
        compiler_params=pltpu.CompilerParams(vmem_limit_bytes=VMEM_LIMIT),
        name="mlstm",
    )(m_qkvo, m_qkvo, m_qkvo, m_qkvo, g4, g4, gate_bias, conv_w, conv_w,
      conv_b.reshape(1, -1), conv_b.reshape(1, -1), norm_g.reshape(1, -1))


def _outproj_kernel(att_ref, hm_ref, x_ref, mod_ref, g2_ref, wo_ref, wq_ref,
                    x1_ref, h2_ref, qp_ref):
    mod = mod_ref[0]
    y = (jnp.dot(att_ref[0], wo_ref[0:D_ATT, :], preferred_element_type=F32)
         + jnp.dot(hm_ref[0], wo_ref[D_ATT:, :], preferred_element_type=F32))
    x1 = x_ref[0] + mod[2:3, :] * y
    x1_ref[0] = x1
    h2 = _rms_rows(x1) * g2_ref[...]
    h2 = h2 * (1.0 + mod[4:5, :]) + mod[3:4, :]
    hb = h2.astype(BF16)
    h2_ref[0] = _pack_bf16_halves(hb)
    qp_ref[0] = jnp.dot(hb, wq_ref[...], preferred_element_type=F32).astype(BF16)


def _outproj(att, hm, x, mod, g2, w_out, w_q, b0):
    bsz, s, _ = att.shape
    d = x.shape[-1]
    tm = min(ROW_TILE, s)
    nq = w_q.shape[1]
    tile = lambda n: pl.BlockSpec((1, tm, n), lambda b, i: (b, i, 0))
    return pl.pallas_call(
        _outproj_kernel,
        grid=(bsz, s // tm),
        in_specs=[tile(D_ATT), tile(D_MLSTM),
                  pl.BlockSpec((1, tm, d), lambda b, i: (b + b0, i, 0)),
                  pl.BlockSpec((1, 6, d), lambda b, i: (b + b0, 0, 0)),
                  pl.BlockSpec((1, d), lambda b, i: (0, 0)),
                  pl.BlockSpec((d, d), lambda b, i: (0, 0)),
                  pl.BlockSpec((d, nq), lambda b, i: (0, 0))],
        out_specs=[tile(d), tile(d // 2), tile(nq)],
        out_shape=[jax.ShapeDtypeStruct((bsz, s, d), F32),
                   jax.ShapeDtypeStruct((bsz, s, d // 2), jnp.int32),
                   jax.ShapeDtypeStruct((bsz, s, nq), BF16)],
        compiler_params=pltpu.CompilerParams(vmem_limit_bytes=VMEM_LIMIT),
        name="outproj",
    )(att, hm, x, mod, g2.reshape(1, d), w_out, w_q)


def _top16_rows(blocks, ids):
    big = jnp.int32(1 << 30)
    vals, pos = [], []
    for _ in range(PEER_TOPK):
        m = functools.reduce(jnp.maximum, blocks)
        m = jnp.max(m, axis=0, keepdims=True)
        cand = functools.reduce(jnp.minimum,
                                [jnp.where(b == m, i, big) for b, i in zip(blocks, ids)])
        p = jnp.min(cand, axis=0, keepdims=True)
        blocks = [jnp.where(i == p, -jnp.inf, b) for b, i in zip(blocks, ids)]
        vals.append(m)
        pos.append(p)
    return jnp.concatenate(vals, axis=0), jnp.concatenate(pos, axis=0)


def _pick_rows(table, sel):
    out = jnp.zeros(sel.shape, table.dtype)
    for r in range(PEER_TOPK):
        out = jnp.where(sel == r, table[r:r + 1, :], out)
    return out


def _route_kernel(q_ref, keys_ref, idx_ref, g_ref):
    k = PEER_TOPK
    half = PEER_QDIM // 2
    key_id = lax.broadcasted_iota(jnp.int32, (N_KEYS, LANES), 0)
    sub_id = lax.broadcasted_iota(jnp.int32, (k, LANES), 0)
    for t in range(q_ref.shape[0] // LANES):
        rows = pl.ds(t * LANES, LANES)
        sv, si = [], []
        for p in range(2):
            qh = q_ref[rows, p * half:(p + 1) * half]
            s = lax.dot_general(keys_ref[0, p], qh, _NT, preferred_element_type=F32)
            v, i = _top16_rows([s], [key_id])
            sv.append(v)
            si.append(i)
        blocks = [sv[0][i:i + 1, :] + sv[1] for i in range(k)]
        ids = [sub_id + i * k for i in range(k)]
        top_s, pos = _top16_rows(blocks, ids)
        idx = (_pick_rows(si[0], lax.shift_right_logical(pos, 4)) * N_KEYS
               + _pick_rows(si[1], lax.bitwise_and(pos, k - 1)))
        e = jnp.exp(top_s - top_s[0:1, :])
        idx_ref[:, t * LANES:(t + 1) * LANES] = idx
        g_ref[:, t * LANES:(t + 1) * LANES] = e / jnp.sum(e, axis=0, keepdims=True)


def _route(qp, sub_keys):
    t, _ = qp.shape
    tt = min(ROUTE_TILE, t)
    k = PEER_TOPK
    return pl.pallas_call(
        _route_kernel,
        grid=(t // tt, PEER_HEADS),
        in_specs=[pl.BlockSpec((tt, PEER_QDIM), lambda i, h: (i, h)),
                  pl.BlockSpec((1, 2, N_KEYS, PEER_QDIM // 2), lambda i, h: (h, 0, 0, 0))],
        out_specs=[pl.BlockSpec((k, tt), lambda i, h: (h, i)),
                   pl.BlockSpec((k, tt), lambda i, h: (h, i))],
        out_shape=[jax.ShapeDtypeStruct((PEER_HEADS * k, t), jnp.int32),
                   jax.ShapeDtypeStruct((PEER_HEADS * k, t), F32)],
        name="route",
    )(qp, sub_keys)


def _final_kernel(x1_ref, po_ref, mod_ref, g_ref, o_ref):
    x2 = x1_ref[0] + mod_ref[0][5:6, :] * po_ref[0]
    o_ref[0] = _rms_rows(x2) * g_ref[...]


def _final_kernel_into(x1_ref, po_ref, mod_ref, g_ref, prev_ref, o_ref):
    del prev_ref
    _final_kernel(x1_ref, po_ref, mod_ref, g_ref, o_ref)


def _final(x1, peer_out, mod, final_g, b0, out):
    bsz, s, d = x1.shape
    tm = min(ROW_TILE, s)
    tile = pl.BlockSpec((1, tm, d), lambda b, i: (b, i, 0))
    in_specs = [tile, tile, pl.BlockSpec((1, 6, d), lambda b, i: (b + b0, 0, 0)),
                pl.BlockSpec((1, d), lambda b, i: (0, 0))]
    args = (x1, peer_out, mod, final_g.reshape(1, d))
    if out is not None:
        in_specs.append(pl.BlockSpec(memory_space=pl.ANY))
        args += (out,)
    return pl.pallas_call(
        _final_kernel if out is None else _final_kernel_into,
        grid=(bsz, s // tm),
        in_specs=in_specs,
        out_specs=pl.BlockSpec((1, tm, d), lambda b, i: (b + b0, i, 0)),
        out_shape=jax.ShapeDtypeStruct((mod.shape[0], s, d), F32),
        input_output_aliases={} if out is None else {4: 0},
        name="final",
    )(*args)


GELU_C = math.sqrt(2.0 / math.pi)
SC_LANES = 16
SC_WORKERS = 32
SC_CORES = 2
SC_TOKEN_BLOCK = 32
SC_GATHER_DEPTH = 4
SC_ILV = plsc.PackFormat.INTERLEAVED
N_PAIRS = PEER_HEADS * PEER_TOPK


def _gelu_tanh(x):
    z = GELU_C * (x + 0.044715 * (x * x * x))
    t = 1.0 - 2.0 / (jnp.exp(2.0 * z) + 1.0)
    return x * (0.5 * (1.0 + t))


def _experts(h2, idx, g, u, v):
    t_total, dw = h2.shape
    d = 2 * dw
    tpw = t_total // SC_WORKERS
    tb = SC_TOKEN_BLOCK
    k = PEER_TOPK
    nh = PEER_HEADS
    n_items = tb * nh
    nbuf = SC_GATHER_DEPTH
    assert tpw % tb == 0 and n_items % nbuf == 0
    mesh = plsc.VectorSubcoreMesh(core_axis_name="c", subcore_axis_name="s")

    @functools.partial(
        pl.kernel, mesh=mesh,
        out_type=jax.ShapeDtypeStruct((t_total, d), F32),
        scratch_types=[
            pltpu.VMEM((tb, N_PAIRS), jnp.int32),
            pltpu.VMEM((tb, N_PAIRS), F32),
            pltpu.VMEM((tb, dw), jnp.int32),
            pltpu.VMEM((tb, d), F32),
            pltpu.VMEM((nbuf, k, dw), jnp.int32),
            pltpu.VMEM((nbuf, k, dw), jnp.int32),
            pltpu.SemaphoreType.DMA((nbuf,)),
            pltpu.SemaphoreType.DMA((nbuf,)),
        ],
        compiler_params=pltpu.CompilerParams(needs_layout_passes=False),
        name="experts",
    )
    def experts(h_hbm, idx_hbm, g_hbm, u_hbm, v_hbm, out_hbm,
                idx_b, g_b, h_b, out_b, urows, vrows, usem, vsem):
        wid = lax.axis_index("s") * SC_CORES + lax.axis_index("c")
        base = wid * tpw
        lane = lax.iota(jnp.int32, SC_LANES)

        def split(item):
            return lax.shift_right_logical(item, 3), lax.bitwise_and(item, nh - 1)

        def copies(item, b):
            tt, hd = split(item)
            ids = idx_b.at[tt, pl.ds(hd * k, k)]
            return (pltpu.make_async_copy(u_hbm.at[ids], urows.at[b], usem.at[b]),
                    pltpu.make_async_copy(v_hbm.at[ids], vrows.at[b], vsem.at[b]))

        def fetch(item, b):
            cu, cv = copies(item, b)
            cu.start()
            cv.start()

        def words(ref, *lead, off):
            return plsc.bitcast(ref[(*lead, pl.ds(off, SC_LANES))], BF16)

        def sum4_unpack(pr):
            return plsc.unpack((pr[0] + pr[1]) + (pr[2] + pr[3]), format=SC_ILV)

        def compute(item, b):
            tt, hd = split(item)
            cu, cv = copies(item, b)
            cu.wait()

            def ubody(j, accs):
                hs = [words(h_b, tt, off=(j * 4 + q) * SC_LANES) for q in range(4)]
                new = []
                for p, a in enumerate(accs):
                    lo, hi = sum4_unpack([words(urows, b, p, off=(j * 4 + q) * SC_LANES) * hs[q]
                                          for q in range(4)])
                    new.append(a + (lo + hi))
                return tuple(new)

            accs = plsc.parallel_loop(
                0, dw // (4 * SC_LANES), 1,
                carry=tuple(jnp.zeros((SC_LANES,), F32) for _ in range(k)))(ubody)
            s = jnp.zeros((SC_LANES,), F32)
            for p in range(k):
                s = jnp.where(lane == p, jnp.sum(accs[p]), s)
            c = g_b[tt, pl.ds(hd * k, k)] * _gelu_tanh(s)
            cbb = []
            for p in range(k):
                cp = jnp.full((SC_LANES,), c[p])
                cbb.append(plsc.pack(cp, cp, format=SC_ILV))
            cv.wait()

            @plsc.parallel_loop(0, dw // SC_LANES, 1, unroll=2)
            def _(ch):
                sa = pl.ds(ch * SC_LANES, SC_LANES)
                sb = pl.ds(dw + ch * SC_LANES, SC_LANES)
                los, his = [], []
                for grp in range(k // 4):
                    lo, hi = sum4_unpack([cbb[grp * 4 + q]
                                          * words(vrows, b, grp * 4 + q, off=ch * SC_LANES)
                                          for q in range(4)])
                    los.append(lo)
                    his.append(hi)
                out_b[tt, sa] = out_b[tt, sa] + ((los[0] + los[1]) + (los[2] + los[3]))
                out_b[tt, sb] = out_b[tt, sb] + ((his[0] + his[1]) + (his[2] + his[3]))

        @pl.loop(0, tpw // tb)
        def _(blk):
            t0 = base + blk * tb
            pltpu.sync_copy(idx_hbm.at[pl.ds(t0, tb)], idx_b)
            pltpu.sync_copy(g_hbm.at[pl.ds(t0, tb)], g_b)
            pltpu.sync_copy(h_hbm.at[pl.ds(t0, tb)], h_b)
            for b in range(nbuf - 1):
                fetch(b, b)

            @pl.loop(0, tb)
            def _(tt):
                @pl.loop(0, d // SC_LANES)
                def _(j):
                    out_b[tt, pl.ds(j * SC_LANES, SC_LANES)] = jnp.zeros((SC_LANES,), F32)

            @pl.loop(0, n_items, step=nbuf)
            def _(it):
                for b in range(nbuf):
                    ahead = it + b + nbuf - 1

                    @pl.when(ahead < n_items)
                    def _():
                        fetch(ahead, (b + nbuf - 1) % nbuf)

                    compute(it + b, b)

            pltpu.sync_copy(out_b, out_hbm.at[pl.ds(t0, tb)])

    return experts(h2, idx, g, u, v)


BATCH_CHUNKS = 16


def kernel(x, c, w_ada, b_ada, norm1_g, norm2_g, w_in, conv_w, conv_b, b_igate, b_fgate, lam_q1, lam_k1, lam_q2, lam_k2, diff_sub_g, mlstm_norm_g, w_out, peer_w_q, peer_sub_keys, peer_u, peer_v, rel_bias, final_g):
    bsz, s, d = x.shape
    mod = _ada(c, w_ada[0], b_ada[0]).reshape(bsz, 6, d)

    w = w_in[0]
    w_att = w[:, :3 * D_ATT].astype(BF16)
    w_m = w[:, 3 * D_ATT:3 * D_ATT + 4 * D_MLSTM].astype(BF16)
    w_gt = w[:, 3 * D_ATT + 4 * D_MLSTM:].T.astype(BF16)
    w_o = w_out[0].astype(BF16)
    w_q = peer_w_q[0].astype(BF16)
    sub_keys = peer_sub_keys[0].astype(BF16)
    u_packed = _pack_bf16_halves(peer_u[0])
    v_packed = _pack_bf16_halves(peer_v[0])
    bias_tiles = _relbias(rel_bias)
    lam4 = jnp.stack([lam_q1[0], lam_k1[0], lam_q2[0], lam_k2[0]])
    gate_bias = jnp.concatenate([b_igate[0], b_fgate[0]])

    nb = bsz // BATCH_CHUNKS
    out = None
    for ci in range(BATCH_CHUNKS):
        b0 = ci * nb
        att_qkv, m_qkvo, gates = _inproj(x, mod, norm1_g[0], w_att, w_m, w_gt, b0, nb)
        att = _attn(att_qkv, bias_tiles, lam4, diff_sub_g[0])
        hm = _mlstm(m_qkvo, gates, gate_bias, conv_w[0], conv_b[0], mlstm_norm_g[0])
        x1, h2, qp = _outproj(att, hm, x, mod, norm2_g[0], w_o, w_q, b0)
        idx_t, g_t = _route(qp.reshape(nb * s, -1), sub_keys)
        peer_out = _experts(h2.reshape(nb * s, d // 2), idx_t.T, g_t.T, u_packed, v_packed)
        out = _final(x1, peer_out.reshape(nb, s, d), mod, final_g, b0, out)
    return out
```

```python
import functools
import math

import numpy as np
import jax
import jax.numpy as jnp
from jax import lax
from jax.experimental import pallas as pl
from jax.experimental.pallas import tpu as pltpu
from jax.experimental.pallas import tpu_sc as plsc

F32 = jnp.float32
BF16 = jnp.bfloat16

ATT_HEADS = 4
ATT_QK_DIM = 64
ATT_V_DIM = 128
D_ATT = ATT_HEADS * ATT_V_DIM
M_HEADS = 4
M_DIM = 128
D_MLSTM = M_HEADS * M_DIM
CONV_W = 4
N_BUCKETS = 32
MAX_DIST = 128
N_KEYS = 128
PEER_HEADS = 8
PEER_TOPK = 16
PEER_QDIM = 256
EPS = 1e-6
LAMBDA_INIT = 0.8 - 0.6 * math.exp(-0.3 * 0)

ATT_BLOCK = 256
M_CHUNK = 128
ROW_TILE = 512
ROUTE_TILE = 512
LANES = 128
VMEM_LIMIT = 48 * 1024 * 1024

_NT = (((1,), (1,)), ((), ()))


def _rms_rows(x):
    return x * lax.rsqrt(jnp.mean(x * x, axis=-1, keepdims=True) + EPS)


def _sigmoid(x):
    return 1.0 / (1.0 + jnp.exp(-x))


def _pack_bf16_halves(x):
    bits = lax.bitcast_convert_type(x.astype(BF16).astype(F32), jnp.int32)
    n = x.shape[-1] // 2
    return lax.bitwise_or(lax.shift_right_logical(bits[..., :n], 16),
                          lax.bitwise_and(bits[..., n:], jnp.int32(-65536)))


def _ada_kernel(c_ref, w_ref, b_ref, o_ref):
    c = c_ref[...]
    cond = c * _sigmoid(c)
    o_ref[...] = jnp.dot(cond, w_ref[...], preferred_element_type=F32) + b_ref[...]


def _ada(c, w, b):
    bsz, d = c.shape
    n = w.shape[1]
    return pl.pallas_call(
        _ada_kernel,
        grid=(n // d,),
        in_specs=[pl.BlockSpec((bsz, d), lambda j: (0, 0)),
                  pl.BlockSpec((d, d), lambda j: (0, j)),
                  pl.BlockSpec((1, d), lambda j: (0, j))],
        out_specs=pl.BlockSpec((bsz, d), lambda j: (0, j)),
        out_shape=jax.ShapeDtypeStruct((bsz, n), F32),
        name="ada",
    )(c, w, b.reshape(1, n))


def _inproj_kernel(x_ref, mod_ref, g_ref, wa_ref, wm_ref, wg_ref, oa_ref, om_ref, og_ref):
    x = x_ref[0]
    mod = mod_ref[0]
    h = _rms_rows(x) * g_ref[...]
    h = h * (1.0 + mod[1:2, :]) + mod[0:1, :]
    hb = h.astype(BF16)
    oa_ref[0] = jnp.dot(hb, wa_ref[...], preferred_element_type=F32).astype(BF16)
    om_ref[0] = jnp.dot(hb, wm_ref[...], preferred_element_type=F32).astype(BF16)
    og_ref[0] = lax.dot_general(wg_ref[...], hb, _NT, preferred_element_type=F32)


def _inproj(x, mod, g, w_att, w_m, w_gt, b0, bsz):
    _, s, d = x.shape
    tm = min(ROW_TILE, s)
    na, nm, ng = w_att.shape[1], w_m.shape[1], w_gt.shape[0]
    return pl.pallas_call(
        _inproj_kernel,
        grid=(bsz, s // tm),
        in_specs=[pl.BlockSpec((1, tm, d), lambda b, i: (b + b0, i, 0)),
                  pl.BlockSpec((1, 6, d), lambda b, i: (b + b0, 0, 0)),
                  pl.BlockSpec((1, d), lambda b, i: (0, 0)),
                  pl.BlockSpec((d, na), lambda b, i: (0, 0)),
                  pl.BlockSpec((d, nm), lambda b, i: (0, 0)),
                  pl.BlockSpec((ng, d), lambda b, i: (0, 0))],
        out_specs=[pl.BlockSpec((1, tm, na), lambda b, i: (b, i, 0)),
                   pl.BlockSpec((1, tm, nm), lambda b, i: (b, i, 0)),
                   pl.BlockSpec((1, ng, tm), lambda b, i: (b, 0, i))],
        out_shape=[jax.ShapeDtypeStruct((bsz, s, na), BF16),
                   jax.ShapeDtypeStruct((bsz, s, nm), BF16),
                   jax.ShapeDtypeStruct((bsz, ng, s), F32)],
        compiler_params=pltpu.CompilerParams(vmem_limit_bytes=VMEM_LIMIT),
        name="inproj",
    )(x, mod, g.reshape(1, d), w_att, w_m, w_gt)


def _rel_buckets():
    n = np.arange(2 * ATT_BLOCK)
    max_exact = N_BUCKETS // 2
    nf = np.maximum(n, 1).astype(np.float64)
    large = max_exact + (np.log(nf / max_exact) / math.log(MAX_DIST / max_exact)
                         * (N_BUCKETS - max_exact)).astype(np.int64)
    large = np.minimum(large, N_BUCKETS - 1)
    bucket = np.where(n < max_exact, n, large)
    qk = np.arange(ATT_BLOCK)[:, None] - np.arange(ATT_BLOCK)[None, :]
    tiles = np.stack([bucket[np.maximum(qk, 0)], bucket[ATT_BLOCK + qk]])
    return tiles.astype(np.int32)


def _relbias_kernel(rb_ref, bk_ref, o_ref):
    h = pl.program_id(0)
    bk = bk_ref[...]
    acc = jnp.zeros(bk.shape, F32)
    for b in range(N_BUCKETS):
        acc = jnp.where(bk == b, rb_ref[b, h], acc)
    o_ref[0] = acc


def _relbias(rel_bias):
    tiles = jnp.asarray(_rel_buckets())
    return pl.pallas_call(
        _relbias_kernel,
        grid=(ATT_HEADS,),
        in_specs=[pl.BlockSpec(memory_space=pltpu.SMEM),
                  pl.BlockSpec((2, ATT_BLOCK, ATT_BLOCK), lambda h: (0, 0, 0))],
        out_specs=pl.BlockSpec((1, 2, ATT_BLOCK, ATT_BLOCK), lambda h: (h, 0, 0, 0)),
        out_shape=jax.ShapeDtypeStruct((ATT_HEADS, 2, ATT_BLOCK, ATT_BLOCK), F32),
        name="relbias",
    )(rel_bias, tiles)


def _attn_kernel(q_ref, k_ref, v_ref, bias_ref, lam_ref, subg_ref, o_ref,
                 qz_ref, m_ref, l_ref, acc_ref):
    tq = ATT_BLOCK
    qi = pl.program_id(2)
    scale = ATT_QK_DIM ** -0.5

    q = q_ref[0]
    lane = lax.broadcasted_iota(jnp.int32, q.shape, 1)
    zero = jnp.zeros_like(q)
    qz_ref[0:tq, :] = jnp.where(lane < ATT_QK_DIM, q, zero)
    qz_ref[tq:2 * tq, :] = jnp.where(lane >= ATT_QK_DIM, q, zero)
    m_ref[...] = jnp.full(m_ref.shape, -jnp.inf, F32)
    l_ref[...] = jnp.zeros(l_ref.shape, F32)
    acc_ref[...] = jnp.zeros(acc_ref.shape, F32)

    def step(j, bias, masked):
        start = pl.multiple_of(j * tq, tq)
        k = k_ref[0, pl.ds(start, tq), :]
        v = v_ref[0, pl.ds(start, tq), :]
        s = lax.dot_general(qz_ref[...], k, _NT, preferred_element_type=F32) * scale
        if isinstance(bias, tuple):
            s = s + jnp.concatenate([bias[0], bias[0]], axis=0)
        else:
            s = s + bias
        if masked:
            row = lax.broadcasted_iota(jnp.int32, (tq, tq), 0)
            col = lax.broadcasted_iota(jnp.int32, (tq, tq), 1)
            keep = jnp.concatenate([col <= row, col <= row], axis=0)
            s = jnp.where(keep, s, jnp.finfo(F32).min)
        m_old = m_ref[...]
        m_new = jnp.maximum(m_old, jnp.max(s, axis=-1, keepdims=True))
        alpha = jnp.exp(m_old - m_new)
        p = jnp.exp(s - m_new)
        l_ref[...] = alpha * l_ref[...] + jnp.sum(p, axis=-1, keepdims=True)
        acc_ref[...] = alpha * acc_ref[...] + jnp.dot(p.astype(BF16), v, preferred_element_type=F32)
        m_ref[...] = m_new

    far_bias = bias_ref[0, 1, tq - 1:tq, 0:1]

    def far_body(j, carry):
        step(j, far_bias, False)
        return carry

    lax.fori_loop(0, jnp.maximum(qi - 1, 0), far_body, 0)

    @pl.when(qi >= 1)
    def _():
        step(qi - 1, (bias_ref[0, 1],), False)

    step(qi, (bias_ref[0, 0],), True)

    lam = (jnp.exp(jnp.sum(lam_ref[0:1, :] * lam_ref[1:2, :], axis=-1, keepdims=True))
           - jnp.exp(jnp.sum(lam_ref[2:3, :] * lam_ref[3:4, :], axis=-1, keepdims=True))
           + LAMBDA_INIT)
    o = acc_ref[...] / l_ref[...]
    o = o[0:tq, :] - lam * o[tq:2 * tq, :]
    o = _rms_rows(o) * subg_ref[...] * (1.0 - LAMBDA_INIT)
    o_ref[0] = o.astype(o_ref.dtype)


def _attn(att_qkv, bias_tiles, lam4, sub_g):
    bsz, s, _ = att_qkv.shape
    tq = ATT_BLOCK
    nh = ATT_HEADS
    return pl.pallas_call(
        _attn_kernel,
        grid=(bsz, nh, s // tq),
        in_specs=[pl.BlockSpec((1, tq, ATT_V_DIM), lambda b, h, i: (b, i, h)),
                  pl.BlockSpec((1, s, ATT_V_DIM), lambda b, h, i: (b, 0, nh + h)),
                  pl.BlockSpec((1, s, ATT_V_DIM), lambda b, h, i: (b, 0, 2 * nh + h)),
                  pl.BlockSpec((1, 2, tq, tq), lambda b, h, i: (h, 0, 0, 0)),
                  pl.BlockSpec((4, ATT_QK_DIM), lambda b, h, i: (0, 0)),
                  pl.BlockSpec((1, ATT_V_DIM), lambda b, h, i: (0, 0))],
        out_specs=pl.BlockSpec((1, tq, ATT_V_DIM), lambda b, h, i: (b, i, h)),
        out_shape=jax.ShapeDtypeStruct((bsz, s, D_ATT), BF16),
        scratch_shapes=[pltpu.VMEM((2 * tq, ATT_V_DIM), BF16),
                        pltpu.VMEM((2 * tq, 1), F32),
                        pltpu.VMEM((2 * tq, 1), F32),
                        pltpu.VMEM((2 * tq, ATT_V_DIM), F32)],
        compiler_params=pltpu.CompilerParams(vmem_limit_bytes=VMEM_LIMIT),
        name="attn",
    )(att_qkv, att_qkv, att_qkv, bias_tiles, lam4, sub_g.reshape(1, ATT_V_DIM))


def _mlstm_kernel(q_ref, k_ref, v_ref, o_ref, gi_ref, gf_ref, bias_ref, cwq_ref, cwk_ref,
                  cbq_ref, cbk_ref, ng_ref, out_ref, qs_ref, ks_ref, b_ref, ig_ref):
    s = q_ref.shape[1]
    L = M_CHUNK
    nc = s // L
    h = pl.program_id(1)

    row = lax.broadcasted_iota(jnp.int32, (s, M_DIM), 0)

    def conv_silu(x_ref, w_ref, cb_ref):
        x = x_ref[0].astype(F32)
        w = w_ref[...]
        out = None
        for j in range(CONV_W):
            shift = CONV_W - 1 - j
            xs = x if shift == 0 else jnp.where(row >= shift, pltpu.roll(x, shift, 0), 0.0)
            term = xs * w[j:j + 1, :]
            out = term if out is None else out + term
        out = out + cb_ref[...]
        return out * _sigmoid(out)

    qs_ref[...] = conv_silu(q_ref, cwq_ref, cbq_ref).astype(BF16)
    ks_ref[...] = (conv_silu(k_ref, cwk_ref, cbk_ref) * (M_DIM ** -0.5)).astype(BF16)

    ig = gi_ref[0, 0] + bias_ref[h]
    f = gf_ref[0, 0] + bias_ref[M_HEADS + h]
    logf = jnp.minimum(f, 0.0) - jnp.log(1.0 + jnp.exp(-jnp.abs(f)))
    r = lax.broadcasted_iota(jnp.int32, (L, L), 0)
    c = lax.broadcasted_iota(jnp.int32, (L, L), 1)
    tri = (r <= c).astype(F32)
    b_ref[...] = jnp.dot(logf, tri, preferred_element_type=F32,
                         precision=lax.Precision.HIGHEST)
    ig_ref[...] = ig
    eye = r == c
    causal = c <= r

    def to_col(x_row):
        return jnp.sum(jnp.where(eye, x_row, 0.0), axis=1, keepdims=True)

    def chunk(ci, carry):
        C, n, m = carry
        start = pl.multiple_of(ci * L, L)
        qc = qs_ref[pl.ds(start, L), :]
        kc = ks_ref[pl.ds(start, L), :]
        vc = v_ref[0, pl.ds(start, L), :]
        b_r = b_ref[pl.ds(ci, 1), :]
        ig_r = ig_ref[pl.ds(ci, 1), :]
        b_last = b_r[:, L - 1:L]
        a_r = b_last - b_r + ig_r
        b_c = to_col(b_r)
        a_c = to_col(a_r)

        logd = jnp.where(causal, b_c - b_r + ig_r, -jnp.inf)
        m_inter = b_c + m
        m_j = jnp.maximum(jnp.max(logd, axis=1, keepdims=True), m_inter)
        w = jnp.exp(logd - m_j)
        sqk = lax.dot_general(qc, kc, _NT, preferred_element_type=F32) * w
        inter = jnp.exp(m_inter - m_j)
        num = (jnp.dot(sqk.astype(BF16), vc, preferred_element_type=F32)
               + inter * jnp.dot(qc, C.astype(BF16), preferred_element_type=F32))
        den = (jnp.sum(sqk, axis=1, keepdims=True)
               + inter * jnp.sum(qc.astype(F32) * n, axis=1, keepdims=True))
        hc = num / jnp.maximum(jnp.abs(den), jnp.exp(-m_j))

        og = _sigmoid(o_ref[0, pl.ds(start, L), :].astype(F32))
        out_ref[0, pl.ds(start, L), :] = (_rms_rows(og * hc) * ng_ref[...]).astype(out_ref.dtype)

        m_new = jnp.maximum(b_last + m, jnp.max(a_r, axis=1, keepdims=True))
        decay = jnp.exp(b_last + m - m_new)
        kw = kc.astype(F32) * jnp.exp(a_c - m_new)
        C_new = decay * C + jnp.dot(kw.T.astype(BF16), vc, preferred_element_type=F32)
        n_new = decay * n + jnp.sum(kw, axis=0, keepdims=True)
        return C_new, n_new, m_new

    init = (jnp.zeros((M_DIM, M_DIM), F32), jnp.zeros((1, M_DIM), F32), jnp.zeros((1, 1), F32))
    lax.fori_loop(0, nc, chunk, init)


def _mlstm(m_qkvo, gates, gate_bias, conv_w, conv_b, norm_g):
    bsz, s, _ = m_qkvo.shape
    L = M_CHUNK
    nc = s // L
    nh = M_HEADS
    d = M_DIM
    g4 = gates.reshape(bsz, 2 * nh, nc, L)
    seq = lambda off: pl.BlockSpec((1, s, d), lambda b, h: (b, 0, off + h))
    return pl.pallas_call(
        _mlstm_kernel,
        grid=(bsz, nh),
        in_specs=[seq(0), seq(nh), seq(2 * nh), seq(3 * nh),
                  pl.BlockSpec((1, 1, nc, L), lambda b, h: (b, h, 0, 0)),
                  pl.BlockSpec((1, 1, nc, L), lambda b, h: (b, nh + h, 0, 0)),
                  pl.BlockSpec(memory_space=pltpu.SMEM),
                  pl.BlockSpec((CONV_W, d), lambda b, h: (0, h)),
                  pl.BlockSpec((CONV_W, d), lambda b, h: (0, nh + h)),
                  pl.BlockSpec((1, d), lambda b, h: (0, h)),
                  pl.BlockSpec((1, d), lambda b, h: (0, nh + h)),
                  pl.BlockSpec((1, d), lambda b, h: (0, h))],
        out_specs=pl.BlockSpec((1, s, d), lambda b, h: (b, 0, h)),
        out_shape=jax.ShapeDtypeStruct((bsz, s, D_MLSTM), BF16),
        scratch_shapes=[pltpu.VMEM((s, d), BF16), pltpu.VMEM((s, d), BF16),
                        pltpu.VMEM((nc, L), F32), pltpu.VMEM((nc, L), F32)],
        compiler_params=pltpu.CompilerParams(vmem_limit_bytes=VMEM_LIMIT),
        name="mlstm",
    )(m_qkvo, m_qkvo, m_qkvo, m_qkvo, g4, g4, gate_bias, conv_w, conv_w,
      conv_b.reshape(1, -1), conv_b.reshape(1, -1), norm_g.reshape(1, -1))


def _outproj_kernel(att_ref, hm_ref, x_ref, mod_ref, g2_ref, wo_ref, wq_ref,
                    x1_ref, h2_ref, qp_ref):
    mod = mod_ref[0]
    y = (jnp.dot(att_ref[0], wo_ref[0:D_ATT, :], preferred_element_type=F32)
         + jnp.dot(hm_ref[0], wo_ref[D_ATT:, :], preferred_element_type=F32))
    x1 = x_ref[0] + mod[2:3, :] * y
    x1_ref[0] = x1
    h2 = _rms_rows(x1) * g2_ref[...]
    h2 = h2 * (1.0 + mod[4:5, :]) + mod[3:4, :]
    hb = h2.astype(BF16)
    h2_ref[0] = _pack_bf16_halves(hb)
    qp_ref[0] = jnp.dot(hb, wq_ref[...], preferred_element_type=F32).astype(BF16)


def _outproj(att, hm, x, mod, g2, w_out, w_q, b0):
    bsz, s, _ = att.shape
    d = x.shape[-1]
    tm = min(ROW_TILE, s)
    nq = w_q.shape[1]
    tile = lambda n: pl.BlockSpec((1, tm, n), lambda b, i: (b, i, 0))
    return pl.pallas_call(
        _outproj_kernel,
        grid=(bsz, s // tm),
        in_specs=[tile(D_ATT), tile(D_MLSTM),
                  pl.BlockSpec((1, tm, d), lambda b, i: (b + b0, i, 0)),
                  pl.BlockSpec((1, 6, d), lambda b, i: (b + b0, 0, 0)),
                  pl.BlockSpec((1, d), lambda b, i: (0, 0)),
                  pl.BlockSpec((d, d), lambda b, i: (0, 0)),
                  pl.BlockSpec((d, nq), lambda b, i: (0, 0))],
        out_specs=[tile(d), tile(d // 2), tile(nq)],
        out_shape=[jax.ShapeDtypeStruct((bsz, s, d), F32),
                   jax.ShapeDtypeStruct((bsz, s, d // 2), jnp.int32),
                   jax.ShapeDtypeStruct((bsz, s, nq), BF16)],
        compiler_params=pltpu.CompilerParams(vmem_limit_bytes=VMEM_LIMIT),
        name="outproj",
    )(att, hm, x, mod, g2.reshape(1, d), w_out, w_q)


def _top16_rows(blocks, ids):
    big = jnp.float32(1 << 20)
    vals, pos = [], []
    for _ in range(PEER_TOPK):
        m = functools.reduce(jnp.maximum, blocks)
        m = jnp.max(m, axis=0, keepdims=True)
        cand = functools.reduce(jnp.minimum,
                                [jnp.where(b == m, i, big) for b, i in zip(blocks, ids)])
        p = jnp.min(cand, axis=0, keepdims=True)
        blocks = [jnp.where(i == p, -jnp.inf, b) for b, i in zip(blocks, ids)]
        vals.append(m)
        pos.append(p)
    return jnp.concatenate(vals, axis=0), jnp.concatenate(pos, axis=0)


def _pick_rows(table, sel):
    out = jnp.zeros(sel.shape, table.dtype)
    for r in range(PEER_TOPK):
        out = jnp.where(sel == r, table[r:r + 1, :], out)
    return out


def _pair_candidates(a, b):
    k, sub = PEER_TOPK, 8
    j_id = lax.broadcasted_iota(jnp.int32, (sub, LANES), 0).astype(F32)
    blocks = [a[0:1, :] + b[0:sub, :], a[0:1, :] + b[sub:k, :], a[1:2, :] + b[0:sub, :]]
    ids = [j_id, j_id + sub, j_id + k]
    for i in range(2, sub):
        blocks.append(jnp.where(j_id < k // (i + 1), a[i:i + 1, :] + b[0:sub, :], -jnp.inf))
        ids.append(j_id + i * k)
    blocks.append(a[sub:k, :] + b[0:1, :])
    ids.append((j_id + sub) * k)
    return blocks, ids


def _route_kernel(q_ref, keys_ref, idx_ref, g_ref):
    k = PEER_TOPK
    half = PEER_QDIM // 2
    key_id = lax.broadcasted_iota(jnp.int32, (N_KEYS, LANES), 0).astype(F32)
    for t in range(q_ref.shape[0] // LANES):
        rows = pl.ds(t * LANES, LANES)
        sv, si = [], []
        for p in range(2):
            qh = q_ref[rows, p * half:(p + 1) * half]
            s = lax.dot_general(keys_ref[0, p], qh, _NT, preferred_element_type=F32)
            v, i = _top16_rows([s], [key_id])
            sv.append(v)
            si.append(i.astype(jnp.int32))
        top_s, pos = _top16_rows(*_pair_candidates(sv[0], sv[1]))
        pos = pos.astype(jnp.int32)
        idx = (_pick_rows(si[0], lax.shift_right_logical(pos, 4)) * N_KEYS
               + _pick_rows(si[1], lax.bitwise_and(pos, k - 1)))
        e = jnp.exp(top_s - top_s[0:1, :])
        idx_ref[:, t * LANES:(t + 1) * LANES] = idx
        g_ref[:, t * LANES:(t + 1) * LANES] = e / jnp.sum(e, axis=0, keepdims=True)


def _route(qp, sub_keys):
    t, _ = qp.shape
    tt = min(ROUTE_TILE, t)
    k = PEER_TOPK
    return pl.pallas_call(
        _route_kernel,
        grid=(t // tt, PEER_HEADS),
        in_specs=[pl.BlockSpec((tt, PEER_QDIM), lambda i, h: (i, h)),
                  pl.BlockSpec((1, 2, N_KEYS, PEER_QDIM // 2), lambda i, h: (h, 0, 0, 0))],
        out_specs=[pl.BlockSpec((k, tt), lambda i, h: (h, i)),
                   pl.BlockSpec((k, tt), lambda i, h: (h, i))],
        out_shape=[jax.ShapeDtypeStruct((PEER_HEADS * k, t), jnp.int32),
                   jax.ShapeDtypeStruct((PEER_HEADS * k, t), F32)],
        name="route",
    )(qp, sub_keys)


def _final_kernel(x1_ref, po_ref, mod_ref, g_ref, o_ref):
    x2 = x1_ref[0] + mod_ref[0][5:6, :] * po_ref[0]
    o_ref[0] = _rms_rows(x2) * g_ref[...]


def _final_kernel_into(x1_ref, po_ref, mod_ref, g_ref, prev_ref, o_ref):
    del prev_ref
    _final_kernel(x1_ref, po_ref, mod_ref, g_ref, o_ref)


def _final(x1, peer_out, mod, final_g, b0, out):
    bsz, s, d = x1.shape
    tm = min(ROW_TILE, s)
    tile = pl.BlockSpec((1, tm, d), lambda b, i: (b, i, 0))
    in_specs = [tile, tile, pl.BlockSpec((1, 6, d), lambda b, i: (b + b0, 0, 0)),
                pl.BlockSpec((1, d), lambda b, i: (0, 0))]
    args = (x1, peer_out, mod, final_g.reshape(1, d))
    if out is not None:
        in_specs.append(pl.BlockSpec(memory_space=pl.ANY))
        args += (out,)
    return pl.pallas_call(
        _final_kernel if out is None else _final_kernel_into,
        grid=(bsz, s // tm),
        in_specs=in_specs,
        out_specs=pl.BlockSpec((1, tm, d), lambda b, i: (b + b0, i, 0)),
        out_shape=jax.ShapeDtypeStruct((mod.shape[0], s, d), F32),
        input_output_aliases={} if out is None else {4: 0},
        name="final",
    )(*args)


GELU_C = math.sqrt(2.0 / math.pi)
SC_LANES = 16
SC_WORKERS = 32
SC_CORES = 2
SC_TOKEN_BLOCK = 32
SC_GATHER_DEPTH = 4
SC_ILV = plsc.PackFormat.INTERLEAVED
N_PAIRS = PEER_HEADS * PEER_TOPK


def _gelu_tanh(x):
    z = GELU_C * (x + 0.044715 * (x * x * x))
    t = 1.0 - 2.0 / (jnp.exp(2.0 * z) + 1.0)
    return x * (0.5 * (1.0 + t))


def _experts(h2, idx, g, u, v):
    t_total, dw = h2.shape
    d = 2 * dw
    tpw = t_total // SC_WORKERS
    tb = SC_TOKEN_BLOCK
    k = PEER_TOPK
    nh = PEER_HEADS
    n_items = tb * nh
    nbuf = SC_GATHER_DEPTH
    assert tpw % tb == 0 and n_items % nbuf == 0
    mesh = plsc.VectorSubcoreMesh(core_axis_name="c", subcore_axis_name="s")

    @functools.partial(
        pl.kernel, mesh=mesh,
        out_type=jax.ShapeDtypeStruct((t_total, d), F32),
        scratch_types=[
            pltpu.VMEM((tb, N_PAIRS), jnp.int32),
            pltpu.VMEM((tb, N_PAIRS), F32),
            pltpu.VMEM((tb, dw), jnp.int32),
            pltpu.VMEM((tb, d), F32),
            pltpu.VMEM((nbuf, k, dw), jnp.int32),
            pltpu.VMEM((nbuf, k, dw), jnp.int32),
            pltpu.SemaphoreType.DMA((nbuf,)),
            pltpu.SemaphoreType.DMA((nbuf,)),
        ],
        compiler_params=pltpu.CompilerParams(needs_layout_passes=False),
        name="experts",
    )
    def experts(h_hbm, idx_hbm, g_hbm, u_hbm, v_hbm, out_hbm,
                idx_b, g_b, h_b, out_b, urows, vrows, usem, vsem):
        wid = lax.axis_index("s") * SC_CORES + lax.axis_index("c")
        base = wid * tpw
        lane = lax.iota(jnp.int32, SC_LANES)

        def split(item):
            return lax.shift_right_logical(item, 3), lax.bitwise_and(item, nh - 1)

        def copies(item, b):
            tt, hd = split(item)
            ids = idx_b.at[tt, pl.ds(hd * k, k)]
            return (pltpu.make_async_copy(u_hbm.at[ids], urows.at[b], usem.at[b]),
                    pltpu.make_async_copy(v_hbm.at[ids], vrows.at[b], vsem.at[b]))

        def fetch(item, b):
            cu, cv = copies(item, b)
            cu.start()
            cv.start()

        def words(ref, *lead, off):
            return plsc.bitcast(ref[(*lead, pl.ds(off, SC_LANES))], BF16)

        def sum4_unpack(pr):
            return plsc.unpack((pr[0] + pr[1]) + (pr[2] + pr[3]), format=SC_ILV)

        def compute(item, b):
            tt, hd = split(item)
            cu, cv = copies(item, b)
            cu.wait()

            def ubody(j, accs):
                hs = [words(h_b, tt, off=(j * 4 + q) * SC_LANES) for q in range(4)]
                new = []
                for p, a in enumerate(accs):
                    lo, hi = sum4_unpack([words(urows, b, p, off=(j * 4 + q) * SC_LANES) * hs[q]
                                          for q in range(4)])
                    new.append(a + (lo + hi))
                return tuple(new)

            accs = plsc.parallel_loop(
                0, dw // (4 * SC_LANES), 1,
                carry=tuple(jnp.zeros((SC_LANES,), F32) for _ in range(k)))(ubody)
            s = jnp.zeros((SC_LANES,), F32)
            for p in range(k):
                s = jnp.where(lane == p, jnp.sum(accs[p]), s)
            c = g_b[tt, pl.ds(hd * k, k)] * _gelu_tanh(s)
            cbb = []
            for p in range(k):
                cp = jnp.full((SC_LANES,), c[p])
                cbb.append(plsc.pack(cp, cp, format=SC_ILV))
            cv.wait()

            @plsc.parallel_loop(0, dw // SC_LANES, 1, unroll=2)
            def _(ch):
                sa = pl.ds(ch * SC_LANES, SC_LANES)
                sb = pl.ds(dw + ch * SC_LANES, SC_LANES)
                los, his = [], []
                for grp in range(k // 4):
                    lo, hi = sum4_unpack([cbb[grp * 4 + q]
                                          * words(vrows, b, grp * 4 + q, off=ch * SC_LANES)
                                          for q in range(4)])
                    los.append(lo)
                    his.append(hi)
                out_b[tt, sa] = out_b[tt, sa] + ((los[0] + los[1]) + (los[2] + los[3]))
                out_b[tt, sb] = out_b[tt, sb] + ((his[0] + his[1]) + (his[2] + his[3]))

        @pl.loop(0, tpw // tb)
        def _(blk):
            t0 = base + blk * tb
            pltpu.sync_copy(idx_hbm.at[pl.ds(t0, tb)], idx_b)
            pltpu.sync_copy(g_hbm.at[pl.ds(t0, tb)], g_b)
            pltpu.sync_copy(h_hbm.at[pl.ds(t0, tb)], h_b)
            for b in range(nbuf - 1):
                fetch(b, b)

            @pl.loop(0, tb)
            def _(tt):
                @pl.loop(0, d // SC_LANES)
                def _(j):
                    out_b[tt, pl.ds(j * SC_LANES, SC_LANES)] = jnp.zeros((SC_LANES,), F32)

            @pl.loop(0, n_items, step=nbuf)
            def _(it):
                for b in range(nbuf):
                    ahead = it + b + nbuf - 1

                    @pl.when(ahead < n_items)
                    def _():
                        fetch(ahead, (b + nbuf - 1) % nbuf)

                    compute(it + b, b)

            pltpu.sync_copy(out_b, out_hbm.at[pl.ds(t0, tb)])

    return experts(h2, idx, g, u, v)


BATCH_CHUNKS = 16


def kernel(x, c, w_ada, b_ada, norm1_g, norm2_g, w_in, conv_w, conv_b, b_igate, b_fgate, lam_q1, lam_k1, lam_q2, lam_k2, diff_sub_g, mlstm_norm_g, w_out, peer_w_q, peer_sub_keys, peer_u, peer_v, rel_bias, final_g):
    bsz, s, d = x.shape
    mod = _ada(c, w_ada[0], b_ada[0]).reshape(bsz, 6, d)

    w = w_in[0]
    w_att = w[:, :3 * D_ATT].astype(BF16)
    w_m = w[:, 3 * D_ATT:3 * D_ATT + 4 * D_MLSTM].astype(BF16)
    w_gt = w[:, 3 * D_ATT + 4 * D_MLSTM:].T.astype(BF16)
    w_o = w_out[0].astype(BF16)
    w_q = peer_w_q[0].astype(BF16)
    sub_keys = peer_sub_keys[0].astype(BF16)
    u_packed = _pack_bf16_halves(peer_u[0])
    v_packed = _pack_bf16_halves(peer_v[0])
    bias_tiles = _relbias(rel_bias)
    lam4 = jnp.stack([lam_q1[0], lam_k1[0], lam_q2[0], lam_k2[0]])
    gate_bias = jnp.concatenate([b_igate[0], b_fgate[0]])

    nb = bsz // BATCH_CHUNKS
    out = None
    for ci in range(BATCH_CHUNKS):
        b0 = ci * nb
        att_qkv, m_qkvo, gates = _inproj(x, mod, norm1_g[0], w_att, w_m, w_gt, b0, nb)
        att = _attn(att_qkv, bias_tiles, lam4, diff_sub_g[0])
        hm = _mlstm(m_qkvo, gates, gate_bias, conv_w[0], conv_b[0], mlstm_norm_g[0])
        x1, h2, qp = _outproj(att, hm, x, mod, norm2_g[0], w_o, w_q, b0)
        idx_t, g_t = _route(qp.reshape(nb * s, -1), sub_keys)
        peer_out = _experts(h2.reshape(nb * s, d // 2), idx_t.T, g_t.T, u_packed, v_packed)
        out = _final(x1, peer_out.reshape(nb, s, d), mod, final_g, b0, out)
    return out
```

```python
import functools
import math

import numpy as np
import jax
import jax.numpy as jnp
from jax import lax
from jax.experimental import pallas as pl
from jax.experimental.pallas import tpu as pltpu
from jax.experimental.pallas import tpu_sc as plsc

F32 = jnp.float32
BF16 = jnp.bfloat16

ATT_HEADS = 4
ATT_QK_DIM = 64
ATT_V_DIM = 128
D_ATT = ATT_HEADS * ATT_V_DIM
M_HEADS = 4
M_DIM = 128
D_MLSTM = M_HEADS * M_DIM
CONV_W = 4
N_BUCKETS = 32
MAX_DIST = 128
N_KEYS = 128
PEER_HEADS = 8
PEER_TOPK = 16
PEER_QDIM = 256
EPS = 1e-6
LAMBDA_INIT = 0.8 - 0.6 * math.exp(-0.3 * 0)

ATT_BLOCK = 256
M_CHUNK = 128
ROW_TILE = 512
ROUTE_TILE = 512
LANES = 128
VMEM_LIMIT = 48 * 1024 * 1024

_NT = (((1,), (1,)), ((), ()))


def _rms_rows(x):
    return x * lax.rsqrt(jnp.mean(x * x, axis=-1, keepdims=True) + EPS)


def _sigmoid(x):
    return 1.0 / (1.0 + jnp.exp(-x))


def _pack_bf16_halves(x):
    bits = lax.bitcast_convert_type(x.astype(BF16).astype(F32), jnp.int32)
    n = x.shape[-1] // 2
    return lax.bitwise_or(lax.shift_right_logical(bits[..., :n], 16),
                          lax.bitwise_and(bits[..., n:], jnp.int32(-65536)))


def _ada_kernel(c_ref, w_ref, b_ref, o_ref):
    c = c_ref[...]
    cond = c * _sigmoid(c)
    o_ref[...] = jnp.dot(cond, w_ref[...], preferred_element_type=F32) + b_ref[...]


def _ada(c, w, b):
    bsz, d = c.shape
    n = w.shape[1]
    return pl.pallas_call(
        _ada_kernel,
        grid=(n // d,),
        in_specs=[pl.BlockSpec((bsz, d), lambda j: (0, 0)),
                  pl.BlockSpec((d, d), lambda j: (0, j)),
                  pl.BlockSpec((1, d), lambda j: (0, j))],
        out_specs=pl.BlockSpec((bsz, d), lambda j: (0, j)),
        out_shape=jax.ShapeDtypeStruct((bsz, n), F32),
        name="ada",
    )(c, w, b.reshape(1, n))


def _inproj_kernel(x_ref, mod_ref, g_ref, wa_ref, wvt_ref, wm_ref, wg_ref,
                   oa_ref, ovt_ref, om_ref, og_ref):
    x = x_ref[0]
    mod = mod_ref[0]
    h = _rms_rows(x) * g_ref[...]
    h = h * (1.0 + mod[1:2, :]) + mod[0:1, :]
    hb = h.astype(BF16)
    oa_ref[0] = jnp.dot(hb, wa_ref[...], preferred_element_type=F32).astype(BF16)
    ovt_ref[0] = lax.dot_general(wvt_ref[...], hb, _NT, preferred_element_type=F32).astype(BF16)
    om_ref[0] = jnp.dot(hb, wm_ref[...], preferred_element_type=F32).astype(BF16)
    og_ref[0] = lax.dot_general(wg_ref[...], hb, _NT, preferred_element_type=F32)


def _inproj(x, mod, g, w_qk, w_vt, w_m, w_gt, b0, bsz):
    _, s, d = x.shape
    tm = min(ROW_TILE, s)
    na, nv, nm, ng = w_qk.shape[1], w_vt.shape[0], w_m.shape[1], w_gt.shape[0]
    const = lambda shape: pl.BlockSpec(shape, lambda b, i: (0, 0))
    return pl.pallas_call(
        _inproj_kernel,
        grid=(bsz, s // tm),
        in_specs=[pl.BlockSpec((1, tm, d), lambda b, i: (b + b0, i, 0)),
                  pl.BlockSpec((1, 6, d), lambda b, i: (b + b0, 0, 0)),
                  const((1, d)), const((d, na)), const((nv, d)), const((d, nm)), const((ng, d))],
        out_specs=[pl.BlockSpec((1, tm, na), lambda b, i: (b, i, 0)),
                   pl.BlockSpec((1, nv, tm), lambda b, i: (b, 0, i)),
                   pl.BlockSpec((1, tm, nm), lambda b, i: (b, i, 0)),
                   pl.BlockSpec((1, ng, tm), lambda b, i: (b, 0, i))],
        out_shape=[jax.ShapeDtypeStruct((bsz, s, na), BF16),
                   jax.ShapeDtypeStruct((bsz, nv, s), BF16),
                   jax.ShapeDtypeStruct((bsz, s, nm), BF16),
                   jax.ShapeDtypeStruct((bsz, ng, s), F32)],
        compiler_params=pltpu.CompilerParams(vmem_limit_bytes=VMEM_LIMIT),
        name="inproj",
    )(x, mod, g.reshape(1, d), w_qk, w_vt, w_m, w_gt)


def _rel_buckets():
    n = np.arange(2 * ATT_BLOCK)
    max_exact = N_BUCKETS // 2
    nf = np.maximum(n, 1).astype(np.float64)
    large = max_exact + (np.log(nf / max_exact) / math.log(MAX_DIST / max_exact)
                         * (N_BUCKETS - max_exact)).astype(np.int64)
    large = np.minimum(large, N_BUCKETS - 1)
    bucket = np.where(n < max_exact, n, large)
    qk = np.arange(ATT_BLOCK)[None, :] - np.arange(ATT_BLOCK)[:, None]
    tiles = np.stack([bucket[np.maximum(qk, 0)], bucket[ATT_BLOCK + qk]])
    return tiles.astype(np.int32)


def _relbias_kernel(rb_ref, bk_ref, o_ref):
    h = pl.program_id(0)
    bk = bk_ref[...]
    acc = jnp.zeros(bk.shape, F32)
    for b in range(N_BUCKETS):
        acc = jnp.where(bk == b, rb_ref[b, h], acc)
    o_ref[0] = acc


def _relbias(rel_bias):
    tiles = jnp.asarray(_rel_buckets())
    return pl.pallas_call(
        _relbias_kernel,
        grid=(ATT_HEADS,),
        in_specs=[pl.BlockSpec(memory_space=pltpu.SMEM),
                  pl.BlockSpec((2, ATT_BLOCK, ATT_BLOCK), lambda h: (0, 0, 0))],
        out_specs=pl.BlockSpec((1, 2, ATT_BLOCK, ATT_BLOCK), lambda h: (h, 0, 0, 0)),
        out_shape=jax.ShapeDtypeStruct((ATT_HEADS, 2, ATT_BLOCK, ATT_BLOCK), F32),
        name="relbias",
    )(rel_bias, tiles)


def _attn_kernel(q_ref, k_ref, vt_ref, bias_ref, lam_ref, subg_ref, o_ref,
                 qz_ref, m_ref, l_ref, acc_ref):
    tq = ATT_BLOCK
    qi = pl.program_id(2)
    scale = ATT_QK_DIM ** -0.5

    q = q_ref[0]
    lane = lax.broadcasted_iota(jnp.int32, q.shape, 1)
    zero = jnp.zeros_like(q)
    qz_ref[0:tq, :] = jnp.where(lane < ATT_QK_DIM, q, zero)
    qz_ref[tq:2 * tq, :] = jnp.where(lane >= ATT_QK_DIM, q, zero)
    m_ref[...] = jnp.full(m_ref.shape, -jnp.inf, F32)
    l_ref[...] = jnp.zeros(l_ref.shape, F32)
    acc_ref[...] = jnp.zeros(acc_ref.shape, F32)

    def step(j, bias, masked):
        start = pl.multiple_of(j * tq, tq)
        k = k_ref[0, pl.ds(start, tq), :]
        vt = vt_ref[0, :, pl.ds(start, tq)]
        s = lax.dot_general(k, qz_ref[...], _NT, preferred_element_type=F32) * scale
        if isinstance(bias, tuple):
            s = s + jnp.concatenate([bias[0], bias[0]], axis=1)
        else:
            s = s + bias
        if masked:
            key = lax.broadcasted_iota(jnp.int32, (tq, tq), 0)
            qry = lax.broadcasted_iota(jnp.int32, (tq, tq), 1)
            keep = jnp.concatenate([key <= qry, key <= qry], axis=1)
            s = jnp.where(keep, s, jnp.finfo(F32).min)
        m_old = m_ref[...]
        m_new = jnp.maximum(m_old, jnp.max(s, axis=0, keepdims=True))
        alpha = jnp.exp(m_old - m_new)
        p = jnp.exp(s - m_new)
        l_ref[...] = alpha * l_ref[...] + jnp.sum(p, axis=0, keepdims=True)
        acc_ref[...] = alpha * acc_ref[...] + jnp.dot(vt, p.astype(BF16), preferred_element_type=F32)
        m_ref[...] = m_new

    far_bias = bias_ref[0, 1, 0:1, tq - 1:tq]

    def far_body(j, carry):
        step(j, far_bias, False)
        return carry

    lax.fori_loop(0, jnp.maximum(qi - 1, 0), far_body, 0)

    @pl.when(qi >= 1)
    def _():
        step(qi - 1, (bias_ref[0, 1],), False)

    step(qi, (bias_ref[0, 0],), True)

    lam = (jnp.exp(jnp.sum(lam_ref[0:1, :] * lam_ref[1:2, :], axis=-1, keepdims=True))
           - jnp.exp(jnp.sum(lam_ref[2:3, :] * lam_ref[3:4, :], axis=-1, keepdims=True))
           + LAMBDA_INIT)
    ot = acc_ref[...] / l_ref[...]
    o = (ot[:, 0:tq] - lam * ot[:, tq:2 * tq]).T
    o = _rms_rows(o) * subg_ref[...] * (1.0 - LAMBDA_INIT)
    o_ref[0] = o.astype(o_ref.dtype)


def _attn(att_qk, att_vt, bias_tiles, lam4, sub_g):
    bsz, s, _ = att_qk.shape
    tq = ATT_BLOCK
    nh = ATT_HEADS
    return pl.pallas_call(
        _attn_kernel,
        grid=(bsz, nh, s // tq),
        in_specs=[pl.BlockSpec((1, tq, ATT_V_DIM), lambda b, h, i: (b, i, h)),
                  pl.BlockSpec((1, s, ATT_V_DIM), lambda b, h, i: (b, 0, nh + h)),
                  pl.BlockSpec((1, ATT_V_DIM, s), lambda b, h, i: (b, h, 0)),
                  pl.BlockSpec((1, 2, tq, tq), lambda b, h, i: (h, 0, 0, 0)),
                  pl.BlockSpec((4, ATT_QK_DIM), lambda b, h, i: (0, 0)),
                  pl.BlockSpec((1, ATT_V_DIM), lambda b, h, i: (0, 0))],
        out_specs=pl.BlockSpec((1, tq, ATT_V_DIM), lambda b, h, i: (b, i, h)),
        out_shape=jax.ShapeDtypeStruct((bsz, s, D_ATT), BF16),
        scratch_shapes=[pltpu.VMEM((2 * tq, ATT_V_DIM), BF16),
                        pltpu.VMEM((1, 2 * tq), F32),
                        pltpu.VMEM((1, 2 * tq), F32),
                        pltpu.VMEM((ATT_V_DIM, 2 * tq), F32)],
        compiler_params=pltpu.CompilerParams(vmem_limit_bytes=VMEM_LIMIT),
        name="attn",
    )(att_qk, att_qk, att_vt, bias_tiles, lam4, sub_g.reshape(1, ATT_V_DIM))


def _mlstm_kernel(q_ref, k_ref, v_ref, o_ref, gi_ref, gf_ref, bias_ref, cwq_ref, cwk_ref,
                  cbq_ref, cbk_ref, ng_ref, out_ref, qs_ref, ks_ref, b_ref, ig_ref):
    s = q_ref.shape[1]
    L = M_CHUNK
    nc = s // L
    h = pl.program_id(1)

    row = lax.broadcasted_iota(jnp.int32, (s, M_DIM), 0)

    def conv_silu(x_ref, w_ref, cb_ref):
        x = x_ref[0].astype(F32)
        w = w_ref[...]
        out = None
        for j in range(CONV_W):
            shift = CONV_W - 1 - j
            xs = x if shift == 0 else jnp.where(row >= shift, pltpu.roll(x, shift, 0), 0.0)
            term = xs * w[j:j + 1, :]
            out = term if out is None else out + term
        out = out + cb_ref[...]
        return out * _sigmoid(out)

    qs_ref[...] = conv_silu(q_ref, cwq_ref, cbq_ref).astype(BF16)
    ks_ref[...] = (conv_silu(k_ref, cwk_ref, cbk_ref) * (M_DIM ** -0.5)).astype(BF16)

    ig = gi_ref[0, 0] + bias_ref[h]
    f = gf_ref[0, 0] + bias_ref[M_HEADS + h]
    logf = jnp.minimum(f, 0.0) - jnp.log(1.0 + jnp.exp(-jnp.abs(f)))
    r = lax.broadcasted_iota(jnp.int32, (L, L), 0)
    c = lax.broadcasted_iota(jnp.int32, (L, L), 1)
    tri = (r <= c).astype(F32)
    b_ref[...] = jnp.dot(logf, tri, preferred_element_type=F32,
                         precision=lax.Precision.HIGHEST)
    ig_ref[...] = ig
    eye = r == c
    causal = c <= r

    def to_col(x_row):
        return jnp.sum(jnp.where(eye, x_row, 0.0), axis=1, keepdims=True)

    def chunk(ci, carry):
        C, n, m = carry
        start = pl.multiple_of(ci * L, L)
        qc = qs_ref[pl.ds(start, L), :]
        kc = ks_ref[pl.ds(start, L), :]
        vc = v_ref[0, pl.ds(start, L), :]
        b_r = b_ref[pl.ds(ci, 1), :]
        ig_r = ig_ref[pl.ds(ci, 1), :]
        b_last = b_r[:, L - 1:L]
        a_r = b_last - b_r + ig_r
        b_c = to_col(b_r)
        a_c = to_col(a_r)

        logd = jnp.where(causal, b_c - b_r + ig_r, -jnp.inf)
        m_inter = b_c + m
        m_j = jnp.maximum(jnp.max(logd, axis=1, keepdims=True), m_inter)
        w = jnp.exp(logd - m_j)
        sqk = lax.dot_general(qc, kc, _NT, preferred_element_type=F32) * w
        inter = jnp.exp(m_inter - m_j)
        num = (jnp.dot(sqk.astype(BF16), vc, preferred_element_type=F32)
               + inter * jnp.dot(qc, C.astype(BF16), preferred_element_type=F32))
        den = (jnp.sum(sqk, axis=1, keepdims=True)
               + inter * jnp.sum(qc.astype(F32) * n, axis=1, keepdims=True))
        hc = num / jnp.maximum(jnp.abs(den), jnp.exp(-m_j))

        og = _sigmoid(o_ref[0, pl.ds(start, L), :].astype(F32))
        out_ref[0, pl.ds(start, L), :] = (_rms_rows(og * hc) * ng_ref[...]).astype(out_ref.dtype)

        m_new = jnp.maximum(b_last + m, jnp.max(a_r, axis=1, keepdims=True))
        decay = jnp.exp(b_last + m - m_new)
        kw = kc.astype(F32) * jnp.exp(a_c - m_new)
        C_new = decay * C + jnp.dot(kw.T.astype(BF16), vc, preferred_element_type=F32)
        n_new = decay * n + jnp.sum(kw, axis=0, keepdims=True)
        return C_new, n_new, m_new

    init = (jnp.zeros((M_DIM, M_DIM), F32), jnp.zeros((1, M_DIM), F32), jnp.zeros((1, 1), F32))
    lax.fori_loop(0, nc, chunk, init)


def _mlstm(m_qkvo, gates, gate_bias, conv_w, conv_b, norm_g):
    bsz, s, _ = m_qkvo.shape
    L = M_CHUNK
    nc = s // L
    nh = M_HEADS
    d = M_DIM
    g4 = gates.reshape(bsz, 2 * nh, nc, L)
    seq = lambda off: pl.BlockSpec((1, s, d), lambda b, h: (b, 0, off + h))
    return pl.pallas_call(
        _mlstm_kernel,
        grid=(bsz, nh),
        in_specs=[seq(0), seq(nh), seq(2 * nh), seq(3 * nh),
                  pl.BlockSpec((1, 1, nc, L), lambda b, h: (b, h, 0, 0)),
                  pl.BlockSpec((1, 1, nc, L), lambda b, h: (b, nh + h, 0, 0)),
                  pl.BlockSpec(memory_space=pltpu.SMEM),
                  pl.BlockSpec((CONV_W, d), lambda b, h: (0, h)),
                  pl.BlockSpec((CONV_W, d), lambda b, h: (0, nh + h)),
                  pl.BlockSpec((1, d), lambda b, h: (0, h)),
                  pl.BlockSpec((1, d), lambda b, h: (0, nh + h)),
                  pl.BlockSpec((1, d), lambda b, h: (0, h))],
        out_specs=pl.BlockSpec((1, s, d), lambda b, h: (b, 0, h)),
        out_shape=jax.ShapeDtypeStruct((bsz, s, D_MLSTM), BF16),
        scratch_shapes=[pltpu.VMEM((s, d), BF16), pltpu.VMEM((s, d), BF16),
                        pltpu.VMEM((nc, L), F32), pltpu.VMEM((nc, L), F32)],
        compiler_params=pltpu.CompilerParams(vmem_limit_bytes=VMEM_LIMIT),
        name="mlstm",
    )(m_qkvo, m_qkvo, m_qkvo, m_qkvo, g4, g4, gate_bias, conv_w, conv_w,
      conv_b.reshape(1, -1), conv_b.reshape(1, -1), norm_g.reshape(1, -1))


def _outproj_kernel(att_ref, hm_ref, x_ref, mod_ref, g2_ref, wo_ref, wq_ref,
                    x1_ref, h2_ref, qp_ref):
    mod = mod_ref[0]
    y = (jnp.dot(att_ref[0], wo_ref[0:D_ATT, :], preferred_element_type=F32)
         + jnp.dot(hm_ref[0], wo_ref[D_ATT:, :], preferred_element_type=F32))
    x1 = x_ref[0] + mod[2:3, :] * y
    x1_ref[0] = x1
    h2 = _rms_rows(x1) * g2_ref[...]
    h2 = h2 * (1.0 + mod[4:5, :]) + mod[3:4, :]
    hb = h2.astype(BF16)
    h2_ref[0] = _pack_bf16_halves(hb)
    qp_ref[0] = jnp.dot(hb, wq_ref[...], preferred_element_type=F32).astype(BF16)


def _outproj(att, hm, x, mod, g2, w_out, w_q, b0):
    bsz, s, _ = att.shape
    d = x.shape[-1]
    tm = min(ROW_TILE, s)
    nq = w_q.shape[1]
    tile = lambda n: pl.BlockSpec((1, tm, n), lambda b, i: (b, i, 0))
    return pl.pallas_call(
        _outproj_kernel,
        grid=(bsz, s // tm),
        in_specs=[tile(D_ATT), tile(D_MLSTM),
                  pl.BlockSpec((1, tm, d), lambda b, i: (b + b0, i, 0)),
                  pl.BlockSpec((1, 6, d), lambda b, i: (b + b0, 0, 0)),
                  pl.BlockSpec((1, d), lambda b, i: (0, 0)),
                  pl.BlockSpec((d, d), lambda b, i: (0, 0)),
                  pl.BlockSpec((d, nq), lambda b, i: (0, 0))],
        out_specs=[tile(d), tile(d // 2), tile(nq)],
        out_shape=[jax.ShapeDtypeStruct((bsz, s, d), F32),
                   jax.ShapeDtypeStruct((bsz, s, d // 2), jnp.int32),
                   jax.ShapeDtypeStruct((bsz, s, nq), BF16)],
        compiler_params=pltpu.CompilerParams(vmem_limit_bytes=VMEM_LIMIT),
        name="outproj",
    )(att, hm, x, mod, g2.reshape(1, d), w_out, w_q)


def _top16_rows(blocks, ids):
    big = jnp.float32(1 << 20)
    vals, pos = [], []
    for _ in range(PEER_TOPK):
        m = functools.reduce(jnp.maximum, blocks)
        m = jnp.max(m, axis=0, keepdims=True)
        cand = functools.reduce(jnp.minimum,
                                [jnp.where(b == m, i, big) for b, i in zip(blocks, ids)])
        p = jnp.min(cand, axis=0, keepdims=True)
        blocks = [jnp.where(i == p, -jnp.inf, b) for b, i in zip(blocks, ids)]
        vals.append(m)
        pos.append(p)
    return jnp.concatenate(vals, axis=0), jnp.concatenate(pos, axis=0)


def _pick_rows(table, sel):
    out = jnp.zeros(sel.shape, table.dtype)
    for r in range(PEER_TOPK):
        out = jnp.where(sel == r, table[r:r + 1, :], out)
    return out


def _pair_candidates(a, b):
    k, sub = PEER_TOPK, 8
    j_id = lax.broadcasted_iota(jnp.int32, (sub, LANES), 0).astype(F32)
    blocks = [a[0:1, :] + b[0:sub, :], a[0:1, :] + b[sub:k, :], a[1:2, :] + b[0:sub, :]]
    ids = [j_id, j_id + sub, j_id + k]
    for i in range(2, sub):
        blocks.append(jnp.where(j_id < k // (i + 1), a[i:i + 1, :] + b[0:sub, :], -jnp.inf))
        ids.append(j_id + i * k)
    blocks.append(a[sub:k, :] + b[0:1, :])
    ids.append((j_id + sub) * k)
    return blocks, ids


def _route_kernel(q_ref, keys_ref, idx_ref, g_ref):
    k = PEER_TOPK
    half = PEER_QDIM // 2
    key_id = lax.broadcasted_iota(jnp.int32, (N_KEYS, LANES), 0).astype(F32)
    for t in range(q_ref.shape[0] // LANES):
        rows = pl.ds(t * LANES, LANES)
        sv, si = [], []
        for p in range(2):
            qh = q_ref[rows, p * half:(p + 1) * half]
            s = lax.dot_general(keys_ref[0, p], qh, _NT, preferred_element_type=F32)
            v, i = _top16_rows([s], [key_id])
            sv.append(v)
            si.append(i.astype(jnp.int32))
        top_s, pos = _top16_rows(*_pair_candidates(sv[0], sv[1]))
        pos = pos.astype(jnp.int32)
        idx = (_pick_rows(si[0], lax.shift_right_logical(pos, 4)) * N_KEYS
               + _pick_rows(si[1], lax.bitwise_and(pos, k - 1)))
        e = jnp.exp(top_s - top_s[0:1, :])
        idx_ref[:, t * LANES:(t + 1) * LANES] = idx
        g_ref[:, t * LANES:(t + 1) * LANES] = e / jnp.sum(e, axis=0, keepdims=True)


def _route(qp, sub_keys):
    t, _ = qp.shape
    tt = min(ROUTE_TILE, t)
    k = PEER_TOPK
    return pl.pallas_call(
        _route_kernel,
        grid=(t // tt, PEER_HEADS),
        in_specs=[pl.BlockSpec((tt, PEER_QDIM), lambda i, h: (i, h)),
                  pl.BlockSpec((1, 2, N_KEYS, PEER_QDIM // 2), lambda i, h: (h, 0, 0, 0))],
        out_specs=[pl.BlockSpec((k, tt), lambda i, h: (h, i)),
                   pl.BlockSpec((k, tt), lambda i, h: (h, i))],
        out_shape=[jax.ShapeDtypeStruct((PEER_HEADS * k, t), jnp.int32),
                   jax.ShapeDtypeStruct((PEER_HEADS * k, t), F32)],
        name="route",
    )(qp, sub_keys)


def _final_kernel(x1_ref, po_ref, mod_ref, g_ref, o_ref):
    x2 = x1_ref[0] + mod_ref[0][5:6, :] * po_ref[0]
    o_ref[0] = _rms_rows(x2) * g_ref[...]


def _final_kernel_into(x1_ref, po_ref, mod_ref, g_ref, prev_ref, o_ref):
    del prev_ref
    _final_kernel(x1_ref, po_ref, mod_ref, g_ref, o_ref)


def _final(x1, peer_out, mod, final_g, b0, out):
    bsz, s, d = x1.shape
    tm = min(ROW_TILE, s)
    tile = pl.BlockSpec((1, tm, d), lambda b, i: (b, i, 0))
    in_specs = [tile, tile, pl.BlockSpec((1, 6, d), lambda b, i: (b + b0, 0, 0)),
                pl.BlockSpec((1, d), lambda b, i: (0, 0))]
    args = (x1, peer_out, mod, final_g.reshape(1, d))
    if out is not None:
        in_specs.append(pl.BlockSpec(memory_space=pl.ANY))
        args += (out,)
    return pl.pallas_call(
        _final_kernel if out is None else _final_kernel_into,
        grid=(bsz, s // tm),
        in_specs=in_specs,
        out_specs=pl.BlockSpec((1, tm, d), lambda b, i: (b + b0, i, 0)),
        out_shape=jax.ShapeDtypeStruct((mod.shape[0], s, d), F32),
        input_output_aliases={} if out is None else {4: 0},
        name="final",
    )(*args)


GELU_C = math.sqrt(2.0 / math.pi)
SC_LANES = 16
SC_WORKERS = 32
SC_CORES = 2
SC_TOKEN_BLOCK = 32
SC_GATHER_DEPTH = 4
SC_ILV = plsc.PackFormat.INTERLEAVED
N_PAIRS = PEER_HEADS * PEER_TOPK


def _gelu_tanh(x):
    z = GELU_C * (x + 0.044715 * (x * x * x))
    t = 1.0 - 2.0 / (jnp.exp(2.0 * z) + 1.0)
    return x * (0.5 * (1.0 + t))


def _experts(h2, idx, g, u, v):
    t_total, dw = h2.shape
    d = 2 * dw
    tpw = t_total // SC_WORKERS
    tb = SC_TOKEN_BLOCK
    k = PEER_TOPK
    nh = PEER_HEADS
    n_items = tb * nh
    nbuf = SC_GATHER_DEPTH
    assert tpw % tb == 0 and n_items % nbuf == 0
    mesh = plsc.VectorSubcoreMesh(core_axis_name="c", subcore_axis_name="s")

    @functools.partial(
        pl.kernel, mesh=mesh,
        out_type=jax.ShapeDtypeStruct((t_total, d), F32),
        scratch_types=[
            pltpu.VMEM((tb, N_PAIRS), jnp.int32),
            pltpu.VMEM((tb, N_PAIRS), F32),
            pltpu.VMEM((tb, dw), jnp.int32),
            pltpu.VMEM((tb, d), F32),
            pltpu.VMEM((nbuf, k, dw), jnp.int32),
            pltpu.VMEM((nbuf, k, dw), jnp.int32),
            pltpu.SemaphoreType.DMA((nbuf,)),
            pltpu.SemaphoreType.DMA((nbuf,)),
        ],
        compiler_params=pltpu.CompilerParams(needs_layout_passes=False),
        name="experts",
    )
    def experts(h_hbm, idx_hbm, g_hbm, u_hbm, v_hbm, out_hbm,
                idx_b, g_b, h_b, out_b, urows, vrows, usem, vsem):
        wid = lax.axis_index("s") * SC_CORES + lax.axis_index("c")
        base = wid * tpw
        lane = lax.iota(jnp.int32, SC_LANES)

        def split(item):
            return lax.shift_right_logical(item, 3), lax.bitwise_and(item, nh - 1)

        def copies(item, b):
            tt, hd = split(item)
            ids = idx_b.at[tt, pl.ds(hd * k, k)]
            return (pltpu.make_async_copy(u_hbm.at[ids], urows.at[b], usem.at[b]),
                    pltpu.make_async_copy(v_hbm.at[ids], vrows.at[b], vsem.at[b]))

        def fetch(item, b):
            cu, cv = copies(item, b)
            cu.start()
            cv.start()

        def words(ref, *lead, off):
            return plsc.bitcast(ref[(*lead, pl.ds(off, SC_LANES))], BF16)

        def sum4_unpack(pr):
            return plsc.unpack((pr[0] + pr[1]) + (pr[2] + pr[3]), format=SC_ILV)

        def compute(item, b):
            tt, hd = split(item)
            cu, cv = copies(item, b)
            cu.wait()

            def ubody(j, accs):
                hs = [words(h_b, tt, off=(j * 4 + q) * SC_LANES) for q in range(4)]
                new = []
                for p, a in enumerate(accs):
                    lo, hi = sum4_unpack([words(urows, b, p, off=(j * 4 + q) * SC_LANES) * hs[q]
                                          for q in range(4)])
                    new.append(a + (lo + hi))
                return tuple(new)

            accs = plsc.parallel_loop(
                0, dw // (4 * SC_LANES), 1,
                carry=tuple(jnp.zeros((SC_LANES,), F32) for _ in range(k)))(ubody)
            s = jnp.zeros((SC_LANES,), F32)
            for p in range(k):
                s = jnp.where(lane == p, jnp.sum(accs[p]), s)
            c = g_b[tt, pl.ds(hd * k, k)] * _gelu_tanh(s)
            cbb = []
            for p in range(k):
                cp = jnp.full((SC_LANES,), c[p])
                cbb.append(plsc.pack(cp, cp, format=SC_ILV))
            cv.wait()

            @plsc.parallel_loop(0, dw // SC_LANES, 1, unroll=2)
            def _(ch):
                sa = pl.ds(ch * SC_LANES, SC_LANES)
                sb = pl.ds(dw + ch * SC_LANES, SC_LANES)
                los, his = [], []
                for grp in range(k // 4):
                    lo, hi = sum4_unpack([cbb[grp * 4 + q]
                                          * words(vrows, b, grp * 4 + q, off=ch * SC_LANES)
                                          for q in range(4)])
                    los.append(lo)
                    his.append(hi)
                out_b[tt, sa] = out_b[tt, sa] + ((los[0] + los[1]) + (los[2] + los[3]))
                out_b[tt, sb] = out_b[tt, sb] + ((his[0] + his[1]) + (his[2] + his[3]))

        @pl.loop(0, tpw // tb)
        def _(blk):
            t0 = base + blk * tb
            pltpu.sync_copy(idx_hbm.at[pl.ds(t0, tb)], idx_b)
            pltpu.sync_copy(g_hbm.at[pl.ds(t0, tb)], g_b)
            pltpu.sync_copy(h_hbm.at[pl.ds(t0, tb)], h_b)
            for b in range(nbuf - 1):
                fetch(b, b)

            @pl.loop(0, tb)
            def _(tt):
                @pl.loop(0, d // SC_LANES)
                def _(j):
                    out_b[tt, pl.ds(j * SC_LANES, SC_LANES)] = jnp.zeros((SC_LANES,), F32)

            @pl.loop(0, n_items, step=nbuf)
            def _(it):
                for b in range(nbuf):
                    ahead = it + b + nbuf - 1

                    @pl.when(ahead < n_items)
                    def _():
                        fetch(ahead, (b + nbuf - 1) % nbuf)

                    compute(it + b, b)

            pltpu.sync_copy(out_b, out_hbm.at[pl.ds(t0, tb)])

    return experts(h2, idx, g, u, v)


BATCH_CHUNKS = 16


def kernel(x, c, w_ada, b_ada, norm1_g, norm2_g, w_in, conv_w, conv_b, b_igate, b_fgate, lam_q1, lam_k1, lam_q2, lam_k2, diff_sub_g, mlstm_norm_g, w_out, peer_w_q, peer_sub_keys, peer_u, peer_v, rel_bias, final_g):
    bsz, s, d = x.shape
    mod = _ada(c, w_ada[0], b_ada[0]).reshape(bsz, 6, d)

    w = w_in[0]
    w_qk = w[:, :2 * D_ATT].astype(BF16)
    w_vt = w[:, 2 * D_ATT:3 * D_ATT].T.astype(BF16)
    w_m =w[:, 3 * D_ATT:3 * D_ATT + 4 * D_MLSTM].astype(BF16)
    w_gt = w[:, 3 * D_ATT + 4 * D_MLSTM:].T.astype(BF16)
    w_o = w_out[0].astype(BF16)
    w_q = peer_w_q[0].astype(BF16)
    sub_keys = peer_sub_keys[0].astype(BF16)
    u_packed = _pack_bf16_halves(peer_u[0])
    v_packed = _pack_bf16_halves(peer_v[0])
    bias_tiles = _relbias(rel_bias)
    lam4 = jnp.stack([lam_q1[0], lam_k1[0], lam_q2[0], lam_k2[0]])
    gate_bias = jnp.concatenate([b_igate[0], b_fgate[0]])

    nb = bsz // BATCH_CHUNKS
    out = None
    for ci in range(BATCH_CHUNKS):
        b0 = ci * nb
        att_qk, att_vt, m_qkvo, gates = _inproj(x, mod, norm1_g[0], w_qk, w_vt, w_m, w_gt, b0, nb)
        att = _attn(att_qk, att_vt, bias_tiles, lam4, diff_sub_g[0])
        hm = _mlstm(m_qkvo, gates, gate_bias, conv_w[0], conv_b[0], mlstm_norm_g[0])
        x1, h2, qp = _outproj(att, hm, x, mod, norm2_g[0], w_o, w_q, b0)
        idx_t, g_t = _route(qp.reshape(nb * s, -1), sub_keys)
        peer_out = _experts(h2.reshape(nb * s, d // 2), idx_t.T, g_t.T, u_packed, v_packed)
        out = _final(x1, peer_out.reshape(nb, s, d), mod, final_g, b0, out)
    return out
```

```python
import functools
import math

import numpy as np
import jax
import jax.numpy as jnp
from jax import lax
from jax.experimental import pallas as pl
from jax.experimental.pallas import tpu as pltpu
from jax.experimental.pallas import tpu_sc as plsc

F32 = jnp.float32
BF16 = jnp.bfloat16

ATT_HEADS = 4
ATT_QK_DIM = 64
ATT_V_DIM = 128
D_ATT = ATT_HEADS * ATT_V_DIM
M_HEADS = 4
M_DIM = 128
D_MLSTM = M_HEADS * M_DIM
CONV_W = 4
N_BUCKETS = 32
MAX_DIST = 128
N_KEYS = 128
PEER_HEADS = 8
PEER_TOPK = 16
PEER_QDIM = 256
EPS = 1e-6
LAMBDA_INIT = 0.8 - 0.6 * math.exp(-0.3 * 0)

ATT_BLOCK = 256
M_CHUNK = 128
ROW_TILE = 512
ROUTE_TILE = 512
LANES = 128
VMEM_LIMIT = 48 * 1024 * 1024

_NT = (((1,), (1,)), ((), ()))


def _rms_rows(x):
    return x * lax.rsqrt(jnp.mean(x * x, axis=-1, keepdims=True) + EPS)


def _sigmoid(x):
    return 1.0 / (1.0 + jnp.exp(-x))


def _pack_bf16_halves(x):
    bits = lax.bitcast_convert_type(x.astype(BF16).astype(F32), jnp.int32)
    n = x.shape[-1] // 2
    return lax.bitwise_or(lax.shift_right_logical(bits[..., :n], 16),
                          lax.bitwise_and(bits[..., n:], jnp.int32(-65536)))


def _pack_table_kernel(x_ref, o_ref):
    o_ref[...] = _pack_bf16_halves(x_ref[...])


def _pack_table(x):
    n, d = x.shape
    rows = min(ROW_TILE, n)
    return pl.pallas_call(
        _pack_table_kernel,
        grid=(n // rows,),
        in_specs=[pl.BlockSpec((rows, d), lambda i: (i, 0))],
        out_specs=pl.BlockSpec((rows, d // 2), lambda i: (i, 0)),
        out_shape=jax.ShapeDtypeStruct((n, d // 2), jnp.int32),
        name="pack_table",
    )(x)


def _ada_kernel(c_ref, w_ref, b_ref, o_ref):
    c = c_ref[...]
    cond = c * _sigmoid(c)
    o_ref[...] = jnp.dot(cond, w_ref[...], preferred_element_type=F32) + b_ref[...]


def _ada(c, w, b):
    bsz, d = c.shape
    n = w.shape[1]
    return pl.pallas_call(
        _ada_kernel,
        grid=(n // d,),
        in_specs=[pl.BlockSpec((bsz, d), lambda j: (0, 0)),
                  pl.BlockSpec((d, d), lambda j: (0, j)),
                  pl.BlockSpec((1, d), lambda j: (0, j))],
        out_specs=pl.BlockSpec((bsz, d), lambda j: (0, j)),
        out_shape=jax.ShapeDtypeStruct((bsz, n), F32),
        name="ada",
    )(c, w, b.reshape(1, n))


def _inproj_kernel(x_ref, mod_ref, g_ref, wa_ref, wvt_ref, wm_ref, wg_ref,
                   oa_ref, ovt_ref, om_ref, og_ref):
    x = x_ref[0]
    mod = mod_ref[0]
    h = _rms_rows(x) * g_ref[...]
    h = h * (1.0 + mod[1:2, :]) + mod[0:1, :]
    hb = h.astype(BF16)
    oa_ref[0] = jnp.dot(hb, wa_ref[...], preferred_element_type=F32).astype(BF16)
    ovt_ref[0] = lax.dot_general(wvt_ref[...], hb, _NT, preferred_element_type=F32).astype(BF16)
    om_ref[0] = jnp.dot(hb, wm_ref[...], preferred_element_type=F32).astype(BF16)
    og_ref[0] = lax.dot_general(wg_ref[...], hb, _NT, preferred_element_type=F32)


def _inproj(x, mod, g, w_qk, w_vt, w_m, w_gt, b0, bsz):
    _, s, d = x.shape
    tm = min(ROW_TILE, s)
    na, nv, nm, ng = w_qk.shape[1], w_vt.shape[0], w_m.shape[1], w_gt.shape[0]
    const = lambda shape: pl.BlockSpec(shape, lambda b, i: (0, 0))
    return pl.pallas_call(
        _inproj_kernel,
        grid=(bsz, s // tm),
        in_specs=[pl.BlockSpec((1, tm, d), lambda b, i: (b + b0, i, 0)),
                  pl.BlockSpec((1, 6, d), lambda b, i: (b + b0, 0, 0)),
                  const((1, d)), const((d, na)), const((nv, d)), const((d, nm)), const((ng, d))],
        out_specs=[pl.BlockSpec((1, tm, na), lambda b, i: (b, i, 0)),
                   pl.BlockSpec((1, nv, tm), lambda b, i: (b, 0, i)),
                   pl.BlockSpec((1, tm, nm), lambda b, i: (b, i, 0)),
                   pl.BlockSpec((1, ng, tm), lambda b, i: (b, 0, i))],
        out_shape=[jax.ShapeDtypeStruct((bsz, s, na), BF16),
                   jax.ShapeDtypeStruct((bsz, nv, s), BF16),
                   jax.ShapeDtypeStruct((bsz, s, nm), BF16),
                   jax.ShapeDtypeStruct((bsz, ng, s), F32)],
        compiler_params=pltpu.CompilerParams(vmem_limit_bytes=VMEM_LIMIT),
        name="inproj",
    )(x, mod, g.reshape(1, d), w_qk, w_vt, w_m, w_gt)


def _rel_buckets():
    n = np.arange(2 * ATT_BLOCK)
    max_exact = N_BUCKETS // 2
    nf = np.maximum(n, 1).astype(np.float64)
    large = max_exact + (np.log(nf / max_exact) / math.log(MAX_DIST / max_exact)
                         * (N_BUCKETS - max_exact)).astype(np.int64)
    large = np.minimum(large, N_BUCKETS - 1)
    bucket = np.where(n < max_exact, n, large)
    qk = np.arange(ATT_BLOCK)[None, :] - np.arange(ATT_BLOCK)[:, None]
    tiles = np.stack([bucket[np.maximum(qk, 0)], bucket[ATT_BLOCK + qk]])
    return tiles.astype(np.int32)


def _relbias_kernel(rb_ref, bk_ref, o_ref):
    h = pl.program_id(0)
    bk = bk_ref[...]
    acc = jnp.zeros(bk.shape, F32)
    for b in range(N_BUCKETS):
        acc = jnp.where(bk == b, rb_ref[b, h], acc)
    o_ref[0] = acc


def _relbias(rel_bias):
    tiles = jnp.asarray(_rel_buckets())
    return pl.pallas_call(
        _relbias_kernel,
        grid=(ATT_HEADS,),
        in_specs=[pl.BlockSpec(memory_space=pltpu.SMEM),
                  pl.BlockSpec((2, ATT_BLOCK, ATT_BLOCK), lambda h: (0, 0, 0))],
        out_specs=pl.BlockSpec((1, 2, ATT_BLOCK, ATT_BLOCK), lambda h: (h, 0, 0, 0)),
        out_shape=jax.ShapeDtypeStruct((ATT_HEADS, 2, ATT_BLOCK, ATT_BLOCK), F32),
        name="relbias",
    )(rel_bias, tiles)


def _attn_kernel(q_ref, k_ref, vt_ref, bias_ref, lam_ref, subg_ref, o_ref,
                 qz_ref, m_ref, l_ref, acc_ref):
    tq = ATT_BLOCK
    qi = pl.program_id(2)
    scale = ATT_QK_DIM ** -0.5

    q = q_ref[0]
    lane = lax.broadcasted_iota(jnp.int32, q.shape, 1)
    zero = jnp.zeros_like(q)
    qz_ref[0:tq, :] = jnp.where(lane < ATT_QK_DIM, q, zero)
    qz_ref[tq:2 * tq, :] = jnp.where(lane >= ATT_QK_DIM, q, zero)
    m_ref[...] = jnp.full(m_ref.shape, -jnp.inf, F32)
    l_ref[...] = jnp.zeros(l_ref.shape, F32)
    acc_ref[...] = jnp.zeros(acc_ref.shape, F32)

    def step(j, bias, masked):
        start = pl.multiple_of(j * tq, tq)
        k = k_ref[0, pl.ds(start, tq), :]
        vt = vt_ref[0, :, pl.ds(start, tq)]
        s = lax.dot_general(k, qz_ref[...], _NT, preferred_element_type=F32) * scale
        if isinstance(bias, tuple):
            s = s + jnp.concatenate([bias[0], bias[0]], axis=1)
        else:
            s = s + bias
        if masked:
            key = lax.broadcasted_iota(jnp.int32, (tq, tq), 0)
            qry = lax.broadcasted_iota(jnp.int32, (tq, tq), 1)
            keep = jnp.concatenate([key <= qry, key <= qry], axis=1)
            s = jnp.where(keep, s, jnp.finfo(F32).min)
        m_old = m_ref[...]
        m_new = jnp.maximum(m_old, jnp.max(s, axis=0, keepdims=True))
        alpha = jnp.exp(m_old - m_new)
        p = jnp.exp(s - m_new)
        l_ref[...] = alpha * l_ref[...] + jnp.sum(p, axis=0, keepdims=True)
        acc_ref[...] = alpha * acc_ref[...] + jnp.dot(vt, p.astype(BF16), preferred_element_type=F32)
        m_ref[...] = m_new

    far_bias = bias_ref[0, 1, 0:1, tq - 1:tq]

    def far_body(j, carry):
        step(j, far_bias, False)
        return carry

    lax.fori_loop(0, jnp.maximum(qi - 1, 0), far_body, 0)

    @pl.when(qi >= 1)
    def _():
        step(qi - 1, (bias_ref[0, 1],), False)

    step(qi, (bias_ref[0, 0],), True)

    lam = (jnp.exp(jnp.sum(lam_ref[0:1, :] * lam_ref[1:2, :], axis=-1, keepdims=True))
           - jnp.exp(jnp.sum(lam_ref[2:3, :] * lam_ref[3:4, :], axis=-1, keepdims=True))
           + LAMBDA_INIT)
    ot = acc_ref[...] / l_ref[...]
    o = (ot[:, 0:tq] - lam * ot[:, tq:2 * tq]).T
    o = _rms_rows(o) * subg_ref[...] * (1.0 - LAMBDA_INIT)
    o_ref[0] = o.astype(o_ref.dtype)


def _attn(att_qk, att_vt, bias_tiles, lam4, sub_g):
    bsz, s, _ = att_qk.shape
    tq = ATT_BLOCK
    nh = ATT_HEADS
    return pl.pallas_call(
        _attn_kernel,
        grid=(bsz, nh, s // tq),
        in_specs=[pl.BlockSpec((1, tq, ATT_V_DIM), lambda b, h, i: (b, i, h)),
                  pl.BlockSpec((1, s, ATT_V_DIM), lambda b, h, i: (b, 0, nh + h)),
                  pl.BlockSpec((1, ATT_V_DIM, s), lambda b, h, i: (b, h, 0)),
                  pl.BlockSpec((1, 2, tq, tq), lambda b, h, i: (h, 0, 0, 0)),
                  pl.BlockSpec((4, ATT_QK_DIM), lambda b, h, i: (0, 0)),
                  pl.BlockSpec((1, ATT_V_DIM), lambda b, h, i: (0, 0))],
        out_specs=pl.BlockSpec((1, tq, ATT_V_DIM), lambda b, h, i: (b, i, h)),
        out_shape=jax.ShapeDtypeStruct((bsz, s, D_ATT), BF16),
        scratch_shapes=[pltpu.VMEM((2 * tq, ATT_V_DIM), BF16),
                        pltpu.VMEM((1, 2 * tq), F32),
                        pltpu.VMEM((1, 2 * tq), F32),
                        pltpu.VMEM((ATT_V_DIM, 2 * tq), F32)],
        compiler_params=pltpu.CompilerParams(vmem_limit_bytes=VMEM_LIMIT),
        name="attn",
    )(att_qk, att_qk, att_vt, bias_tiles, lam4, sub_g.reshape(1, ATT_V_DIM))


def _mlstm_kernel(q_ref, k_ref, v_ref, o_ref, gi_ref, gf_ref, bias_ref, cwq_ref, cwk_ref,
                  cbq_ref, cbk_ref, ng_ref, out_ref, qs_ref, ks_ref, b_ref, ig_ref):
    s = q_ref.shape[1]
    L = M_CHUNK
    nc = s // L
    h = pl.program_id(1)

    row = lax.broadcasted_iota(jnp.int32, (s, M_DIM), 0)

    def conv_silu(x_ref, w_ref, cb_ref):
        x = x_ref[0].astype(F32)
        w = w_ref[...]
        out = None
        for j in range(CONV_W):
            shift = CONV_W - 1 - j
            xs = x if shift == 0 else jnp.where(row >= shift, pltpu.roll(x, shift, 0), 0.0)
            term = xs * w[j:j + 1, :]
            out = term if out is None else out + term
        out = out + cb_ref[...]
        return out * _sigmoid(out)

    qs_ref[...] = conv_silu(q_ref, cwq_ref, cbq_ref).astype(BF16)
    ks_ref[...] = (conv_silu(k_ref, cwk_ref, cbk_ref) * (M_DIM ** -0.5)).astype(BF16)

    ig = gi_ref[0, 0] + bias_ref[h]
    f = gf_ref[0, 0] + bias_ref[M_HEADS + h]
    logf = jnp.minimum(f, 0.0) - jnp.log(1.0 + jnp.exp(-jnp.abs(f)))
    r = lax.broadcasted_iota(jnp.int32, (L, L), 0)
    c = lax.broadcasted_iota(jnp.int32, (L, L), 1)
    tri = (r <= c).astype(F32)
    b_ref[...] = jnp.dot(logf, tri, preferred_element_type=F32,
                         precision=lax.Precision.HIGHEST)
    ig_ref[...] = ig
    eye = r == c
    causal = c <= r

    def to_col(x_row):
        return jnp.sum(jnp.where(eye, x_row, 0.0), axis=1, keepdims=True)

    def chunk(ci, carry):
        C, n, m = carry
        start = pl.multiple_of(ci * L, L)
        qc = qs_ref[pl.ds(start, L), :]
        kc = ks_ref[pl.ds(start, L), :]
        vc = v_ref[0, pl.ds(start, L), :]
        b_r = b_ref[pl.ds(ci, 1), :]
        ig_r = ig_ref[pl.ds(ci, 1), :]
        b_last = b_r[:, L - 1:L]
        a_r = b_last - b_r + ig_r
        b_c = to_col(b_r)
        a_c = to_col(a_r)

        logd = jnp.where(causal, b_c - b_r + ig_r, -jnp.inf)
        m_inter = b_c + m
        m_j = jnp.maximum(jnp.max(logd, axis=1, keepdims=True), m_inter)
        w = jnp.exp(logd - m_j)
        sqk = lax.dot_general(qc, kc, _NT, preferred_element_type=F32) * w
        inter = jnp.exp(m_inter - m_j)
        num = (jnp.dot(sqk.astype(BF16), vc, preferred_element_type=F32)
               + inter * jnp.dot(qc, C.astype(BF16), preferred_element_type=F32))
        den = (jnp.sum(sqk, axis=1, keepdims=True)
               + inter * jnp.sum(qc.astype(F32) * n, axis=1, keepdims=True))
        hc = num / jnp.maximum(jnp.abs(den), jnp.exp(-m_j))

        og = _sigmoid(o_ref[0, pl.ds(start, L), :].astype(F32))
        out_ref[0, pl.ds(start, L), :] = (_rms_rows(og * hc) * ng_ref[...]).astype(out_ref.dtype)

        m_new = jnp.maximum(b_last + m, jnp.max(a_r, axis=1, keepdims=True))
        decay = jnp.exp(b_last + m - m_new)
        kw = kc.astype(F32) * jnp.exp(a_c - m_new)
        C_new = decay * C + jnp.dot(kw.T.astype(BF16), vc, preferred_element_type=F32)
        n_new = decay * n + jnp.sum(kw, axis=0, keepdims=True)
        return C_new, n_new, m_new

    init = (jnp.zeros((M_DIM, M_DIM), F32), jnp.zeros((1, M_DIM), F32), jnp.zeros((1, 1), F32))
    lax.fori_loop(0, nc, chunk, init)


def _mlstm(m_qkvo, gates, gate_bias, conv_w, conv_b, norm_g):
    bsz, s, _ = m_qkvo.shape
    L = M_CHUNK
    nc = s // L
    nh = M_HEADS
    d = M_DIM
    g4 = gates.reshape(bsz, 2 * nh, nc, L)
    seq = lambda off: pl.BlockSpec((1, s, d), lambda b, h: (b, 0, off + h))
    return pl.pallas_call(
        _mlstm_kernel,
        grid=(bsz, nh),
        in_specs=[seq(0), seq(nh), seq(2 * nh), seq(3 * nh),
                  pl.BlockSpec((1, 1, nc, L), lambda b, h: (b, h, 0, 0)),
                  pl.BlockSpec((1, 1, nc, L), lambda b, h: (b, nh + h, 0, 0)),
                  pl.BlockSpec(memory_space=pltpu.SMEM),
                  pl.BlockSpec((CONV_W, d), lambda b, h: (0, h)),
                  pl.BlockSpec((CONV_W, d), lambda b, h: (0, nh + h)),
                  pl.BlockSpec((1, d), lambda b, h: (0, h)),
                  pl.BlockSpec((1, d), lambda b, h: (0, nh + h)),
                  pl.BlockSpec((1, d), lambda b, h: (0, h))],
        out_specs=pl.BlockSpec((1, s, d), lambda b, h: (b, 0, h)),
        out_shape=jax.ShapeDtypeStruct((bsz, s, D_MLSTM), BF16),
        scratch_shapes=[pltpu.VMEM((s, d), BF16), pltpu.VMEM((s, d), BF16),
                        pltpu.VMEM((nc, L), F32), pltpu.VMEM((nc, L), F32)],
        compiler_params=pltpu.CompilerParams(vmem_limit_bytes=VMEM_LIMIT),
        name="mlstm",
    )(m_qkvo, m_qkvo, m_qkvo, m_qkvo, g4, g4, gate_bias, conv_w, conv_w,
      conv_b.reshape(1, -1), conv_b.reshape(1, -1), norm_g.reshape(1, -1))


def _outproj_kernel(att_ref, hm_ref, x_ref, mod_ref, g2_ref, wo_ref, wq_ref,
                    x1_ref, h2_ref, qp_ref):
    mod = mod_ref[0]
    y = (jnp.dot(att_ref[0], wo_ref[0:D_ATT, :], preferred_element_type=F32)
         + jnp.dot(hm_ref[0], wo_ref[D_ATT:, :], preferred_element_type=F32))
    x1 = x_ref[0] + mod[2:3, :] * y
    x1_ref[0] = x1
    h2 = _rms_rows(x1) * g2_ref[...]
    h2 = h2 * (1.0 + mod[4:5, :]) + mod[3:4, :]
    hb = h2.astype(BF16)
    h2_ref[0] = _pack_bf16_halves(hb)
    qp_ref[0] = jnp.dot(hb, wq_ref[...], preferred_element_type=F32).astype(BF16)


def _outproj(att, hm, x, mod, g2, w_out, w_q, b0):
    bsz, s, _ = att.shape
    d = x.shape[-1]
    tm = min(ROW_TILE, s)
    nq = w_q.shape[1]
    tile = lambda n: pl.BlockSpec((1, tm, n), lambda b, i: (b, i, 0))
    return pl.pallas_call(
        _outproj_kernel,
        grid=(bsz, s // tm),
        in_specs=[tile(D_ATT), tile(D_MLSTM),
                  pl.BlockSpec((1, tm, d), lambda b, i: (b + b0, i, 0)),
                  pl.BlockSpec((1, 6, d), lambda b, i: (b + b0, 0, 0)),
                  pl.BlockSpec((1, d), lambda b, i: (0, 0)),
                  pl.BlockSpec((d, d), lambda b, i: (0, 0)),
                  pl.BlockSpec((d, nq), lambda b, i: (0, 0))],
        out_specs=[tile(d), tile(d // 2), tile(nq)],
        out_shape=[jax.ShapeDtypeStruct((bsz, s, d), F32),
                   jax.ShapeDtypeStruct((bsz, s, d // 2), jnp.int32),
                   jax.ShapeDtypeStruct((bsz, s, nq), BF16)],
        compiler_params=pltpu.CompilerParams(vmem_limit_bytes=VMEM_LIMIT),
        name="outproj",
    )(att, hm, x, mod, g2.reshape(1, d), w_out, w_q)


def _top16_rows(blocks, ids):
    big = jnp.float32(1 << 20)
    vals, pos = [], []
    for _ in range(PEER_TOPK):
        m = functools.reduce(jnp.maximum, blocks)
        m = jnp.max(m, axis=0, keepdims=True)
        cand = functools.reduce(jnp.minimum,
                                [jnp.where(b == m, i, big) for b, i in zip(blocks, ids)])
        p = jnp.min(cand, axis=0, keepdims=True)
        blocks = [jnp.where(i == p, -jnp.inf, b) for b, i in zip(blocks, ids)]
        vals.append(m)
        pos.append(p)
    return jnp.concatenate(vals, axis=0), jnp.concatenate(pos, axis=0)


def _pick_rows(table, sel):
    out = jnp.zeros(sel.shape, table.dtype)
    for r in range(PEER_TOPK):
        out = jnp.where(sel == r, table[r:r + 1, :], out)
    return out


def _pair_candidates(a, b):
    k, sub = PEER_TOPK, 8
    j_id = lax.broadcasted_iota(jnp.int32, (sub, LANES), 0).astype(F32)
    blocks = [a[0:1, :] + b[0:sub, :], a[0:1, :] + b[sub:k, :], a[1:2, :] + b[0:sub, :]]
    ids = [j_id, j_id + sub, j_id + k]
    for i in range(2, sub):
        blocks.append(jnp.where(j_id < k // (i + 1), a[i:i + 1, :] + b[0:sub, :], -jnp.inf))
        ids.append(j_id + i * k)
    blocks.append(a[sub:k, :] + b[0:1, :])
    ids.append((j_id + sub) * k)
    return blocks, ids


def _route_kernel(q_ref, keys_ref, idx_ref, g_ref):
    k = PEER_TOPK
    half = PEER_QDIM // 2
    key_id = lax.broadcasted_iota(jnp.int32, (N_KEYS, LANES), 0).astype(F32)
    for t in range(q_ref.shape[0] // LANES):
        rows = pl.ds(t * LANES, LANES)
        sv, si = [], []
        for p in range(2):
            qh = q_ref[rows, p * half:(p + 1) * half]
            s = lax.dot_general(keys_ref[0, p], qh, _NT, preferred_element_type=F32)
            v, i = _top16_rows([s], [key_id])
            sv.append(v)
            si.append(i.astype(jnp.int32))
        top_s, pos = _top16_rows(*_pair_candidates(sv[0], sv[1]))
        pos = pos.astype(jnp.int32)
        idx = (_pick_rows(si[0], lax.shift_right_logical(pos, 4)) * N_KEYS
               + _pick_rows(si[1], lax.bitwise_and(pos, k - 1)))
        e = jnp.exp(top_s - top_s[0:1, :])
        idx_ref[:, t * LANES:(t + 1) * LANES] = idx
        g_ref[:, t * LANES:(t + 1) * LANES] = e / jnp.sum(e, axis=0, keepdims=True)


def _route(qp, sub_keys):
    t, _ = qp.shape
    tt = min(ROUTE_TILE, t)
    k = PEER_TOPK
    return pl.pallas_call(
        _route_kernel,
        grid=(t // tt, PEER_HEADS),
        in_specs=[pl.BlockSpec((tt, PEER_QDIM), lambda i, h: (i, h)),
                  pl.BlockSpec((1, 2, N_KEYS, PEER_QDIM // 2), lambda i, h: (h, 0, 0, 0))],
        out_specs=[pl.BlockSpec((k, tt), lambda i, h: (h, i)),
                   pl.BlockSpec((k, tt), lambda i, h: (h, i))],
        out_shape=[jax.ShapeDtypeStruct((PEER_HEADS * k, t), jnp.int32),
                   jax.ShapeDtypeStruct((PEER_HEADS * k, t), F32)],
        name="route",
    )(qp, sub_keys)


def _final_kernel(x1_ref, po_ref, mod_ref, g_ref, o_ref):
    x2 = x1_ref[0] + mod_ref[0][5:6, :] * po_ref[0]
    o_ref[0] = _rms_rows(x2) * g_ref[...]


def _final_kernel_into(x1_ref, po_ref, mod_ref, g_ref, prev_ref, o_ref):
    del prev_ref
    _final_kernel(x1_ref, po_ref, mod_ref, g_ref, o_ref)


def _final(x1, peer_out, mod, final_g, b0, out):
    bsz, s, d = x1.shape
    tm = min(ROW_TILE, s)
    tile = pl.BlockSpec((1, tm, d), lambda b, i: (b, i, 0))
    in_specs = [tile, tile, pl.BlockSpec((1, 6, d), lambda b, i: (b + b0, 0, 0)),
                pl.BlockSpec((1, d), lambda b, i: (0, 0))]
    args = (x1, peer_out, mod, final_g.reshape(1, d))
    if out is not None:
        in_specs.append(pl.BlockSpec(memory_space=pl.ANY))
        args += (out,)
    return pl.pallas_call(
        _final_kernel if out is None else _final_kernel_into,
        grid=(bsz, s // tm),
        in_specs=in_specs,
        out_specs=pl.BlockSpec((1, tm, d), lambda b, i: (b + b0, i, 0)),
        out_shape=jax.ShapeDtypeStruct((mod.shape[0], s, d), F32),
        input_output_aliases={} if out is None else {4: 0},
        name="final",
    )(*args)


GELU_C = math.sqrt(2.0 / math.pi)
SC_LANES = 16
SC_WORKERS = 32
SC_CORES = 2
SC_TOKEN_BLOCK = 32
SC_GATHER_DEPTH = 4
SC_ILV = plsc.PackFormat.INTERLEAVED
N_PAIRS = PEER_HEADS * PEER_TOPK


def _gelu_tanh(x):
    z = GELU_C * (x + 0.044715 * (x * x * x))
    t = 1.0 - 2.0 / (jnp.exp(2.0 * z) + 1.0)
    return x * (0.5 * (1.0 + t))


def _experts(h2, idx, g, u, v):
    t_total, dw = h2.shape
    d = 2 * dw
    tpw = t_total // SC_WORKERS
    tb = SC_TOKEN_BLOCK
    k = PEER_TOPK
    nh = PEER_HEADS
    n_items = tb * nh
    nbuf = SC_GATHER_DEPTH
    assert tpw % tb == 0 and n_items % nbuf == 0
    mesh = plsc.VectorSubcoreMesh(core_axis_name="c", subcore_axis_name="s")

    @functools.partial(
        pl.kernel, mesh=mesh,
        out_type=jax.ShapeDtypeStruct((t_total, d), F32),
        scratch_types=[
            pltpu.VMEM((tb, N_PAIRS), jnp.int32),
            pltpu.VMEM((tb, N_PAIRS), F32),
            pltpu.VMEM((tb, dw), jnp.int32),
            pltpu.VMEM((tb, d), F32),
            pltpu.VMEM((nbuf, k, dw), jnp.int32),
            pltpu.VMEM((nbuf, k, dw), jnp.int32),
            pltpu.SemaphoreType.DMA((nbuf,)),
            pltpu.SemaphoreType.DMA((nbuf,)),
        ],
        compiler_params=pltpu.CompilerParams(needs_layout_passes=False),
        name="experts",
    )
    def experts(h_hbm, idx_hbm, g_hbm, u_hbm, v_hbm, out_hbm,
                idx_b, g_b, h_b, out_b, urows, vrows, usem, vsem):
        wid = lax.axis_index("s") * SC_CORES + lax.axis_index("c")
        base = wid * tpw
        lane = lax.iota(jnp.int32, SC_LANES)

        def split(item):
            return lax.shift_right_logical(item, 3), lax.bitwise_and(item, nh - 1)

        def copies(item, b):
            tt, hd = split(item)
            ids = idx_b.at[tt, pl.ds(hd * k, k)]
            return (pltpu.make_async_copy(u_hbm.at[ids], urows.at[b], usem.at[b]),
                    pltpu.make_async_copy(v_hbm.at[ids], vrows.at[b], vsem.at[b]))

        def fetch(item, b):
            cu, cv = copies(item, b)
            cu.start()
            cv.start()

        def words(ref, *lead, off):
            return plsc.bitcast(ref[(*lead, pl.ds(off, SC_LANES))], BF16)

        def sum4_unpack(pr):
            return plsc.unpack((pr[0] + pr[1]) + (pr[2] + pr[3]), format=SC_ILV)

        def compute(item, b):
            tt, hd = split(item)
            cu, cv = copies(item, b)
            cu.wait()

            def ubody(j, accs):
                hs = [words(h_b, tt, off=(j * 4 + q) * SC_LANES) for q in range(4)]
                new = []
                for p, a in enumerate(accs):
                    lo, hi = sum4_unpack([words(urows, b, p, off=(j * 4 + q) * SC_LANES) * hs[q]
                                          for q in range(4)])
                    new.append(a + (lo + hi))
                return tuple(new)

            accs = plsc.parallel_loop(
                0, dw // (4 * SC_LANES), 1,
                carry=tuple(jnp.zeros((SC_LANES,), F32) for _ in range(k)))(ubody)
            s = jnp.zeros((SC_LANES,), F32)
            for p in range(k):
                s = jnp.where(lane == p, jnp.sum(accs[p]), s)
            c = g_b[tt, pl.ds(hd * k, k)] * _gelu_tanh(s)
            cbb = []
            for p in range(k):
                cp = jnp.full((SC_LANES,), c[p])
                cbb.append(plsc.pack(cp, cp, format=SC_ILV))
            cv.wait()

            @plsc.parallel_loop(0, dw // SC_LANES, 1, unroll=2)
            def _(ch):
                sa = pl.ds(ch * SC_LANES, SC_LANES)
                sb = pl.ds(dw + ch * SC_LANES, SC_LANES)
                los, his = [], []
                for grp in range(k // 4):
                    lo, hi = sum4_unpack([cbb[grp * 4 + q]
                                          * words(vrows, b, grp * 4 + q, off=ch * SC_LANES)
                                          for q in range(4)])
                    los.append(lo)
                    his.append(hi)
                out_b[tt, sa] = out_b[tt, sa] + ((los[0] + los[1]) + (los[2] + los[3]))
                out_b[tt, sb] = out_b[tt, sb] + ((his[0] + his[1]) + (his[2] + his[3]))

        @pl.loop(0, tpw // tb)
        def _(blk):
            t0 = base + blk * tb
            pltpu.sync_copy(idx_hbm.at[pl.ds(t0, tb)], idx_b)
            pltpu.sync_copy(g_hbm.at[pl.ds(t0, tb)], g_b)
            pltpu.sync_copy(h_hbm.at[pl.ds(t0, tb)], h_b)
            for b in range(nbuf - 1):
                fetch(b, b)

            @pl.loop(0, tb)
            def _(tt):
                @pl.loop(0, d // SC_LANES)
                def _(j):
                    out_b[tt, pl.ds(j * SC_LANES, SC_LANES)] = jnp.zeros((SC_LANES,), F32)

            @pl.loop(0, n_items, step=nbuf)
            def _(it):
                for b in range(nbuf):
                    ahead = it + b + nbuf - 1

                    @pl.when(ahead < n_items)
                    def _():
                        fetch(ahead, (b + nbuf - 1) % nbuf)

                    compute(it + b, b)

            pltpu.sync_copy(out_b, out_hbm.at[pl.ds(t0, tb)])

    return experts(h2, idx, g, u, v)


BATCH_CHUNKS = (1,) * 16


def kernel(x, c, w_ada, b_ada, norm1_g, norm2_g, w_in, conv_w, conv_b, b_igate, b_fgate, lam_q1, lam_k1, lam_q2, lam_k2, diff_sub_g, mlstm_norm_g, w_out, peer_w_q, peer_sub_keys, peer_u, peer_v, rel_bias, final_g):
    bsz, s, d = x.shape
    mod = _ada(c, w_ada[0], b_ada[0]).reshape(bsz, 6, d)

    w = w_in[0]
    w_qk = w[:, :2 * D_ATT].astype(BF16)
    w_vt = w[:, 2 * D_ATT:3 * D_ATT].T.astype(BF16)
    w_m =w[:, 3 * D_ATT:3 * D_ATT + 4 * D_MLSTM].astype(BF16)
    w_gt = w[:, 3 * D_ATT + 4 * D_MLSTM:].T.astype(BF16)
    w_o = w_out[0].astype(BF16)
    w_q = peer_w_q[0].astype(BF16)
    sub_keys = peer_sub_keys[0].astype(BF16)
    u_packed = _pack_table(peer_u[0])
    v_packed = _pack_table(peer_v[0])
    bias_tiles = _relbias(rel_bias)
    lam4 = jnp.stack([lam_q1[0], lam_k1[0], lam_q2[0], lam_k2[0]])
    gate_bias = jnp.concatenate([b_igate[0], b_fgate[0]])

    assert sum(BATCH_CHUNKS) == bsz
    out = None
    b0 = 0
    for nb in BATCH_CHUNKS:
        att_qk, att_vt, m_qkvo, gates = _inproj(x, mod, norm1_g[0], w_qk, w_vt, w_m, w_gt, b0, nb)
        att = _attn(att_qk, att_vt, bias_tiles, lam4, diff_sub_g[0])
        hm = _mlstm(m_qkvo, gates, gate_bias, conv_w[0], conv_b[0], mlstm_norm_g[0])
        x1, h2, qp = _outproj(att, hm, x, mod, norm2_g[0], w_o, w_q, b0)
        idx_t, g_t = _route(qp.reshape(nb * s, -1), sub_keys)
        peer_out = _experts(h2.reshape(nb * s, d // 2), idx_t.T, g_t.T, u_packed, v_packed)
        out = _final(x1, peer_out.reshape(nb, s, d), mod, final_g, b0, out)
        b0 += nb
    return out
```

```python
import functools
import math

import numpy as np
import jax
import jax.numpy as jnp
from jax import lax
from jax.experimental import pallas as pl
from jax.experimental.pallas import tpu as pltpu
from jax.experimental.pallas import tpu_sc as plsc

F32 = jnp.float32
BF16 = jnp.bfloat16

ATT_HEADS = 4
ATT_QK_DIM = 64
ATT_V_DIM = 128
D_ATT = ATT_HEADS * ATT_V_DIM
M_HEADS = 4
M_DIM = 128
D_MLSTM = M_HEADS * M_DIM
CONV_W = 4
N_BUCKETS = 32
MAX_DIST = 128
N_KEYS = 128
PEER_HEADS = 8
PEER_TOPK = 16
PEER_QDIM = 256
EPS = 1e-6
LAMBDA_INIT = 0.8 - 0.6 * math.exp(-0.3 * 0)

ATT_BLOCK = 256
M_CHUNK = 128
ROW_TILE = 512
ROUTE_TILE = 512
LANES = 128
VMEM_LIMIT = 48 * 1024 * 1024

_NT = (((1,), (1,)), ((), ()))


def _rms_rows(x):
    return x * lax.rsqrt(jnp.mean(x * x, axis=-1, keepdims=True) + EPS)


def _sigmoid(x):
    return 1.0 / (1.0 + jnp.exp(-x))


def _pack_bf16_halves(x):
    bits = lax.bitcast_convert_type(x.astype(BF16).astype(F32), jnp.int32)
    n = x.shape[-1] // 2
    return lax.bitwise_or(lax.shift_right_logical(bits[..., :n], 16),
                          lax.bitwise_and(bits[..., n:], jnp.int32(-65536)))


def _ada_kernel(c_ref, w_ref, b_ref, o_ref):
    c = c_ref[...]
    cond = c * _sigmoid(c)
    o_ref[...] = jnp.dot(cond, w_ref[...], preferred_element_type=F32) + b_ref[...]


def _ada(c, w, b):
    bsz, d = c.shape
    n = w.shape[1]
    return pl.pallas_call(
        _ada_kernel,
        grid=(n // d,),
        in_specs=[pl.BlockSpec((bsz, d), lambda j: (0, 0)),
                  pl.BlockSpec((d, d), lambda j: (0, j)),
                  pl.BlockSpec((1, d), lambda j: (0, j))],
        out_specs=pl.BlockSpec((bsz, d), lambda j: (0, j)),
        out_shape=jax.ShapeDtypeStruct((bsz, n), F32),
        name="ada",
    )(c, w, b.reshape(1, n))


def _inproj_kernel(x_ref, mod_ref, g_ref, wa_ref, wvt_ref, wm_ref, wg_ref,
                   oa_ref, ovt_ref, om_ref, og_ref):
    x = x_ref[0]
    mod = mod_ref[0]
    h = _rms_rows(x) * g_ref[...]
    h = h * (1.0 + mod[1:2, :]) + mod[0:1, :]
    hb = h.astype(BF16)
    oa_ref[0] = jnp.dot(hb, wa_ref[...], preferred_element_type=F32).astype(BF16)
    ovt_ref[0] = lax.dot_general(wvt_ref[...], hb, _NT, preferred_element_type=F32).astype(BF16)
    om_ref[0] = jnp.dot(hb, wm_ref[...], preferred_element_type=F32).astype(BF16)
    og_ref[0] = lax.dot_general(wg_ref[...], hb, _NT, preferred_element_type=F32)


def _inproj(x, mod, g, w_qk, w_vt, w_m, w_gt, b0, bsz):
    _, s, d = x.shape
    tm = min(ROW_TILE, s)
    na, nv, nm, ng = w_qk.shape[1], w_vt.shape[0], w_m.shape[1], w_gt.shape[0]
    const = lambda shape: pl.BlockSpec(shape, lambda b, i: (0, 0))
    return pl.pallas_call(
        _inproj_kernel,
        grid=(bsz, s // tm),
        in_specs=[pl.BlockSpec((1, tm, d), lambda b, i: (b + b0, i, 0)),
                  pl.BlockSpec((1, 6, d), lambda b, i: (b + b0, 0, 0)),
                  const((1, d)), const((d, na)), const((nv, d)), const((d, nm)), const((ng, d))],
        out_specs=[pl.BlockSpec((1, tm, na), lambda b, i: (b, i, 0)),
                   pl.BlockSpec((1, nv, tm), lambda b, i: (b, 0, i)),
                   pl.BlockSpec((1, tm, nm), lambda b, i: (b, i, 0)),
                   pl.BlockSpec((1, ng, tm), lambda b, i: (b, 0, i))],
        out_shape=[jax.ShapeDtypeStruct((bsz, s, na), BF16),
                   jax.ShapeDtypeStruct((bsz, nv, s), BF16),
                   jax.ShapeDtypeStruct((bsz, s, nm), BF16),
                   jax.ShapeDtypeStruct((bsz, ng, s), F32)],
        compiler_params=pltpu.CompilerParams(vmem_limit_bytes=VMEM_LIMIT),
        name="inproj",
    )(x, mod, g.reshape(1, d), w_qk, w_vt, w_m, w_gt)


def _rel_buckets():
    n = np.arange(2 * ATT_BLOCK)
    max_exact = N_BUCKETS // 2
    nf = np.maximum(n, 1).astype(np.float64)
    large = max_exact + (np.log(nf / max_exact) / math.log(MAX_DIST / max_exact)
                         * (N_BUCKETS - max_exact)).astype(np.int64)
    large = np.minimum(large, N_BUCKETS - 1)
    bucket = np.where(n < max_exact, n, large)
    qk = np.arange(ATT_BLOCK)[None, :] - np.arange(ATT_BLOCK)[:, None]
    tiles = np.stack([bucket[np.maximum(qk, 0)], bucket[ATT_BLOCK + qk]])
    return tiles.astype(np.int32)


def _relbias_kernel(rb_ref, bk_ref, o_ref):
    h = pl.program_id(0)
    bk = bk_ref[...]
    acc = jnp.zeros(bk.shape, F32)
    for b in range(N_BUCKETS):
        acc = jnp.where(bk == b, rb_ref[b, h], acc)
    o_ref[0] = acc


def _relbias(rel_bias):
    tiles = jnp.asarray(_rel_buckets())
    return pl.pallas_call(
        _relbias_kernel,
        grid=(ATT_HEADS,),
        in_specs=[pl.BlockSpec(memory_space=pltpu.SMEM),
                  pl.BlockSpec((2, ATT_BLOCK, ATT_BLOCK), lambda h: (0, 0, 0))],
        out_specs=pl.BlockSpec((1, 2, ATT_BLOCK, ATT_BLOCK), lambda h: (h, 0, 0, 0)),
        out_shape=jax.ShapeDtypeStruct((ATT_HEADS, 2, ATT_BLOCK, ATT_BLOCK), F32),
        name="relbias",
    )(rel_bias, tiles)


def _attn_kernel(q_ref, k_ref, vt_ref, bias_ref, lam_ref, subg_ref, o_ref,
                 qz_ref, m_ref, l_ref, acc_ref):
    tq = ATT_BLOCK
    qi = pl.program_id(2)
    scale = ATT_QK_DIM ** -0.5

    q = q_ref[0]
    lane = lax.broadcasted_iota(jnp.int32, q.shape, 1)
    zero = jnp.zeros_like(q)
    qz_ref[0:tq, :] = jnp.where(lane < ATT_QK_DIM, q, zero)
    qz_ref[tq:2 * tq, :] = jnp.where(lane >= ATT_QK_DIM, q, zero)
    m_ref[...] = jnp.full(m_ref.shape, -jnp.inf, F32)
    l_ref[...] = jnp.zeros(l_ref.shape, F32)
    acc_ref[...] = jnp.zeros(acc_ref.shape, F32)

    def step(j, bias, masked):
        start = pl.multiple_of(j * tq, tq)
        k = k_ref[0, pl.ds(start, tq), :]
        vt = vt_ref[0, :, pl.ds(start, tq)]
        s = lax.dot_general(k, qz_ref[...], _NT, preferred_element_type=F32) * scale
        if isinstance(bias, tuple):
            s = s + jnp.concatenate([bias[0], bias[0]], axis=1)
        else:
            s = s + bias
        if masked:
            key = lax.broadcasted_iota(jnp.int32, (tq, tq), 0)
            qry = lax.broadcasted_iota(jnp.int32, (tq, tq), 1)
            keep = jnp.concatenate([key <= qry, key <= qry], axis=1)
            s = jnp.where(keep, s, jnp.finfo(F32).min)
        m_old = m_ref[...]
        m_new = jnp.maximum(m_old, jnp.max(s, axis=0, keepdims=True))
        alpha = jnp.exp(m_old - m_new)
        p = jnp.exp(s - m_new)
        l_ref[...] = alpha * l_ref[...] + jnp.sum(p, axis=0, keepdims=True)
        acc_ref[...] = alpha * acc_ref[...] + jnp.dot(vt, p.astype(BF16), preferred_element_type=F32)
        m_ref[...] = m_new

    far_bias = bias_ref[0, 1, 0:1, tq - 1:tq]

    def far_body(j, carry):
        step(j, far_bias, False)
        return carry

    lax.fori_loop(0, jnp.maximum(qi - 1, 0), far_body, 0)

    @pl.when(qi >= 1)
    def _():
        step(qi - 1, (bias_ref[0, 1],), False)

    step(qi, (bias_ref[0, 0],), True)

    lam = (jnp.exp(jnp.sum(lam_ref[0:1, :] * lam_ref[1:2, :], axis=-1, keepdims=True))
           - jnp.exp(jnp.sum(lam_ref[2:3, :] * lam_ref[3:4, :], axis=-1, keepdims=True))
           + LAMBDA_INIT)
    ot = acc_ref[...] / l_ref[...]
    o = (ot[:, 0:tq] - lam * ot[:, tq:2 * tq]).T
    o = _rms_rows(o) * subg_ref[...] * (1.0 - LAMBDA_INIT)
    o_ref[0] = o.astype(o_ref.dtype)


def _attn(att_qk, att_vt, bias_tiles, lam4, sub_g):
    bsz, s, _ = att_qk.shape
    tq = ATT_BLOCK
    nh = ATT_HEADS
    return pl.pallas_call(
        _attn_kernel,
        grid=(bsz, nh, s // tq),
        in_specs=[pl.BlockSpec((1, tq, ATT_V_DIM), lambda b, h, i: (b, i, h)),
                  pl.BlockSpec((1, s, ATT_V_DIM), lambda b, h, i: (b, 0, nh + h)),
                  pl.BlockSpec((1, ATT_V_DIM, s), lambda b, h, i: (b, h, 0)),
                  pl.BlockSpec((1, 2, tq, tq), lambda b, h, i: (h, 0, 0, 0)),
                  pl.BlockSpec((4, ATT_QK_DIM), lambda b, h, i: (0, 0)),
                  pl.BlockSpec((1, ATT_V_DIM), lambda b, h, i: (0, 0))],
        out_specs=pl.BlockSpec((1, tq, ATT_V_DIM), lambda b, h, i: (b, i, h)),
        out_shape=jax.ShapeDtypeStruct((bsz, s, D_ATT), BF16),
        scratch_shapes=[pltpu.VMEM((2 * tq, ATT_V_DIM), BF16),
                        pltpu.VMEM((1, 2 * tq), F32),
                        pltpu.VMEM((1, 2 * tq), F32),
                        pltpu.VMEM((ATT_V_DIM, 2 * tq), F32)],
        compiler_params=pltpu.CompilerParams(vmem_limit_bytes=VMEM_LIMIT),
        name="attn",
    )(att_qk, att_qk, att_vt, bias_tiles, lam4, sub_g.reshape(1, ATT_V_DIM))


def _mlstm_kernel(q_ref, k_ref, v_ref, o_ref, gi_ref, gf_ref, bias_ref, cwq_ref, cwk_ref,
                  cbq_ref, cbk_ref, ng_ref, out_ref, qs_ref, ks_ref, b_ref, ig_ref):
    s = q_ref.shape[1]
    L = M_CHUNK
    nc = s // L
    h = pl.program_id(1)

    row = lax.broadcasted_iota(jnp.int32, (s, M_DIM), 0)

    def conv_silu(x_ref, w_ref, cb_ref):
        x = x_ref[0].astype(F32)
        w = w_ref[...]
        out = None
        for j in range(CONV_W):
            shift = CONV_W - 1 - j
            xs = x if shift == 0 else jnp.where(row >= shift, pltpu.roll(x, shift, 0), 0.0)
            term = xs * w[j:j + 1, :]
            out = term if out is None else out + term
        out = out + cb_ref[...]
        return out * _sigmoid(out)

    qs_ref[...] = conv_silu(q_ref, cwq_ref, cbq_ref).astype(BF16)
    ks_ref[...] = (conv_silu(k_ref, cwk_ref, cbk_ref) * (M_DIM ** -0.5)).astype(BF16)

    ig = gi_ref[0, 0] + bias_ref[h]
    f = gf_ref[0, 0] + bias_ref[M_HEADS + h]
    logf = jnp.minimum(f, 0.0) - jnp.log(1.0 + jnp.exp(-jnp.abs(f)))
    r = lax.broadcasted_iota(jnp.int32, (L, L), 0)
    c = lax.broadcasted_iota(jnp.int32, (L, L), 1)
    tri = (r <= c).astype(F32)
    b_ref[...] = jnp.dot(logf, tri, preferred_element_type=F32,
                         precision=lax.Precision.HIGHEST)
    ig_ref[...] = ig
    eye = r == c
    causal = c <= r

    def to_col(x_row):
        return jnp.sum(jnp.where(eye, x_row, 0.0), axis=1, keepdims=True)

    def chunk(ci, carry):
        C, n, m = carry
        start = pl.multiple_of(ci * L, L)
        qc = qs_ref[pl.ds(start, L), :]
        kc = ks_ref[pl.ds(start, L), :]
        vc = v_ref[0, pl.ds(start, L), :]
        b_r = b_ref[pl.ds(ci, 1), :]
        ig_r = ig_ref[pl.ds(ci, 1), :]
        b_last = b_r[:, L - 1:L]
        a_r = b_last - b_r + ig_r
        b_c = to_col(b_r)
        a_c = to_col(a_r)

        logd = jnp.where(causal, b_c - b_r + ig_r, -jnp.inf)
        m_inter = b_c + m
        m_j = jnp.maximum(jnp.max(logd, axis=1, keepdims=True), m_inter)
        w = jnp.exp(logd - m_j)
        sqk = lax.dot_general(qc, kc, _NT, preferred_element_type=F32) * w
        inter = jnp.exp(m_inter - m_j)
        num = (jnp.dot(sqk.astype(BF16), vc, preferred_element_type=F32)
               + inter * jnp.dot(qc, C.astype(BF16), preferred_element_type=F32))
        den = (jnp.sum(sqk, axis=1, keepdims=True)
               + inter * jnp.sum(qc.astype(F32) * n, axis=1, keepdims=True))
        hc = num / jnp.maximum(jnp.abs(den), jnp.exp(-m_j))

        og = _sigmoid(o_ref[0, pl.ds(start, L), :].astype(F32))
        out_ref[0, pl.ds(start, L), :] = (_rms_rows(og * hc) * ng_ref[...]).astype(out_ref.dtype)

        m_new = jnp.maximum(b_last + m, jnp.max(a_r, axis=1, keepdims=True))
        decay = jnp.exp(b_last + m - m_new)
        kw = kc.astype(F32) * jnp.exp(a_c - m_new)
        C_new = decay * C + jnp.dot(kw.T.astype(BF16), vc, preferred_element_type=F32)
        n_new = decay * n + jnp.sum(kw, axis=0, keepdims=True)
        return C_new, n_new, m_new

    init = (jnp.zeros((M_DIM, M_DIM), F32), jnp.zeros((1, M_DIM), F32), jnp.zeros((1, 1), F32))
    lax.fori_loop(0, nc, chunk, init)


def _mlstm(m_qkvo, gates, gate_bias, conv_w, conv_b, norm_g):
    bsz, s, _ = m_qkvo.shape
    L = M_CHUNK
    nc = s // L
    nh = M_HEADS
    d = M_DIM
    g4 = gates.reshape(bsz, 2 * nh, nc, L)
    seq = lambda off: pl.BlockSpec((1, s, d), lambda b, h: (b, 0, off + h))
    return pl.pallas_call(
        _mlstm_kernel,
        grid=(bsz, nh),
        in_specs=[seq(0), seq(nh), seq(2 * nh), seq(3 * nh),
                  pl.BlockSpec((1, 1, nc, L), lambda b, h: (b, h, 0, 0)),
                  pl.BlockSpec((1, 1, nc, L), lambda b, h: (b, nh + h, 0, 0)),
                  pl.BlockSpec(memory_space=pltpu.SMEM),
                  pl.BlockSpec((CONV_W, d), lambda b, h: (0, h)),
                  pl.BlockSpec((CONV_W, d), lambda b, h: (0, nh + h)),
                  pl.BlockSpec((1, d), lambda b, h: (0, h)),
                  pl.BlockSpec((1, d), lambda b, h: (0, nh + h)),
                  pl.BlockSpec((1, d), lambda b, h: (0, h))],
        out_specs=pl.BlockSpec((1, s, d), lambda b, h: (b, 0, h)),
        out_shape=jax.ShapeDtypeStruct((bsz, s, D_MLSTM), BF16),
        scratch_shapes=[pltpu.VMEM((s, d), BF16), pltpu.VMEM((s, d), BF16),
                        pltpu.VMEM((nc, L), F32), pltpu.VMEM((nc, L), F32)],
        compiler_params=pltpu.CompilerParams(vmem_limit_bytes=VMEM_LIMIT),
        name="mlstm",
    )(m_qkvo, m_qkvo, m_qkvo, m_qkvo, g4, g4, gate_bias, conv_w, conv_w,
      conv_b.reshape(1, -1), conv_b.reshape(1, -1), norm_g.reshape(1, -1))


def _outproj_kernel(att_ref, hm_ref, x_ref, mod_ref, g2_ref, wo_ref, wq_ref,
                    x1_ref, h2_ref, qp_ref):
    mod = mod_ref[0]
    y = (jnp.dot(att_ref[0], wo_ref[0:D_ATT, :], preferred_element_type=F32)
         + jnp.dot(hm_ref[0], wo_ref[D_ATT:, :], preferred_element_type=F32))
    x1 = x_ref[0] + mod[2:3, :] * y
    x1_ref[0] = x1
    h2 = _rms_rows(x1) * g2_ref[...]
    h2 = h2 * (1.0 + mod[4:5, :]) + mod[3:4, :]
    hb = h2.astype(BF16)
    h2_ref[0] = _pack_bf16_halves(hb)
    qp_ref[0] = jnp.dot(hb, wq_ref[...], preferred_element_type=F32).astype(BF16)


def _outproj(att, hm, x, mod, g2, w_out, w_q, b0):
    bsz, s, _ = att.shape
    d = x.shape[-1]
    tm = min(ROW_TILE, s)
    nq = w_q.shape[1]
    tile = lambda n: pl.BlockSpec((1, tm, n), lambda b, i: (b, i, 0))
    return pl.pallas_call(
        _outproj_kernel,
        grid=(bsz, s // tm),
        in_specs=[tile(D_ATT), tile(D_MLSTM),
                  pl.BlockSpec((1, tm, d), lambda b, i: (b + b0, i, 0)),
                  pl.BlockSpec((1, 6, d), lambda b, i: (b + b0, 0, 0)),
                  pl.BlockSpec((1, d), lambda b, i: (0, 0)),
                  pl.BlockSpec((d, d), lambda b, i: (0, 0)),
                  pl.BlockSpec((d, nq), lambda b, i: (0, 0))],
        out_specs=[tile(d), tile(d // 2), tile(nq)],
        out_shape=[jax.ShapeDtypeStruct((bsz, s, d), F32),
                   jax.ShapeDtypeStruct((bsz, s, d // 2), jnp.int32),
                   jax.ShapeDtypeStruct((bsz, s, nq), BF16)],
        compiler_params=pltpu.CompilerParams(vmem_limit_bytes=VMEM_LIMIT),
        name="outproj",
    )(att, hm, x, mod, g2.reshape(1, d), w_out, w_q)


def _top16_rows(blocks, ids):
    big = jnp.float32(1 << 20)
    vals, pos = [], []
    for _ in range(PEER_TOPK):
        m = functools.reduce(jnp.maximum, blocks)
        m = jnp.max(m, axis=0, keepdims=True)
        cand = functools.reduce(jnp.minimum,
                                [jnp.where(b == m, i, big) for b, i in zip(blocks, ids)])
        p = jnp.min(cand, axis=0, keepdims=True)
        blocks = [jnp.where(i == p, -jnp.inf, b) for b, i in zip(blocks, ids)]
        vals.append(m)
        pos.append(p)
    return jnp.concatenate(vals, axis=0), jnp.concatenate(pos, axis=0)


def _pick_rows(table, sel):
    out = jnp.zeros(sel.shape, table.dtype)
    for r in range(PEER_TOPK):
        out = jnp.where(sel == r, table[r:r + 1, :], out)
    return out


def _pair_candidates(a, b):
    k, sub = PEER_TOPK, 8
    j_id = lax.broadcasted_iota(jnp.int32, (sub, LANES), 0).astype(F32)
    blocks = [a[0:1, :] + b[0:sub, :], a[0:1, :] + b[sub:k, :], a[1:2, :] + b[0:sub, :]]
    ids = [j_id, j_id + sub, j_id + k]
    for i in range(2, sub):
        blocks.append(jnp.where(j_id < k // (i + 1), a[i:i + 1, :] + b[0:sub, :], -jnp.inf))
        ids.append(j_id + i * k)
    blocks.append(a[sub:k, :] + b[0:1, :])
    ids.append((j_id + sub) * k)
    return blocks, ids


def _route_kernel(q_ref, keys_ref, idx_ref, g_ref):
    k = PEER_TOPK
    half = PEER_QDIM // 2
    key_id = lax.broadcasted_iota(jnp.int32, (N_KEYS, LANES), 0).astype(F32)
    for t in range(q_ref.shape[0] // LANES):
        rows = pl.ds(t * LANES, LANES)
        sv, si = [], []
        for p in range(2):
            qh = q_ref[rows, p * half:(p + 1) * half]
            s = lax.dot_general(keys_ref[0, p], qh, _NT, preferred_element_type=F32)
            v, i = _top16_rows([s], [key_id])
            sv.append(v)
            si.append(i.astype(jnp.int32))
        top_s, pos = _top16_rows(*_pair_candidates(sv[0], sv[1]))
        pos = pos.astype(jnp.int32)
        idx = (_pick_rows(si[0], lax.shift_right_logical(pos, 4)) * N_KEYS
               + _pick_rows(si[1], lax.bitwise_and(pos, k - 1)))
        e = jnp.exp(top_s - top_s[0:1, :])
        idx_ref[:, t * LANES:(t + 1) * LANES] = idx
        g_ref[:, t * LANES:(t + 1) * LANES] = e / jnp.sum(e, axis=0, keepdims=True)


def _route(qp, sub_keys):
    t, _ = qp.shape
    tt = min(ROUTE_TILE, t)
    k = PEER_TOPK
    return pl.pallas_call(
        _route_kernel,
        grid=(t // tt, PEER_HEADS),
        in_specs=[pl.BlockSpec((tt, PEER_QDIM), lambda i, h: (i, h)),
                  pl.BlockSpec((1, 2, N_KEYS, PEER_QDIM // 2), lambda i, h: (h, 0, 0, 0))],
        out_specs=[pl.BlockSpec((k, tt), lambda i, h: (h, i)),
                   pl.BlockSpec((k, tt), lambda i, h: (h, i))],
        out_shape=[jax.ShapeDtypeStruct((PEER_HEADS * k, t), jnp.int32),
                   jax.ShapeDtypeStruct((PEER_HEADS * k, t), F32)],
        name="route",
    )(qp, sub_keys)


def _final_kernel(x1_ref, po_ref, mod_ref, g_ref, o_ref):
    x2 = x1_ref[0] + mod_ref[0][5:6, :] * po_ref[0]
    o_ref[0] = _rms_rows(x2) * g_ref[...]


def _final_kernel_into(x1_ref, po_ref, mod_ref, g_ref, prev_ref, o_ref):
    del prev_ref
    _final_kernel(x1_ref, po_ref, mod_ref, g_ref, o_ref)


def _final(x1, peer_out, mod, final_g, b0, out):
    bsz, s, d = x1.shape
    tm = min(ROW_TILE, s)
    tile = pl.BlockSpec((1, tm, d), lambda b, i: (b, i, 0))
    in_specs = [tile, tile, pl.BlockSpec((1, 6, d), lambda b, i: (b + b0, 0, 0)),
                pl.BlockSpec((1, d), lambda b, i: (0, 0))]
    args = (x1, peer_out, mod, final_g.reshape(1, d))
    if out is not None:
        in_specs.append(pl.BlockSpec(memory_space=pl.ANY))
        args += (out,)
    return pl.pallas_call(
        _final_kernel if out is None else _final_kernel_into,
        grid=(bsz, s // tm),
        in_specs=in_specs,
        out_specs=pl.BlockSpec((1, tm, d), lambda b, i: (b + b0, i, 0)),
        out_shape=jax.ShapeDtypeStruct((mod.shape[0], s, d), F32),
        input_output_aliases={} if out is None else {4: 0},
        name="final",
    )(*args)


GELU_C = math.sqrt(2.0 / math.pi)
SC_LANES = 16
SC_WORKERS = 32
SC_CORES = 2
SC_TOKEN_BLOCK = 32
SC_GATHER_DEPTH = 4
SC_PACK_ROWS = 16
SC_ILV = plsc.PackFormat.INTERLEAVED
N_PAIRS = PEER_HEADS * PEER_TOPK


def _gelu_tanh(x):
    z = GELU_C * (x + 0.044715 * (x * x * x))
    t = 1.0 - 2.0 / (jnp.exp(2.0 * z) + 1.0)
    return x * (0.5 * (1.0 + t))


def _pack_tables(u, v):
    n_rows, d = u.shape
    dw = d // 2
    rb = SC_PACK_ROWS
    nb = SC_GATHER_DEPTH
    rpw = n_rows // SC_WORKERS
    nblk = rpw // rb
    assert rpw % rb == 0 and nblk % nb == 0 and nb % 2 == 0
    mesh = plsc.VectorSubcoreMesh(core_axis_name="c", subcore_axis_name="s")
    out_t = jax.ShapeDtypeStruct((n_rows, dw), jnp.int32)

    @functools.partial(
        pl.kernel, mesh=mesh, out_type=(out_t, out_t),
        scratch_types=[pltpu.VMEM((nb, rb, d), F32), pltpu.VMEM((2, rb, dw), jnp.int32),
                       pltpu.SemaphoreType.DMA((nb,)), pltpu.SemaphoreType.DMA((2,))],
        compiler_params=pltpu.CompilerParams(needs_layout_passes=False),
        name="pack_tables",
    )
    def pack(u_hbm, v_hbm, uo_hbm, vo_hbm, src, dst, isem, osem):
        wid = lax.axis_index("s") * SC_CORES + lax.axis_index("c")
        base = wid * rpw
        for tab, out in ((u_hbm, uo_hbm), (v_hbm, vo_hbm)):
            def in_copy(blk, b):
                return pltpu.make_async_copy(tab.at[pl.ds(base + blk * rb, rb)], src.at[b], isem.at[b])

            def out_copy(blk, s):
                return pltpu.make_async_copy(dst.at[s], out.at[pl.ds(base + blk * rb, rb)], osem.at[s])

            for b in range(nb - 1):
                in_copy(b, b).start()

            @pl.loop(0, nblk, step=nb)
            def _(blk):
                for b in range(nb):
                    ahead = blk + b + nb - 1

                    @pl.when(ahead < nblk)
                    def _():
                        in_copy(ahead, (b + nb - 1) % nb).start()

                    in_copy(blk + b, b).wait()
                    s = b % 2

                    @pl.when(blk + b >= 2)
                    def _():
                        out_copy(blk + b - 2, s).wait()

                    @pl.loop(0, rb)
                    def _(r):
                        @plsc.parallel_loop(0, dw // SC_LANES, 1, unroll=4)
                        def _(c):
                            lo = src[b, r, pl.ds(c * SC_LANES, SC_LANES)]
                            hi = src[b, r, pl.ds(dw + c * SC_LANES, SC_LANES)]
                            dst[s, r, pl.ds(c * SC_LANES, SC_LANES)] = plsc.bitcast(
                                plsc.pack(lo, hi, format=SC_ILV), jnp.int32)

                    out_copy(blk + b, s).start()

            out_copy(nblk - 2, 0).wait()
            out_copy(nblk - 1, 1).wait()

    return pack(u, v)


def _experts(h2, idx, g, u, v):
    t_total, dw = h2.shape
    d = 2 * dw
    tpw = t_total // SC_WORKERS
    tb = SC_TOKEN_BLOCK
    k = PEER_TOPK
    nh = PEER_HEADS
    n_items = tb * nh
    nbuf = SC_GATHER_DEPTH
    assert tpw % tb == 0 and n_items % nbuf == 0
    mesh = plsc.VectorSubcoreMesh(core_axis_name="c", subcore_axis_name="s")

    @functools.partial(
        pl.kernel, mesh=mesh,
        out_type=jax.ShapeDtypeStruct((t_total, d), F32),
        scratch_types=[
            pltpu.VMEM((tb, N_PAIRS), jnp.int32),
            pltpu.VMEM((tb, N_PAIRS), F32),
            pltpu.VMEM((tb, dw), jnp.int32),
            pltpu.VMEM((tb, d), F32),
            pltpu.VMEM((nbuf, k, dw), jnp.int32),
            pltpu.VMEM((nbuf, k, dw), jnp.int32),
            pltpu.SemaphoreType.DMA((nbuf,)),
            pltpu.SemaphoreType.DMA((nbuf,)),
        ],
        compiler_params=pltpu.CompilerParams(needs_layout_passes=False),
        name="experts",
    )
    def experts(h_hbm, idx_hbm, g_hbm, u_hbm, v_hbm, out_hbm,
                idx_b, g_b, h_b, out_b, urows, vrows, usem, vsem):
        wid = lax.axis_index("s") * SC_CORES + lax.axis_index("c")
        base = wid * tpw
        lane = lax.iota(jnp.int32, SC_LANES)

        def split(item):
            return lax.shift_right_logical(item, 3), lax.bitwise_and(item, nh - 1)

        def copies(item, b):
            tt, hd = split(item)
            ids = idx_b.at[tt, pl.ds(hd * k, k)]
            return (pltpu.make_async_copy(u_hbm.at[ids], urows.at[b], usem.at[b]),
                    pltpu.make_async_copy(v_hbm.at[ids], vrows.at[b], vsem.at[b]))

        def fetch(item, b):
            cu, cv = copies(item, b)
            cu.start()
            cv.start()

        def words(ref, *lead, off):
            return plsc.bitcast(ref[(*lead, pl.ds(off, SC_LANES))], BF16)

        def sum4_unpack(pr):
            return plsc.unpack((pr[0] + pr[1]) + (pr[2] + pr[3]), format=SC_ILV)

        def compute(item, b):
            tt, hd = split(item)
            cu, cv = copies(item, b)
            cu.wait()

            def ubody(j, accs):
                hs = [words(h_b, tt, off=(j * 4 + q) * SC_LANES) for q in range(4)]
                new = []
                for p, a in enumerate(accs):
                    lo, hi = sum4_unpack([words(urows, b, p, off=(j * 4 + q) * SC_LANES) * hs[q]
                                          for q in range(4)])
                    new.append(a + (lo + hi))
                return tuple(new)

            accs = plsc.parallel_loop(
                0, dw // (4 * SC_LANES), 1,
                carry=tuple(jnp.zeros((SC_LANES,), F32) for _ in range(k)))(ubody)
            s = jnp.zeros((SC_LANES,), F32)
            for p in range(k):
                s = jnp.where(lane == p, jnp.sum(accs[p]), s)
            c = g_b[tt, pl.ds(hd * k, k)] * _gelu_tanh(s)
            cbb = []
            for p in range(k):
                cp = jnp.full((SC_LANES,), c[p])
                cbb.append(plsc.pack(cp, cp, format=SC_ILV))
            cv.wait()

            @plsc.parallel_loop(0, dw // SC_LANES, 1, unroll=2)
            def _(ch):
                sa = pl.ds(ch * SC_LANES, SC_LANES)
                sb = pl.ds(dw + ch * SC_LANES, SC_LANES)
                los, his = [], []
                for grp in range(k // 4):
                    lo, hi = sum4_unpack([cbb[grp * 4 + q]
                                          * words(vrows, b, grp * 4 + q, off=ch * SC_LANES)
                                          for q in range(4)])
                    los.append(lo)
                    his.append(hi)
                out_b[tt, sa] = out_b[tt, sa] + ((los[0] + los[1]) + (los[2] + los[3]))
                out_b[tt, sb] = out_b[tt, sb] + ((his[0] + his[1]) + (his[2] + his[3]))

        @pl.loop(0, tpw // tb)
        def _(blk):
            t0 = base + blk * tb
            pltpu.sync_copy(idx_hbm.at[pl.ds(t0, tb)], idx_b)
            pltpu.sync_copy(g_hbm.at[pl.ds(t0, tb)], g_b)
            pltpu.sync_copy(h_hbm.at[pl.ds(t0, tb)], h_b)
            for b in range(nbuf - 1):
                fetch(b, b)

            @pl.loop(0, tb)
            def _(tt):
                @pl.loop(0, d // SC_LANES)
                def _(j):
                    out_b[tt, pl.ds(j * SC_LANES, SC_LANES)] = jnp.zeros((SC_LANES,), F32)

            @pl.loop(0, n_items, step=nbuf)
            def _(it):
                for b in range(nbuf):
                    ahead = it + b + nbuf - 1

                    @pl.when(ahead < n_items)
                    def _():
                        fetch(ahead, (b + nbuf - 1) % nbuf)

                    compute(it + b, b)

            pltpu.sync_copy(out_b, out_hbm.at[pl.ds(t0, tb)])

    return experts(h2, idx, g, u, v)


BATCH_CHUNKS = (1,) * 16


def kernel(x, c, w_ada, b_ada, norm1_g, norm2_g, w_in, conv_w, conv_b, b_igate, b_fgate, lam_q1, lam_k1, lam_q2, lam_k2, diff_sub_g, mlstm_norm_g, w_out, peer_w_q, peer_sub_keys, peer_u, peer_v, rel_bias, final_g):
    bsz, s, d = x.shape
    mod = _ada(c, w_ada[0], b_ada[0]).reshape(bsz, 6, d)

    w = w_in[0]
    w_qk = w[:, :2 * D_ATT].astype(BF16)
    w_vt = w[:, 2 * D_ATT:3 * D_ATT].T.astype(BF16)
    w_m =w[:, 3 * D_ATT:3 * D_ATT + 4 * D_MLSTM].astype(BF16)
    w_gt = w[:, 3 * D_ATT + 4 * D_MLSTM:].T.astype(BF16)
    w_o = w_out[0].astype(BF16)
    w_q = peer_w_q[0].astype(BF16)
    sub_keys = peer_sub_keys[0].astype(BF16)
    u_packed, v_packed = _pack_tables(peer_u[0], peer_v[0])
    bias_tiles = _relbias(rel_bias)
    lam4 = jnp.stack([lam_q1[0], lam_k1[0], lam_q2[0], lam_k2[0]])
    gate_bias = jnp.concatenate([b_igate[0], b_fgate[0]])

    assert sum(BATCH_CHUNKS) == bsz
    out = None
    b0 = 0
    for nb in BATCH_CHUNKS:
        att_qk, att_vt, m_qkvo, gates = _inproj(x, mod, norm1_g[0], w_qk, w_vt, w_m, w_gt, b0, nb)
        att = _attn(att_qk, att_vt, bias_tiles, lam4, diff_sub_g[0])
        hm = _mlstm(m_qkvo, gates, gate_bias, conv_w[0], conv_b[0], mlstm_norm_g[0])
        x1, h2, qp = _outproj(att, hm, x, mod, norm2_g[0], w_o, w_q, b0)
        idx_t, g_t = _route(qp.reshape(nb * s, -1), sub_keys)
        peer_out = _experts(h2.reshape(nb * s, d // 2), idx_t.T, g_t.T, u_packed, v_packed)
        out = _final(x1, peer_out.reshape(nb, s, d), mod, final_g, b0, out)
        b0 += nb
    return out
```

```python
import functools
import math

import numpy as np
import jax
import jax.numpy as jnp
from jax import lax
from jax.experimental import pallas as pl
from jax.experimental.pallas import tpu as pltpu
from jax.experimental.pallas import tpu_sc as plsc

F32 = jnp.float32
BF16 = jnp.bfloat16

ATT_HEADS = 4
ATT_QK_DIM = 64
ATT_V_DIM = 128
D_ATT = ATT_HEADS * ATT_V_DIM
M_HEADS = 4
M_DIM = 128
D_MLSTM = M_HEADS * M_DIM
CONV_W = 4
N_BUCKETS = 32
MAX_DIST = 128
N_KEYS = 128
PEER_HEADS = 8
PEER_TOPK = 16
PEER_QDIM = 256
EPS = 1e-6
LAMBDA_INIT = 0.8 - 0.6 * math.exp(-0.3 * 0)

ATT_BLOCK = 256
M_CHUNK = 128
ROW_TILE = 512
ROUTE_TILE = 512
LANES = 128
V7X_VMEM_BYTES = 64 * 1024 * 1024
VMEM_LIMIT = V7X_VMEM_BYTES * 3 // 4

_NT = (((1,), (1,)), ((), ()))


def _rms_rows(x):
    return x * lax.rsqrt(jnp.mean(x * x, axis=-1, keepdims=True) + EPS)


def _sigmoid(x):
    return 1.0 / (1.0 + jnp.exp(-x))


def _pack_bf16_halves(x):
    bits = lax.bitcast_convert_type(x.astype(BF16).astype(F32), jnp.int32)
    n = x.shape[-1] // 2
    return lax.bitwise_or(lax.shift_right_logical(bits[..., :n], 16),
                          lax.bitwise_and(bits[..., n:], jnp.int32(-65536)))


def _ada_kernel(c_ref, w_ref, b_ref, o_ref):
    c = c_ref[...]
    cond = c * _sigmoid(c)
    o_ref[...] = jnp.dot(cond, w_ref[...], preferred_element_type=F32) + b_ref[...]


def _ada(c, w, b):
    bsz, d = c.shape
    n = w.shape[1]
    return pl.pallas_call(
        _ada_kernel,
        grid=(n // d,),
        in_specs=[pl.BlockSpec((bsz, d), lambda j: (0, 0)),
                  pl.BlockSpec((d, d), lambda j: (0, j)),
                  pl.BlockSpec((1, d), lambda j: (0, j))],
        out_specs=pl.BlockSpec((bsz, d), lambda j: (0, j)),
        out_shape=jax.ShapeDtypeStruct((bsz, n), F32),
        name="ada",
    )(c, w, b.reshape(1, n))


def _inproj_kernel(x_ref, mod_ref, g_ref, wa_ref, wvt_ref, wm_ref, wg_ref,
                   oa_ref, ovt_ref, om_ref, og_ref):
    x = x_ref[0]
    mod = mod_ref[0]
    h = _rms_rows(x) * g_ref[...]
    h = h * (1.0 + mod[1:2, :]) + mod[0:1, :]
    hb = h.astype(BF16)
    oa_ref[0] = jnp.dot(hb, wa_ref[...], preferred_element_type=F32).astype(BF16)
    ovt_ref[0] = lax.dot_general(wvt_ref[...], hb, _NT, preferred_element_type=F32).astype(BF16)
    om_ref[0] = jnp.dot(hb, wm_ref[...], preferred_element_type=F32).astype(BF16)
    og_ref[0] = lax.dot_general(wg_ref[...], hb, _NT, preferred_element_type=F32)


def _inproj(x, mod, g, w_qk, w_vt, w_m, w_gt, b0, bsz):
    _, s, d = x.shape
    tm = min(ROW_TILE, s)
    na, nv, nm, ng = w_qk.shape[1], w_vt.shape[0], w_m.shape[1], w_gt.shape[0]
    const = lambda shape: pl.BlockSpec(shape, lambda b, i: (0, 0))
    return pl.pallas_call(
        _inproj_kernel,
        grid=(bsz, s // tm),
        in_specs=[pl.BlockSpec((1, tm, d), lambda b, i: (b + b0, i, 0)),
                  pl.BlockSpec((1, 6, d), lambda b, i: (b + b0, 0, 0)),
                  const((1, d)), const((d, na)), const((nv, d)), const((d, nm)), const((ng, d))],
        out_specs=[pl.BlockSpec((1, tm, na), lambda b, i: (b, i, 0)),
                   pl.BlockSpec((1, nv, tm), lambda b, i: (b, 0, i)),
                   pl.BlockSpec((1, tm, nm), lambda b, i: (b, i, 0)),
                   pl.BlockSpec((1, ng, tm), lambda b, i: (b, 0, i))],
        out_shape=[jax.ShapeDtypeStruct((bsz, s, na), BF16),
                   jax.ShapeDtypeStruct((bsz, nv, s), BF16),
                   jax.ShapeDtypeStruct((bsz, s, nm), BF16),
                   jax.ShapeDtypeStruct((bsz, ng, s), F32)],
        compiler_params=pltpu.CompilerParams(vmem_limit_bytes=VMEM_LIMIT),
        name="inproj",
    )(x, mod, g.reshape(1, d), w_qk, w_vt, w_m, w_gt)


def _rel_buckets():
    n = np.arange(2 * ATT_BLOCK)
    max_exact = N_BUCKETS // 2
    nf = np.maximum(n, 1).astype(np.float64)
    large = max_exact + (np.log(nf / max_exact) / math.log(MAX_DIST / max_exact)
                         * (N_BUCKETS - max_exact)).astype(np.int64)
    large = np.minimum(large, N_BUCKETS - 1)
    bucket = np.where(n < max_exact, n, large)
    assert (bucket[ATT_BLOCK + 1:] == N_BUCKETS - 1).all() and 2 * ATT_BLOCK > MAX_DIST
    qk = np.arange(ATT_BLOCK)[None, :] - np.arange(ATT_BLOCK)[:, None]
    tiles = np.stack([bucket[np.maximum(qk, 0)], bucket[ATT_BLOCK + qk]])
    return tiles.astype(np.int32)


def _relbias_kernel(rb_ref, bk_ref, o_ref):
    h = pl.program_id(0)
    bk = bk_ref[...]
    acc = jnp.zeros(bk.shape, F32)
    for b in range(N_BUCKETS):
        acc = jnp.where(bk == b, rb_ref[b, h], acc)
    o_ref[0] = acc


def _relbias(rel_bias):
    tiles = jnp.asarray(_rel_buckets())
    return pl.pallas_call(
        _relbias_kernel,
        grid=(ATT_HEADS,),
        in_specs=[pl.BlockSpec(memory_space=pltpu.SMEM),
                  pl.BlockSpec((2, ATT_BLOCK, ATT_BLOCK), lambda h: (0, 0, 0))],
        out_specs=pl.BlockSpec((1, 2, ATT_BLOCK, ATT_BLOCK), lambda h: (h, 0, 0, 0)),
        out_shape=jax.ShapeDtypeStruct((ATT_HEADS, 2, ATT_BLOCK, ATT_BLOCK), F32),
        name="relbias",
    )(rel_bias, tiles)


def _attn_kernel(q_ref, k_ref, vt_ref, bias_ref, lam_ref, subg_ref, o_ref,
                 qz_ref, m_ref, l_ref, acc_ref):
    tq = ATT_BLOCK
    qi = pl.program_id(2)
    scale = ATT_QK_DIM ** -0.5

    q = q_ref[0]
    lane = lax.broadcasted_iota(jnp.int32, q.shape, 1)
    zero = jnp.zeros_like(q)
    qz_ref[0:tq, :] = jnp.where(lane < ATT_QK_DIM, q, zero)
    qz_ref[tq:2 * tq, :] = jnp.where(lane >= ATT_QK_DIM, q, zero)
    m_ref[...] = jnp.full(m_ref.shape, -jnp.inf, F32)
    l_ref[...] = jnp.zeros(l_ref.shape, F32)
    acc_ref[...] = jnp.zeros(acc_ref.shape, F32)

    def step(j, bias, masked):
        start = pl.multiple_of(j * tq, tq)
        k = k_ref[0, pl.ds(start, tq), :]
        vt = vt_ref[0, :, pl.ds(start, tq)]
        s = lax.dot_general(k, qz_ref[...], _NT, preferred_element_type=F32) * scale
        if isinstance(bias, tuple):
            s = s + jnp.concatenate([bias[0], bias[0]], axis=1)
        else:
            s = s + bias
        if masked:
            key = lax.broadcasted_iota(jnp.int32, (tq, tq), 0)
            qry = lax.broadcasted_iota(jnp.int32, (tq, tq), 1)
            keep = jnp.concatenate([key <= qry, key <= qry], axis=1)
            s = jnp.where(keep, s, jnp.finfo(F32).min)
        m_old = m_ref[...]
        m_new = jnp.maximum(m_old, jnp.max(s, axis=0, keepdims=True))
        alpha = jnp.exp(m_old - m_new)
        p = jnp.exp(s - m_new)
        l_ref[...] = alpha * l_ref[...] + jnp.sum(p, axis=0, keepdims=True)
        acc_ref[...] = alpha * acc_ref[...] + jnp.dot(vt, p.astype(BF16), preferred_element_type=F32)
        m_ref[...] = m_new

    far_bias = bias_ref[0, 1, 0:1, tq - 1:tq]

    def far_body(j, carry):
        step(j, far_bias, False)
        return carry

    lax.fori_loop(0, jnp.maximum(qi - 1, 0), far_body, 0)

    @pl.when(qi >= 1)
    def _():
        step(qi - 1, (bias_ref[0, 1],), False)

    step(qi, (bias_ref[0, 0],), True)

    lam = (jnp.exp(jnp.sum(lam_ref[0:1, :] * lam_ref[1:2, :], axis=-1, keepdims=True))
           - jnp.exp(jnp.sum(lam_ref[2:3, :] * lam_ref[3:4, :], axis=-1, keepdims=True))
           + LAMBDA_INIT)
    ot = acc_ref[...] / l_ref[...]
    o = (ot[:, 0:tq] - lam * ot[:, tq:2 * tq]).T
    o = _rms_rows(o) * subg_ref[...] * (1.0 - LAMBDA_INIT)
    o_ref[0] = o.astype(o_ref.dtype)


def _attn(att_qk, att_vt, bias_tiles, lam4, sub_g):
    bsz, s, _ = att_qk.shape
    tq = ATT_BLOCK
    nh = ATT_HEADS
    return pl.pallas_call(
        _attn_kernel,
        grid=(bsz, nh, s // tq),
        in_specs=[pl.BlockSpec((1, tq, ATT_V_DIM), lambda b, h, i: (b, i, h)),
                  pl.BlockSpec((1, s, ATT_V_DIM), lambda b, h, i: (b, 0, nh + h)),
                  pl.BlockSpec((1, ATT_V_DIM, s), lambda b, h, i: (b, h, 0)),
                  pl.BlockSpec((1, 2, tq, tq), lambda b, h, i: (h, 0, 0, 0)),
                  pl.BlockSpec((4, ATT_QK_DIM), lambda b, h, i: (0, 0)),
                  pl.BlockSpec((1, ATT_V_DIM), lambda b, h, i: (0, 0))],
        out_specs=pl.BlockSpec((1, tq, ATT_V_DIM), lambda b, h, i: (b, i, h)),
        out_shape=jax.ShapeDtypeStruct((bsz, s, D_ATT), BF16),
        scratch_shapes=[pltpu.VMEM((2 * tq, ATT_V_DIM), BF16),
                        pltpu.VMEM((1, 2 * tq), F32),
                        pltpu.VMEM((1, 2 * tq), F32),
                        pltpu.VMEM((ATT_V_DIM, 2 * tq), F32)],
        compiler_params=pltpu.CompilerParams(vmem_limit_bytes=VMEM_LIMIT),
        name="attn",
    )(att_qk, att_qk, att_vt, bias_tiles, lam4, sub_g.reshape(1, ATT_V_DIM))


def _mlstm_kernel(q_ref, k_ref, v_ref, o_ref, gi_ref, gf_ref, bias_ref, cwq_ref, cwk_ref,
                  cbq_ref, cbk_ref, ng_ref, out_ref, qs_ref, ks_ref, b_ref, ig_ref):
    s = q_ref.shape[1]
    L = M_CHUNK
    nc = s // L
    h = pl.program_id(1)

    row = lax.broadcasted_iota(jnp.int32, (s, M_DIM), 0)

    def conv_silu(x_ref, w_ref, cb_ref):
        x = x_ref[0].astype(F32)
        w = w_ref[...]
        out = None
        for j in range(CONV_W):
            shift = CONV_W - 1 - j
            xs = x if shift == 0 else jnp.where(row >= shift, pltpu.roll(x, shift, 0), 0.0)
            term = xs * w[j:j + 1, :]
            out = term if out is None else out + term
        out = out + cb_ref[...]
        return out * _sigmoid(out)

    qs_ref[...] = conv_silu(q_ref, cwq_ref, cbq_ref).astype(BF16)
    ks_ref[...] = (conv_silu(k_ref, cwk_ref, cbk_ref) * (M_DIM ** -0.5)).astype(BF16)

    ig = gi_ref[0, 0] + bias_ref[h]
    f = gf_ref[0, 0] + bias_ref[M_HEADS + h]
    logf = jnp.minimum(f, 0.0) - jnp.log(1.0 + jnp.exp(-jnp.abs(f)))
    r = lax.broadcasted_iota(jnp.int32, (L, L), 0)
    c = lax.broadcasted_iota(jnp.int32, (L, L), 1)
    tri = (r <= c).astype(F32)
    b_ref[...] = jnp.dot(logf, tri, preferred_element_type=F32,
                         precision=lax.Precision.HIGHEST)
    ig_ref[...] = ig
    eye = r == c
    causal = c <= r

    def to_col(x_row):
        return jnp.sum(jnp.where(eye, x_row, 0.0), axis=1, keepdims=True)

    def chunk(ci, carry):
        C, n, m = carry
        start = pl.multiple_of(ci * L, L)
        qc = qs_ref[pl.ds(start, L), :]
        kc = ks_ref[pl.ds(start, L), :]
        vc = v_ref[0, pl.ds(start, L), :]
        b_r = b_ref[pl.ds(ci, 1), :]
        ig_r = ig_ref[pl.ds(ci, 1), :]
        b_last = b_r[:, L - 1:L]
        a_r = b_last - b_r + ig_r
        b_c = to_col(b_r)
        a_c = to_col(a_r)

        logd = jnp.where(causal, b_c - b_r + ig_r, -jnp.inf)
        m_inter = b_c + m
        m_j = jnp.maximum(jnp.max(logd, axis=1, keepdims=True), m_inter)
        w = jnp.exp(logd - m_j)
        sqk = lax.dot_general(qc, kc, _NT, preferred_element_type=F32) * w
        inter = jnp.exp(m_inter - m_j)
        num = (jnp.dot(sqk.astype(BF16), vc, preferred_element_type=F32)
               + inter * jnp.dot(qc, C.astype(BF16), preferred_element_type=F32))
        den = (jnp.sum(sqk, axis=1, keepdims=True)
               + inter * jnp.sum(qc.astype(F32) * n, axis=1, keepdims=True))
        hc = num / jnp.maximum(jnp.abs(den), jnp.exp(-m_j))

        og = _sigmoid(o_ref[0, pl.ds(start, L), :].astype(F32))
        out_ref[0, pl.ds(start, L), :] = (_rms_rows(og * hc) * ng_ref[...]).astype(out_ref.dtype)

        m_new = jnp.maximum(b_last + m, jnp.max(a_r, axis=1, keepdims=True))
        decay = jnp.exp(b_last + m - m_new)
        kw = kc.astype(F32) * jnp.exp(a_c - m_new)
        C_new = decay * C + jnp.dot(kw.T.astype(BF16), vc, preferred_element_type=F32)
        n_new = decay * n + jnp.sum(kw, axis=0, keepdims=True)
        return C_new, n_new, m_new

    init = (jnp.zeros((M_DIM, M_DIM), F32), jnp.zeros((1, M_DIM), F32), jnp.zeros((1, 1), F32))
    lax.fori_loop(0, nc, chunk, init)


def _mlstm(m_qkvo, gates, gate_bias, conv_w, conv_b, norm_g):
    bsz, s, _ = m_qkvo.shape
    L = M_CHUNK
    nc = s // L
    nh = M_HEADS
    d = M_DIM
    g4 = gates.reshape(bsz, 2 * nh, nc, L)
    seq = lambda off: pl.BlockSpec((1, s, d), lambda b, h: (b, 0, off + h))
    return pl.pallas_call(
        _mlstm_kernel,
        grid=(bsz, nh),
        in_specs=[seq(0), seq(nh), seq(2 * nh), seq(3 * nh),
                  pl.BlockSpec((1, 1, nc, L), lambda b, h: (b, h, 0, 0)),
                  pl.BlockSpec((1, 1, nc, L), lambda b, h: (b, nh + h, 0, 0)),
                  pl.BlockSpec(memory_space=pltpu.SMEM),
                  pl.BlockSpec((CONV_W, d), lambda b, h: (0, h)),
                  pl.BlockSpec((CONV_W, d), lambda b, h: (0, nh + h)),
                  pl.BlockSpec((1, d), lambda b, h: (0, h)),
                  pl.BlockSpec((1, d), lambda b, h: (0, nh + h)),
                  pl.BlockSpec((1, d), lambda b, h: (0, h))],
        out_specs=pl.BlockSpec((1, s, d), lambda b, h: (b, 0, h)),
        out_shape=jax.ShapeDtypeStruct((bsz, s, D_MLSTM), BF16),
        scratch_shapes=[pltpu.VMEM((s, d), BF16), pltpu.VMEM((s, d), BF16),
                        pltpu.VMEM((nc, L), F32), pltpu.VMEM((nc, L), F32)],
        compiler_params=pltpu.CompilerParams(vmem_limit_bytes=VMEM_LIMIT),
        name="mlstm",
    )(m_qkvo, m_qkvo, m_qkvo, m_qkvo, g4, g4, gate_bias, conv_w, conv_w,
      conv_b.reshape(1, -1), conv_b.reshape(1, -1), norm_g.reshape(1, -1))


def _outproj_kernel(att_ref, hm_ref, x_ref, mod_ref, g2_ref, wo_ref, wq_ref,
                    x1_ref, h2_ref, qp_ref):
    mod = mod_ref[0]
    y = (jnp.dot(att_ref[0], wo_ref[0:D_ATT, :], preferred_element_type=F32)
         + jnp.dot(hm_ref[0], wo_ref[D_ATT:, :], preferred_element_type=F32))
    x1 = x_ref[0] + mod[2:3, :] * y
    x1_ref[0] = x1
    h2 = _rms_rows(x1) * g2_ref[...]
    h2 = h2 * (1.0 + mod[4:5, :]) + mod[3:4, :]
    hb = h2.astype(BF16)
    h2_ref[0] = _pack_bf16_halves(hb)
    qp_ref[0] = jnp.dot(hb, wq_ref[...], preferred_element_type=F32).astype(BF16)


def _outproj(att, hm, x, mod, g2, w_out, w_q, b0):
    bsz, s, _ = att.shape
    d = x.shape[-1]
    tm = min(ROW_TILE, s)
    nq = w_q.shape[1]
    tile = lambda n: pl.BlockSpec((1, tm, n), lambda b, i: (b, i, 0))
    return pl.pallas_call(
        _outproj_kernel,
        grid=(bsz, s // tm),
        in_specs=[tile(D_ATT), tile(D_MLSTM),
                  pl.BlockSpec((1, tm, d), lambda b, i: (b + b0, i, 0)),
                  pl.BlockSpec((1, 6, d), lambda b, i: (b + b0, 0, 0)),
                  pl.BlockSpec((1, d), lambda b, i: (0, 0)),
                  pl.BlockSpec((d, d), lambda b, i: (0, 0)),
                  pl.BlockSpec((d, nq), lambda b, i: (0, 0))],
        out_specs=[tile(d), tile(d // 2), tile(nq)],
        out_shape=[jax.ShapeDtypeStruct((bsz, s, d), F32),
                   jax.ShapeDtypeStruct((bsz, s, d // 2), jnp.int32),
                   jax.ShapeDtypeStruct((bsz, s, nq), BF16)],
        compiler_params=pltpu.CompilerParams(vmem_limit_bytes=VMEM_LIMIT),
        name="outproj",
    )(att, hm, x, mod, g2.reshape(1, d), w_out, w_q)


def _top16_rows(blocks, ids):
    big = jnp.float32(1 << 20)
    vals, pos = [], []
    for _ in range(PEER_TOPK):
        m = functools.reduce(jnp.maximum, blocks)
        m = jnp.max(m, axis=0, keepdims=True)
        cand = functools.reduce(jnp.minimum,
                                [jnp.where(b == m, i, big) for b, i in zip(blocks, ids)])
        p = jnp.min(cand, axis=0, keepdims=True)
        blocks = [jnp.where(i == p, -jnp.inf, b) for b, i in zip(blocks, ids)]
        vals.append(m)
        pos.append(p)
    return jnp.concatenate(vals, axis=0), jnp.concatenate(pos, axis=0)


def _pick_rows(table, sel):
    out = jnp.zeros(sel.shape, table.dtype)
    for r in range(PEER_TOPK):
        out = jnp.where(sel == r, table[r:r + 1, :], out)
    return out


def _pair_candidates(a, b):
    k, sub = PEER_TOPK, 8
    j_id = lax.broadcasted_iota(jnp.int32, (sub, LANES), 0).astype(F32)
    blocks = [a[0:1, :] + b[0:sub, :], a[0:1, :] + b[sub:k, :], a[1:2, :] + b[0:sub, :]]
    ids = [j_id, j_id + sub, j_id + k]
    for i in range(2, sub):
        blocks.append(jnp.where(j_id < k // (i + 1), a[i:i + 1, :] + b[0:sub, :], -jnp.inf))
        ids.append(j_id + i * k)
    blocks.append(a[sub:k, :] + b[0:1, :])
    ids.append((j_id + sub) * k)
    return blocks, ids


def _route_kernel(q_ref, keys_ref, idx_ref, g_ref):
    k = PEER_TOPK
    half = PEER_QDIM // 2
    key_id = lax.broadcasted_iota(jnp.int32, (N_KEYS, LANES), 0).astype(F32)
    for t in range(q_ref.shape[0] // LANES):
        rows = pl.ds(t * LANES, LANES)
        sv, si = [], []
        for p in range(2):
            qh = q_ref[rows, p * half:(p + 1) * half]
            s = lax.dot_general(keys_ref[0, p], qh, _NT, preferred_element_type=F32)
            v, i = _top16_rows([s], [key_id])
            sv.append(v)
            si.append(i.astype(jnp.int32))
        top_s, pos = _top16_rows(*_pair_candidates(sv[0], sv[1]))
        pos = pos.astype(jnp.int32)
        idx = (_pick_rows(si[0], lax.shift_right_logical(pos, k.bit_length() - 1)) * N_KEYS
               + _pick_rows(si[1], lax.bitwise_and(pos, k - 1)))
        e = jnp.exp(top_s - top_s[0:1, :])
        idx_ref[:, t * LANES:(t + 1) * LANES] = idx
        g_ref[:, t * LANES:(t + 1) * LANES] = e / jnp.sum(e, axis=0, keepdims=True)


def _route(qp, sub_keys):
    t, _ = qp.shape
    tt = min(ROUTE_TILE, t)
    k = PEER_TOPK
    return pl.pallas_call(
        _route_kernel,
        grid=(t // tt, PEER_HEADS),
        in_specs=[pl.BlockSpec((tt, PEER_QDIM), lambda i, h: (i, h)),
                  pl.BlockSpec((1, 2, N_KEYS, PEER_QDIM // 2), lambda i, h: (h, 0, 0, 0))],
        out_specs=[pl.BlockSpec((k, tt), lambda i, h: (h, i)),
                   pl.BlockSpec((k, tt), lambda i, h: (h, i))],
        out_shape=[jax.ShapeDtypeStruct((PEER_HEADS * k, t), jnp.int32),
                   jax.ShapeDtypeStruct((PEER_HEADS * k, t), F32)],
        name="route",
    )(qp, sub_keys)


def _final_kernel(x1_ref, po_ref, mod_ref, g_ref, o_ref):
    x2 = x1_ref[0] + mod_ref[0][5:6, :] * po_ref[0]
    o_ref[0] = _rms_rows(x2) * g_ref[...]


def _final_kernel_into(x1_ref, po_ref, mod_ref, g_ref, prev_ref, o_ref):
    del prev_ref
    _final_kernel(x1_ref, po_ref, mod_ref, g_ref, o_ref)


def _final(x1, peer_out, mod, final_g, b0, out):
    bsz, s, d = x1.shape
    tm = min(ROW_TILE, s)
    tile = pl.BlockSpec((1, tm, d), lambda b, i: (b, i, 0))
    in_specs = [tile, tile, pl.BlockSpec((1, 6, d), lambda b, i: (b + b0, 0, 0)),
                pl.BlockSpec((1, d), lambda b, i: (0, 0))]
    args = (x1, peer_out, mod, final_g.reshape(1, d))
    if out is not None:
        in_specs.append(pl.BlockSpec(memory_space=pl.ANY))
        args += (out,)
    return pl.pallas_call(
        _final_kernel if out is None else _final_kernel_into,
        grid=(bsz, s // tm),
        in_specs=in_specs,
        out_specs=pl.BlockSpec((1, tm, d), lambda b, i: (b + b0, i, 0)),
        out_shape=jax.ShapeDtypeStruct((mod.shape[0], s, d), F32),
        input_output_aliases={} if out is None else {4: 0},
        name="final",
    )(*args)


GELU_C = math.sqrt(2.0 / math.pi)
SC_LANES = 16
SC_WORKERS = 32
SC_CORES = 2
SC_TOKEN_BLOCK = 32
SC_GATHER_DEPTH = 4
SC_PACK_ROWS = 16
SC_ILV = plsc.PackFormat.INTERLEAVED
N_PAIRS = PEER_HEADS * PEER_TOPK


def _gelu_tanh(x):
    z = GELU_C * (x + 0.044715 * (x * x * x))
    t = 1.0 - 2.0 / (jnp.exp(2.0 * z) + 1.0)
    return x * (0.5 * (1.0 + t))


def _pack_tables(u, v):
    n_rows, d = u.shape
    dw = d // 2
    rb = SC_PACK_ROWS
    nb = SC_GATHER_DEPTH
    rpw = n_rows // SC_WORKERS
    nblk = rpw // rb
    assert rpw % rb == 0 and nblk % nb == 0 and nb % 2 == 0
    mesh = plsc.VectorSubcoreMesh(core_axis_name="c", subcore_axis_name="s")
    out_t = jax.ShapeDtypeStruct((n_rows, dw), jnp.int32)

    @functools.partial(
        pl.kernel, mesh=mesh, out_type=(out_t, out_t),
        scratch_types=[pltpu.VMEM((nb, rb, d), F32), pltpu.VMEM((2, rb, dw), jnp.int32),
                       pltpu.SemaphoreType.DMA((nb,)), pltpu.SemaphoreType.DMA((2,))],
        compiler_params=pltpu.CompilerParams(needs_layout_passes=False),
        name="pack_tables",
    )
    def pack(u_hbm, v_hbm, uo_hbm, vo_hbm, src, dst, isem, osem):
        wid = lax.axis_index("s") * SC_CORES + lax.axis_index("c")
        base = wid * rpw
        for tab, out in ((u_hbm, uo_hbm), (v_hbm, vo_hbm)):
            def in_copy(blk, b):
                return pltpu.make_async_copy(tab.at[pl.ds(base + blk * rb, rb)], src.at[b], isem.at[b])

            def out_copy(blk, s):
                return pltpu.make_async_copy(dst.at[s], out.at[pl.ds(base + blk * rb, rb)], osem.at[s])

            for b in range(nb - 1):
                in_copy(b, b).start()

            @pl.loop(0, nblk, step=nb)
            def _(blk):
                for b in range(nb):
                    ahead = blk + b + nb - 1

                    @pl.when(ahead < nblk)
                    def _():
                        in_copy(ahead, (b + nb - 1) % nb).start()

                    in_copy(blk + b, b).wait()
                    s = b % 2

                    @pl.when(blk + b >= 2)
                    def _():
                        out_copy(blk + b - 2, s).wait()

                    @pl.loop(0, rb)
                    def _(r):
                        @plsc.parallel_loop(0, dw // SC_LANES, 1, unroll=4)
                        def _(c):
                            lo = src[b, r, pl.ds(c * SC_LANES, SC_LANES)]
                            hi = src[b, r, pl.ds(dw + c * SC_LANES, SC_LANES)]
                            dst[s, r, pl.ds(c * SC_LANES, SC_LANES)] = plsc.bitcast(
                                plsc.pack(lo, hi, format=SC_ILV), jnp.int32)

                    out_copy(blk + b, s).start()

            out_copy(nblk - 2, 0).wait()
            out_copy(nblk - 1, 1).wait()

    return pack(u, v)


def _experts(h2, idx, g, u, v):
    t_total, dw = h2.shape
    d = 2 * dw
    tpw = t_total // SC_WORKERS
    tb = SC_TOKEN_BLOCK
    k = PEER_TOPK
    nh = PEER_HEADS
    n_items = tb * nh
    nbuf = SC_GATHER_DEPTH
    head_bits = nh.bit_length() - 1
    assert tpw % tb == 0 and n_items % nbuf == 0 and nh == 1 << head_bits
    mesh = plsc.VectorSubcoreMesh(core_axis_name="c", subcore_axis_name="s")

    @functools.partial(
        pl.kernel, mesh=mesh,
        out_type=jax.ShapeDtypeStruct((t_total, d), F32),
        scratch_types=[
            pltpu.VMEM((tb, N_PAIRS), jnp.int32),
            pltpu.VMEM((tb, N_PAIRS), F32),
            pltpu.VMEM((tb, dw), jnp.int32),
            pltpu.VMEM((tb, d), F32),
            pltpu.VMEM((nbuf, k, dw), jnp.int32),
            pltpu.VMEM((nbuf, k, dw), jnp.int32),
            pltpu.SemaphoreType.DMA((nbuf,)),
            pltpu.SemaphoreType.DMA((nbuf,)),
        ],
        compiler_params=pltpu.CompilerParams(needs_layout_passes=False),
        name="experts",
    )
    def experts(h_hbm, idx_hbm, g_hbm, u_hbm, v_hbm, out_hbm,
                idx_b, g_b, h_b, out_b, urows, vrows, usem, vsem):
        wid = lax.axis_index("s") * SC_CORES + lax.axis_index("c")
        base = wid * tpw
        lane = lax.iota(jnp.int32, SC_LANES)

        def split(item):
            return lax.shift_right_logical(item, head_bits), lax.bitwise_and(item, nh - 1)

        def copies(item, b):
            tt, hd = split(item)
            ids = idx_b.at[tt, pl.ds(hd * k, k)]
            return (pltpu.make_async_copy(u_hbm.at[ids], urows.at[b], usem.at[b]),
                    pltpu.make_async_copy(v_hbm.at[ids], vrows.at[b], vsem.at[b]))

        def fetch(item, b):
            cu, cv = copies(item, b)
            cu.start()
            cv.start()

        def words(ref, *lead, off):
            return plsc.bitcast(ref[(*lead, pl.ds(off, SC_LANES))], BF16)

        def sum4_unpack(pr):
            return plsc.unpack((pr[0] + pr[1]) + (pr[2] + pr[3]), format=SC_ILV)

        def compute(item, b):
            tt, hd = split(item)
            cu, cv = copies(item, b)
            cu.wait()

            def ubody(j, accs):
                hs = [words(h_b, tt, off=(j * 4 + q) * SC_LANES) for q in range(4)]
                new = []
                for p, a in enumerate(accs):
                    lo, hi = sum4_unpack([words(urows, b, p, off=(j * 4 + q) * SC_LANES) * hs[q]
                                          for q in range(4)])
                    new.append(a + (lo + hi))
                return tuple(new)

            accs = plsc.parallel_loop(
                0, dw // (4 * SC_LANES), 1,
                carry=tuple(jnp.zeros((SC_LANES,), F32) for _ in range(k)))(ubody)
            s = jnp.zeros((SC_LANES,), F32)
            for p in range(k):
                s = jnp.where(lane == p, jnp.sum(accs[p]), s)
            c = g_b[tt, pl.ds(hd * k, k)] * _gelu_tanh(s)
            cbb = []
            for p in range(k):
                cp = jnp.full((SC_LANES,), c[p])
                cbb.append(plsc.pack(cp, cp, format=SC_ILV))
            cv.wait()

            @plsc.parallel_loop(0, dw // SC_LANES, 1, unroll=2)
            def _(ch):
                sa = pl.ds(ch * SC_LANES, SC_LANES)
                sb = pl.ds(dw + ch * SC_LANES, SC_LANES)
                los, his = [], []
                for grp in range(k // 4):
                    lo, hi = sum4_unpack([cbb[grp * 4 + q]
                                          * words(vrows, b, grp * 4 + q, off=ch * SC_LANES)
                                          for q in range(4)])
                    los.append(lo)
                    his.append(hi)
                out_b[tt, sa] = out_b[tt, sa] + ((los[0] + los[1]) + (los[2] + los[3]))
                out_b[tt, sb] = out_b[tt, sb] + ((his[0] + his[1]) + (his[2] + his[3]))

        @pl.loop(0, tpw // tb)
        def _(blk):
            t0 = base + blk * tb
            pltpu.sync_copy(idx_hbm.at[pl.ds(t0, tb)], idx_b)
            pltpu.sync_copy(g_hbm.at[pl.ds(t0, tb)], g_b)
            pltpu.sync_copy(h_hbm.at[pl.ds(t0, tb)], h_b)
            for b in range(nbuf - 1):
                fetch(b, b)

            @pl.loop(0, tb)
            def _(tt):
                @pl.loop(0, d // SC_LANES)
                def _(j):
                    out_b[tt, pl.ds(j * SC_LANES, SC_LANES)] = jnp.zeros((SC_LANES,), F32)

            @pl.loop(0, n_items, step=nbuf)
            def _(it):
                for b in range(nbuf):
                    ahead = it + b + nbuf - 1

                    @pl.when(ahead < n_items)
                    def _():
                        fetch(ahead, (b + nbuf - 1) % nbuf)

                    compute(it + b, b)

            pltpu.sync_copy(out_b, out_hbm.at[pl.ds(t0, tb)])

    return experts(h2, idx, g, u, v)


BATCH_CHUNKS = (1,) * 16


def kernel(x, c, w_ada, b_ada, norm1_g, norm2_g, w_in, conv_w, conv_b, b_igate, b_fgate, lam_q1, lam_k1, lam_q2, lam_k2, diff_sub_g, mlstm_norm_g, w_out, peer_w_q, peer_sub_keys, peer_u, peer_v, rel_bias, final_g):
    bsz, s, d = x.shape
    mod = _ada(c, w_ada[0], b_ada[0]).reshape(bsz, 6, d)

    w = w_in[0]
    w_qk = w[:, :2 * D_ATT].astype(BF16)
    w_vt = w[:, 2 * D_ATT:3 * D_ATT].T.astype(BF16)
    w_m = w[:, 3 * D_ATT:3 * D_ATT + 4 * D_MLSTM].astype(BF16)
    w_gt = w[:, 3 * D_ATT + 4 * D_MLSTM:].T.astype(BF16)
    w_o = w_out[0].astype(BF16)
    w_q = peer_w_q[0].astype(BF16)
    sub_keys = peer_sub_keys[0].astype(BF16)
    bias_tiles = _relbias(rel_bias)
    lam4 = jnp.stack([lam_q1[0], lam_k1[0], lam_q2[0], lam_k2[0]])
    gate_bias = jnp.concatenate([b_igate[0], b_fgate[0]])

    assert sum(BATCH_CHUNKS) == bsz
    tables = None
    out = None
    b0 = 0
    for nb in BATCH_CHUNKS:
        att_qk, att_vt, m_qkvo, gates = _inproj(x, mod, norm1_g[0], w_qk, w_vt, w_m, w_gt, b0, nb)
        att = _attn(att_qk, att_vt, bias_tiles, lam4, diff_sub_g[0])
        hm = _mlstm(m_qkvo, gates, gate_bias, conv_w[0], conv_b[0], mlstm_norm_g[0])
        x1, h2, qp = _outproj(att, hm, x, mod, norm2_g[0], w_o, w_q, b0)
        idx_t, g_t = _route(qp.reshape(nb * s, -1), sub_keys)
        if tables is None:
            tables = _pack_tables(peer_u[0], peer_v[0])
        peer_out = _experts(h2.reshape(nb * s, d // 2), idx_t.T, g_t.T, *tables)
        out = _final(x1, peer_out.reshape(nb, s, d), mod, final_g, b0, out)
        b0 += nb
    return out
```

```python
import functools
import math

import numpy as np
import jax
import jax.numpy as jnp
from jax import lax
from jax.experimental import pallas as pl
from jax.experimental.pallas import tpu as pltpu
from jax.experimental.pallas import tpu_sc as plsc

F32 = jnp.float32
BF16 = jnp.bfloat16

ATT_HEADS = 4
ATT_QK_DIM = 64
ATT_V_DIM = 128
D_ATT = ATT_HEADS * ATT_V_DIM
M_HEADS = 4
M_DIM = 128
D_MLSTM = M_HEADS * M_DIM
CONV_W = 4
N_BUCKETS = 32
MAX_DIST = 128
N_KEYS = 128
PEER_HEADS = 8
PEER_TOPK = 16
PEER_QDIM = 256
EPS = 1e-6
LAMBDA_INIT = 0.8 - 0.6 * math.exp(-0.3 * 0)

ATT_BLOCK = 256
M_CHUNK = 128
ROW_TILE = 512
ROUTE_TILE = 512
LANES = 128
V7X_VMEM_BYTES = 64 * 1024 * 1024
VMEM_LIMIT = V7X_VMEM_BYTES * 3 // 4

_NT = (((1,), (1,)), ((), ()))


def _rms_rows(x):
    return x * lax.rsqrt(jnp.mean(x * x, axis=-1, keepdims=True) + EPS)


def _sigmoid(x):
    return 1.0 / (1.0 + jnp.exp(-x))


def _pack_bf16_halves(x):
    bits = lax.bitcast_convert_type(x.astype(BF16).astype(F32), jnp.int32)
    n = x.shape[-1] // 2
    return lax.bitwise_or(lax.shift_right_logical(bits[..., :n], 16),
                          lax.bitwise_and(bits[..., n:], jnp.int32(-65536)))


def _ada_kernel(c_ref, w_ref, b_ref, o_ref):
    c = c_ref[...]
    cond = c * _sigmoid(c)
    o_ref[...] = jnp.dot(cond, w_ref[...], preferred_element_type=F32) + b_ref[...]


def _ada(c, w, b):
    bsz, d = c.shape
    n = w.shape[1]
    return pl.pallas_call(
        _ada_kernel,
        grid=(n // d,),
        in_specs=[pl.BlockSpec((bsz, d), lambda j: (0, 0)),
                  pl.BlockSpec((d, d), lambda j: (0, j)),
                  pl.BlockSpec((1, d), lambda j: (0, j))],
        out_specs=pl.BlockSpec((bsz, d), lambda j: (0, j)),
        out_shape=jax.ShapeDtypeStruct((bsz, n), F32),
        name="ada",
    )(c, w, b.reshape(1, n))


def _inproj_kernel(x_ref, mod_ref, g_ref, wa_ref, wvt_ref, wm_ref, wg_ref,
                   oa_ref, ovt_ref, om_ref, og_ref):
    x = x_ref[0]
    mod = mod_ref[0]
    h = _rms_rows(x) * g_ref[...]
    h = h * (1.0 + mod[1:2, :]) + mod[0:1, :]
    hb = h.astype(BF16)
    oa_ref[0] = jnp.dot(hb, wa_ref[...], preferred_element_type=F32).astype(BF16)
    ovt_ref[0] = lax.dot_general(wvt_ref[...], hb, _NT, preferred_element_type=F32).astype(BF16)
    om_ref[0] = jnp.dot(hb, wm_ref[...], preferred_element_type=F32).astype(BF16)
    og_ref[0] = lax.dot_general(wg_ref[...], hb, _NT, preferred_element_type=F32)


def _inproj(x, mod, g, w_qk, w_vt, w_m, w_gt, b0, bsz):
    _, s, d = x.shape
    tm = min(ROW_TILE, s)
    na, nv, nm, ng = w_qk.shape[1], w_vt.shape[0], w_m.shape[1], w_gt.shape[0]
    const = lambda shape: pl.BlockSpec(shape, lambda b, i: (0, 0))
    return pl.pallas_call(
        _inproj_kernel,
        grid=(bsz, s // tm),
        in_specs=[pl.BlockSpec((1, tm, d), lambda b, i: (b + b0, i, 0)),
                  pl.BlockSpec((1, 6, d), lambda b, i: (b + b0, 0, 0)),
                  const((1, d)), const((d, na)), const((nv, d)), const((d, nm)), const((ng, d))],
        out_specs=[pl.BlockSpec((1, tm, na), lambda b, i: (b, i, 0)),
                   pl.BlockSpec((1, nv, tm), lambda b, i: (b, 0, i)),
                   pl.BlockSpec((1, tm, nm), lambda b, i: (b, i, 0)),
                   pl.BlockSpec((1, ng, tm), lambda b, i: (b, 0, i))],
        out_shape=[jax.ShapeDtypeStruct((bsz, s, na), BF16),
                   jax.ShapeDtypeStruct((bsz, nv, s), BF16),
                   jax.ShapeDtypeStruct((bsz, s, nm), BF16),
                   jax.ShapeDtypeStruct((bsz, ng, s), F32)],
        compiler_params=pltpu.CompilerParams(vmem_limit_bytes=VMEM_LIMIT),
        name="inproj",
    )(x, mod, g.reshape(1, d), w_qk, w_vt, w_m, w_gt)


def _rel_buckets():
    n = np.arange(2 * ATT_BLOCK)
    max_exact = N_BUCKETS // 2
    nf = np.maximum(n, 1).astype(np.float64)
    large = max_exact + (np.log(nf / max_exact) / math.log(MAX_DIST / max_exact)
                         * (N_BUCKETS - max_exact)).astype(np.int64)
    large = np.minimum(large, N_BUCKETS - 1)
    bucket = np.where(n < max_exact, n, large)
    assert (bucket[ATT_BLOCK + 1:] == N_BUCKETS - 1).all() and 2 * ATT_BLOCK > MAX_DIST
    qk = np.arange(ATT_BLOCK)[None, :] - np.arange(ATT_BLOCK)[:, None]
    tiles = np.stack([bucket[np.maximum(qk, 0)], bucket[ATT_BLOCK + qk]])
    return tiles.astype(np.int32)


def _relbias_kernel(rb_ref, bk_ref, o_ref):
    h = pl.program_id(0)
    bk = bk_ref[...]
    acc = jnp.zeros(bk.shape, F32)
    for b in range(N_BUCKETS):
        acc = jnp.where(bk == b, rb_ref[b, h], acc)
    o_ref[0] = acc


def _relbias(rel_bias):
    tiles = jnp.asarray(_rel_buckets())
    return pl.pallas_call(
        _relbias_kernel,
        grid=(ATT_HEADS,),
        in_specs=[pl.BlockSpec(memory_space=pltpu.SMEM),
                  pl.BlockSpec((2, ATT_BLOCK, ATT_BLOCK), lambda h: (0, 0, 0))],
        out_specs=pl.BlockSpec((1, 2, ATT_BLOCK, ATT_BLOCK), lambda h: (h, 0, 0, 0)),
        out_shape=jax.ShapeDtypeStruct((ATT_HEADS, 2, ATT_BLOCK, ATT_BLOCK), F32),
        name="relbias",
    )(rel_bias, tiles)


def _attn_kernel(q_ref, k_ref, vt_ref, bias_ref, lam_ref, subg_ref, o_ref,
                 qz_ref, m_ref, l_ref, acc_ref, *, q0):
    tq = ATT_BLOCK
    qi = pl.program_id(2) + q0
    scale = ATT_QK_DIM ** -0.5

    q = q_ref[0]
    lane = lax.broadcasted_iota(jnp.int32, q.shape, 1)
    zero = jnp.zeros_like(q)
    qz_ref[0:tq, :] = jnp.where(lane < ATT_QK_DIM, q, zero)
    qz_ref[tq:2 * tq, :] = jnp.where(lane >= ATT_QK_DIM, q, zero)
    m_ref[...] = jnp.full(m_ref.shape, -jnp.inf, F32)
    l_ref[...] = jnp.zeros(l_ref.shape, F32)
    acc_ref[...] = jnp.zeros(acc_ref.shape, F32)

    def step(j, bias, masked):
        start = pl.multiple_of(j * tq, tq)
        k = k_ref[0, pl.ds(start, tq), :]
        vt = vt_ref[0, :, pl.ds(start, tq)]
        s = lax.dot_general(k, qz_ref[...], _NT, preferred_element_type=F32) * scale
        if isinstance(bias, tuple):
            s = s + jnp.concatenate([bias[0], bias[0]], axis=1)
        else:
            s = s + bias
        if masked:
            key = lax.broadcasted_iota(jnp.int32, (tq, tq), 0)
            qry = lax.broadcasted_iota(jnp.int32, (tq, tq), 1)
            keep = jnp.concatenate([key <= qry, key <= qry], axis=1)
            s = jnp.where(keep, s, jnp.finfo(F32).min)
        m_old = m_ref[...]
        m_new = jnp.maximum(m_old, jnp.max(s, axis=0, keepdims=True))
        alpha = jnp.exp(m_old - m_new)
        p = jnp.exp(s - m_new)
        l_ref[...] = alpha * l_ref[...] + jnp.sum(p, axis=0, keepdims=True)
        acc_ref[...] = alpha * acc_ref[...] + jnp.dot(vt, p.astype(BF16), preferred_element_type=F32)
        m_ref[...] = m_new

    far_bias = bias_ref[0, 1, 0:1, tq - 1:tq]

    def far_body(j, carry):
        step(j, far_bias, False)
        return carry

    lax.fori_loop(0, jnp.maximum(qi - 1, 0), far_body, 0)

    @pl.when(qi >= 1)
    def _():
        step(qi - 1, (bias_ref[0, 1],), False)

    step(qi, (bias_ref[0, 0],), True)

    lam = (jnp.exp(jnp.sum(lam_ref[0:1, :] * lam_ref[1:2, :], axis=-1, keepdims=True))
           - jnp.exp(jnp.sum(lam_ref[2:3, :] * lam_ref[3:4, :], axis=-1, keepdims=True))
           + LAMBDA_INIT)
    ot = acc_ref[...] / l_ref[...]
    o = (ot[:, 0:tq] - lam * ot[:, tq:2 * tq]).T
    o = _rms_rows(o) * subg_ref[...] * (1.0 - LAMBDA_INIT)
    o_ref[0] = o.astype(o_ref.dtype)


def _attn(att_qk, att_vt, bias_tiles, lam4, sub_g, q_start=0):
    bsz, s, _ = att_qk.shape
    tq = ATT_BLOCK
    nh = ATT_HEADS
    q0 = q_start // tq
    nq = s // tq - q0
    return pl.pallas_call(
        functools.partial(_attn_kernel, q0=q0),
        grid=(bsz, nh, nq),
        in_specs=[pl.BlockSpec((1, tq, ATT_V_DIM), lambda b, h, i: (b, i + q0, h)),
                  pl.BlockSpec((1, s, ATT_V_DIM), lambda b, h, i: (b, 0, nh + h)),
                  pl.BlockSpec((1, ATT_V_DIM, s), lambda b, h, i: (b, h, 0)),
                  pl.BlockSpec((1, 2, tq, tq), lambda b, h, i: (h, 0, 0, 0)),
                  pl.BlockSpec((4, ATT_QK_DIM), lambda b, h, i: (0, 0)),
                  pl.BlockSpec((1, ATT_V_DIM), lambda b, h, i: (0, 0))],
        out_specs=pl.BlockSpec((1, tq, ATT_V_DIM), lambda b, h, i: (b, i, h)),
        out_shape=jax.ShapeDtypeStruct((bsz, nq * tq, D_ATT), BF16),
        scratch_shapes=[pltpu.VMEM((2 * tq, ATT_V_DIM), BF16),
                        pltpu.VMEM((1, 2 * tq), F32),
                        pltpu.VMEM((1, 2 * tq), F32),
                        pltpu.VMEM((ATT_V_DIM, 2 * tq), F32)],
        compiler_params=pltpu.CompilerParams(vmem_limit_bytes=VMEM_LIMIT),
        name="attn",
    )(att_qk, att_qk, att_vt, bias_tiles, lam4, sub_g.reshape(1, ATT_V_DIM))


def _mlstm_kernel(q_ref, k_ref, v_ref, o_ref, gi_ref, gf_ref, bias_ref, cwq_ref, cwk_ref,
                  cbq_ref, cbk_ref, ng_ref, out_ref, qs_ref, ks_ref, b_ref, ig_ref):
    s = q_ref.shape[1]
    L = M_CHUNK
    nc = s // L
    h = pl.program_id(1)

    row = lax.broadcasted_iota(jnp.int32, (s, M_DIM), 0)

    def conv_silu(x_ref, w_ref, cb_ref):
        x = x_ref[0].astype(F32)
        w = w_ref[...]
        out = None
        for j in range(CONV_W):
            shift = CONV_W - 1 - j
            xs = x if shift == 0 else jnp.where(row >= shift, pltpu.roll(x, shift, 0), 0.0)
            term = xs * w[j:j + 1, :]
            out = term if out is None else out + term
        out = out + cb_ref[...]
        return out * _sigmoid(out)

    qs_ref[...] = conv_silu(q_ref, cwq_ref, cbq_ref).astype(BF16)
    ks_ref[...] = (conv_silu(k_ref, cwk_ref, cbk_ref) * (M_DIM ** -0.5)).astype(BF16)

    ig = gi_ref[0, 0] + bias_ref[h]
    f = gf_ref[0, 0] + bias_ref[M_HEADS + h]
    logf = jnp.minimum(f, 0.0) - jnp.log(1.0 + jnp.exp(-jnp.abs(f)))
    r = lax.broadcasted_iota(jnp.int32, (L, L), 0)
    c = lax.broadcasted_iota(jnp.int32, (L, L), 1)
    tri = (r <= c).astype(F32)
    b_ref[...] = jnp.dot(logf, tri, preferred_element_type=F32,
                         precision=lax.Precision.HIGHEST)
    ig_ref[...] = ig
    eye = r == c
    causal = c <= r

    def to_col(x_row):
        return jnp.sum(jnp.where(eye, x_row, 0.0), axis=1, keepdims=True)

    def chunk(ci, carry):
        C, n, m = carry
        start = pl.multiple_of(ci * L, L)
        qc = qs_ref[pl.ds(start, L), :]
        kc = ks_ref[pl.ds(start, L), :]
        vc = v_ref[0, pl.ds(start, L), :]
        b_r = b_ref[pl.ds(ci, 1), :]
        ig_r = ig_ref[pl.ds(ci, 1), :]
        b_last = b_r[:, L - 1:L]
        a_r = b_last - b_r + ig_r
        b_c = to_col(b_r)
        a_c = to_col(a_r)

        logd = jnp.where(causal, b_c - b_r + ig_r, -jnp.inf)
        m_inter = b_c + m
        m_j = jnp.maximum(jnp.max(logd, axis=1, keepdims=True), m_inter)
        w = jnp.exp(logd - m_j)
        sqk = lax.dot_general(qc, kc, _NT, preferred_element_type=F32) * w
        inter = jnp.exp(m_inter - m_j)
        num = (jnp.dot(sqk.astype(BF16), vc, preferred_element_type=F32)
               + inter * jnp.dot(qc, C.astype(BF16), preferred_element_type=F32))
        den = (jnp.sum(sqk, axis=1, keepdims=True)
               + inter * jnp.sum(qc.astype(F32) * n, axis=1, keepdims=True))
        hc = num / jnp.maximum(jnp.abs(den), jnp.exp(-m_j))

        og = _sigmoid(o_ref[0, pl.ds(start, L), :].astype(F32))
        out_ref[0, pl.ds(start, L), :] = (_rms_rows(og * hc) * ng_ref[...]).astype(out_ref.dtype)

        m_new = jnp.maximum(b_last + m, jnp.max(a_r, axis=1, keepdims=True))
        decay = jnp.exp(b_last + m - m_new)
        kw = kc.astype(F32) * jnp.exp(a_c - m_new)
        C_new = decay * C + jnp.dot(kw.T.astype(BF16), vc, preferred_element_type=F32)
        n_new = decay * n + jnp.sum(kw, axis=0, keepdims=True)
        return C_new, n_new, m_new

    init = (jnp.zeros((M_DIM, M_DIM), F32), jnp.zeros((1, M_DIM), F32), jnp.zeros((1, 1), F32))
    lax.fori_loop(0, nc, chunk, init)


def _mlstm(m_qkvo, gates, gate_bias, conv_w, conv_b, norm_g):
    bsz, s, _ = m_qkvo.shape
    L = M_CHUNK
    nc = s // L
    nh = M_HEADS
    d = M_DIM
    g4 = gates.reshape(bsz, 2 * nh, nc, L)
    seq = lambda off: pl.BlockSpec((1, s, d), lambda b, h: (b, 0, off + h))
    return pl.pallas_call(
        _mlstm_kernel,
        grid=(bsz, nh),
        in_specs=[seq(0), seq(nh), seq(2 * nh), seq(3 * nh),
                  pl.BlockSpec((1, 1, nc, L), lambda b, h: (b, h, 0, 0)),
                  pl.BlockSpec((1, 1, nc, L), lambda b, h: (b, nh + h, 0, 0)),
                  pl.BlockSpec(memory_space=pltpu.SMEM),
                  pl.BlockSpec((CONV_W, d), lambda b, h: (0, h)),
                  pl.BlockSpec((CONV_W, d), lambda b, h: (0, nh + h)),
                  pl.BlockSpec((1, d), lambda b, h: (0, h)),
                  pl.BlockSpec((1, d), lambda b, h: (0, nh + h)),
                  pl.BlockSpec((1, d), lambda b, h: (0, h))],
        out_specs=pl.BlockSpec((1, s, d), lambda b, h: (b, 0, h)),
        out_shape=jax.ShapeDtypeStruct((bsz, s, D_MLSTM), BF16),
        scratch_shapes=[pltpu.VMEM((s, d), BF16), pltpu.VMEM((s, d), BF16),
                        pltpu.VMEM((nc, L), F32), pltpu.VMEM((nc, L), F32)],
        compiler_params=pltpu.CompilerParams(vmem_limit_bytes=VMEM_LIMIT),
        name="mlstm",
    )(m_qkvo, m_qkvo, m_qkvo, m_qkvo, g4, g4, gate_bias, conv_w, conv_w,
      conv_b.reshape(1, -1), conv_b.reshape(1, -1), norm_g.reshape(1, -1))


def _outproj_kernel(att_ref, hm_ref, x_ref, mod_ref, g2_ref, wo_ref, wq_ref,
                    x1_ref, h2_ref, qp_ref):
    mod = mod_ref[0]
    y = (jnp.dot(att_ref[0], wo_ref[0:D_ATT, :], preferred_element_type=F32)
         + jnp.dot(hm_ref[0], wo_ref[D_ATT:, :], preferred_element_type=F32))
    x1 = x_ref[0] + mod[2:3, :] * y
    x1_ref[0] = x1
    h2 = _rms_rows(x1) * g2_ref[...]
    h2 = h2 * (1.0 + mod[4:5, :]) + mod[3:4, :]
    hb = h2.astype(BF16)
    h2_ref[0] = _pack_bf16_halves(hb)
    qp_ref[0] = jnp.dot(hb, wq_ref[...], preferred_element_type=F32).astype(BF16)


def _outproj(att, hm, x, mod, g2, w_out, w_q, b0):
    bsz, s, _ = att.shape
    d = x.shape[-1]
    tm = min(ROW_TILE, s)
    nq = w_q.shape[1]
    tile = lambda n: pl.BlockSpec((1, tm, n), lambda b, i: (b, i, 0))
    return pl.pallas_call(
        _outproj_kernel,
        grid=(bsz, s // tm),
        in_specs=[tile(D_ATT), tile(D_MLSTM),
                  pl.BlockSpec((1, tm, d), lambda b, i: (b + b0, i, 0)),
                  pl.BlockSpec((1, 6, d), lambda b, i: (b + b0, 0, 0)),
                  pl.BlockSpec((1, d), lambda b, i: (0, 0)),
                  pl.BlockSpec((d, d), lambda b, i: (0, 0)),
                  pl.BlockSpec((d, nq), lambda b, i: (0, 0))],
        out_specs=[tile(d), tile(d // 2), tile(nq)],
        out_shape=[jax.ShapeDtypeStruct((bsz, s, d), F32),
                   jax.ShapeDtypeStruct((bsz, s, d // 2), jnp.int32),
                   jax.ShapeDtypeStruct((bsz, s, nq), BF16)],
        compiler_params=pltpu.CompilerParams(vmem_limit_bytes=VMEM_LIMIT),
        name="outproj",
    )(att, hm, x, mod, g2.reshape(1, d), w_out, w_q)


def _top16_rows(blocks, ids):
    big = jnp.float32(1 << 20)
    vals, pos = [], []
    for _ in range(PEER_TOPK):
        m = functools.reduce(jnp.maximum, blocks)
        m = jnp.max(m, axis=0, keepdims=True)
        cand = functools.reduce(jnp.minimum,
                                [jnp.where(b == m, i, big) for b, i in zip(blocks, ids)])
        p = jnp.min(cand, axis=0, keepdims=True)
        blocks = [jnp.where(i == p, -jnp.inf, b) for b, i in zip(blocks, ids)]
        vals.append(m)
        pos.append(p)
    return jnp.concatenate(vals, axis=0), jnp.concatenate(pos, axis=0)


def _pick_rows(table, sel):
    out = jnp.zeros(sel.shape, table.dtype)
    for r in range(PEER_TOPK):
        out = jnp.where(sel == r, table[r:r + 1, :], out)
    return out


def _pair_candidates(a, b):
    k, sub = PEER_TOPK, 8
    j_id = lax.broadcasted_iota(jnp.int32, (sub, LANES), 0).astype(F32)
    blocks = [a[0:1, :] + b[0:sub, :], a[0:1, :] + b[sub:k, :], a[1:2, :] + b[0:sub, :]]
    ids = [j_id, j_id + sub, j_id + k]
    for i in range(2, sub):
        blocks.append(jnp.where(j_id < k // (i + 1), a[i:i + 1, :] + b[0:sub, :], -jnp.inf))
        ids.append(j_id + i * k)
    blocks.append(a[sub:k, :] + b[0:1, :])
    ids.append((j_id + sub) * k)
    return blocks, ids


def _route_kernel(q_ref, keys_ref, idx_ref, g_ref):
    k = PEER_TOPK
    half = PEER_QDIM // 2
    key_id = lax.broadcasted_iota(jnp.int32, (N_KEYS, LANES), 0).astype(F32)
    for t in range(q_ref.shape[0] // LANES):
        rows = pl.ds(t * LANES, LANES)
        sv, si = [], []
        for p in range(2):
            qh = q_ref[rows, p * half:(p + 1) * half]
            s = lax.dot_general(keys_ref[0, p], qh, _NT, preferred_element_type=F32)
            v, i = _top16_rows([s], [key_id])
            sv.append(v)
            si.append(i.astype(jnp.int32))
        top_s, pos = _top16_rows(*_pair_candidates(sv[0], sv[1]))
        pos = pos.astype(jnp.int32)
        idx = (_pick_rows(si[0], lax.shift_right_logical(pos, k.bit_length() - 1)) * N_KEYS
               + _pick_rows(si[1], lax.bitwise_and(pos, k - 1)))
        e = jnp.exp(top_s - top_s[0:1, :])
        idx_ref[:, t * LANES:(t + 1) * LANES] = idx
        g_ref[:, t * LANES:(t + 1) * LANES] = e / jnp.sum(e, axis=0, keepdims=True)


def _route(qp, sub_keys):
    t, _ = qp.shape
    tt = min(ROUTE_TILE, t)
    k = PEER_TOPK
    return pl.pallas_call(
        _route_kernel,
        grid=(t // tt, PEER_HEADS),
        in_specs=[pl.BlockSpec((tt, PEER_QDIM), lambda i, h: (i, h)),
                  pl.BlockSpec((1, 2, N_KEYS, PEER_QDIM // 2), lambda i, h: (h, 0, 0, 0))],
        out_specs=[pl.BlockSpec((k, tt), lambda i, h: (h, i)),
                   pl.BlockSpec((k, tt), lambda i, h: (h, i))],
        out_shape=[jax.ShapeDtypeStruct((PEER_HEADS * k, t), jnp.int32),
                   jax.ShapeDtypeStruct((PEER_HEADS * k, t), F32)],
        name="route",
    )(qp, sub_keys)


def _final_kernel(x1_ref, po_ref, mod_ref, g_ref, o_ref):
    x2 = x1_ref[0] + mod_ref[0][5:6, :] * po_ref[0]
    o_ref[0] = _rms_rows(x2) * g_ref[...]


def _final_kernel_into(x1_ref, po_ref, mod_ref, g_ref, prev_ref, o_ref):
    del prev_ref
    _final_kernel(x1_ref, po_ref, mod_ref, g_ref, o_ref)


def _final(x1, peer_out, mod, final_g, b0, out, s_full, s0=0):
    bsz, s, d = x1.shape
    tm = min(ROW_TILE, s)
    i0 = s0 // tm
    tile = pl.BlockSpec((1, tm, d), lambda b, i: (b, i, 0))
    in_specs = [tile, tile, pl.BlockSpec((1, 6, d), lambda b, i: (b + b0, 0, 0)),
                pl.BlockSpec((1, d), lambda b, i: (0, 0))]
    args = (x1, peer_out, mod, final_g.reshape(1, d))
    if out is not None:
        in_specs.append(pl.BlockSpec(memory_space=pl.ANY))
        args += (out,)
    return pl.pallas_call(
        _final_kernel if out is None else _final_kernel_into,
        grid=(bsz, s // tm),
        in_specs=in_specs,
        out_specs=pl.BlockSpec((1, tm, d), lambda b, i: (b + b0, i + i0, 0)),
        out_shape=jax.ShapeDtypeStruct((mod.shape[0], s_full, d), F32),
        input_output_aliases={} if out is None else {4: 0},
        name="final",
    )(*args)


GELU_C = math.sqrt(2.0 / math.pi)
SC_LANES = 16
SC_WORKERS = 32
SC_CORES = 2
SC_TOKEN_BLOCK = 32
SC_GATHER_DEPTH = 4
SC_PACK_ROWS = 16
SC_ILV = plsc.PackFormat.INTERLEAVED
N_PAIRS = PEER_HEADS * PEER_TOPK


def _gelu_tanh(x):
    z = GELU_C * (x + 0.044715 * (x * x * x))
    t = 1.0 - 2.0 / (jnp.exp(2.0 * z) + 1.0)
    return x * (0.5 * (1.0 + t))


def _pack_tables(u, v):
    n_rows, d = u.shape
    dw = d // 2
    rb = SC_PACK_ROWS
    nb = SC_GATHER_DEPTH
    rpw = n_rows // SC_WORKERS
    nblk = rpw // rb
    assert rpw % rb == 0 and nblk % nb == 0 and nb % 2 == 0
    mesh = plsc.VectorSubcoreMesh(core_axis_name="c", subcore_axis_name="s")
    out_t = jax.ShapeDtypeStruct((n_rows, dw), jnp.int32)

    @functools.partial(
        pl.kernel, mesh=mesh, out_type=(out_t, out_t),
        scratch_types=[pltpu.VMEM((nb, rb, d), F32), pltpu.VMEM((2, rb, dw), jnp.int32),
                       pltpu.SemaphoreType.DMA((nb,)), pltpu.SemaphoreType.DMA((2,))],
        compiler_params=pltpu.CompilerParams(needs_layout_passes=False),
        name="pack_tables",
    )
    def pack(u_hbm, v_hbm, uo_hbm, vo_hbm, src, dst, isem, osem):
        wid = lax.axis_index("s") * SC_CORES + lax.axis_index("c")
        base = wid * rpw
        for tab, out in ((u_hbm, uo_hbm), (v_hbm, vo_hbm)):
            def in_copy(blk, b):
                return pltpu.make_async_copy(tab.at[pl.ds(base + blk * rb, rb)], src.at[b], isem.at[b])

            def out_copy(blk, s):
                return pltpu.make_async_copy(dst.at[s], out.at[pl.ds(base + blk * rb, rb)], osem.at[s])

            for b in range(nb - 1):
                in_copy(b, b).start()

            @pl.loop(0, nblk, step=nb)
            def _(blk):
                for b in range(nb):
                    ahead = blk + b + nb - 1

                    @pl.when(ahead < nblk)
                    def _():
                        in_copy(ahead, (b + nb - 1) % nb).start()

                    in_copy(blk + b, b).wait()
                    s = b % 2

                    @pl.when(blk + b >= 2)
                    def _():
                        out_copy(blk + b - 2, s).wait()

                    @pl.loop(0, rb)
                    def _(r):
                        @plsc.parallel_loop(0, dw // SC_LANES, 1, unroll=4)
                        def _(c):
                            lo = src[b, r, pl.ds(c * SC_LANES, SC_LANES)]
                            hi = src[b, r, pl.ds(dw + c * SC_LANES, SC_LANES)]
                            dst[s, r, pl.ds(c * SC_LANES, SC_LANES)] = plsc.bitcast(
                                plsc.pack(lo, hi, format=SC_ILV), jnp.int32)

                    out_copy(blk + b, s).start()

            out_copy(nblk - 2, 0).wait()
            out_copy(nblk - 1, 1).wait()

    return pack(u, v)


def _experts(h2, idx, g, u, v):
    t_total, dw = h2.shape
    d = 2 * dw
    tpw = t_total // SC_WORKERS
    tb = min(tpw, SC_TOKEN_BLOCK)
    k = PEER_TOPK
    nh = PEER_HEADS
    n_items = tb * nh
    nbuf = SC_GATHER_DEPTH
    head_bits = nh.bit_length() - 1
    assert tpw % tb == 0 and n_items % nbuf == 0 and nh == 1 << head_bits
    mesh = plsc.VectorSubcoreMesh(core_axis_name="c", subcore_axis_name="s")

    @functools.partial(
        pl.kernel, mesh=mesh,
        out_type=jax.ShapeDtypeStruct((t_total, d), F32),
        scratch_types=[
            pltpu.VMEM((tb, N_PAIRS), jnp.int32),
            pltpu.VMEM((tb, N_PAIRS), F32),
            pltpu.VMEM((tb, dw), jnp.int32),
            pltpu.VMEM((tb, d), F32),
            pltpu.VMEM((nbuf, k, dw), jnp.int32),
            pltpu.VMEM((nbuf, k, dw), jnp.int32),
            pltpu.SemaphoreType.DMA((nbuf,)),
            pltpu.SemaphoreType.DMA((nbuf,)),
        ],
        compiler_params=pltpu.CompilerParams(needs_layout_passes=False),
        name="experts",
    )
    def experts(h_hbm, idx_hbm, g_hbm, u_hbm, v_hbm, out_hbm,
                idx_b, g_b, h_b, out_b, urows, vrows, usem, vsem):
        wid = lax.axis_index("s") * SC_CORES + lax.axis_index("c")
        base = wid * tpw
        lane = lax.iota(jnp.int32, SC_LANES)

        def split(item):
            return lax.shift_right_logical(item, head_bits), lax.bitwise_and(item, nh - 1)

        def copies(item, b):
            tt, hd = split(item)
            ids = idx_b.at[tt, pl.ds(hd * k, k)]
            return (pltpu.make_async_copy(u_hbm.at[ids], urows.at[b], usem.at[b]),
                    pltpu.make_async_copy(v_hbm.at[ids], vrows.at[b], vsem.at[b]))

        def fetch(item, b):
            cu, cv = copies(item, b)
            cu.start()
            cv.start()

        def words(ref, *lead, off):
            return plsc.bitcast(ref[(*lead, pl.ds(off, SC_LANES))], BF16)

        def sum4_unpack(pr):
            return plsc.unpack((pr[0] + pr[1]) + (pr[2] + pr[3]), format=SC_ILV)

        def compute(item, b):
            tt, hd = split(item)
            cu, cv = copies(item, b)
            cu.wait()

            def ubody(j, accs):
                hs = [words(h_b, tt, off=(j * 4 + q) * SC_LANES) for q in range(4)]
                new = []
                for p, a in enumerate(accs):
                    lo, hi = sum4_unpack([words(urows, b, p, off=(j * 4 + q) * SC_LANES) * hs[q]
                                          for q in range(4)])
                    new.append(a + (lo + hi))
                return tuple(new)

            accs = plsc.parallel_loop(
                0, dw // (4 * SC_LANES), 1,
                carry=tuple(jnp.zeros((SC_LANES,), F32) for _ in range(k)))(ubody)
            s = jnp.zeros((SC_LANES,), F32)
            for p in range(k):
                s = jnp.where(lane == p, jnp.sum(accs[p]), s)
            c = g_b[tt, pl.ds(hd * k, k)] * _gelu_tanh(s)
            cbb = []
            for p in range(k):
                cp = jnp.full((SC_LANES,), c[p])
                cbb.append(plsc.pack(cp, cp, format=SC_ILV))
            cv.wait()

            @plsc.parallel_loop(0, dw // SC_LANES, 1, unroll=2)
            def _(ch):
                sa = pl.ds(ch * SC_LANES, SC_LANES)
                sb = pl.ds(dw + ch * SC_LANES, SC_LANES)
                los, his = [], []
                for grp in range(k // 4):
                    lo, hi = sum4_unpack([cbb[grp * 4 + q]
                                          * words(vrows, b, grp * 4 + q, off=ch * SC_LANES)
                                          for q in range(4)])
                    los.append(lo)
                    his.append(hi)
                out_b[tt, sa] = out_b[tt, sa] + ((los[0] + los[1]) + (los[2] + los[3]))
                out_b[tt, sb] = out_b[tt, sb] + ((his[0] + his[1]) + (his[2] + his[3]))

        @pl.loop(0, tpw // tb)
        def _(blk):
            t0 = base + blk * tb
            pltpu.sync_copy(idx_hbm.at[pl.ds(t0, tb)], idx_b)
            pltpu.sync_copy(g_hbm.at[pl.ds(t0, tb)], g_b)
            pltpu.sync_copy(h_hbm.at[pl.ds(t0, tb)], h_b)
            for b in range(nbuf - 1):
                fetch(b, b)

            @pl.loop(0, tb)
            def _(tt):
                @pl.loop(0, d // SC_LANES)
                def _(j):
                    out_b[tt, pl.ds(j * SC_LANES, SC_LANES)] = jnp.zeros((SC_LANES,), F32)

            @pl.loop(0, n_items, step=nbuf)
            def _(it):
                for b in range(nbuf):
                    ahead = it + b + nbuf - 1

                    @pl.when(ahead < n_items)
                    def _():
                        fetch(ahead, (b + nbuf - 1) % nbuf)

                    compute(it + b, b)

            pltpu.sync_copy(out_b, out_hbm.at[pl.ds(t0, tb)])

    return experts(h2, idx, g, u, v)


BATCH_CHUNKS = (1,) * 15
FIRST_ROW_PIECES = 2


def kernel(x, c, w_ada, b_ada, norm1_g, norm2_g, w_in, conv_w, conv_b, b_igate, b_fgate, lam_q1, lam_k1, lam_q2, lam_k2, diff_sub_g, mlstm_norm_g, w_out, peer_w_q, peer_sub_keys, peer_u, peer_v, rel_bias, final_g):
    bsz, s, d = x.shape
    mod = _ada(c, w_ada[0], b_ada[0]).reshape(bsz, 6, d)

    w = w_in[0]
    w_qk = w[:, :2 * D_ATT].astype(BF16)
    w_vt = w[:, 2 * D_ATT:3 * D_ATT].T.astype(BF16)
    w_m = w[:, 3 * D_ATT:3 * D_ATT + 4 * D_MLSTM].astype(BF16)
    w_gt = w[:, 3 * D_ATT + 4 * D_MLSTM:].T.astype(BF16)
    w_o = w_out[0].astype(BF16)
    w_q = peer_w_q[0].astype(BF16)
    sub_keys = peer_sub_keys[0].astype(BF16)
    bias_tiles = _relbias(rel_bias)
    lam4 = jnp.stack([lam_q1[0], lam_k1[0], lam_q2[0], lam_k2[0]])
    gate_bias = jnp.concatenate([b_igate[0], b_fgate[0]])

    assert sum(BATCH_CHUNKS) == bsz - 1
    tables = None
    out = None

    def mix_and_experts(att, hm, x_part, b0, out, s0=0):
        nonlocal tables
        nb, sp, _ = att.shape
        x1, h2, qp = _outproj(att, hm, x_part, mod, norm2_g[0], w_o, w_q, b0)
        idx_t, g_t = _route(qp.reshape(nb * sp, -1), sub_keys)
        if tables is None:
            tables = _pack_tables(peer_u[0], peer_v[0])
        peer_out = _experts(h2.reshape(nb * sp, d // 2), idx_t.T, g_t.T, *tables)
        return _final(x1, peer_out.reshape(nb, sp, d), mod, final_g, b0, out, s, s0)

    sp = s // FIRST_ROW_PIECES
    parts = []
    for p in range(FIRST_ROW_PIECES):
        x_p = lax.slice(x, (0, p * sp, 0), (1, (p + 1) * sp, d))
        parts.append(_inproj(x_p, mod, norm1_g[0], w_qk, w_vt, w_m, w_gt, 0, 1))
        att_qk, att_vt, m_qkvo, gates = (jnp.concatenate(t, axis=a) for t, a in zip(zip(*parts), (1, 2, 1, 2)))
        att = _attn(att_qk, att_vt, bias_tiles, lam4, diff_sub_g[0], q_start=p * sp)
        hm = _mlstm(m_qkvo, gates, gate_bias, conv_w[0], conv_b[0], mlstm_norm_g[0])[:, p * sp:]
        out = mix_and_experts(att, hm, x_p, 0, out, p * sp)

    b0 = 1
    for nb in BATCH_CHUNKS:
        att_qk, att_vt, m_qkvo, gates = _inproj(x, mod, norm1_g[0], w_qk, w_vt, w_m, w_gt, b0, nb)
        att = _attn(att_qk, att_vt, bias_tiles, lam4, diff_sub_g[0])
        hm = _mlstm(m_qkvo, gates, gate_bias, conv_w[0], conv_b[0], mlstm_norm_g[0])
        out = mix_and_experts(att, hm, x, b0, out)
        b0 += nb
    return out
```

```python
import functools
import math

import numpy as np
import jax
import jax.numpy as jnp
from jax import lax
from jax.experimental import pallas as pl
from jax.experimental.pallas import tpu as pltpu
from jax.experimental.pallas import tpu_sc as plsc

F32 = jnp.float32
BF16 = jnp.bfloat16

ATT_HEADS = 4
ATT_QK_DIM = 64
ATT_V_DIM = 128
D_ATT = ATT_HEADS * ATT_V_DIM
M_HEADS = 4
M_DIM = 128
D_MLSTM = M_HEADS * M_DIM
CONV_W = 4
N_BUCKETS = 32
MAX_DIST = 128
N_KEYS = 128
PEER_HEADS = 8
PEER_TOPK = 16
PEER_QDIM = 256
EPS = 1e-6
LAMBDA_INIT = 0.8 - 0.6 * math.exp(-0.3 * 0)

ATT_BLOCK = 256
M_CHUNK = 128
ROW_TILE = 512
ROUTE_TILE = 512
LANES = 128
V7X_VMEM_BYTES = 64 * 1024 * 1024
VMEM_LIMIT = V7X_VMEM_BYTES * 3 // 4

_NT = (((1,), (1,)), ((), ()))


def _rms_rows(x):
    return x * lax.rsqrt(jnp.mean(x * x, axis=-1, keepdims=True) + EPS)


def _sigmoid(x):
    return 1.0 / (1.0 + jnp.exp(-x))


def _pack_bf16_halves(x):
    bits = lax.bitcast_convert_type(x.astype(BF16).astype(F32), jnp.int32)
    n = x.shape[-1] // 2
    return lax.bitwise_or(lax.shift_right_logical(bits[..., :n], 16),
                          lax.bitwise_and(bits[..., n:], jnp.int32(-65536)))


def _ada_kernel(c_ref, w_ref, b_ref, o_ref):
    c = c_ref[...]
    cond = c * _sigmoid(c)
    o_ref[...] = jnp.dot(cond, w_ref[...], preferred_element_type=F32) + b_ref[...]


def _ada(c, w, b):
    bsz, d = c.shape
    n = w.shape[1]
    return pl.pallas_call(
        _ada_kernel,
        grid=(n // d,),
        in_specs=[pl.BlockSpec((bsz, d), lambda j: (0, 0)),
                  pl.BlockSpec((d, d), lambda j: (0, j)),
                  pl.BlockSpec((1, d), lambda j: (0, j))],
        out_specs=pl.BlockSpec((bsz, d), lambda j: (0, j)),
        out_shape=jax.ShapeDtypeStruct((bsz, n), F32),
        name="ada",
    )(c, w, b.reshape(1, n))


def _inproj_kernel(x_ref, mod_ref, g_ref, wa_ref, wvt_ref, wm_ref, wg_ref,
                   oa_ref, ovt_ref, om_ref, og_ref):
    x = x_ref[0]
    mod = mod_ref[0]
    h = _rms_rows(x) * g_ref[...]
    h = h * (1.0 + mod[1:2, :]) + mod[0:1, :]
    hb = h.astype(BF16)
    oa_ref[0] = jnp.dot(hb, wa_ref[...], preferred_element_type=F32).astype(BF16)
    ovt_ref[0] = lax.dot_general(wvt_ref[...], hb, _NT, preferred_element_type=F32).astype(BF16)
    om_ref[0] = jnp.dot(hb, wm_ref[...], preferred_element_type=F32).astype(BF16)
    og_ref[0] = lax.dot_general(wg_ref[...], hb, _NT, preferred_element_type=F32)


def _inproj(x, mod, g, w_qk, w_vt, w_m, w_gt, b0, bsz):
    _, s, d = x.shape
    tm = min(ROW_TILE, s)
    na, nv, nm, ng = w_qk.shape[1], w_vt.shape[0], w_m.shape[1], w_gt.shape[0]
    const = lambda shape: pl.BlockSpec(shape, lambda b, i: (0, 0))
    return pl.pallas_call(
        _inproj_kernel,
        grid=(bsz, s // tm),
        in_specs=[pl.BlockSpec((1, tm, d), lambda b, i: (b + b0, i, 0)),
                  pl.BlockSpec((1, 6, d), lambda b, i: (b + b0, 0, 0)),
                  const((1, d)), const((d, na)), const((nv, d)), const((d, nm)), const((ng, d))],
        out_specs=[pl.BlockSpec((1, tm, na), lambda b, i: (b, i, 0)),
                   pl.BlockSpec((1, nv, tm), lambda b, i: (b, 0, i)),
                   pl.BlockSpec((1, tm, nm), lambda b, i: (b, i, 0)),
                   pl.BlockSpec((1, ng, tm), lambda b, i: (b, 0, i))],
        out_shape=[jax.ShapeDtypeStruct((bsz, s, na), BF16),
                   jax.ShapeDtypeStruct((bsz, nv, s), BF16),
                   jax.ShapeDtypeStruct((bsz, s, nm), BF16),
                   jax.ShapeDtypeStruct((bsz, ng, s), F32)],
        compiler_params=pltpu.CompilerParams(vmem_limit_bytes=VMEM_LIMIT),
        name="inproj",
    )(x, mod, g.reshape(1, d), w_qk, w_vt, w_m, w_gt)


def _rel_buckets():
    n = np.arange(2 * ATT_BLOCK)
    max_exact = N_BUCKETS // 2
    nf = np.maximum(n, 1).astype(np.float64)
    large = max_exact + (np.log(nf / max_exact) / math.log(MAX_DIST / max_exact)
                         * (N_BUCKETS - max_exact)).astype(np.int64)
    large = np.minimum(large, N_BUCKETS - 1)
    bucket = np.where(n < max_exact, n, large)
    assert (bucket[ATT_BLOCK + 1:] == N_BUCKETS - 1).all() and 2 * ATT_BLOCK > MAX_DIST
    qk = np.arange(ATT_BLOCK)[None, :] - np.arange(ATT_BLOCK)[:, None]
    tiles = np.stack([bucket[np.maximum(qk, 0)], bucket[ATT_BLOCK + qk]])
    return tiles.astype(np.int32)


def _relbias_kernel(rb_ref, bk_ref, o_ref):
    h = pl.program_id(0)
    bk = bk_ref[...]
    acc = jnp.zeros(bk.shape, F32)
    for b in range(N_BUCKETS):
        acc = jnp.where(bk == b, rb_ref[b, h], acc)
    o_ref[0] = acc


def _relbias(rel_bias):
    tiles = jnp.asarray(_rel_buckets())
    return pl.pallas_call(
        _relbias_kernel,
        grid=(ATT_HEADS,),
        in_specs=[pl.BlockSpec(memory_space=pltpu.SMEM),
                  pl.BlockSpec((2, ATT_BLOCK, ATT_BLOCK), lambda h: (0, 0, 0))],
        out_specs=pl.BlockSpec((1, 2, ATT_BLOCK, ATT_BLOCK), lambda h: (h, 0, 0, 0)),
        out_shape=jax.ShapeDtypeStruct((ATT_HEADS, 2, ATT_BLOCK, ATT_BLOCK), F32),
        name="relbias",
    )(rel_bias, tiles)


def _attn_kernel(q_ref, k_ref, vt_ref, bias_ref, lam_ref, subg_ref, o_ref,
                 qz_ref, m_ref, l_ref, acc_ref, *, q0):
    tq = ATT_BLOCK
    qi = pl.program_id(2) + q0
    scale = ATT_QK_DIM ** -0.5

    q = q_ref[0]
    lane = lax.broadcasted_iota(jnp.int32, q.shape, 1)
    zero = jnp.zeros_like(q)
    qz_ref[0:tq, :] = jnp.where(lane < ATT_QK_DIM, q, zero)
    qz_ref[tq:2 * tq, :] = jnp.where(lane >= ATT_QK_DIM, q, zero)
    m_ref[...] = jnp.full(m_ref.shape, -jnp.inf, F32)
    l_ref[...] = jnp.zeros(l_ref.shape, F32)
    acc_ref[...] = jnp.zeros(acc_ref.shape, F32)

    def step(j, bias, masked):
        start = pl.multiple_of(j * tq, tq)
        k = k_ref[0, pl.ds(start, tq), :]
        vt = vt_ref[0, :, pl.ds(start, tq)]
        s = lax.dot_general(k, qz_ref[...], _NT, preferred_element_type=F32) * scale
        if isinstance(bias, tuple):
            s = s + jnp.concatenate([bias[0], bias[0]], axis=1)
        else:
            s = s + bias
        if masked:
            key = lax.broadcasted_iota(jnp.int32, (tq, tq), 0)
            qry = lax.broadcasted_iota(jnp.int32, (tq, tq), 1)
            keep = jnp.concatenate([key <= qry, key <= qry], axis=1)
            s = jnp.where(keep, s, jnp.finfo(F32).min)
        m_old = m_ref[...]
        m_new = jnp.maximum(m_old, jnp.max(s, axis=0, keepdims=True))
        alpha = jnp.exp(m_old - m_new)
        p = jnp.exp(s - m_new)
        l_ref[...] = alpha * l_ref[...] + jnp.sum(p, axis=0, keepdims=True)
        acc_ref[...] = alpha * acc_ref[...] + jnp.dot(vt, p.astype(BF16), preferred_element_type=F32)
        m_ref[...] = m_new

    far_bias = bias_ref[0, 1, 0:1, tq - 1:tq]

    def far_body(j, carry):
        step(j, far_bias, False)
        return carry

    lax.fori_loop(0, jnp.maximum(qi - 1, 0), far_body, 0)

    @pl.when(qi >= 1)
    def _():
        step(qi - 1, (bias_ref[0, 1],), False)

    step(qi, (bias_ref[0, 0],), True)

    lam = (jnp.exp(jnp.sum(lam_ref[0:1, :] * lam_ref[1:2, :], axis=-1, keepdims=True))
           - jnp.exp(jnp.sum(lam_ref[2:3, :] * lam_ref[3:4, :], axis=-1, keepdims=True))
           + LAMBDA_INIT)
    ot = acc_ref[...] / l_ref[...]
    o = (ot[:, 0:tq] - lam * ot[:, tq:2 * tq]).T
    o = _rms_rows(o) * subg_ref[...] * (1.0 - LAMBDA_INIT)
    o_ref[0] = o.astype(o_ref.dtype)


def _attn(att_qk, att_vt, bias_tiles, lam4, sub_g, q_start=0):
    bsz, s, _ = att_qk.shape
    tq = ATT_BLOCK
    nh = ATT_HEADS
    q0 = q_start // tq
    nq = s // tq - q0
    return pl.pallas_call(
        functools.partial(_attn_kernel, q0=q0),
        grid=(bsz, nh, nq),
        in_specs=[pl.BlockSpec((1, tq, ATT_V_DIM), lambda b, h, i: (b, i + q0, h)),
                  pl.BlockSpec((1, s, ATT_V_DIM), lambda b, h, i: (b, 0, nh + h)),
                  pl.BlockSpec((1, ATT_V_DIM, s), lambda b, h, i: (b, h, 0)),
                  pl.BlockSpec((1, 2, tq, tq), lambda b, h, i: (h, 0, 0, 0)),
                  pl.BlockSpec((4, ATT_QK_DIM), lambda b, h, i: (0, 0)),
                  pl.BlockSpec((1, ATT_V_DIM), lambda b, h, i: (0, 0))],
        out_specs=pl.BlockSpec((1, tq, ATT_V_DIM), lambda b, h, i: (b, i, h)),
        out_shape=jax.ShapeDtypeStruct((bsz, nq * tq, D_ATT), BF16),
        scratch_shapes=[pltpu.VMEM((2 * tq, ATT_V_DIM), BF16),
                        pltpu.VMEM((1, 2 * tq), F32),
                        pltpu.VMEM((1, 2 * tq), F32),
                        pltpu.VMEM((ATT_V_DIM, 2 * tq), F32)],
        compiler_params=pltpu.CompilerParams(vmem_limit_bytes=VMEM_LIMIT),
        name="attn",
    )(att_qk, att_qk, att_vt, bias_tiles, lam4, sub_g.reshape(1, ATT_V_DIM))


def _mlstm_kernel(q_ref, k_ref, v_ref, o_ref, gi_ref, gf_ref, bias_ref, cwq_ref, cwk_ref,
                  cbq_ref, cbk_ref, ng_ref, out_ref, qs_ref, ks_ref, b_ref, ig_ref):
    s = q_ref.shape[1]
    L = M_CHUNK
    nc = s // L
    h = pl.program_id(1)

    row = lax.broadcasted_iota(jnp.int32, (s, M_DIM), 0)

    def conv_silu(x_ref, w_ref, cb_ref):
        x = x_ref[0].astype(F32)
        w = w_ref[...]
        out = None
        for j in range(CONV_W):
            shift = CONV_W - 1 - j
            xs = x if shift == 0 else jnp.where(row >= shift, pltpu.roll(x, shift, 0), 0.0)
            term = xs * w[j:j + 1, :]
            out = term if out is None else out + term
        out = out + cb_ref[...]
        return out * _sigmoid(out)

    qs_ref[...] = conv_silu(q_ref, cwq_ref, cbq_ref).astype(BF16)
    ks_ref[...] = (conv_silu(k_ref, cwk_ref, cbk_ref) * (M_DIM ** -0.5)).astype(BF16)

    ig = gi_ref[0, 0] + bias_ref[h]
    f = gf_ref[0, 0] + bias_ref[M_HEADS + h]
    logf = jnp.minimum(f, 0.0) - jnp.log(1.0 + jnp.exp(-jnp.abs(f)))
    r = lax.broadcasted_iota(jnp.int32, (L, L), 0)
    c = lax.broadcasted_iota(jnp.int32, (L, L), 1)
    tri = (r <= c).astype(F32)
    b_ref[...] = jnp.dot(logf, tri, preferred_element_type=F32,
                         precision=lax.Precision.HIGHEST)
    ig_ref[...] = ig
    eye = r == c
    causal = c <= r

    def to_col(x_row):
        return jnp.sum(jnp.where(eye, x_row, 0.0), axis=1, keepdims=True)

    def chunk(ci, carry):
        C, n, m = carry
        start = pl.multiple_of(ci * L, L)
        qc = qs_ref[pl.ds(start, L), :]
        kc = ks_ref[pl.ds(start, L), :]
        vc = v_ref[0, pl.ds(start, L), :]
        b_r = b_ref[pl.ds(ci, 1), :]
        ig_r = ig_ref[pl.ds(ci, 1), :]
        b_last = b_r[:, L - 1:L]
        a_r = b_last - b_r + ig_r
        b_c = to_col(b_r)
        a_c = to_col(a_r)

        logd = jnp.where(causal, b_c - b_r + ig_r, -jnp.inf)
        m_inter = b_c + m
        m_j = jnp.maximum(jnp.max(logd, axis=1, keepdims=True), m_inter)
        w = jnp.exp(logd - m_j)
        sqk = lax.dot_general(qc, kc, _NT, preferred_element_type=F32) * w
        inter = jnp.exp(m_inter - m_j)
        num = (jnp.dot(sqk.astype(BF16), vc, preferred_element_type=F32)
               + inter * jnp.dot(qc, C.astype(BF16), preferred_element_type=F32))
        den = (jnp.sum(sqk, axis=1, keepdims=True)
               + inter * jnp.sum(qc.astype(F32) * n, axis=1, keepdims=True))
        hc = num / jnp.maximum(jnp.abs(den), jnp.exp(-m_j))

        og = _sigmoid(o_ref[0, pl.ds(start, L), :].astype(F32))
        out_ref[0, pl.ds(start, L), :] = (_rms_rows(og * hc) * ng_ref[...]).astype(out_ref.dtype)

        m_new = jnp.maximum(b_last + m, jnp.max(a_r, axis=1, keepdims=True))
        decay = jnp.exp(b_last + m - m_new)
        kw = kc.astype(F32) * jnp.exp(a_c - m_new)
        C_new = decay * C + jnp.dot(kw.T.astype(BF16), vc, preferred_element_type=F32)
        n_new = decay * n + jnp.sum(kw, axis=0, keepdims=True)
        return C_new, n_new, m_new

    init = (jnp.zeros((M_DIM, M_DIM), F32), jnp.zeros((1, M_DIM), F32), jnp.zeros((1, 1), F32))
    lax.fori_loop(0, nc, chunk, init)


def _mlstm(m_qkvo, gates, gate_bias, conv_w, conv_b, norm_g):
    bsz, s, _ = m_qkvo.shape
    L = M_CHUNK
    nc = s // L
    nh = M_HEADS
    d = M_DIM
    g4 = gates.reshape(bsz, 2 * nh, nc, L)
    seq = lambda off: pl.BlockSpec((1, s, d), lambda b, h: (b, 0, off + h))
    return pl.pallas_call(
        _mlstm_kernel,
        grid=(bsz, nh),
        in_specs=[seq(0), seq(nh), seq(2 * nh), seq(3 * nh),
                  pl.BlockSpec((1, 1, nc, L), lambda b, h: (b, h, 0, 0)),
                  pl.BlockSpec((1, 1, nc, L), lambda b, h: (b, nh + h, 0, 0)),
                  pl.BlockSpec(memory_space=pltpu.SMEM),
                  pl.BlockSpec((CONV_W, d), lambda b, h: (0, h)),
                  pl.BlockSpec((CONV_W, d), lambda b, h: (0, nh + h)),
                  pl.BlockSpec((1, d), lambda b, h: (0, h)),
                  pl.BlockSpec((1, d), lambda b, h: (0, nh + h)),
                  pl.BlockSpec((1, d), lambda b, h: (0, h))],
        out_specs=pl.BlockSpec((1, s, d), lambda b, h: (b, 0, h)),
        out_shape=jax.ShapeDtypeStruct((bsz, s, D_MLSTM), BF16),
        scratch_shapes=[pltpu.VMEM((s, d), BF16), pltpu.VMEM((s, d), BF16),
                        pltpu.VMEM((nc, L), F32), pltpu.VMEM((nc, L), F32)],
        compiler_params=pltpu.CompilerParams(vmem_limit_bytes=VMEM_LIMIT),
        name="mlstm",
    )(m_qkvo, m_qkvo, m_qkvo, m_qkvo, g4, g4, gate_bias, conv_w, conv_w,
      conv_b.reshape(1, -1), conv_b.reshape(1, -1), norm_g.reshape(1, -1))


def _outproj_kernel(att_ref, hm_ref, x_ref, mod_ref, g2_ref, wo_ref, wq_ref,
                    x1_ref, h2_ref, qp_ref):
    mod = mod_ref[0]
    y = (jnp.dot(att_ref[0], wo_ref[0:D_ATT, :], preferred_element_type=F32)
         + jnp.dot(hm_ref[0], wo_ref[D_ATT:, :], preferred_element_type=F32))
    x1 = x_ref[0] + mod[2:3, :] * y
    x1_ref[0] = x1
    h2 = _rms_rows(x1) * g2_ref[...]
    h2 = h2 * (1.0 + mod[4:5, :]) + mod[3:4, :]
    hb = h2.astype(BF16)
    h2_ref[0] = _pack_bf16_halves(hb)
    qp_ref[0] = jnp.dot(hb, wq_ref[...], preferred_element_type=F32).astype(BF16)


def _outproj(att, hm, x, mod, g2, w_out, w_q, b0):
    bsz, s, _ = att.shape
    d = x.shape[-1]
    tm = min(ROW_TILE, s)
    nq = w_q.shape[1]
    tile = lambda n: pl.BlockSpec((1, tm, n), lambda b, i: (b, i, 0))
    return pl.pallas_call(
        _outproj_kernel,
        grid=(bsz, s // tm),
        in_specs=[tile(D_ATT), tile(D_MLSTM),
                  pl.BlockSpec((1, tm, d), lambda b, i: (b + b0, i, 0)),
                  pl.BlockSpec((1, 6, d), lambda b, i: (b + b0, 0, 0)),
                  pl.BlockSpec((1, d), lambda b, i: (0, 0)),
                  pl.BlockSpec((d, d), lambda b, i: (0, 0)),
                  pl.BlockSpec((d, nq), lambda b, i: (0, 0))],
        out_specs=[tile(d), tile(d // 2), tile(nq)],
        out_shape=[jax.ShapeDtypeStruct((bsz, s, d), F32),
                   jax.ShapeDtypeStruct((bsz, s, d // 2), jnp.int32),
                   jax.ShapeDtypeStruct((bsz, s, nq), BF16)],
        compiler_params=pltpu.CompilerParams(vmem_limit_bytes=VMEM_LIMIT),
        name="outproj",
    )(att, hm, x, mod, g2.reshape(1, d), w_out, w_q)


def _top16_rows(blocks, ids):
    big = jnp.float32(1 << 20)
    vals, pos = [], []
    for _ in range(PEER_TOPK):
        m = functools.reduce(jnp.maximum, blocks)
        m = jnp.max(m, axis=0, keepdims=True)
        cand = functools.reduce(jnp.minimum,
                                [jnp.where(b == m, i, big) for b, i in zip(blocks, ids)])
        p = jnp.min(cand, axis=0, keepdims=True)
        blocks = [jnp.where(i == p, -jnp.inf, b) for b, i in zip(blocks, ids)]
        vals.append(m)
        pos.append(p)
    return jnp.concatenate(vals, axis=0), jnp.concatenate(pos, axis=0)


def _pick_rows(table, sel):
    out = jnp.zeros(sel.shape, table.dtype)
    for r in range(PEER_TOPK):
        out = jnp.where(sel == r, table[r:r + 1, :], out)
    return out


def _pair_candidates(a, b):
    k, sub = PEER_TOPK, 8
    j_id = lax.broadcasted_iota(jnp.int32, (sub, LANES), 0).astype(F32)
    blocks = [a[0:1, :] + b[0:sub, :], a[0:1, :] + b[sub:k, :], a[1:2, :] + b[0:sub, :]]
    ids = [j_id, j_id + sub, j_id + k]
    for i in range(2, sub):
        blocks.append(jnp.where(j_id < k // (i + 1), a[i:i + 1, :] + b[0:sub, :], -jnp.inf))
        ids.append(j_id + i * k)
    blocks.append(a[sub:k, :] + b[0:1, :])
    ids.append((j_id + sub) * k)
    return blocks, ids


def _route_kernel(q_ref, keys_ref, idx_ref, g_ref):
    k = PEER_TOPK
    half = PEER_QDIM // 2
    key_id = lax.broadcasted_iota(jnp.int32, (N_KEYS, LANES), 0).astype(F32)
    for t in range(q_ref.shape[0] // LANES):
        rows = pl.ds(t * LANES, LANES)
        sv, si = [], []
        for p in range(2):
            qh = q_ref[rows, p * half:(p + 1) * half]
            s = lax.dot_general(keys_ref[0, p], qh, _NT, preferred_element_type=F32)
            v, i = _top16_rows([s], [key_id])
            sv.append(v)
            si.append(i.astype(jnp.int32))
        top_s, pos = _top16_rows(*_pair_candidates(sv[0], sv[1]))
        pos = pos.astype(jnp.int32)
        idx = (_pick_rows(si[0], lax.shift_right_logical(pos, k.bit_length() - 1)) * N_KEYS
               + _pick_rows(si[1], lax.bitwise_and(pos, k - 1)))
        e = jnp.exp(top_s - top_s[0:1, :])
        idx_ref[:, t * LANES:(t + 1) * LANES] = idx
        g_ref[:, t * LANES:(t + 1) * LANES] = e / jnp.sum(e, axis=0, keepdims=True)


def _route(qp, sub_keys):
    t, _ = qp.shape
    tt = min(ROUTE_TILE, t)
    k = PEER_TOPK
    return pl.pallas_call(
        _route_kernel,
        grid=(t // tt, PEER_HEADS),
        in_specs=[pl.BlockSpec((tt, PEER_QDIM), lambda i, h: (i, h)),
                  pl.BlockSpec((1, 2, N_KEYS, PEER_QDIM // 2), lambda i, h: (h, 0, 0, 0))],
        out_specs=[pl.BlockSpec((k, tt), lambda i, h: (h, i)),
                   pl.BlockSpec((k, tt), lambda i, h: (h, i))],
        out_shape=[jax.ShapeDtypeStruct((PEER_HEADS * k, t), jnp.int32),
                   jax.ShapeDtypeStruct((PEER_HEADS * k, t), F32)],
        name="route",
    )(qp, sub_keys)


def _final_kernel(x1_ref, po_ref, mod_ref, g_ref, o_ref):
    x2 = x1_ref[0] + mod_ref[0][5:6, :] * po_ref[0]
    o_ref[0] = _rms_rows(x2) * g_ref[...]


def _final_kernel_into(x1_ref, po_ref, mod_ref, g_ref, prev_ref, o_ref):
    del prev_ref
    _final_kernel(x1_ref, po_ref, mod_ref, g_ref, o_ref)


def _final(x1, peer_out, mod, final_g, b0, out, s_full, s0=0):
    bsz, s, d = x1.shape
    tm = min(ROW_TILE, s)
    i0 = s0 // tm
    tile = pl.BlockSpec((1, tm, d), lambda b, i: (b, i, 0))
    in_specs = [tile, tile, pl.BlockSpec((1, 6, d), lambda b, i: (b + b0, 0, 0)),
                pl.BlockSpec((1, d), lambda b, i: (0, 0))]
    args = (x1, peer_out, mod, final_g.reshape(1, d))
    if out is not None:
        in_specs.append(pl.BlockSpec(memory_space=pl.ANY))
        args += (out,)
    return pl.pallas_call(
        _final_kernel if out is None else _final_kernel_into,
        grid=(bsz, s // tm),
        in_specs=in_specs,
        out_specs=pl.BlockSpec((1, tm, d), lambda b, i: (b + b0, i + i0, 0)),
        out_shape=jax.ShapeDtypeStruct((mod.shape[0], s_full, d), F32),
        input_output_aliases={} if out is None else {4: 0},
        name="final",
    )(*args)


GELU_C = math.sqrt(2.0 / math.pi)
SC_LANES = 16
SC_WORKERS = 32
SC_CORES = 2
SC_TOKEN_BLOCK = 32
SC_GATHER_DEPTH = 4
SC_PACK_ROWS = 16
SC_ILV = plsc.PackFormat.INTERLEAVED
N_PAIRS = PEER_HEADS * PEER_TOPK


def _gelu_tanh(x):
    z = GELU_C * (x + 0.044715 * (x * x * x))
    t = 1.0 - 2.0 / (jnp.exp(2.0 * z) + 1.0)
    return x * (0.5 * (1.0 + t))


def _pack_tables(u, v):
    n_rows, d = u.shape
    dw = d // 2
    rb = SC_PACK_ROWS
    nb = SC_GATHER_DEPTH
    rpw = n_rows // SC_WORKERS
    nblk = rpw // rb
    assert rpw % rb == 0 and nblk % nb == 0 and nb % 2 == 0
    mesh = plsc.VectorSubcoreMesh(core_axis_name="c", subcore_axis_name="s")
    out_t = jax.ShapeDtypeStruct((n_rows, dw), jnp.int32)

    @functools.partial(
        pl.kernel, mesh=mesh, out_type=(out_t, out_t),
        scratch_types=[pltpu.VMEM((nb, rb, d), F32), pltpu.VMEM((2, rb, dw), jnp.int32),
                       pltpu.SemaphoreType.DMA((nb,)), pltpu.SemaphoreType.DMA((2,))],
        compiler_params=pltpu.CompilerParams(needs_layout_passes=False),
        name="pack_tables",
    )
    def pack(u_hbm, v_hbm, uo_hbm, vo_hbm, src, dst, isem, osem):
        wid = lax.axis_index("s") * SC_CORES + lax.axis_index("c")
        base = wid * rpw
        for tab, out in ((u_hbm, uo_hbm), (v_hbm, vo_hbm)):
            def in_copy(blk, b):
                return pltpu.make_async_copy(tab.at[pl.ds(base + blk * rb, rb)], src.at[b], isem.at[b])

            def out_copy(blk, s):
                return pltpu.make_async_copy(dst.at[s], out.at[pl.ds(base + blk * rb, rb)], osem.at[s])

            for b in range(nb - 1):
                in_copy(b, b).start()

            @pl.loop(0, nblk, step=nb)
            def _(blk):
                for b in range(nb):
                    ahead = blk + b + nb - 1

                    @pl.when(ahead < nblk)
                    def _():
                        in_copy(ahead, (b + nb - 1) % nb).start()

                    in_copy(blk + b, b).wait()
                    s = b % 2

                    @pl.when(blk + b >= 2)
                    def _():
                        out_copy(blk + b - 2, s).wait()

                    @pl.loop(0, rb)
                    def _(r):
                        @plsc.parallel_loop(0, dw // SC_LANES, 1, unroll=4)
                        def _(c):
                            lo = src[b, r, pl.ds(c * SC_LANES, SC_LANES)]
                            hi = src[b, r, pl.ds(dw + c * SC_LANES, SC_LANES)]
                            dst[s, r, pl.ds(c * SC_LANES, SC_LANES)] = plsc.bitcast(
                                plsc.pack(lo, hi, format=SC_ILV), jnp.int32)

                    out_copy(blk + b, s).start()

            out_copy(nblk - 2, 0).wait()
            out_copy(nblk - 1, 1).wait()

    return pack(u, v)


def _experts(h2, idx, g, u, v):
    t_total, dw = h2.shape
    d = 2 * dw
    tpw = t_total // SC_WORKERS
    tb = min(tpw, SC_TOKEN_BLOCK)
    k = PEER_TOPK
    nh = PEER_HEADS
    n_items = tb * nh
    nbuf = SC_GATHER_DEPTH
    head_bits = nh.bit_length() - 1
    assert tpw % tb == 0 and n_items % nbuf == 0 and nh == 1 << head_bits
    mesh = plsc.VectorSubcoreMesh(core_axis_name="c", subcore_axis_name="s")

    @functools.partial(
        pl.kernel, mesh=mesh,
        out_type=jax.ShapeDtypeStruct((t_total, d), F32),
        scratch_types=[
            pltpu.VMEM((tb, N_PAIRS), jnp.int32),
            pltpu.VMEM((tb, N_PAIRS), F32),
            pltpu.VMEM((tb, dw), jnp.int32),
            pltpu.VMEM((tb, d), F32),
            pltpu.VMEM((nbuf, k, dw), jnp.int32),
            pltpu.VMEM((nbuf, k, dw), jnp.int32),
            pltpu.SemaphoreType.DMA((nbuf,)),
            pltpu.SemaphoreType.DMA((nbuf,)),
        ],
        compiler_params=pltpu.CompilerParams(needs_layout_passes=False),
        name="experts",
    )
    def experts(h_hbm, idx_hbm, g_hbm, u_hbm, v_hbm, out_hbm,
                idx_b, g_b, h_b, out_b, urows, vrows, usem, vsem):
        wid = lax.axis_index("s") * SC_CORES + lax.axis_index("c")
        base = wid * tpw
        lane = lax.iota(jnp.int32, SC_LANES)

        def split(item):
            return lax.shift_right_logical(item, head_bits), lax.bitwise_and(item, nh - 1)

        def copies(item, b):
            tt, hd = split(item)
            ids = idx_b.at[tt, pl.ds(hd * k, k)]
            return (pltpu.make_async_copy(u_hbm.at[ids], urows.at[b], usem.at[b]),
                    pltpu.make_async_copy(v_hbm.at[ids], vrows.at[b], vsem.at[b]))

        def fetch(item, b):
            cu, cv = copies(item, b)
            cu.start()
            cv.start()

        def words(ref, *lead, off):
            return plsc.bitcast(ref[(*lead, pl.ds(off, SC_LANES))], BF16)

        def sum4_unpack(pr):
            return plsc.unpack((pr[0] + pr[1]) + (pr[2] + pr[3]), format=SC_ILV)

        def compute(item, b):
            tt, hd = split(item)
            cu, cv = copies(item, b)
            cu.wait()

            def ubody(j, accs):
                hs = [words(h_b, tt, off=(j * 4 + q) * SC_LANES) for q in range(4)]
                new = []
                for p, a in enumerate(accs):
                    lo, hi = sum4_unpack([words(urows, b, p, off=(j * 4 + q) * SC_LANES) * hs[q]
                                          for q in range(4)])
                    new.append(a + (lo + hi))
                return tuple(new)

            accs = plsc.parallel_loop(
                0, dw // (4 * SC_LANES), 1,
                carry=tuple(jnp.zeros((SC_LANES,), F32) for _ in range(k)))(ubody)
            s = jnp.zeros((SC_LANES,), F32)
            for p in range(k):
                s = jnp.where(lane == p, jnp.sum(accs[p]), s)
            c = g_b[tt, pl.ds(hd * k, k)] * _gelu_tanh(s)
            cbb = []
            for p in range(k):
                cp = jnp.full((SC_LANES,), c[p])
                cbb.append(plsc.pack(cp, cp, format=SC_ILV))
            cv.wait()

            @plsc.parallel_loop(0, dw // SC_LANES, 1, unroll=2)
            def _(ch):
                sa = pl.ds(ch * SC_LANES, SC_LANES)
                sb = pl.ds(dw + ch * SC_LANES, SC_LANES)
                los, his = [], []
                for grp in range(k // 4):
                    lo, hi = sum4_unpack([cbb[grp * 4 + q]
                                          * words(vrows, b, grp * 4 + q, off=ch * SC_LANES)
                                          for q in range(4)])
                    los.append(lo)
                    his.append(hi)
                out_b[tt, sa] = out_b[tt, sa] + ((los[0] + los[1]) + (los[2] + los[3]))
                out_b[tt, sb] = out_b[tt, sb] + ((his[0] + his[1]) + (his[2] + his[3]))

        @pl.loop(0, tpw // tb)
        def _(blk):
            t0 = base + blk * tb
            pltpu.sync_copy(idx_hbm.at[pl.ds(t0, tb)], idx_b)
            pltpu.sync_copy(g_hbm.at[pl.ds(t0, tb)], g_b)
            pltpu.sync_copy(h_hbm.at[pl.ds(t0, tb)], h_b)
            for b in range(nbuf - 1):
                fetch(b, b)

            @pl.loop(0, tb)
            def _(tt):
                @pl.loop(0, d // SC_LANES)
                def _(j):
                    out_b[tt, pl.ds(j * SC_LANES, SC_LANES)] = jnp.zeros((SC_LANES,), F32)

            @pl.loop(0, n_items)
            def _(it):
                ahead = it + nbuf - 1

                @pl.when(ahead < n_items)
                def _():
                    fetch(ahead, lax.bitwise_and(ahead, nbuf - 1))

                compute(it, lax.bitwise_and(it, nbuf - 1))

            pltpu.sync_copy(out_b, out_hbm.at[pl.ds(t0, tb)])

    return experts(h2, idx, g, u, v)


BATCH_CHUNKS = (1,) * 15
FIRST_ROW_PIECES = 2


def kernel(x, c, w_ada, b_ada, norm1_g, norm2_g, w_in, conv_w, conv_b, b_igate, b_fgate, lam_q1, lam_k1, lam_q2, lam_k2, diff_sub_g, mlstm_norm_g, w_out, peer_w_q, peer_sub_keys, peer_u, peer_v, rel_bias, final_g):
    bsz, s, d = x.shape
    mod = _ada(c, w_ada[0], b_ada[0]).reshape(bsz, 6, d)

    w = w_in[0]
    w_qk = w[:, :2 * D_ATT].astype(BF16)
    w_vt = w[:, 2 * D_ATT:3 * D_ATT].T.astype(BF16)
    w_m = w[:, 3 * D_ATT:3 * D_ATT + 4 * D_MLSTM].astype(BF16)
    w_gt = w[:, 3 * D_ATT + 4 * D_MLSTM:].T.astype(BF16)
    w_o = w_out[0].astype(BF16)
    w_q = peer_w_q[0].astype(BF16)
    sub_keys = peer_sub_keys[0].astype(BF16)
    bias_tiles = _relbias(rel_bias)
    lam4 = jnp.stack([lam_q1[0], lam_k1[0], lam_q2[0], lam_k2[0]])
    gate_bias = jnp.concatenate([b_igate[0], b_fgate[0]])

    assert sum(BATCH_CHUNKS) == bsz - 1
    tables = None
    out = None

    def mix_and_experts(att, hm, x_part, b0, out, s0=0):
        nonlocal tables
        nb, sp, _ = att.shape
        x1, h2, qp = _outproj(att, hm, x_part, mod, norm2_g[0], w_o, w_q, b0)
        idx_t, g_t = _route(qp.reshape(nb * sp, -1), sub_keys)
        if tables is None:
            tables = _pack_tables(peer_u[0], peer_v[0])
        peer_out = _experts(h2.reshape(nb * sp, d // 2), idx_t.T, g_t.T, *tables)
        return _final(x1, peer_out.reshape(nb, sp, d), mod, final_g, b0, out, s, s0)

    sp = s // FIRST_ROW_PIECES
    parts = []
    for p in range(FIRST_ROW_PIECES):
        x_p = lax.slice(x, (0, p * sp, 0), (1, (p + 1) * sp, d))
        parts.append(_inproj(x_p, mod, norm1_g[0], w_qk, w_vt, w_m, w_gt, 0, 1))
        att_qk, att_vt, m_qkvo, gates = (jnp.concatenate(t, axis=a) for t, a in zip(zip(*parts), (1, 2, 1, 2)))
        att = _attn(att_qk, att_vt, bias_tiles, lam4, diff_sub_g[0], q_start=p * sp)
        hm = _mlstm(m_qkvo, gates, gate_bias, conv_w[0], conv_b[0], mlstm_norm_g[0])[:, p * sp:]
        out = mix_and_experts(att, hm, x_p, 0, out, p * sp)

    b0 = 1
    for nb in BATCH_CHUNKS:
        att_qk, att_vt, m_qkvo, gates = _inproj(x, mod, norm1_g[0], w_qk, w_vt, w_m, w_gt, b0, nb)
        att = _attn(att_qk, att_vt, bias_tiles, lam4, diff_sub_g[0])
        hm = _mlstm(m_qkvo, gates, gate_bias, conv_w[0], conv_b[0], mlstm_norm_g[0])
        out = mix_and_experts(att, hm, x, b0, out)
        b0 += nb
    return out
```

```python
import functools
import math

import numpy as np
import jax
import jax.numpy as jnp
from jax import lax
from jax.experimental import pallas as pl
from jax.experimental.pallas import tpu as pltpu
from jax.experimental.pallas import tpu_sc as plsc

F32 = jnp.float32
BF16 = jnp.bfloat16

ATT_HEADS = 4
ATT_QK_DIM = 64
ATT_V_DIM = 128
D_ATT = ATT_HEADS * ATT_V_DIM
M_HEADS = 4
M_DIM = 128
D_MLSTM = M_HEADS * M_DIM
CONV_W = 4
N_BUCKETS = 32
MAX_DIST = 128
N_KEYS = 128
PEER_HEADS = 8
PEER_TOPK = 16
PEER_QDIM = 256
EPS = 1e-6
LAMBDA_INIT = 0.8 - 0.6 * math.exp(-0.3 * 0)

ATT_BLOCK = 256
M_CHUNK = 128
ROW_TILE = 512
ROUTE_TILE = 512
LANES = 128
V7X_VMEM_BYTES = 64 * 1024 * 1024
VMEM_LIMIT = V7X_VMEM_BYTES * 3 // 4

_NT = (((1,), (1,)), ((), ()))


def _rms_rows(x):
    return x * lax.rsqrt(jnp.mean(x * x, axis=-1, keepdims=True) + EPS)


def _sigmoid(x):
    return 1.0 / (1.0 + jnp.exp(-x))


def _pack_bf16_halves(x):
    bits = lax.bitcast_convert_type(x.astype(BF16).astype(F32), jnp.int32)
    n = x.shape[-1] // 2
    return lax.bitwise_or(lax.shift_right_logical(bits[..., :n], 16),
                          lax.bitwise_and(bits[..., n:], jnp.int32(-65536)))


def _ada_kernel(c_ref, w_ref, b_ref, o_ref):
    c = c_ref[...]
    cond = c * _sigmoid(c)
    o_ref[...] = jnp.dot(cond, w_ref[...], preferred_element_type=F32) + b_ref[...]


def _ada(c, w, b):
    bsz, d = c.shape
    n = w.shape[1]
    return pl.pallas_call(
        _ada_kernel,
        grid=(n // d,),
        in_specs=[pl.BlockSpec((bsz, d), lambda j: (0, 0)),
                  pl.BlockSpec((d, d), lambda j: (0, j)),
                  pl.BlockSpec((1, d), lambda j: (0, j))],
        out_specs=pl.BlockSpec((bsz, d), lambda j: (0, j)),
        out_shape=jax.ShapeDtypeStruct((bsz, n), F32),
        name="ada",
    )(c, w, b.reshape(1, n))


def _inproj_kernel(x_ref, mod_ref, g_ref, wa_ref, wvt_ref, wm_ref, wg_ref,
                   oa_ref, ovt_ref, om_ref, og_ref):
    x = x_ref[0]
    mod = mod_ref[0]
    h = _rms_rows(x) * g_ref[...]
    h = h * (1.0 + mod[1:2, :]) + mod[0:1, :]
    hb = h.astype(BF16)
    oa_ref[0] = jnp.dot(hb, wa_ref[...], preferred_element_type=F32).astype(BF16)
    ovt_ref[0] = lax.dot_general(wvt_ref[...], hb, _NT, preferred_element_type=F32).astype(BF16)
    om_ref[0] = jnp.dot(hb, wm_ref[...], preferred_element_type=F32).astype(BF16)
    og_ref[0] = lax.dot_general(wg_ref[...], hb, _NT, preferred_element_type=F32)


def _inproj(x, mod, g, w_qk, w_vt, w_m, w_gt, b0, bsz, s0=0, s=None):
    d = x.shape[-1]
    s = x.shape[1] if s is None else s
    tm = min(ROW_TILE, s)
    i0 = s0 // tm
    na, nv, nm, ng = w_qk.shape[1], w_vt.shape[0], w_m.shape[1], w_gt.shape[0]
    const = lambda shape: pl.BlockSpec(shape, lambda b, i: (0, 0))
    return pl.pallas_call(
        _inproj_kernel,
        grid=(bsz, s // tm),
        in_specs=[pl.BlockSpec((1, tm, d), lambda b, i: (b + b0, i + i0, 0)),
                  pl.BlockSpec((1, 6, d), lambda b, i: (b + b0, 0, 0)),
                  const((1, d)), const((d, na)), const((nv, d)), const((d, nm)), const((ng, d))],
        out_specs=[pl.BlockSpec((1, tm, na), lambda b, i: (b, i, 0)),
                   pl.BlockSpec((1, nv, tm), lambda b, i: (b, 0, i)),
                   pl.BlockSpec((1, tm, nm), lambda b, i: (b, i, 0)),
                   pl.BlockSpec((1, ng, tm), lambda b, i: (b, 0, i))],
        out_shape=[jax.ShapeDtypeStruct((bsz, s, na), BF16),
                   jax.ShapeDtypeStruct((bsz, nv, s), BF16),
                   jax.ShapeDtypeStruct((bsz, s, nm), BF16),
                   jax.ShapeDtypeStruct((bsz, ng, s), F32)],
        compiler_params=pltpu.CompilerParams(vmem_limit_bytes=VMEM_LIMIT),
        name="inproj",
    )(x, mod, g.reshape(1, d), w_qk, w_vt, w_m, w_gt)


def _rel_buckets():
    n = np.arange(2 * ATT_BLOCK)
    max_exact = N_BUCKETS // 2
    nf = np.maximum(n, 1).astype(np.float64)
    large = max_exact + (np.log(nf / max_exact) / math.log(MAX_DIST / max_exact)
                         * (N_BUCKETS - max_exact)).astype(np.int64)
    large = np.minimum(large, N_BUCKETS - 1)
    bucket = np.where(n < max_exact, n, large)
    assert (bucket[ATT_BLOCK + 1:] == N_BUCKETS - 1).all() and 2 * ATT_BLOCK > MAX_DIST
    qk = np.arange(ATT_BLOCK)[None, :] - np.arange(ATT_BLOCK)[:, None]
    tiles = np.stack([bucket[np.maximum(qk, 0)], bucket[ATT_BLOCK + qk]])
    return tiles.astype(np.int32)


def _relbias_kernel(rb_ref, bk_ref, o_ref):
    h = pl.program_id(0)
    bk = bk_ref[...]
    acc = jnp.zeros(bk.shape, F32)
    for b in range(N_BUCKETS):
        acc = jnp.where(bk == b, rb_ref[b, h], acc)
    o_ref[0] = acc


def _relbias(rel_bias):
    tiles = jnp.asarray(_rel_buckets())
    return pl.pallas_call(
        _relbias_kernel,
        grid=(ATT_HEADS,),
        in_specs=[pl.BlockSpec(memory_space=pltpu.SMEM),
                  pl.BlockSpec((2, ATT_BLOCK, ATT_BLOCK), lambda h: (0, 0, 0))],
        out_specs=pl.BlockSpec((1, 2, ATT_BLOCK, ATT_BLOCK), lambda h: (h, 0, 0, 0)),
        out_shape=jax.ShapeDtypeStruct((ATT_HEADS, 2, ATT_BLOCK, ATT_BLOCK), F32),
        name="relbias",
    )(rel_bias, tiles)


def _attn_kernel(q_ref, k_ref, vt_ref, bias_ref, lam_ref, subg_ref, o_ref,
                 qz_ref, m_ref, l_ref, acc_ref, *, q0):
    tq = ATT_BLOCK
    qi = pl.program_id(2) + q0
    scale = ATT_QK_DIM ** -0.5

    q = q_ref[0]
    lane = lax.broadcasted_iota(jnp.int32, q.shape, 1)
    zero = jnp.zeros_like(q)
    qz_ref[0:tq, :] = jnp.where(lane < ATT_QK_DIM, q, zero)
    qz_ref[tq:2 * tq, :] = jnp.where(lane >= ATT_QK_DIM, q, zero)
    m_ref[...] = jnp.full(m_ref.shape, -jnp.inf, F32)
    l_ref[...] = jnp.zeros(l_ref.shape, F32)
    acc_ref[...] = jnp.zeros(acc_ref.shape, F32)

    def step(j, bias, masked):
        start = pl.multiple_of(j * tq, tq)
        k = k_ref[0, pl.ds(start, tq), :]
        vt = vt_ref[0, :, pl.ds(start, tq)]
        s = lax.dot_general(k, qz_ref[...], _NT, preferred_element_type=F32) * scale
        if isinstance(bias, tuple):
            s = s + jnp.concatenate([bias[0], bias[0]], axis=1)
        else:
            s = s + bias
        if masked:
            key = lax.broadcasted_iota(jnp.int32, (tq, tq), 0)
            qry = lax.broadcasted_iota(jnp.int32, (tq, tq), 1)
            keep = jnp.concatenate([key <= qry, key <= qry], axis=1)
            s = jnp.where(keep, s, jnp.finfo(F32).min)
        m_old = m_ref[...]
        m_new = jnp.maximum(m_old, jnp.max(s, axis=0, keepdims=True))
        alpha = jnp.exp(m_old - m_new)
        p = jnp.exp(s - m_new)
        l_ref[...] = alpha * l_ref[...] + jnp.sum(p, axis=0, keepdims=True)
        acc_ref[...] = alpha * acc_ref[...] + jnp.dot(vt, p.astype(BF16), preferred_element_type=F32)
        m_ref[...] = m_new

    far_bias = bias_ref[0, 1, 0:1, tq - 1:tq]

    def far_body(j, carry):
        step(j, far_bias, False)
        return carry

    lax.fori_loop(0, jnp.maximum(qi - 1, 0), far_body, 0)

    @pl.when(qi >= 1)
    def _():
        step(qi - 1, (bias_ref[0, 1],), False)

    step(qi, (bias_ref[0, 0],), True)

    lam = (jnp.exp(jnp.sum(lam_ref[0:1, :] * lam_ref[1:2, :], axis=-1, keepdims=True))
           - jnp.exp(jnp.sum(lam_ref[2:3, :] * lam_ref[3:4, :], axis=-1, keepdims=True))
           + LAMBDA_INIT)
    ot = acc_ref[...] / l_ref[...]
    o = (ot[:, 0:tq] - lam * ot[:, tq:2 * tq]).T
    o = _rms_rows(o) * subg_ref[...] * (1.0 - LAMBDA_INIT)
    o_ref[0] = o.astype(o_ref.dtype)


def _attn(att_qk, att_vt, bias_tiles, lam4, sub_g, q_start=0):
    bsz, s, _ = att_qk.shape
    tq = ATT_BLOCK
    nh = ATT_HEADS
    q0 = q_start // tq
    nq = s // tq - q0
    return pl.pallas_call(
        functools.partial(_attn_kernel, q0=q0),
        grid=(bsz, nh, nq),
        in_specs=[pl.BlockSpec((1, tq, ATT_V_DIM), lambda b, h, i: (b, i + q0, h)),
                  pl.BlockSpec((1, s, ATT_V_DIM), lambda b, h, i: (b, 0, nh + h)),
                  pl.BlockSpec((1, ATT_V_DIM, s), lambda b, h, i: (b, h, 0)),
                  pl.BlockSpec((1, 2, tq, tq), lambda b, h, i: (h, 0, 0, 0)),
                  pl.BlockSpec((4, ATT_QK_DIM), lambda b, h, i: (0, 0)),
                  pl.BlockSpec((1, ATT_V_DIM), lambda b, h, i: (0, 0))],
        out_specs=pl.BlockSpec((1, tq, ATT_V_DIM), lambda b, h, i: (b, i, h)),
        out_shape=jax.ShapeDtypeStruct((bsz, nq * tq, D_ATT), BF16),
        scratch_shapes=[pltpu.VMEM((2 * tq, ATT_V_DIM), BF16),
                        pltpu.VMEM((1, 2 * tq), F32),
                        pltpu.VMEM((1, 2 * tq), F32),
                        pltpu.VMEM((ATT_V_DIM, 2 * tq), F32)],
        compiler_params=pltpu.CompilerParams(vmem_limit_bytes=VMEM_LIMIT),
        name="attn",
    )(att_qk, att_qk, att_vt, bias_tiles, lam4, sub_g.reshape(1, ATT_V_DIM))


def _mlstm_kernel(q_ref, k_ref, v_ref, o_ref, gi_ref, gf_ref, bias_ref, cwq_ref, cwk_ref,
                  cbq_ref, cbk_ref, ng_ref, out_ref, qs_ref, ks_ref, b_ref, ig_ref):
    s = q_ref.shape[1]
    L = M_CHUNK
    nc = s // L
    h = pl.program_id(1)

    row = lax.broadcasted_iota(jnp.int32, (s, M_DIM), 0)

    def conv_silu(x_ref, w_ref, cb_ref):
        x = x_ref[0].astype(F32)
        w = w_ref[...]
        out = None
        for j in range(CONV_W):
            shift = CONV_W - 1 - j
            xs = x if shift == 0 else jnp.where(row >= shift, pltpu.roll(x, shift, 0), 0.0)
            term = xs * w[j:j + 1, :]
            out = term if out is None else out + term
        out = out + cb_ref[...]
        return out * _sigmoid(out)

    qs_ref[...] = conv_silu(q_ref, cwq_ref, cbq_ref).astype(BF16)
    ks_ref[...] = (conv_silu(k_ref, cwk_ref, cbk_ref) * (M_DIM ** -0.5)).astype(BF16)

    ig = gi_ref[0, 0] + bias_ref[h]
    f = gf_ref[0, 0] + bias_ref[M_HEADS + h]
    logf = jnp.minimum(f, 0.0) - jnp.log(1.0 + jnp.exp(-jnp.abs(f)))
    r = lax.broadcasted_iota(jnp.int32, (L, L), 0)
    c = lax.broadcasted_iota(jnp.int32, (L, L), 1)
    tri = (r <= c).astype(F32)
    b_ref[...] = jnp.dot(logf, tri, preferred_element_type=F32,
                         precision=lax.Precision.HIGHEST)
    ig_ref[...] = ig
    eye = r == c
    causal = c <= r

    def to_col(x_row):
        return jnp.sum(jnp.where(eye, x_row, 0.0), axis=1, keepdims=True)

    def chunk(ci, carry):
        C, n, m = carry
        start = pl.multiple_of(ci * L, L)
        qc = qs_ref[pl.ds(start, L), :]
        kc = ks_ref[pl.ds(start, L), :]
        vc = v_ref[0, pl.ds(start, L), :]
        b_r = b_ref[pl.ds(ci, 1), :]
        ig_r = ig_ref[pl.ds(ci, 1), :]
        b_last = b_r[:, L - 1:L]
        a_r = b_last - b_r + ig_r
        b_c = to_col(b_r)
        a_c = to_col(a_r)

        logd = jnp.where(causal, b_c - b_r + ig_r, -jnp.inf)
        m_inter = b_c + m
        m_j = jnp.maximum(jnp.max(logd, axis=1, keepdims=True), m_inter)
        w = jnp.exp(logd - m_j)
        sqk = lax.dot_general(qc, kc, _NT, preferred_element_type=F32) * w
        inter = jnp.exp(m_inter - m_j)
        num = (jnp.dot(sqk.astype(BF16), vc, preferred_element_type=F32)
               + inter * jnp.dot(qc, C.astype(BF16), preferred_element_type=F32))
        den = (jnp.sum(sqk, axis=1, keepdims=True)
               + inter * jnp.sum(qc.astype(F32) * n, axis=1, keepdims=True))
        hc = num / jnp.maximum(jnp.abs(den), jnp.exp(-m_j))

        og = _sigmoid(o_ref[0, pl.ds(start, L), :].astype(F32))
        out_ref[0, pl.ds(start, L), :] = (_rms_rows(og * hc) * ng_ref[...]).astype(out_ref.dtype)

        m_new = jnp.maximum(b_last + m, jnp.max(a_r, axis=1, keepdims=True))
        decay = jnp.exp(b_last + m - m_new)
        kw = kc.astype(F32) * jnp.exp(a_c - m_new)
        C_new = decay * C + jnp.dot(kw.T.astype(BF16), vc, preferred_element_type=F32)
        n_new = decay * n + jnp.sum(kw, axis=0, keepdims=True)
        return C_new, n_new, m_new

    init = (jnp.zeros((M_DIM, M_DIM), F32), jnp.zeros((1, M_DIM), F32), jnp.zeros((1, 1), F32))
    lax.fori_loop(0, nc, chunk, init)


def _mlstm(m_qkvo, gates, gate_bias, conv_w, conv_b, norm_g):
    bsz, s, _ = m_qkvo.shape
    L = M_CHUNK
    nc = s // L
    nh = M_HEADS
    d = M_DIM
    g4 = gates.reshape(bsz, 2 * nh, nc, L)
    seq = lambda off: pl.BlockSpec((1, s, d), lambda b, h: (b, 0, off + h))
    return pl.pallas_call(
        _mlstm_kernel,
        grid=(bsz, nh),
        in_specs=[seq(0), seq(nh), seq(2 * nh), seq(3 * nh),
                  pl.BlockSpec((1, 1, nc, L), lambda b, h: (b, h, 0, 0)),
                  pl.BlockSpec((1, 1, nc, L), lambda b, h: (b, nh + h, 0, 0)),
                  pl.BlockSpec(memory_space=pltpu.SMEM),
                  pl.BlockSpec((CONV_W, d), lambda b, h: (0, h)),
                  pl.BlockSpec((CONV_W, d), lambda b, h: (0, nh + h)),
                  pl.BlockSpec((1, d), lambda b, h: (0, h)),
                  pl.BlockSpec((1, d), lambda b, h: (0, nh + h)),
                  pl.BlockSpec((1, d), lambda b, h: (0, h))],
        out_specs=pl.BlockSpec((1, s, d), lambda b, h: (b, 0, h)),
        out_shape=jax.ShapeDtypeStruct((bsz, s, D_MLSTM), BF16),
        scratch_shapes=[pltpu.VMEM((s, d), BF16), pltpu.VMEM((s, d), BF16),
                        pltpu.VMEM((nc, L), F32), pltpu.VMEM((nc, L), F32)],
        compiler_params=pltpu.CompilerParams(vmem_limit_bytes=VMEM_LIMIT),
        name="mlstm",
    )(m_qkvo, m_qkvo, m_qkvo, m_qkvo, g4, g4, gate_bias, conv_w, conv_w,
      conv_b.reshape(1, -1), conv_b.reshape(1, -1), norm_g.reshape(1, -1))


def _outproj_kernel(att_ref, hm_ref, x_ref, mod_ref, g2_ref, wo_ref, wq_ref,
                    x1_ref, h2_ref, qp_ref):
    mod = mod_ref[0]
    y = (jnp.dot(att_ref[0], wo_ref[0:D_ATT, :], preferred_element_type=F32)
         + jnp.dot(hm_ref[0], wo_ref[D_ATT:, :], preferred_element_type=F32))
    x1 = x_ref[0] + mod[2:3, :] * y
    x1_ref[0] = x1
    h2 = _rms_rows(x1) * g2_ref[...]
    h2 = h2 * (1.0 + mod[4:5, :]) + mod[3:4, :]
    hb = h2.astype(BF16)
    h2_ref[0] = _pack_bf16_halves(hb)
    qp_ref[0] = jnp.dot(hb, wq_ref[...], preferred_element_type=F32).astype(BF16)


def _outproj(att, hm, x, mod, g2, w_out, w_q, b0, s0=0):
    bsz, s, _ = att.shape
    d = x.shape[-1]
    tm = min(ROW_TILE, s)
    i0 = s0 // tm
    nq = w_q.shape[1]
    tile = lambda n: pl.BlockSpec((1, tm, n), lambda b, i: (b, i, 0))
    return pl.pallas_call(
        _outproj_kernel,
        grid=(bsz, s // tm),
        in_specs=[tile(D_ATT), tile(D_MLSTM),
                  pl.BlockSpec((1, tm, d), lambda b, i: (b + b0, i + i0, 0)),
                  pl.BlockSpec((1, 6, d), lambda b, i: (b + b0, 0, 0)),
                  pl.BlockSpec((1, d), lambda b, i: (0, 0)),
                  pl.BlockSpec((d, d), lambda b, i: (0, 0)),
                  pl.BlockSpec((d, nq), lambda b, i: (0, 0))],
        out_specs=[tile(d), tile(d // 2), tile(nq)],
        out_shape=[jax.ShapeDtypeStruct((bsz, s, d), F32),
                   jax.ShapeDtypeStruct((bsz, s, d // 2), jnp.int32),
                   jax.ShapeDtypeStruct((bsz, s, nq), BF16)],
        compiler_params=pltpu.CompilerParams(vmem_limit_bytes=VMEM_LIMIT),
        name="outproj",
    )(att, hm, x, mod, g2.reshape(1, d), w_out, w_q)


def _top16_rows(blocks, ids):
    big = jnp.float32(1 << 20)
    vals, pos = [], []
    for _ in range(PEER_TOPK):
        m = functools.reduce(jnp.maximum, blocks)
        m = jnp.max(m, axis=0, keepdims=True)
        cand = functools.reduce(jnp.minimum,
                                [jnp.where(b == m, i, big) for b, i in zip(blocks, ids)])
        p = jnp.min(cand, axis=0, keepdims=True)
        blocks = [jnp.where(i == p, -jnp.inf, b) for b, i in zip(blocks, ids)]
        vals.append(m)
        pos.append(p)
    return jnp.concatenate(vals, axis=0), jnp.concatenate(pos, axis=0)


def _pick_rows(table, sel):
    out = jnp.zeros(sel.shape, table.dtype)
    for r in range(PEER_TOPK):
        out = jnp.where(sel == r, table[r:r + 1, :], out)
    return out


def _pair_candidates(a, b):
    k, sub = PEER_TOPK, 8
    j_id = lax.broadcasted_iota(jnp.int32, (sub, LANES), 0).astype(F32)
    blocks = [a[0:1, :] + b[0:sub, :], a[0:1, :] + b[sub:k, :], a[1:2, :] + b[0:sub, :]]
    ids = [j_id, j_id + sub, j_id + k]
    for i in range(2, sub):
        blocks.append(jnp.where(j_id < k // (i + 1), a[i:i + 1, :] + b[0:sub, :], -jnp.inf))
        ids.append(j_id + i * k)
    blocks.append(a[sub:k, :] + b[0:1, :])
    ids.append((j_id + sub) * k)
    return blocks, ids


def _route_kernel(q_ref, keys_ref, idx_ref, g_ref):
    k = PEER_TOPK
    half = PEER_QDIM // 2
    key_id = lax.broadcasted_iota(jnp.int32, (N_KEYS, LANES), 0).astype(F32)
    for t in range(q_ref.shape[0] // LANES):
        rows = pl.ds(t * LANES, LANES)
        sv, si = [], []
        for p in range(2):
            qh = q_ref[rows, p * half:(p + 1) * half]
            s = lax.dot_general(keys_ref[0, p], qh, _NT, preferred_element_type=F32)
            v, i = _top16_rows([s], [key_id])
            sv.append(v)
            si.append(i.astype(jnp.int32))
        top_s, pos = _top16_rows(*_pair_candidates(sv[0], sv[1]))
        pos = pos.astype(jnp.int32)
        idx = (_pick_rows(si[0], lax.shift_right_logical(pos, k.bit_length() - 1)) * N_KEYS
               + _pick_rows(si[1], lax.bitwise_and(pos, k - 1)))
        e = jnp.exp(top_s - top_s[0:1, :])
        idx_ref[:, t * LANES:(t + 1) * LANES] = idx
        g_ref[:, t * LANES:(t + 1) * LANES] = e / jnp.sum(e, axis=0, keepdims=True)


def _route(qp, sub_keys):
    t, _ = qp.shape
    tt = min(ROUTE_TILE, t)
    k = PEER_TOPK
    return pl.pallas_call(
        _route_kernel,
        grid=(t // tt, PEER_HEADS),
        in_specs=[pl.BlockSpec((tt, PEER_QDIM), lambda i, h: (i, h)),
                  pl.BlockSpec((1, 2, N_KEYS, PEER_QDIM // 2), lambda i, h: (h, 0, 0, 0))],
        out_specs=[pl.BlockSpec((k, tt), lambda i, h: (h, i)),
                   pl.BlockSpec((k, tt), lambda i, h: (h, i))],
        out_shape=[jax.ShapeDtypeStruct((PEER_HEADS * k, t), jnp.int32),
                   jax.ShapeDtypeStruct((PEER_HEADS * k, t), F32)],
        name="route",
    )(qp, sub_keys)


def _final_kernel(x1_ref, po_ref, mod_ref, g_ref, o_ref):
    x2 = x1_ref[0] + mod_ref[0][5:6, :] * po_ref[0]
    o_ref[0] = _rms_rows(x2) * g_ref[...]


def _final_kernel_into(x1_ref, po_ref, mod_ref, g_ref, prev_ref, o_ref):
    del prev_ref
    _final_kernel(x1_ref, po_ref, mod_ref, g_ref, o_ref)


def _final(x1, peer_out, mod, final_g, b0, out, s_full, s0=0):
    bsz, s, d = x1.shape
    tm = min(ROW_TILE, s)
    i0 = s0 // tm
    tile = pl.BlockSpec((1, tm, d), lambda b, i: (b, i, 0))
    in_specs = [tile, tile, pl.BlockSpec((1, 6, d), lambda b, i: (b + b0, 0, 0)),
                pl.BlockSpec((1, d), lambda b, i: (0, 0))]
    args = (x1, peer_out, mod, final_g.reshape(1, d))
    if out is not None:
        in_specs.append(pl.BlockSpec(memory_space=pl.ANY))
        args += (out,)
    return pl.pallas_call(
        _final_kernel if out is None else _final_kernel_into,
        grid=(bsz, s // tm),
        in_specs=in_specs,
        out_specs=pl.BlockSpec((1, tm, d), lambda b, i: (b + b0, i + i0, 0)),
        out_shape=jax.ShapeDtypeStruct((mod.shape[0], s_full, d), F32),
        input_output_aliases={} if out is None else {4: 0},
        name="final",
    )(*args)


GELU_C = math.sqrt(2.0 / math.pi)
SC_LANES = 16
SC_WORKERS = 32
SC_CORES = 2
SC_TOKEN_BLOCK = 16
SC_ITEM_HEADS = 2
SC_GATHER_DEPTH = 3
SC_PACK_ROWS = 16
SC_PACK_DEPTH = 4
SC_ILV = plsc.PackFormat.INTERLEAVED
N_PAIRS = PEER_HEADS * PEER_TOPK


def _gelu_tanh(x):
    z = GELU_C * (x + 0.044715 * (x * x * x))
    t = 1.0 - 2.0 / (jnp.exp(2.0 * z) + 1.0)
    return x * (0.5 * (1.0 + t))


def _pack_tables(u, v):
    n_rows, d = u.shape
    dw = d // 2
    rb = SC_PACK_ROWS
    nb = SC_PACK_DEPTH
    rpw = n_rows // SC_WORKERS
    nblk = rpw // rb
    assert rpw % rb == 0 and nblk % nb == 0 and nb % 2 == 0
    mesh = plsc.VectorSubcoreMesh(core_axis_name="c", subcore_axis_name="s")
    out_t = jax.ShapeDtypeStruct((n_rows, dw), jnp.int32)

    @functools.partial(
        pl.kernel, mesh=mesh, out_type=(out_t, out_t),
        scratch_types=[pltpu.VMEM((nb, rb, d), F32), pltpu.VMEM((2, rb, dw), jnp.int32),
                       pltpu.SemaphoreType.DMA((nb,)), pltpu.SemaphoreType.DMA((2,))],
        compiler_params=pltpu.CompilerParams(needs_layout_passes=False),
        name="pack_tables",
    )
    def pack(u_hbm, v_hbm, uo_hbm, vo_hbm, src, dst, isem, osem):
        wid = lax.axis_index("s") * SC_CORES + lax.axis_index("c")
        base = wid * rpw
        for tab, out in ((u_hbm, uo_hbm), (v_hbm, vo_hbm)):
            def in_copy(blk, b):
                return pltpu.make_async_copy(tab.at[pl.ds(base + blk * rb, rb)], src.at[b], isem.at[b])

            def out_copy(blk, s):
                return pltpu.make_async_copy(dst.at[s], out.at[pl.ds(base + blk * rb, rb)], osem.at[s])

            for b in range(nb - 1):
                in_copy(b, b).start()

            @pl.loop(0, nblk, step=nb)
            def _(blk):
                for b in range(nb):
                    ahead = blk + b + nb - 1

                    @pl.when(ahead < nblk)
                    def _():
                        in_copy(ahead, (b + nb - 1) % nb).start()

                    in_copy(blk + b, b).wait()
                    s = b % 2

                    @pl.when(blk + b >= 2)
                    def _():
                        out_copy(blk + b - 2, s).wait()

                    @pl.loop(0, rb)
                    def _(r):
                        @plsc.parallel_loop(0, dw // SC_LANES, 1, unroll=4)
                        def _(c):
                            lo = src[b, r, pl.ds(c * SC_LANES, SC_LANES)]
                            hi = src[b, r, pl.ds(dw + c * SC_LANES, SC_LANES)]
                            dst[s, r, pl.ds(c * SC_LANES, SC_LANES)] = plsc.bitcast(
                                plsc.pack(lo, hi, format=SC_ILV), jnp.int32)

                    out_copy(blk + b, s).start()

            out_copy(nblk - 2, 0).wait()
            out_copy(nblk - 1, 1).wait()

    return pack(u, v)


def _experts(h2, idx, g, u, v):
    t_total, dw = h2.shape
    d = 2 * dw
    tpw = t_total // SC_WORKERS
    tb = min(tpw, SC_TOKEN_BLOCK)
    ng = SC_ITEM_HEADS
    k = PEER_TOPK * ng
    nh = PEER_HEADS // ng
    n_items = tb * nh
    nbuf = SC_GATHER_DEPTH
    head_bits = nh.bit_length() - 1
    assert tpw % tb == 0 and n_items >= nbuf and nh == 1 << head_bits and PEER_TOPK == SC_LANES
    mesh = plsc.VectorSubcoreMesh(core_axis_name="c", subcore_axis_name="s")

    @functools.partial(
        pl.kernel, mesh=mesh,
        out_type=jax.ShapeDtypeStruct((t_total, d), F32),
        scratch_types=[
            pltpu.VMEM((tb, N_PAIRS), jnp.int32),
            pltpu.VMEM((tb, N_PAIRS), F32),
            pltpu.VMEM((tb, dw), jnp.int32),
            pltpu.VMEM((tb, d), F32),
            pltpu.VMEM((nbuf, k, dw), jnp.int32),
            pltpu.VMEM((nbuf, k, dw), jnp.int32),
            pltpu.SemaphoreType.DMA((nbuf,)),
            pltpu.SemaphoreType.DMA((nbuf,)),
        ],
        compiler_params=pltpu.CompilerParams(needs_layout_passes=False),
        name="experts",
    )
    def experts(h_hbm, idx_hbm, g_hbm, u_hbm, v_hbm, out_hbm,
                idx_b, g_b, h_b, out_b, urows, vrows, usem, vsem):
        wid = lax.axis_index("s") * SC_CORES + lax.axis_index("c")
        base = wid * tpw
        lane = lax.iota(jnp.int32, SC_LANES)

        def split(item):
            return lax.shift_right_logical(item, head_bits), lax.bitwise_and(item, nh - 1)

        def copies(item, b):
            tt, hd = split(item)
            ids = idx_b.at[tt, pl.ds(hd * k, k)]
            return (pltpu.make_async_copy(u_hbm.at[ids], urows.at[b], usem.at[b]),
                    pltpu.make_async_copy(v_hbm.at[ids], vrows.at[b], vsem.at[b]))

        def fetch(item, b):
            cu, cv = copies(item, b)
            cu.start()
            cv.start()

        def words(ref, *lead, off):
            return plsc.bitcast(ref[(*lead, pl.ds(off, SC_LANES))], BF16)

        def sum4_unpack(pr):
            return plsc.unpack((pr[0] + pr[1]) + (pr[2] + pr[3]), format=SC_ILV)

        def compute(item, b):
            tt, hd = split(item)
            cu, cv = copies(item, b)
            cu.wait()

            def ubody(j, accs):
                hs = [words(h_b, tt, off=(j * 4 + q) * SC_LANES) for q in range(4)]
                new = []
                for p, a in enumerate(accs):
                    lo, hi = sum4_unpack([words(urows, b, p, off=(j * 4 + q) * SC_LANES) * hs[q]
                                          for q in range(4)])
                    new.append(a + (lo + hi))
                return tuple(new)

            accs = plsc.parallel_loop(
                0, dw // (4 * SC_LANES), 1,
                carry=tuple(jnp.zeros((SC_LANES,), F32) for _ in range(k)))(ubody)
            cbb = []
            for hh in range(ng):
                s = jnp.zeros((SC_LANES,), F32)
                for p in range(SC_LANES):
                    s = jnp.where(lane == p, jnp.sum(accs[hh * SC_LANES + p]), s)
                c = g_b[tt, pl.ds(hd * k + hh * SC_LANES, SC_LANES)] * _gelu_tanh(s)
                for p in range(SC_LANES):
                    cp = jnp.full((SC_LANES,), c[p])
                    cbb.append(plsc.pack(cp, cp, format=SC_ILV))
            cv.wait()

            def tree(xs):
                while len(xs) > 1:
                    xs = [xs[i] + xs[i + 1] for i in range(0, len(xs), 2)]
                return xs[0]

            @plsc.parallel_loop(0, dw // SC_LANES, 1, unroll=2 // ng)
            def _(ch):
                sa = pl.ds(ch * SC_LANES, SC_LANES)
                sb = pl.ds(dw + ch * SC_LANES, SC_LANES)
                los, his = [], []
                for grp in range(k // 4):
                    lo, hi = sum4_unpack([cbb[grp * 4 + q]
                                          * words(vrows, b, grp * 4 + q, off=ch * SC_LANES)
                                          for q in range(4)])
                    los.append(lo)
                    his.append(hi)
                out_b[tt, sa] = out_b[tt, sa] + tree(los)
                out_b[tt, sb] = out_b[tt, sb] + tree(his)

        @pl.loop(0, tpw // tb)
        def _(blk):
            t0 = base + blk * tb
            pltpu.sync_copy(idx_hbm.at[pl.ds(t0, tb)], idx_b)
            pltpu.sync_copy(g_hbm.at[pl.ds(t0, tb)], g_b)
            pltpu.sync_copy(h_hbm.at[pl.ds(t0, tb)], h_b)
            for b in range(nbuf - 1):
                fetch(b, b)

            @pl.loop(0, tb)
            def _(tt):
                @pl.loop(0, d // SC_LANES)
                def _(j):
                    out_b[tt, pl.ds(j * SC_LANES, SC_LANES)] = jnp.zeros((SC_LANES,), F32)

            @pl.loop(0, n_items)
            def _(it):
                ahead = it + nbuf - 1

                @pl.when(ahead < n_items)
                def _():
                    fetch(ahead, lax.rem(ahead, nbuf))

                compute(it, lax.rem(it, nbuf))

            pltpu.sync_copy(out_b, out_hbm.at[pl.ds(t0, tb)])

    return experts(h2, idx, g, u, v)


SPLIT_ROWS = 3
ROW_PIECES = 2
BATCH_CHUNKS = (1,) * 13


def kernel(x, c, w_ada, b_ada, norm1_g, norm2_g, w_in, conv_w, conv_b, b_igate, b_fgate, lam_q1, lam_k1, lam_q2, lam_k2, diff_sub_g, mlstm_norm_g, w_out, peer_w_q, peer_sub_keys, peer_u, peer_v, rel_bias, final_g):
    bsz, s, d = x.shape
    mod = _ada(c, w_ada[0], b_ada[0]).reshape(bsz, 6, d)

    w = w_in[0]
    w_qk = w[:, :2 * D_ATT].astype(BF16)
    w_vt = w[:, 2 * D_ATT:3 * D_ATT].T.astype(BF16)
    w_m = w[:, 3 * D_ATT:3 * D_ATT + 4 * D_MLSTM].astype(BF16)
    w_gt = w[:, 3 * D_ATT + 4 * D_MLSTM:].T.astype(BF16)
    w_o = w_out[0].astype(BF16)
    w_q = peer_w_q[0].astype(BF16)
    sub_keys = peer_sub_keys[0].astype(BF16)
    bias_tiles = _relbias(rel_bias)
    lam4 = jnp.stack([lam_q1[0], lam_k1[0], lam_q2[0], lam_k2[0]])
    gate_bias = jnp.concatenate([b_igate[0], b_fgate[0]])

    assert sum(BATCH_CHUNKS) == bsz - SPLIT_ROWS
    tables = None
    out = None

    def mix_and_experts(att, hm, b0, out, s0=0):
        nonlocal tables
        nb, sp, _ = att.shape
        x1, h2, qp = _outproj(att, hm, x, mod, norm2_g[0], w_o, w_q, b0, s0)
        idx_t, g_t = _route(qp.reshape(nb * sp, -1), sub_keys)
        if tables is None:
            tables = _pack_tables(peer_u[0], peer_v[0])
        peer_out = _experts(h2.reshape(nb * sp, d // 2), idx_t.T, g_t.T, *tables)
        return _final(x1, peer_out.reshape(nb, sp, d), mod, final_g, b0, out, s, s0)

    sp = s // ROW_PIECES
    for b0 in range(SPLIT_ROWS):
        parts = []
        for p in range(ROW_PIECES):
            parts.append(_inproj(x, mod, norm1_g[0], w_qk, w_vt, w_m, w_gt, b0, 1, p * sp, sp))
            att_qk, att_vt, m_qkvo, gates = (jnp.concatenate(t, axis=a)
                                             for t, a in zip(zip(*parts), (1, 2, 1, 2)))
            att = _attn(att_qk, att_vt, bias_tiles, lam4, diff_sub_g[0], q_start=p * sp)
            hm = _mlstm(m_qkvo, gates, gate_bias, conv_w[0], conv_b[0], mlstm_norm_g[0])[:, p * sp:]
            out = mix_and_experts(att, hm, b0, out, p * sp)

    b0 = SPLIT_ROWS
    for nb in BATCH_CHUNKS:
        att_qk, att_vt, m_qkvo, gates = _inproj(x, mod, norm1_g[0], w_qk, w_vt, w_m, w_gt, b0, nb)
        att = _attn(att_qk, att_vt, bias_tiles, lam4, diff_sub_g[0])
        hm = _mlstm(m_qkvo, gates, gate_bias, conv_w[0], conv_b[0], mlstm_norm_g[0])
        out = mix_and_experts(att, hm, b0, out)
        b0 += nb
    return out
```

```python
import functools
import math

import numpy as np
import jax
import jax.numpy as jnp
from jax import lax
from jax.experimental import pallas as pl
from jax.experimental.pallas import tpu as pltpu
from jax.experimental.pallas import tpu_sc as plsc

F32 = jnp.float32
BF16 = jnp.bfloat16

ATT_HEADS = 4
ATT_QK_DIM = 64
ATT_V_DIM = 128
D_ATT = ATT_HEADS * ATT_V_DIM
M_HEADS = 4
M_DIM = 128
D_MLSTM = M_HEADS * M_DIM
CONV_W = 4
N_BUCKETS = 32
MAX_DIST = 128
N_KEYS = 128
PEER_HEADS = 8
PEER_TOPK = 16
PEER_QDIM = 256
EPS = 1e-6
LAMBDA_INIT = 0.8 - 0.6 * math.exp(-0.3 * 0)

ATT_BLOCK = 256
M_CHUNK = 128
ROW_TILE = 512
ROUTE_TILE = 512
LANES = 128
V7X_VMEM_BYTES = 64 * 1024 * 1024
VMEM_LIMIT = V7X_VMEM_BYTES * 3 // 4

_NT = (((1,), (1,)), ((), ()))


def _rms_rows(x):
    return x * lax.rsqrt(jnp.mean(x * x, axis=-1, keepdims=True) + EPS)


def _sigmoid(x):
    return 1.0 / (1.0 + jnp.exp(-x))


def _pack_bf16_halves(x):
    bits = lax.bitcast_convert_type(x.astype(BF16).astype(F32), jnp.int32)
    n = x.shape[-1] // 2
    return lax.bitwise_or(lax.shift_right_logical(bits[..., :n], 16),
                          lax.bitwise_and(bits[..., n:], jnp.int32(-65536)))


def _ada_kernel(c_ref, w_ref, b_ref, o_ref):
    c = c_ref[...]
    cond = c * _sigmoid(c)
    o_ref[...] = jnp.dot(cond, w_ref[...], preferred_element_type=F32) + b_ref[...]


def _ada(c, w, b):
    bsz, d = c.shape
    n = w.shape[1]
    return pl.pallas_call(
        _ada_kernel,
        grid=(n // d,),
        in_specs=[pl.BlockSpec((bsz, d), lambda j: (0, 0)),
                  pl.BlockSpec((d, d), lambda j: (0, j)),
                  pl.BlockSpec((1, d), lambda j: (0, j))],
        out_specs=pl.BlockSpec((bsz, d), lambda j: (0, j)),
        out_shape=jax.ShapeDtypeStruct((bsz, n), F32),
        name="ada",
    )(c, w, b.reshape(1, n))


def _inproj_kernel(x_ref, mod_ref, g_ref, wa_ref, wvt_ref, wm_ref, wg_ref,
                   oa_ref, ovt_ref, om_ref, og_ref):
    x = x_ref[0]
    mod = mod_ref[0]
    h = _rms_rows(x) * g_ref[...]
    h = h * (1.0 + mod[1:2, :]) + mod[0:1, :]
    hb = h.astype(BF16)
    oa_ref[0] = jnp.dot(hb, wa_ref[...], preferred_element_type=F32).astype(BF16)
    ovt_ref[0] = lax.dot_general(wvt_ref[...], hb, _NT, preferred_element_type=F32).astype(BF16)
    om_ref[0] = jnp.dot(hb, wm_ref[...], preferred_element_type=F32).astype(BF16)
    og_ref[0] = lax.dot_general(wg_ref[...], hb, _NT, preferred_element_type=F32)


def _inproj(x, mod, g, w_qk, w_vt, w_m, w_gt, b0, bsz, s0=0, s=None):
    d = x.shape[-1]
    s = x.shape[1] if s is None else s
    tm = min(ROW_TILE, s)
    i0 = s0 // tm
    na, nv, nm, ng = w_qk.shape[1], w_vt.shape[0], w_m.shape[1], w_gt.shape[0]
    const = lambda shape: pl.BlockSpec(shape, lambda b, i: (0, 0))
    return pl.pallas_call(
        _inproj_kernel,
        grid=(bsz, s // tm),
        in_specs=[pl.BlockSpec((1, tm, d), lambda b, i: (b + b0, i + i0, 0)),
                  pl.BlockSpec((1, 6, d), lambda b, i: (b + b0, 0, 0)),
                  const((1, d)), const((d, na)), const((nv, d)), const((d, nm)), const((ng, d))],
        out_specs=[pl.BlockSpec((1, tm, na), lambda b, i: (b, i, 0)),
                   pl.BlockSpec((1, nv, tm), lambda b, i: (b, 0, i)),
                   pl.BlockSpec((1, tm, nm), lambda b, i: (b, i, 0)),
                   pl.BlockSpec((1, ng, tm), lambda b, i: (b, 0, i))],
        out_shape=[jax.ShapeDtypeStruct((bsz, s, na), BF16),
                   jax.ShapeDtypeStruct((bsz, nv, s), BF16),
                   jax.ShapeDtypeStruct((bsz, s, nm), BF16),
                   jax.ShapeDtypeStruct((bsz, ng, s), F32)],
        compiler_params=pltpu.CompilerParams(vmem_limit_bytes=VMEM_LIMIT),
        name="inproj",
    )(x, mod, g.reshape(1, d), w_qk, w_vt, w_m, w_gt)


def _rel_buckets():
    n = np.arange(2 * ATT_BLOCK)
    max_exact = N_BUCKETS // 2
    nf = np.maximum(n, 1).astype(np.float64)
    large = max_exact + (np.log(nf / max_exact) / math.log(MAX_DIST / max_exact)
                         * (N_BUCKETS - max_exact)).astype(np.int64)
    large = np.minimum(large, N_BUCKETS - 1)
    bucket = np.where(n < max_exact, n, large)
    assert (bucket[ATT_BLOCK + 1:] == N_BUCKETS - 1).all() and 2 * ATT_BLOCK > MAX_DIST
    qk = np.arange(ATT_BLOCK)[None, :] - np.arange(ATT_BLOCK)[:, None]
    tiles = np.stack([bucket[np.maximum(qk, 0)], bucket[ATT_BLOCK + qk]])
    return tiles.astype(np.int32)


def _relbias_kernel(rb_ref, bk_ref, o_ref):
    h = pl.program_id(0)
    bk = bk_ref[...]
    acc = jnp.zeros(bk.shape, F32)
    for b in range(N_BUCKETS):
        acc = jnp.where(bk == b, rb_ref[b, h], acc)
    o_ref[0] = acc


def _relbias(rel_bias):
    tiles = jnp.asarray(_rel_buckets())
    return pl.pallas_call(
        _relbias_kernel,
        grid=(ATT_HEADS,),
        in_specs=[pl.BlockSpec(memory_space=pltpu.SMEM),
                  pl.BlockSpec((2, ATT_BLOCK, ATT_BLOCK), lambda h: (0, 0, 0))],
        out_specs=pl.BlockSpec((1, 2, ATT_BLOCK, ATT_BLOCK), lambda h: (h, 0, 0, 0)),
        out_shape=jax.ShapeDtypeStruct((ATT_HEADS, 2, ATT_BLOCK, ATT_BLOCK), F32),
        name="relbias",
    )(rel_bias, tiles)


def _attn_kernel(q_ref, k_ref, vt_ref, bias_ref, lam_ref, subg_ref, o_ref,
                 qz_ref, m_ref, l_ref, acc_ref, *, q0):
    tq = ATT_BLOCK
    qi = pl.program_id(2) + q0
    scale = ATT_QK_DIM ** -0.5

    q = q_ref[0]
    lane = lax.broadcasted_iota(jnp.int32, q.shape, 1)
    zero = jnp.zeros_like(q)
    qz_ref[0:tq, :] = jnp.where(lane < ATT_QK_DIM, q, zero)
    qz_ref[tq:2 * tq, :] = jnp.where(lane >= ATT_QK_DIM, q, zero)
    m_ref[...] = jnp.full(m_ref.shape, -jnp.inf, F32)
    l_ref[...] = jnp.zeros(l_ref.shape, F32)
    acc_ref[...] = jnp.zeros(acc_ref.shape, F32)

    def step(j, bias, masked):
        start = pl.multiple_of(j * tq, tq)
        k = k_ref[0, pl.ds(start, tq), :]
        vt = vt_ref[0, :, pl.ds(start, tq)]
        s = lax.dot_general(k, qz_ref[...], _NT, preferred_element_type=F32) * scale
        if isinstance(bias, tuple):
            s = s + jnp.concatenate([bias[0], bias[0]], axis=1)
        else:
            s = s + bias
        if masked:
            key = lax.broadcasted_iota(jnp.int32, (tq, tq), 0)
            qry = lax.broadcasted_iota(jnp.int32, (tq, tq), 1)
            keep = jnp.concatenate([key <= qry, key <= qry], axis=1)
            s = jnp.where(keep, s, jnp.finfo(F32).min)
        m_old = m_ref[...]
        m_new = jnp.maximum(m_old, jnp.max(s, axis=0, keepdims=True))
        alpha = jnp.exp(m_old - m_new)
        p = jnp.exp(s - m_new)
        l_ref[...] = alpha * l_ref[...] + jnp.sum(p, axis=0, keepdims=True)
        acc_ref[...] = alpha * acc_ref[...] + jnp.dot(vt, p.astype(BF16), preferred_element_type=F32)
        m_ref[...] = m_new

    far_bias = bias_ref[0, 1, 0:1, tq - 1:tq]

    def far_body(j, carry):
        step(j, far_bias, False)
        return carry

    lax.fori_loop(0, jnp.maximum(qi - 1, 0), far_body, 0)

    @pl.when(qi >= 1)
    def _():
        step(qi - 1, (bias_ref[0, 1],), False)

    step(qi, (bias_ref[0, 0],), True)

    lam = (jnp.exp(jnp.sum(lam_ref[0:1, :] * lam_ref[1:2, :], axis=-1, keepdims=True))
           - jnp.exp(jnp.sum(lam_ref[2:3, :] * lam_ref[3:4, :], axis=-1, keepdims=True))
           + LAMBDA_INIT)
    ot = acc_ref[...] / l_ref[...]
    o = (ot[:, 0:tq] - lam * ot[:, tq:2 * tq]).T
    o = _rms_rows(o) * subg_ref[...] * (1.0 - LAMBDA_INIT)
    o_ref[0] = o.astype(o_ref.dtype)


def _attn(att_qk, att_vt, bias_tiles, lam4, sub_g, q_start=0):
    bsz, s, _ = att_qk.shape
    tq = ATT_BLOCK
    nh = ATT_HEADS
    q0 = q_start // tq
    nq = s // tq - q0
    return pl.pallas_call(
        functools.partial(_attn_kernel, q0=q0),
        grid=(bsz, nh, nq),
        in_specs=[pl.BlockSpec((1, tq, ATT_V_DIM), lambda b, h, i: (b, i + q0, h)),
                  pl.BlockSpec((1, s, ATT_V_DIM), lambda b, h, i: (b, 0, nh + h)),
                  pl.BlockSpec((1, ATT_V_DIM, s), lambda b, h, i: (b, h, 0)),
                  pl.BlockSpec((1, 2, tq, tq), lambda b, h, i: (h, 0, 0, 0)),
                  pl.BlockSpec((4, ATT_QK_DIM), lambda b, h, i: (0, 0)),
                  pl.BlockSpec((1, ATT_V_DIM), lambda b, h, i: (0, 0))],
        out_specs=pl.BlockSpec((1, tq, ATT_V_DIM), lambda b, h, i: (b, i, h)),
        out_shape=jax.ShapeDtypeStruct((bsz, nq * tq, D_ATT), BF16),
        scratch_shapes=[pltpu.VMEM((2 * tq, ATT_V_DIM), BF16),
                        pltpu.VMEM((1, 2 * tq), F32),
                        pltpu.VMEM((1, 2 * tq), F32),
                        pltpu.VMEM((ATT_V_DIM, 2 * tq), F32)],
        compiler_params=pltpu.CompilerParams(vmem_limit_bytes=VMEM_LIMIT),
        name="attn",
    )(att_qk, att_qk, att_vt, bias_tiles, lam4, sub_g.reshape(1, ATT_V_DIM))


def _mlstm_kernel(q_ref, k_ref, v_ref, o_ref, gi_ref, gf_ref, bias_ref, cwq_ref, cwk_ref,
                  cbq_ref, cbk_ref, ng_ref, out_ref, qs_ref, ks_ref, b_ref, ig_ref):
    s = q_ref.shape[1]
    L = M_CHUNK
    nc = s // L
    h = pl.program_id(1)

    row = lax.broadcasted_iota(jnp.int32, (s, M_DIM), 0)

    def conv_silu(x_ref, w_ref, cb_ref):
        x = x_ref[0].astype(F32)
        w = w_ref[...]
        out = None
        for j in range(CONV_W):
            shift = CONV_W - 1 - j
            xs = x if shift == 0 else jnp.where(row >= shift, pltpu.roll(x, shift, 0), 0.0)
            term = xs * w[j:j + 1, :]
            out = term if out is None else out + term
        out = out + cb_ref[...]
        return out * _sigmoid(out)

    qs_ref[...] = conv_silu(q_ref, cwq_ref, cbq_ref).astype(BF16)
    ks_ref[...] = (conv_silu(k_ref, cwk_ref, cbk_ref) * (M_DIM ** -0.5)).astype(BF16)

    ig = gi_ref[0, 0] + bias_ref[h]
    f = gf_ref[0, 0] + bias_ref[M_HEADS + h]
    logf = jnp.minimum(f, 0.0) - jnp.log(1.0 + jnp.exp(-jnp.abs(f)))
    r = lax.broadcasted_iota(jnp.int32, (L, L), 0)
    c = lax.broadcasted_iota(jnp.int32, (L, L), 1)
    tri = (r <= c).astype(F32)
    b_ref[...] = jnp.dot(logf, tri, preferred_element_type=F32,
                         precision=lax.Precision.HIGHEST)
    ig_ref[...] = ig
    eye = r == c
    causal = c <= r

    def to_col(x_row):
        return jnp.sum(jnp.where(eye, x_row, 0.0), axis=1, keepdims=True)

    def chunk(ci, carry):
        C, n, m = carry
        start = pl.multiple_of(ci * L, L)
        qc = qs_ref[pl.ds(start, L), :]
        kc = ks_ref[pl.ds(start, L), :]
        vc = v_ref[0, pl.ds(start, L), :]
        b_r = b_ref[pl.ds(ci, 1), :]
        ig_r = ig_ref[pl.ds(ci, 1), :]
        b_last = b_r[:, L - 1:L]
        a_r = b_last - b_r + ig_r
        b_c = to_col(b_r)
        a_c = to_col(a_r)

        logd = jnp.where(causal, b_c - b_r + ig_r, -jnp.inf)
        m_inter = b_c + m
        m_j = jnp.maximum(jnp.max(logd, axis=1, keepdims=True), m_inter)
        w = jnp.exp(logd - m_j)
        sqk = lax.dot_general(qc, kc, _NT, preferred_element_type=F32) * w
        inter = jnp.exp(m_inter - m_j)
        num = (jnp.dot(sqk.astype(BF16), vc, preferred_element_type=F32)
               + inter * jnp.dot(qc, C.astype(BF16), preferred_element_type=F32))
        den = (jnp.sum(sqk, axis=1, keepdims=True)
               + inter * jnp.sum(qc.astype(F32) * n, axis=1, keepdims=True))
        hc = num / jnp.maximum(jnp.abs(den), jnp.exp(-m_j))

        og = _sigmoid(o_ref[0, pl.ds(start, L), :].astype(F32))
        out_ref[0, pl.ds(start, L), :] = (_rms_rows(og * hc) * ng_ref[...]).astype(out_ref.dtype)

        m_new = jnp.maximum(b_last + m, jnp.max(a_r, axis=1, keepdims=True))
        decay = jnp.exp(b_last + m - m_new)
        kw = kc.astype(F32) * jnp.exp(a_c - m_new)
        C_new = decay * C + jnp.dot(kw.T.astype(BF16), vc, preferred_element_type=F32)
        n_new = decay * n + jnp.sum(kw, axis=0, keepdims=True)
        return C_new, n_new, m_new

    init = (jnp.zeros((M_DIM, M_DIM), F32), jnp.zeros((1, M_DIM), F32), jnp.zeros((1, 1), F32))
    lax.fori_loop(0, nc, chunk, init)


def _mlstm(m_qkvo, gates, gate_bias, conv_w, conv_b, norm_g):
    bsz, s, _ = m_qkvo.shape
    L = M_CHUNK
    nc = s // L
    nh = M_HEADS
    d = M_DIM
    g4 = gates.reshape(bsz, 2 * nh, nc, L)
    seq = lambda off: pl.BlockSpec((1, s, d), lambda b, h: (b, 0, off + h))
    return pl.pallas_call(
        _mlstm_kernel,
        grid=(bsz, nh),
        in_specs=[seq(0), seq(nh), seq(2 * nh), seq(3 * nh),
                  pl.BlockSpec((1, 1, nc, L), lambda b, h: (b, h, 0, 0)),
                  pl.BlockSpec((1, 1, nc, L), lambda b, h: (b, nh + h, 0, 0)),
                  pl.BlockSpec(memory_space=pltpu.SMEM),
                  pl.BlockSpec((CONV_W, d), lambda b, h: (0, h)),
                  pl.BlockSpec((CONV_W, d), lambda b, h: (0, nh + h)),
                  pl.BlockSpec((1, d), lambda b, h: (0, h)),
                  pl.BlockSpec((1, d), lambda b, h: (0, nh + h)),
                  pl.BlockSpec((1, d), lambda b, h: (0, h))],
        out_specs=pl.BlockSpec((1, s, d), lambda b, h: (b, 0, h)),
        out_shape=jax.ShapeDtypeStruct((bsz, s, D_MLSTM), BF16),
        scratch_shapes=[pltpu.VMEM((s, d), BF16), pltpu.VMEM((s, d), BF16),
                        pltpu.VMEM((nc, L), F32), pltpu.VMEM((nc, L), F32)],
        compiler_params=pltpu.CompilerParams(vmem_limit_bytes=VMEM_LIMIT),
        name="mlstm",
    )(m_qkvo, m_qkvo, m_qkvo, m_qkvo, g4, g4, gate_bias, conv_w, conv_w,
      conv_b.reshape(1, -1), conv_b.reshape(1, -1), norm_g.reshape(1, -1))


def _outproj_kernel(att_ref, hm_ref, x_ref, mod_ref, g2_ref, wo_ref, wq_ref,
                    x1_ref, h2_ref, qp_ref):
    mod = mod_ref[0]
    y = (jnp.dot(att_ref[0], wo_ref[0:D_ATT, :], preferred_element_type=F32)
         + jnp.dot(hm_ref[0], wo_ref[D_ATT:, :], preferred_element_type=F32))
    x1 = x_ref[0] + mod[2:3, :] * y
    x1_ref[0] = x1
    h2 = _rms_rows(x1) * g2_ref[...]
    h2 = h2 * (1.0 + mod[4:5, :]) + mod[3:4, :]
    hb = h2.astype(BF16)
    h2_ref[0] = _pack_bf16_halves(hb)
    qp_ref[0] = jnp.dot(hb, wq_ref[...], preferred_element_type=F32).astype(BF16)


def _outproj(att, hm, x, mod, g2, w_out, w_q, b0, s0=0):
    bsz, s, _ = att.shape
    d = x.shape[-1]
    tm = min(ROW_TILE, s)
    i0 = s0 // tm
    nq = w_q.shape[1]
    tile = lambda n: pl.BlockSpec((1, tm, n), lambda b, i: (b, i, 0))
    return pl.pallas_call(
        _outproj_kernel,
        grid=(bsz, s // tm),
        in_specs=[tile(D_ATT), tile(D_MLSTM),
                  pl.BlockSpec((1, tm, d), lambda b, i: (b + b0, i + i0, 0)),
                  pl.BlockSpec((1, 6, d), lambda b, i: (b + b0, 0, 0)),
                  pl.BlockSpec((1, d), lambda b, i: (0, 0)),
                  pl.BlockSpec((d, d), lambda b, i: (0, 0)),
                  pl.BlockSpec((d, nq), lambda b, i: (0, 0))],
        out_specs=[tile(d), tile(d // 2), tile(nq)],
        out_shape=[jax.ShapeDtypeStruct((bsz, s, d), F32),
                   jax.ShapeDtypeStruct((bsz, s, d // 2), jnp.int32),
                   jax.ShapeDtypeStruct((bsz, s, nq), BF16)],
        compiler_params=pltpu.CompilerParams(vmem_limit_bytes=VMEM_LIMIT),
        name="outproj",
    )(att, hm, x, mod, g2.reshape(1, d), w_out, w_q)


def _top16_rows(blocks, ids):
    big = jnp.float32(1 << 20)
    vals, pos = [], []
    for _ in range(PEER_TOPK):
        m = functools.reduce(jnp.maximum, blocks)
        m = jnp.max(m, axis=0, keepdims=True)
        cand = functools.reduce(jnp.minimum,
                                [jnp.where(b == m, i, big) for b, i in zip(blocks, ids)])
        p = jnp.min(cand, axis=0, keepdims=True)
        blocks = [jnp.where(i == p, -jnp.inf, b) for b, i in zip(blocks, ids)]
        vals.append(m)
        pos.append(p)
    return jnp.concatenate(vals, axis=0), jnp.concatenate(pos, axis=0)


def _pick_rows(table, sel):
    out = jnp.zeros(sel.shape, table.dtype)
    for r in range(PEER_TOPK):
        out = jnp.where(sel == r, table[r:r + 1, :], out)
    return out


def _pair_candidates(a, b):
    k, sub = PEER_TOPK, 8
    j_id = lax.broadcasted_iota(jnp.int32, (sub, LANES), 0).astype(F32)
    blocks = [a[0:1, :] + b[0:sub, :], a[0:1, :] + b[sub:k, :], a[1:2, :] + b[0:sub, :]]
    ids = [j_id, j_id + sub, j_id + k]
    for i in range(2, sub):
        blocks.append(jnp.where(j_id < k // (i + 1), a[i:i + 1, :] + b[0:sub, :], -jnp.inf))
        ids.append(j_id + i * k)
    blocks.append(a[sub:k, :] + b[0:1, :])
    ids.append((j_id + sub) * k)
    return blocks, ids


def _route_kernel(q_ref, keys_ref, idx_ref, g_ref):
    k = PEER_TOPK
    half = PEER_QDIM // 2
    key_id = lax.broadcasted_iota(jnp.int32, (N_KEYS, LANES), 0).astype(F32)
    for t in range(q_ref.shape[0] // LANES):
        rows = pl.ds(t * LANES, LANES)
        sv, si = [], []
        for p in range(2):
            qh = q_ref[rows, p * half:(p + 1) * half]
            s = lax.dot_general(keys_ref[0, p], qh, _NT, preferred_element_type=F32)
            v, i = _top16_rows([s], [key_id])
            sv.append(v)
            si.append(i.astype(jnp.int32))
        top_s, pos = _top16_rows(*_pair_candidates(sv[0], sv[1]))
        pos = pos.astype(jnp.int32)
        idx = (_pick_rows(si[0], lax.shift_right_logical(pos, k.bit_length() - 1)) * N_KEYS
               + _pick_rows(si[1], lax.bitwise_and(pos, k - 1)))
        e = jnp.exp(top_s - top_s[0:1, :])
        idx_ref[:, t * LANES:(t + 1) * LANES] = idx
        g_ref[:, t * LANES:(t + 1) * LANES] = e / jnp.sum(e, axis=0, keepdims=True)


def _route(qp, sub_keys):
    t, _ = qp.shape
    tt = min(ROUTE_TILE, t)
    k = PEER_TOPK
    return pl.pallas_call(
        _route_kernel,
        grid=(t // tt, PEER_HEADS),
        in_specs=[pl.BlockSpec((tt, PEER_QDIM), lambda i, h: (i, h)),
                  pl.BlockSpec((1, 2, N_KEYS, PEER_QDIM // 2), lambda i, h: (h, 0, 0, 0))],
        out_specs=[pl.BlockSpec((k, tt), lambda i, h: (h, i)),
                   pl.BlockSpec((k, tt), lambda i, h: (h, i))],
        out_shape=[jax.ShapeDtypeStruct((PEER_HEADS * k, t), jnp.int32),
                   jax.ShapeDtypeStruct((PEER_HEADS * k, t), F32)],
        name="route",
    )(qp, sub_keys)


def _final_kernel(x1_ref, po_ref, mod_ref, g_ref, o_ref):
    x2 = x1_ref[0] + mod_ref[0][5:6, :] * po_ref[0]
    o_ref[0] = _rms_rows(x2) * g_ref[...]


def _final_kernel_into(x1_ref, po_ref, mod_ref, g_ref, prev_ref, o_ref):
    del prev_ref
    _final_kernel(x1_ref, po_ref, mod_ref, g_ref, o_ref)


def _final(x1, peer_out, mod, final_g, b0, out, s_full, s0=0):
    bsz, s, d = x1.shape
    tm = min(ROW_TILE, s)
    i0 = s0 // tm
    tile = pl.BlockSpec((1, tm, d), lambda b, i: (b, i, 0))
    in_specs = [tile, tile, pl.BlockSpec((1, 6, d), lambda b, i: (b + b0, 0, 0)),
                pl.BlockSpec((1, d), lambda b, i: (0, 0))]
    args = (x1, peer_out, mod, final_g.reshape(1, d))
    if out is not None:
        in_specs.append(pl.BlockSpec(memory_space=pl.ANY))
        args += (out,)
    return pl.pallas_call(
        _final_kernel if out is None else _final_kernel_into,
        grid=(bsz, s // tm),
        in_specs=in_specs,
        out_specs=pl.BlockSpec((1, tm, d), lambda b, i: (b + b0, i + i0, 0)),
        out_shape=jax.ShapeDtypeStruct((mod.shape[0], s_full, d), F32),
        input_output_aliases={} if out is None else {4: 0},
        name="final",
    )(*args)


GELU_C = math.sqrt(2.0 / math.pi)
SC_LANES = 16
SC_WORKERS = 32
SC_CORES = 2
SC_TOKEN_BLOCK = 16
SC_ITEM_HEADS = 2
SC_GATHER_DEPTH = 3
SC_PACK_ROWS = 16
SC_PACK_DEPTH = 4
SC_ILV = plsc.PackFormat.INTERLEAVED
N_PAIRS = PEER_HEADS * PEER_TOPK


def _gelu_tanh(x):
    z = GELU_C * (x + 0.044715 * (x * x * x))
    t = 1.0 - 2.0 / (jnp.exp(2.0 * z) + 1.0)
    return x * (0.5 * (1.0 + t))


def _pack_tables(u, v):
    n_rows, d = u.shape
    dw = d // 2
    rb = SC_PACK_ROWS
    nb = SC_PACK_DEPTH
    rpw = n_rows // SC_WORKERS
    nblk = rpw // rb
    assert rpw % rb == 0 and nblk % nb == 0 and nb % 2 == 0
    mesh = plsc.VectorSubcoreMesh(core_axis_name="c", subcore_axis_name="s")
    out_t = jax.ShapeDtypeStruct((n_rows, dw), jnp.int32)

    @functools.partial(
        pl.kernel, mesh=mesh, out_type=(out_t, out_t),
        scratch_types=[pltpu.VMEM((nb, rb, d), F32), pltpu.VMEM((2, rb, dw), jnp.int32),
                       pltpu.SemaphoreType.DMA((nb,)), pltpu.SemaphoreType.DMA((2,))],
        compiler_params=pltpu.CompilerParams(needs_layout_passes=False),
        name="pack_tables",
    )
    def pack(u_hbm, v_hbm, uo_hbm, vo_hbm, src, dst, isem, osem):
        wid = lax.axis_index("s") * SC_CORES + lax.axis_index("c")
        base = wid * rpw
        for tab, out in ((u_hbm, uo_hbm), (v_hbm, vo_hbm)):
            def in_copy(blk, b):
                return pltpu.make_async_copy(tab.at[pl.ds(base + blk * rb, rb)], src.at[b], isem.at[b])

            def out_copy(blk, s):
                return pltpu.make_async_copy(dst.at[s], out.at[pl.ds(base + blk * rb, rb)], osem.at[s])

            for b in range(nb - 1):
                in_copy(b, b).start()

            @pl.loop(0, nblk, step=nb)
            def _(blk):
                for b in range(nb):
                    ahead = blk + b + nb - 1

                    @pl.when(ahead < nblk)
                    def _():
                        in_copy(ahead, (b + nb - 1) % nb).start()

                    in_copy(blk + b, b).wait()
                    s = b % 2

                    @pl.when(blk + b >= 2)
                    def _():
                        out_copy(blk + b - 2, s).wait()

                    @pl.loop(0, rb)
                    def _(r):
                        @plsc.parallel_loop(0, dw // SC_LANES, 1, unroll=4)
                        def _(c):
                            lo = src[b, r, pl.ds(c * SC_LANES, SC_LANES)]
                            hi = src[b, r, pl.ds(dw + c * SC_LANES, SC_LANES)]
                            dst[s, r, pl.ds(c * SC_LANES, SC_LANES)] = plsc.bitcast(
                                plsc.pack(lo, hi, format=SC_ILV), jnp.int32)

                    out_copy(blk + b, s).start()

            out_copy(nblk - 2, 0).wait()
            out_copy(nblk - 1, 1).wait()

    return pack(u, v)


def _experts(h2, idx, g, u, v):
    t_total, dw = h2.shape
    d = 2 * dw
    tpw = t_total // SC_WORKERS
    tb = min(tpw, SC_TOKEN_BLOCK)
    ng = SC_ITEM_HEADS
    k = PEER_TOPK * ng
    nh = PEER_HEADS // ng
    n_items = tb * nh
    nbuf = SC_GATHER_DEPTH
    head_bits = nh.bit_length() - 1
    assert tpw % tb == 0 and n_items >= nbuf and nh == 1 << head_bits and PEER_TOPK == SC_LANES
    mesh = plsc.VectorSubcoreMesh(core_axis_name="c", subcore_axis_name="s")

    @functools.partial(
        pl.kernel, mesh=mesh,
        out_type=jax.ShapeDtypeStruct((t_total, d), F32),
        scratch_types=[
            pltpu.VMEM((tb, N_PAIRS), jnp.int32),
            pltpu.VMEM((tb, N_PAIRS), F32),
            pltpu.VMEM((tb, dw), jnp.int32),
            pltpu.VMEM((tb, d), F32),
            pltpu.VMEM((nbuf, k, dw), jnp.int32),
            pltpu.VMEM((nbuf, k, dw), jnp.int32),
            pltpu.SemaphoreType.DMA((nbuf,)),
            pltpu.SemaphoreType.DMA((nbuf,)),
        ],
        compiler_params=pltpu.CompilerParams(needs_layout_passes=False),
        name="experts",
    )
    def experts(h_hbm, idx_hbm, g_hbm, u_hbm, v_hbm, out_hbm,
                idx_b, g_b, h_b, out_b, urows, vrows, usem, vsem):
        wid = lax.axis_index("s") * SC_CORES + lax.axis_index("c")
        base = wid * tpw
        lane = lax.iota(jnp.int32, SC_LANES)

        def split(item):
            return lax.shift_right_logical(item, head_bits), lax.bitwise_and(item, nh - 1)

        def copies(item, b):
            tt, hd = split(item)
            ids = idx_b.at[tt, pl.ds(hd * k, k)]
            return (pltpu.make_async_copy(u_hbm.at[ids], urows.at[b], usem.at[b]),
                    pltpu.make_async_copy(v_hbm.at[ids], vrows.at[b], vsem.at[b]))

        def fetch(item, b):
            cu, cv = copies(item, b)
            cu.start()
            cv.start()

        def words(ref, *lead, off):
            return plsc.bitcast(ref[(*lead, pl.ds(off, SC_LANES))], BF16)

        def sum4_unpack(pr):
            return plsc.unpack((pr[0] + pr[1]) + (pr[2] + pr[3]), format=SC_ILV)

        def compute(item, b):
            tt, hd = split(item)
            cu, cv = copies(item, b)
            cu.wait()

            def ubody(j, accs):
                hs = [words(h_b, tt, off=(j * 4 + q) * SC_LANES) for q in range(4)]
                new = []
                for p, a in enumerate(accs):
                    lo, hi = sum4_unpack([words(urows, b, p, off=(j * 4 + q) * SC_LANES) * hs[q]
                                          for q in range(4)])
                    new.append(a + (lo + hi))
                return tuple(new)

            accs = plsc.parallel_loop(
                0, dw // (4 * SC_LANES), 1,
                carry=tuple(jnp.zeros((SC_LANES,), F32) for _ in range(k)))(ubody)
            cbb = []
            for hh in range(ng):
                s = jnp.zeros((SC_LANES,), F32)
                for p in range(SC_LANES):
                    s = jnp.where(lane == p, jnp.sum(accs[hh * SC_LANES + p]), s)
                c = g_b[tt, pl.ds(hd * k + hh * SC_LANES, SC_LANES)] * _gelu_tanh(s)
                for p in range(SC_LANES):
                    cp = jnp.full((SC_LANES,), c[p])
                    cbb.append(plsc.pack(cp, cp, format=SC_ILV))
            cv.wait()

            def tree(xs):
                while len(xs) > 1:
                    xs = [xs[i] + xs[i + 1] for i in range(0, len(xs), 2)]
                return xs[0]

            @plsc.parallel_loop(0, dw // SC_LANES, 1, unroll=2 // ng)
            def _(ch):
                sa = pl.ds(ch * SC_LANES, SC_LANES)
                sb = pl.ds(dw + ch * SC_LANES, SC_LANES)
                los, his = [], []
                for grp in range(k // 4):
                    lo, hi = sum4_unpack([cbb[grp * 4 + q]
                                          * words(vrows, b, grp * 4 + q, off=ch * SC_LANES)
                                          for q in range(4)])
                    los.append(lo)
                    his.append(hi)
                out_b[tt, sa] = out_b[tt, sa] + tree(los)
                out_b[tt, sb] = out_b[tt, sb] + tree(his)

        @pl.loop(0, tpw // tb)
        def _(blk):
            t0 = base + blk * tb
            pltpu.sync_copy(idx_hbm.at[pl.ds(t0, tb)], idx_b)
            pltpu.sync_copy(g_hbm.at[pl.ds(t0, tb)], g_b)
            pltpu.sync_copy(h_hbm.at[pl.ds(t0, tb)], h_b)
            for b in range(nbuf - 1):
                fetch(b, b)

            @pl.loop(0, tb)
            def _(tt):
                @pl.loop(0, d // SC_LANES)
                def _(j):
                    out_b[tt, pl.ds(j * SC_LANES, SC_LANES)] = jnp.zeros((SC_LANES,), F32)

            @pl.loop(0, n_items)
            def _(it):
                ahead = it + nbuf - 1

                @pl.when(ahead < n_items)
                def _():
                    fetch(ahead, lax.rem(ahead, nbuf))

                compute(it, lax.rem(it, nbuf))

            pltpu.sync_copy(out_b, out_hbm.at[pl.ds(t0, tb)])

    return experts(h2, idx, g, u, v)


SPLIT_ROWS = 4
ROW_PIECES = 2
BATCH_CHUNKS = (1,) * 12


def kernel(x, c, w_ada, b_ada, norm1_g, norm2_g, w_in, conv_w, conv_b, b_igate, b_fgate, lam_q1, lam_k1, lam_q2, lam_k2, diff_sub_g, mlstm_norm_g, w_out, peer_w_q, peer_sub_keys, peer_u, peer_v, rel_bias, final_g):
    bsz, s, d = x.shape
    mod = _ada(c, w_ada[0], b_ada[0]).reshape(bsz, 6, d)

    w = w_in[0]
    w_qk = w[:, :2 * D_ATT].astype(BF16)
    w_vt = w[:, 2 * D_ATT:3 * D_ATT].T.astype(BF16)
    w_m = w[:, 3 * D_ATT:3 * D_ATT + 4 * D_MLSTM].astype(BF16)
    w_gt = w[:, 3 * D_ATT + 4 * D_MLSTM:].T.astype(BF16)
    w_o = w_out[0].astype(BF16)
    w_q = peer_w_q[0].astype(BF16)
    sub_keys = peer_sub_keys[0].astype(BF16)
    bias_tiles = _relbias(rel_bias)
    lam4 = jnp.stack([lam_q1[0], lam_k1[0], lam_q2[0], lam_k2[0]])
    gate_bias = jnp.concatenate([b_igate[0], b_fgate[0]])

    assert sum(BATCH_CHUNKS) == bsz - SPLIT_ROWS
    tables = None
    out = None

    def mix_and_experts(att, hm, b0, out, s0=0):
        nonlocal tables
        nb, sp, _ = att.shape
        x1, h2, qp = _outproj(att, hm, x, mod, norm2_g[0], w_o, w_q, b0, s0)
        idx_t, g_t = _route(qp.reshape(nb * sp, -1), sub_keys)
        if tables is None:
            tables = _pack_tables(peer_u[0], peer_v[0])
        peer_out = _experts(h2.reshape(nb * sp, d // 2), idx_t.T, g_t.T, *tables)
        return _final(x1, peer_out.reshape(nb, sp, d), mod, final_g, b0, out, s, s0)

    sp = s // ROW_PIECES
    for b0 in range(SPLIT_ROWS):
        parts = []
        for p in range(ROW_PIECES):
            parts.append(_inproj(x, mod, norm1_g[0], w_qk, w_vt, w_m, w_gt, b0, 1, p * sp, sp))
            att_qk, att_vt, m_qkvo, gates = (jnp.concatenate(t, axis=a)
                                             for t, a in zip(zip(*parts), (1, 2, 1, 2)))
            att = _attn(att_qk, att_vt, bias_tiles, lam4, diff_sub_g[0], q_start=p * sp)
            hm = _mlstm(m_qkvo, gates, gate_bias, conv_w[0], conv_b[0], mlstm_norm_g[0])[:, p * sp:]
            out = mix_and_experts(att, hm, b0, out, p * sp)

    b0 = SPLIT_ROWS
    for nb in BATCH_CHUNKS:
        att_qk, att_vt, m_qkvo, gates = _inproj(x, mod, norm1_g[0], w_qk, w_vt, w_m, w_gt, b0, nb)
        att = _attn(att_qk, att_vt, bias_tiles, lam4, diff_sub_g[0])
        hm = _mlstm(m_qkvo, gates, gate_bias, conv_w[0], conv_b[0], mlstm_norm_g[0])
        out = mix_and_experts(att, hm, b0, out)
        b0 += nb
    return out
```

```python
import functools
import math

import numpy as np
import jax
import jax.numpy as jnp
from jax import lax
from jax.experimental import pallas as pl
from jax.experimental.pallas import tpu as pltpu
from jax.experimental.pallas import tpu_sc as plsc

F32 = jnp.float32
BF16 = jnp.bfloat16

ATT_HEADS = 4
ATT_QK_DIM = 64
ATT_V_DIM = 128
D_ATT = ATT_HEADS * ATT_V_DIM
M_HEADS = 4
M_DIM = 128
D_MLSTM = M_HEADS * M_DIM
CONV_W = 4
N_BUCKETS = 32
MAX_DIST = 128
N_KEYS = 128
PEER_HEADS = 8
PEER_TOPK = 16
PEER_QDIM = 256
EPS = 1e-6
LAMBDA_INIT = 0.8 - 0.6 * math.exp(-0.3 * 0)

ATT_BLOCK = 256
M_CHUNK = 128
ROW_TILE = 512
ROUTE_TILE = 512
LANES = 128
V7X_VMEM_BYTES = 64 * 1024 * 1024
VMEM_LIMIT = V7X_VMEM_BYTES * 3 // 4

_NT = (((1,), (1,)), ((), ()))


def _rms_rows(x):
    return x * lax.rsqrt(jnp.mean(x * x, axis=-1, keepdims=True) + EPS)


def _sigmoid(x):
    return 1.0 / (1.0 + jnp.exp(-x))


def _pack_bf16_halves(x):
    bits = lax.bitcast_convert_type(x.astype(BF16).astype(F32), jnp.int32)
    n = x.shape[-1] // 2
    return lax.bitwise_or(lax.shift_right_logical(bits[..., :n], 16),
                          lax.bitwise_and(bits[..., n:], jnp.int32(-65536)))


def _ada_kernel(c_ref, w_ref, b_ref, o_ref):
    c = c_ref[...]
    cond = c * _sigmoid(c)
    o_ref[...] = jnp.dot(cond.astype(BF16), w_ref[...].astype(BF16),
                         preferred_element_type=F32) + b_ref[...]


def _ada(c, w, b):
    bsz, d = c.shape
    n = w.shape[1]
    return pl.pallas_call(
        _ada_kernel,
        grid=(n // d,),
        in_specs=[pl.BlockSpec((bsz, d), lambda j: (0, 0)),
                  pl.BlockSpec((d, d), lambda j: (0, j)),
                  pl.BlockSpec((1, d), lambda j: (0, j))],
        out_specs=pl.BlockSpec((bsz, d), lambda j: (0, j)),
        out_shape=jax.ShapeDtypeStruct((bsz, n), F32),
        name="ada",
    )(c, w, b.reshape(1, n))


def _inproj_kernel(x_ref, mod_ref, g_ref, wa_ref, wvt_ref, wm_ref, wg_ref,
                   oa_ref, ovt_ref, om_ref, og_ref):
    x = x_ref[0]
    mod = mod_ref[0]
    h = _rms_rows(x) * g_ref[...]
    h = h * (1.0 + mod[1:2, :]) + mod[0:1, :]
    hb = h.astype(BF16)
    oa_ref[0] = jnp.dot(hb, wa_ref[...], preferred_element_type=F32).astype(BF16)
    ovt_ref[0] = lax.dot_general(wvt_ref[...], hb, _NT, preferred_element_type=F32).astype(BF16)
    om_ref[0] = jnp.dot(hb, wm_ref[...], preferred_element_type=F32).astype(BF16)
    og_ref[0] = lax.dot_general(wg_ref[...], hb, _NT, preferred_element_type=F32)


def _inproj(x, mod, g, w_qk, w_vt, w_m, w_gt, b0, bsz, s0=0, s=None):
    d = x.shape[-1]
    s = x.shape[1] if s is None else s
    tm = min(ROW_TILE, s)
    i0 = s0 // tm
    na, nv, nm, ng = w_qk.shape[1], w_vt.shape[0], w_m.shape[1], w_gt.shape[0]
    const = lambda shape: pl.BlockSpec(shape, lambda b, i: (0, 0))
    return pl.pallas_call(
        _inproj_kernel,
        grid=(bsz, s // tm),
        in_specs=[pl.BlockSpec((1, tm, d), lambda b, i: (b + b0, i + i0, 0)),
                  pl.BlockSpec((1, 6, d), lambda b, i: (b + b0, 0, 0)),
                  const((1, d)), const((d, na)), const((nv, d)), const((d, nm)), const((ng, d))],
        out_specs=[pl.BlockSpec((1, tm, na), lambda b, i: (b, i, 0)),
                   pl.BlockSpec((1, nv, tm), lambda b, i: (b, 0, i)),
                   pl.BlockSpec((1, tm, nm), lambda b, i: (b, i, 0)),
                   pl.BlockSpec((1, ng, tm), lambda b, i: (b, 0, i))],
        out_shape=[jax.ShapeDtypeStruct((bsz, s, na), BF16),
                   jax.ShapeDtypeStruct((bsz, nv, s), BF16),
                   jax.ShapeDtypeStruct((bsz, s, nm), BF16),
                   jax.ShapeDtypeStruct((bsz, ng, s), F32)],
        compiler_params=pltpu.CompilerParams(vmem_limit_bytes=VMEM_LIMIT),
        name="inproj",
    )(x, mod, g.reshape(1, d), w_qk, w_vt, w_m, w_gt)


def _rel_buckets():
    n = np.arange(2 * ATT_BLOCK)
    max_exact = N_BUCKETS // 2
    nf = np.maximum(n, 1).astype(np.float64)
    large = max_exact + (np.log(nf / max_exact) / math.log(MAX_DIST / max_exact)
                         * (N_BUCKETS - max_exact)).astype(np.int64)
    large = np.minimum(large, N_BUCKETS - 1)
    bucket = np.where(n < max_exact, n, large)
    assert (bucket[ATT_BLOCK + 1:] == N_BUCKETS - 1).all() and 2 * ATT_BLOCK > MAX_DIST
    qk = np.arange(ATT_BLOCK)[None, :] - np.arange(ATT_BLOCK)[:, None]
    tiles = np.stack([bucket[np.maximum(qk, 0)], bucket[ATT_BLOCK + qk]])
    return tiles.astype(np.int32)


def _relbias_kernel(rb_ref, bk_ref, o_ref):
    h = pl.program_id(0)
    bk = bk_ref[...]
    acc = jnp.zeros(bk.shape, F32)
    for b in range(N_BUCKETS):
        acc = jnp.where(bk == b, rb_ref[b, h], acc)
    o_ref[0] = acc


def _relbias(rel_bias):
    tiles = jnp.asarray(_rel_buckets())
    return pl.pallas_call(
        _relbias_kernel,
        grid=(ATT_HEADS,),
        in_specs=[pl.BlockSpec(memory_space=pltpu.SMEM),
                  pl.BlockSpec((2, ATT_BLOCK, ATT_BLOCK), lambda h: (0, 0, 0))],
        out_specs=pl.BlockSpec((1, 2, ATT_BLOCK, ATT_BLOCK), lambda h: (h, 0, 0, 0)),
        out_shape=jax.ShapeDtypeStruct((ATT_HEADS, 2, ATT_BLOCK, ATT_BLOCK), F32),
        name="relbias",
    )(rel_bias, tiles)


def _attn_kernel(q_ref, k_ref, vt_ref, bias_ref, lam_ref, subg_ref, o_ref,
                 qz_ref, m_ref, l_ref, acc_ref, *, q0):
    tq = ATT_BLOCK
    qi = pl.program_id(2) + q0
    scale = ATT_QK_DIM ** -0.5

    q = q_ref[0]
    lane = lax.broadcasted_iota(jnp.int32, q.shape, 1)
    zero = jnp.zeros_like(q)
    qz_ref[0:tq, :] = jnp.where(lane < ATT_QK_DIM, q, zero)
    qz_ref[tq:2 * tq, :] = jnp.where(lane >= ATT_QK_DIM, q, zero)
    m_ref[...] = jnp.full(m_ref.shape, -jnp.inf, F32)
    l_ref[...] = jnp.zeros(l_ref.shape, F32)
    acc_ref[...] = jnp.zeros(acc_ref.shape, F32)

    def step(j, bias, masked):
        start = pl.multiple_of(j * tq, tq)
        k = k_ref[0, pl.ds(start, tq), :]
        vt = vt_ref[0, :, pl.ds(start, tq)]
        s = lax.dot_general(k, qz_ref[...], _NT, preferred_element_type=F32) * scale
        if isinstance(bias, tuple):
            s = s + jnp.concatenate([bias[0], bias[0]], axis=1)
        else:
            s = s + bias
        if masked:
            key = lax.broadcasted_iota(jnp.int32, (tq, tq), 0)
            qry = lax.broadcasted_iota(jnp.int32, (tq, tq), 1)
            keep = jnp.concatenate([key <= qry, key <= qry], axis=1)
            s = jnp.where(keep, s, jnp.finfo(F32).min)
        m_old = m_ref[...]
        m_new = jnp.maximum(m_old, jnp.max(s, axis=0, keepdims=True))
        alpha = jnp.exp(m_old - m_new)
        p = jnp.exp(s - m_new)
        l_ref[...] = alpha * l_ref[...] + jnp.sum(p, axis=0, keepdims=True)
        acc_ref[...] = alpha * acc_ref[...] + jnp.dot(vt, p.astype(BF16), preferred_element_type=F32)
        m_ref[...] = m_new

    far_bias = bias_ref[0, 1, 0:1, tq - 1:tq]

    def far_body(j, carry):
        step(j, far_bias, False)
        return carry

    lax.fori_loop(0, jnp.maximum(qi - 1, 0), far_body, 0)

    @pl.when(qi >= 1)
    def _():
        step(qi - 1, (bias_ref[0, 1],), False)

    step(qi, (bias_ref[0, 0],), True)

    lam = (jnp.exp(jnp.sum(lam_ref[0:1, :] * lam_ref[1:2, :], axis=-1, keepdims=True))
           - jnp.exp(jnp.sum(lam_ref[2:3, :] * lam_ref[3:4, :], axis=-1, keepdims=True))
           + LAMBDA_INIT)
    ot = acc_ref[...] / l_ref[...]
    o = (ot[:, 0:tq] - lam * ot[:, tq:2 * tq]).T
    o = _rms_rows(o) * subg_ref[...] * (1.0 - LAMBDA_INIT)
    o_ref[0] = o.astype(o_ref.dtype)


def _attn(att_qk, att_vt, bias_tiles, lam4, sub_g, q_start=0):
    bsz, s, _ = att_qk.shape
    tq = ATT_BLOCK
    nh = ATT_HEADS
    q0 = q_start // tq
    nq = s // tq - q0
    return pl.pallas_call(
        functools.partial(_attn_kernel, q0=q0),
        grid=(bsz, nh, nq),
        in_specs=[pl.BlockSpec((1, tq, ATT_V_DIM), lambda b, h, i: (b, i + q0, h)),
                  pl.BlockSpec((1, s, ATT_V_DIM), lambda b, h, i: (b, 0, nh + h)),
                  pl.BlockSpec((1, ATT_V_DIM, s), lambda b, h, i: (b, h, 0)),
                  pl.BlockSpec((1, 2, tq, tq), lambda b, h, i: (h, 0, 0, 0)),
                  pl.BlockSpec((4, ATT_QK_DIM), lambda b, h, i: (0, 0)),
                  pl.BlockSpec((1, ATT_V_DIM), lambda b, h, i: (0, 0))],
        out_specs=pl.BlockSpec((1, tq, ATT_V_DIM), lambda b, h, i: (b, i, h)),
        out_shape=jax.ShapeDtypeStruct((bsz, nq * tq, D_ATT), BF16),
        scratch_shapes=[pltpu.VMEM((2 * tq, ATT_V_DIM), BF16),
                        pltpu.VMEM((1, 2 * tq), F32),
                        pltpu.VMEM((1, 2 * tq), F32),
                        pltpu.VMEM((ATT_V_DIM, 2 * tq), F32)],
        compiler_params=pltpu.CompilerParams(vmem_limit_bytes=VMEM_LIMIT),
        name="attn",
    )(att_qk, att_qk, att_vt, bias_tiles, lam4, sub_g.reshape(1, ATT_V_DIM))


def _mlstm_kernel(q_ref, k_ref, v_ref, o_ref, gi_ref, gf_ref, bias_ref, cwq_ref, cwk_ref,
                  cbq_ref, cbk_ref, ng_ref, out_ref, qs_ref, ks_ref, b_ref, ig_ref):
    s = q_ref.shape[1]
    L = M_CHUNK
    nc = s // L
    h = pl.program_id(1)

    row = lax.broadcasted_iota(jnp.int32, (s, M_DIM), 0)

    def conv_silu(x_ref, w_ref, cb_ref):
        x = x_ref[0].astype(F32)
        w = w_ref[...]
        out = None
        for j in range(CONV_W):
            shift = CONV_W - 1 - j
            xs = x if shift == 0 else jnp.where(row >= shift, pltpu.roll(x, shift, 0), 0.0)
            term = xs * w[j:j + 1, :]
            out = term if out is None else out + term
        out = out + cb_ref[...]
        return out * _sigmoid(out)

    qs_ref[...] = conv_silu(q_ref, cwq_ref, cbq_ref).astype(BF16)
    ks_ref[...] = (conv_silu(k_ref, cwk_ref, cbk_ref) * (M_DIM ** -0.5)).astype(BF16)

    ig = gi_ref[0, 0] + bias_ref[h]
    f = gf_ref[0, 0] + bias_ref[M_HEADS + h]
    logf = jnp.minimum(f, 0.0) - jnp.log(1.0 + jnp.exp(-jnp.abs(f)))
    r = lax.broadcasted_iota(jnp.int32, (L, L), 0)
    c = lax.broadcasted_iota(jnp.int32, (L, L), 1)
    tri = (r <= c).astype(F32)
    b_ref[...] = jnp.dot(logf, tri, preferred_element_type=F32,
                         precision=lax.Precision.HIGHEST)
    ig_ref[...] = ig
    eye = r == c
    causal = c <= r

    def to_col(x_row):
        return jnp.sum(jnp.where(eye, x_row, 0.0), axis=1, keepdims=True)

    def chunk(ci, carry):
        C, n, m = carry
        start = pl.multiple_of(ci * L, L)
        qc = qs_ref[pl.ds(start, L), :]
        kc = ks_ref[pl.ds(start, L), :]
        vc = v_ref[0, pl.ds(start, L), :]
        b_r = b_ref[pl.ds(ci, 1), :]
        ig_r = ig_ref[pl.ds(ci, 1), :]
        b_last = b_r[:, L - 1:L]
        a_r = b_last - b_r + ig_r
        b_c = to_col(b_r)
        a_c = to_col(a_r)

        logd = jnp.where(causal, b_c - b_r + ig_r, -jnp.inf)
        m_inter = b_c + m
        m_j = jnp.maximum(jnp.max(logd, axis=1, keepdims=True), m_inter)
        w = jnp.exp(logd - m_j)
        sqk = lax.dot_general(qc, kc, _NT, preferred_element_type=F32) * w
        inter = jnp.exp(m_inter - m_j)
        num = (jnp.dot(sqk.astype(BF16), vc, preferred_element_type=F32)
               + inter * jnp.dot(qc, C.astype(BF16), preferred_element_type=F32))
        den = (jnp.sum(sqk, axis=1, keepdims=True)
               + inter * jnp.sum(qc.astype(F32) * n, axis=1, keepdims=True))
        hc = num / jnp.maximum(jnp.abs(den), jnp.exp(-m_j))

        og = _sigmoid(o_ref[0, pl.ds(start, L), :].astype(F32))
        out_ref[0, pl.ds(start, L), :] = (_rms_rows(og * hc) * ng_ref[...]).astype(out_ref.dtype)

        m_new = jnp.maximum(b_last + m, jnp.max(a_r, axis=1, keepdims=True))
        decay = jnp.exp(b_last + m - m_new)
        kw = kc.astype(F32) * jnp.exp(a_c - m_new)
        C_new = decay * C + jnp.dot(kw.T.astype(BF16), vc, preferred_element_type=F32)
        n_new = decay * n + jnp.sum(kw, axis=0, keepdims=True)
        return C_new, n_new, m_new

    init = (jnp.zeros((M_DIM, M_DIM), F32), jnp.zeros((1, M_DIM), F32), jnp.zeros((1, 1), F32))
    lax.fori_loop(0, nc, chunk, init)


def _mlstm(m_qkvo, gates, gate_bias, conv_w, conv_b, norm_g):
    bsz, s, _ = m_qkvo.shape
    L = M_CHUNK
    nc = s // L
    nh = M_HEADS
    d = M_DIM
    g4 = gates.reshape(bsz, 2 * nh, nc, L)
    seq = lambda off: pl.BlockSpec((1, s, d), lambda b, h: (b, 0, off + h))
    return pl.pallas_call(
        _mlstm_kernel,
        grid=(bsz, nh),
        in_specs=[seq(0), seq(nh), seq(2 * nh), seq(3 * nh),
                  pl.BlockSpec((1, 1, nc, L), lambda b, h: (b, h, 0, 0)),
                  pl.BlockSpec((1, 1, nc, L), lambda b, h: (b, nh + h, 0, 0)),
                  pl.BlockSpec(memory_space=pltpu.SMEM),
                  pl.BlockSpec((CONV_W, d), lambda b, h: (0, h)),
                  pl.BlockSpec((CONV_W, d), lambda b, h: (0, nh + h)),
                  pl.BlockSpec((1, d), lambda b, h: (0, h)),
                  pl.BlockSpec((1, d), lambda b, h: (0, nh + h)),
                  pl.BlockSpec((1, d), lambda b, h: (0, h))],
        out_specs=pl.BlockSpec((1, s, d), lambda b, h: (b, 0, h)),
        out_shape=jax.ShapeDtypeStruct((bsz, s, D_MLSTM), BF16),
        scratch_shapes=[pltpu.VMEM((s, d), BF16), pltpu.VMEM((s, d), BF16),
                        pltpu.VMEM((nc, L), F32), pltpu.VMEM((nc, L), F32)],
        compiler_params=pltpu.CompilerParams(vmem_limit_bytes=VMEM_LIMIT),
        name="mlstm",
    )(m_qkvo, m_qkvo, m_qkvo, m_qkvo, g4, g4, gate_bias, conv_w, conv_w,
      conv_b.reshape(1, -1), conv_b.reshape(1, -1), norm_g.reshape(1, -1))


def _outproj_kernel(att_ref, hm_ref, x_ref, mod_ref, g2_ref, wo_ref, wq_ref,
                    x1_ref, h2_ref, qp_ref):
    mod = mod_ref[0]
    y = (jnp.dot(att_ref[0], wo_ref[0:D_ATT, :], preferred_element_type=F32)
         + jnp.dot(hm_ref[0], wo_ref[D_ATT:, :], preferred_element_type=F32))
    x1 = x_ref[0] + mod[2:3, :] * y
    x1_ref[0] = x1
    h2 = _rms_rows(x1) * g2_ref[...]
    h2 = h2 * (1.0 + mod[4:5, :]) + mod[3:4, :]
    hb = h2.astype(BF16)
    h2_ref[0] = _pack_bf16_halves(hb)
    qp_ref[0] = jnp.dot(hb, wq_ref[...], preferred_element_type=F32).astype(BF16)


def _outproj(att, hm, x, mod, g2, w_out, w_q, b0, s0=0):
    bsz, s, _ = att.shape
    d = x.shape[-1]
    tm = min(ROW_TILE, s)
    i0 = s0 // tm
    nq = w_q.shape[1]
    tile = lambda n: pl.BlockSpec((1, tm, n), lambda b, i: (b, i, 0))
    return pl.pallas_call(
        _outproj_kernel,
        grid=(bsz, s // tm),
        in_specs=[tile(D_ATT), tile(D_MLSTM),
                  pl.BlockSpec((1, tm, d), lambda b, i: (b + b0, i + i0, 0)),
                  pl.BlockSpec((1, 6, d), lambda b, i: (b + b0, 0, 0)),
                  pl.BlockSpec((1, d), lambda b, i: (0, 0)),
                  pl.BlockSpec((d, d), lambda b, i: (0, 0)),
                  pl.BlockSpec((d, nq), lambda b, i: (0, 0))],
        out_specs=[tile(d), tile(d // 2), tile(nq)],
        out_shape=[jax.ShapeDtypeStruct((bsz, s, d), F32),
                   jax.ShapeDtypeStruct((bsz, s, d // 2), jnp.int32),
                   jax.ShapeDtypeStruct((bsz, s, nq), BF16)],
        compiler_params=pltpu.CompilerParams(vmem_limit_bytes=VMEM_LIMIT),
        name="outproj",
    )(att, hm, x, mod, g2.reshape(1, d), w_out, w_q)


def _top16_rows(blocks, ids):
    big = jnp.float32(1 << 20)
    vals, pos = [], []
    for _ in range(PEER_TOPK):
        m = functools.reduce(jnp.maximum, blocks)
        m = jnp.max(m, axis=0, keepdims=True)
        cand = functools.reduce(jnp.minimum,
                                [jnp.where(b == m, i, big) for b, i in zip(blocks, ids)])
        p = jnp.min(cand, axis=0, keepdims=True)
        blocks = [jnp.where(i == p, -jnp.inf, b) for b, i in zip(blocks, ids)]
        vals.append(m)
        pos.append(p)
    return jnp.concatenate(vals, axis=0), jnp.concatenate(pos, axis=0)


def _pick_rows(table, sel):
    out = jnp.zeros(sel.shape, table.dtype)
    for r in range(PEER_TOPK):
        out = jnp.where(sel == r, table[r:r + 1, :], out)
    return out


def _pair_candidates(a, b):
    k, sub = PEER_TOPK, 8
    j_id = lax.broadcasted_iota(jnp.int32, (sub, LANES), 0).astype(F32)
    blocks = [a[0:1, :] + b[0:sub, :], a[0:1, :] + b[sub:k, :], a[1:2, :] + b[0:sub, :]]
    ids = [j_id, j_id + sub, j_id + k]
    for i in range(2, sub):
        blocks.append(jnp.where(j_id < k // (i + 1), a[i:i + 1, :] + b[0:sub, :], -jnp.inf))
        ids.append(j_id + i * k)
    blocks.append(a[sub:k, :] + b[0:1, :])
    ids.append((j_id + sub) * k)
    return blocks, ids


def _route_kernel(q_ref, keys_ref, idx_ref, g_ref):
    k = PEER_TOPK
    half = PEER_QDIM // 2
    key_id = lax.broadcasted_iota(jnp.int32, (N_KEYS, LANES), 0).astype(F32)
    for t in range(q_ref.shape[0] // LANES):
        rows = pl.ds(t * LANES, LANES)
        sv, si = [], []
        for p in range(2):
            qh = q_ref[rows, p * half:(p + 1) * half]
            s = lax.dot_general(keys_ref[0, p], qh, _NT, preferred_element_type=F32)
            v, i = _top16_rows([s], [key_id])
            sv.append(v)
            si.append(i.astype(jnp.int32))
        top_s, pos = _top16_rows(*_pair_candidates(sv[0], sv[1]))
        pos = pos.astype(jnp.int32)
        idx = (_pick_rows(si[0], lax.shift_right_logical(pos, k.bit_length() - 1)) * N_KEYS
               + _pick_rows(si[1], lax.bitwise_and(pos, k - 1)))
        e = jnp.exp(top_s - top_s[0:1, :])
        idx_ref[:, t * LANES:(t + 1) * LANES] = idx
        g_ref[:, t * LANES:(t + 1) * LANES] = e / jnp.sum(e, axis=0, keepdims=True)


def _route(qp, sub_keys):
    t, _ = qp.shape
    tt = min(ROUTE_TILE, t)
    k = PEER_TOPK
    return pl.pallas_call(
        _route_kernel,
        grid=(t // tt, PEER_HEADS),
        in_specs=[pl.BlockSpec((tt, PEER_QDIM), lambda i, h: (i, h)),
                  pl.BlockSpec((1, 2, N_KEYS, PEER_QDIM // 2), lambda i, h: (h, 0, 0, 0))],
        out_specs=[pl.BlockSpec((k, tt), lambda i, h: (h, i)),
                   pl.BlockSpec((k, tt), lambda i, h: (h, i))],
        out_shape=[jax.ShapeDtypeStruct((PEER_HEADS * k, t), jnp.int32),
                   jax.ShapeDtypeStruct((PEER_HEADS * k, t), F32)],
        name="route",
    )(qp, sub_keys)


def _final_kernel(x1_ref, po_ref, mod_ref, g_ref, o_ref):
    x2 = x1_ref[0] + mod_ref[0][5:6, :] * po_ref[0]
    o_ref[0] = _rms_rows(x2) * g_ref[...]


def _final_kernel_into(x1_ref, po_ref, mod_ref, g_ref, prev_ref, o_ref):
    del prev_ref
    _final_kernel(x1_ref, po_ref, mod_ref, g_ref, o_ref)


def _final(x1, peer_out, mod, final_g, b0, out, s_full, s0=0):
    bsz, s, d = x1.shape
    tm = min(ROW_TILE, s)
    i0 = s0 // tm
    tile = pl.BlockSpec((1, tm, d), lambda b, i: (b, i, 0))
    in_specs = [tile, tile, pl.BlockSpec((1, 6, d), lambda b, i: (b + b0, 0, 0)),
                pl.BlockSpec((1, d), lambda b, i: (0, 0))]
    args = (x1, peer_out, mod, final_g.reshape(1, d))
    if out is not None:
        in_specs.append(pl.BlockSpec(memory_space=pl.ANY))
        args += (out,)
    return pl.pallas_call(
        _final_kernel if out is None else _final_kernel_into,
        grid=(bsz, s // tm),
        in_specs=in_specs,
        out_specs=pl.BlockSpec((1, tm, d), lambda b, i: (b + b0, i + i0, 0)),
        out_shape=jax.ShapeDtypeStruct((mod.shape[0], s_full, d), F32),
        input_output_aliases={} if out is None else {4: 0},
        name="final",
    )(*args)


GELU_C = math.sqrt(2.0 / math.pi)
SC_LANES = 16
SC_WORKERS = 32
SC_CORES = 2
SC_TOKEN_BLOCK = 16
SC_ITEM_HEADS = 2
SC_GATHER_DEPTH = 3
SC_PACK_ROWS = 16
SC_PACK_DEPTH = 4
SC_ILV = plsc.PackFormat.INTERLEAVED
N_PAIRS = PEER_HEADS * PEER_TOPK


def _gelu_tanh(x):
    z = GELU_C * (x + 0.044715 * (x * x * x))
    t = 1.0 - 2.0 / (jnp.exp(2.0 * z) + 1.0)
    return x * (0.5 * (1.0 + t))


def _pack_tables(u, v):
    n_rows, d = u.shape
    dw = d // 2
    rb = SC_PACK_ROWS
    nb = SC_PACK_DEPTH
    rpw = n_rows // SC_WORKERS
    nblk = rpw // rb
    assert rpw % rb == 0 and nblk % nb == 0 and nb % 2 == 0
    mesh = plsc.VectorSubcoreMesh(core_axis_name="c", subcore_axis_name="s")
    out_t = jax.ShapeDtypeStruct((n_rows, dw), jnp.int32)

    @functools.partial(
        pl.kernel, mesh=mesh, out_type=(out_t, out_t),
        scratch_types=[pltpu.VMEM((nb, rb, d), F32), pltpu.VMEM((2, rb, dw), jnp.int32),
                       pltpu.SemaphoreType.DMA((nb,)), pltpu.SemaphoreType.DMA((2,))],
        compiler_params=pltpu.CompilerParams(needs_layout_passes=False),
        name="pack_tables",
    )
    def pack(u_hbm, v_hbm, uo_hbm, vo_hbm, src, dst, isem, osem):
        wid = lax.axis_index("s") * SC_CORES + lax.axis_index("c")
        base = wid * rpw
        for tab, out in ((u_hbm, uo_hbm), (v_hbm, vo_hbm)):
            def in_copy(blk, b):
                return pltpu.make_async_copy(tab.at[pl.ds(base + blk * rb, rb)], src.at[b], isem.at[b])

            def out_copy(blk, s):
                return pltpu.make_async_copy(dst.at[s], out.at[pl.ds(base + blk * rb, rb)], osem.at[s])

            for b in range(nb - 1):
                in_copy(b, b).start()

            @pl.loop(0, nblk, step=nb)
            def _(blk):
                for b in range(nb):
                    ahead = blk + b + nb - 1

                    @pl.when(ahead < nblk)
                    def _():
                        in_copy(ahead, (b + nb - 1) % nb).start()

                    in_copy(blk + b, b).wait()
                    s = b % 2

                    @pl.when(blk + b >= 2)
                    def _():
                        out_copy(blk + b - 2, s).wait()

                    @pl.loop(0, rb)
                    def _(r):
                        @plsc.parallel_loop(0, dw // SC_LANES, 1, unroll=4)
                        def _(c):
                            lo = src[b, r, pl.ds(c * SC_LANES, SC_LANES)]
                            hi = src[b, r, pl.ds(dw + c * SC_LANES, SC_LANES)]
                            dst[s, r, pl.ds(c * SC_LANES, SC_LANES)] = plsc.bitcast(
                                plsc.pack(lo, hi, format=SC_ILV), jnp.int32)

                    out_copy(blk + b, s).start()

            out_copy(nblk - 2, 0).wait()
            out_copy(nblk - 1, 1).wait()

    return pack(u, v)


def _experts(h2, idx, g, u, v):
    t_total, dw = h2.shape
    d = 2 * dw
    tpw = t_total // SC_WORKERS
    tb = min(tpw, SC_TOKEN_BLOCK)
    ng = SC_ITEM_HEADS
    k = PEER_TOPK * ng
    nh = PEER_HEADS // ng
    n_items = tb * nh
    nbuf = SC_GATHER_DEPTH
    head_bits = nh.bit_length() - 1
    assert tpw % tb == 0 and n_items >= nbuf and nh == 1 << head_bits and PEER_TOPK == SC_LANES
    mesh = plsc.VectorSubcoreMesh(core_axis_name="c", subcore_axis_name="s")

    @functools.partial(
        pl.kernel, mesh=mesh,
        out_type=jax.ShapeDtypeStruct((t_total, d), F32),
        scratch_types=[
            pltpu.VMEM((tb, N_PAIRS), jnp.int32),
            pltpu.VMEM((tb, N_PAIRS), F32),
            pltpu.VMEM((tb, dw), jnp.int32),
            pltpu.VMEM((tb, d), F32),
            pltpu.VMEM((nbuf, k, dw), jnp.int32),
            pltpu.VMEM((nbuf, k, dw), jnp.int32),
            pltpu.SemaphoreType.DMA((nbuf,)),
            pltpu.SemaphoreType.DMA((nbuf,)),
        ],
        compiler_params=pltpu.CompilerParams(needs_layout_passes=False),
        name="experts",
    )
    def experts(h_hbm, idx_hbm, g_hbm, u_hbm, v_hbm, out_hbm,
                idx_b, g_b, h_b, out_b, urows, vrows, usem, vsem):
        wid = lax.axis_index("s") * SC_CORES + lax.axis_index("c")
        base = wid * tpw
        lane = lax.iota(jnp.int32, SC_LANES)

        def split(item):
            return lax.shift_right_logical(item, head_bits), lax.bitwise_and(item, nh - 1)

        def copies(item, b):
            tt, hd = split(item)
            ids = idx_b.at[tt, pl.ds(hd * k, k)]
            return (pltpu.make_async_copy(u_hbm.at[ids], urows.at[b], usem.at[b]),
                    pltpu.make_async_copy(v_hbm.at[ids], vrows.at[b], vsem.at[b]))

        def fetch(item, b):
            cu, cv = copies(item, b)
            cu.start()
            cv.start()

        def words(ref, *lead, off):
            return plsc.bitcast(ref[(*lead, pl.ds(off, SC_LANES))], BF16)

        def sum4_unpack(pr):
            return plsc.unpack((pr[0] + pr[1]) + (pr[2] + pr[3]), format=SC_ILV)

        def compute(item, b):
            tt, hd = split(item)
            cu, cv = copies(item, b)
            cu.wait()

            def ubody(j, accs):
                hs = [words(h_b, tt, off=(j * 4 + q) * SC_LANES) for q in range(4)]
                new = []
                for p, a in enumerate(accs):
                    lo, hi = sum4_unpack([words(urows, b, p, off=(j * 4 + q) * SC_LANES) * hs[q]
                                          for q in range(4)])
                    new.append(a + (lo + hi))
                return tuple(new)

            accs = plsc.parallel_loop(
                0, dw // (4 * SC_LANES), 1,
                carry=tuple(jnp.zeros((SC_LANES,), F32) for _ in range(k)))(ubody)
            cbb = []
            for hh in range(ng):
                s = jnp.zeros((SC_LANES,), F32)
                for p in range(SC_LANES):
                    s = jnp.where(lane == p, jnp.sum(accs[hh * SC_LANES + p]), s)
                c = g_b[tt, pl.ds(hd * k + hh * SC_LANES, SC_LANES)] * _gelu_tanh(s)
                for p in range(SC_LANES):
                    cp = jnp.full((SC_LANES,), c[p])
                    cbb.append(plsc.pack(cp, cp, format=SC_ILV))
            cv.wait()

            def tree(xs):
                while len(xs) > 1:
                    xs = [xs[i] + xs[i + 1] for i in range(0, len(xs), 2)]
                return xs[0]

            @plsc.parallel_loop(0, dw // SC_LANES, 1, unroll=2 // ng)
            def _(ch):
                sa = pl.ds(ch * SC_LANES, SC_LANES)
                sb = pl.ds(dw + ch * SC_LANES, SC_LANES)
                los, his = [], []
                for grp in range(k // 4):
                    lo, hi = sum4_unpack([cbb[grp * 4 + q]
                                          * words(vrows, b, grp * 4 + q, off=ch * SC_LANES)
                                          for q in range(4)])
                    los.append(lo)
                    his.append(hi)
                out_b[tt, sa] = out_b[tt, sa] + tree(los)
                out_b[tt, sb] = out_b[tt, sb] + tree(his)

        @pl.loop(0, tpw // tb)
        def _(blk):
            t0 = base + blk * tb
            pltpu.sync_copy(idx_hbm.at[pl.ds(t0, tb)], idx_b)
            pltpu.sync_copy(g_hbm.at[pl.ds(t0, tb)], g_b)
            pltpu.sync_copy(h_hbm.at[pl.ds(t0, tb)], h_b)
            for b in range(nbuf - 1):
                fetch(b, b)

            @pl.loop(0, tb)
            def _(tt):
                @pl.loop(0, d // SC_LANES)
                def _(j):
                    out_b[tt, pl.ds(j * SC_LANES, SC_LANES)] = jnp.zeros((SC_LANES,), F32)

            @pl.loop(0, n_items)
            def _(it):
                ahead = it + nbuf - 1

                @pl.when(ahead < n_items)
                def _():
                    fetch(ahead, lax.rem(ahead, nbuf))

                compute(it, lax.rem(it, nbuf))

            pltpu.sync_copy(out_b, out_hbm.at[pl.ds(t0, tb)])

    return experts(h2, idx, g, u, v)


SPLIT_ROWS = 4
ROW_PIECES = 2
BATCH_CHUNKS = (1,) * 12


def kernel(x, c, w_ada, b_ada, norm1_g, norm2_g, w_in, conv_w, conv_b, b_igate, b_fgate, lam_q1, lam_k1, lam_q2, lam_k2, diff_sub_g, mlstm_norm_g, w_out, peer_w_q, peer_sub_keys, peer_u, peer_v, rel_bias, final_g):
    bsz, s, d = x.shape
    mod = _ada(c, w_ada[0], b_ada[0]).reshape(bsz, 6, d)

    w = w_in[0]
    w_qk = w[:, :2 * D_ATT].astype(BF16)
    w_vt = w[:, 2 * D_ATT:3 * D_ATT].T.astype(BF16)
    w_m = w[:, 3 * D_ATT:3 * D_ATT + 4 * D_MLSTM].astype(BF16)
    w_gt = w[:, 3 * D_ATT + 4 * D_MLSTM:].T.astype(BF16)
    w_o = w_out[0].astype(BF16)
    w_q = peer_w_q[0].astype(BF16)
    sub_keys = peer_sub_keys[0].astype(BF16)
    bias_tiles = _relbias(rel_bias)
    lam4 = jnp.stack([lam_q1[0], lam_k1[0], lam_q2[0], lam_k2[0]])
    gate_bias = jnp.concatenate([b_igate[0], b_fgate[0]])

    assert sum(BATCH_CHUNKS) == bsz - SPLIT_ROWS
    tables = None
    out = None

    def mix_and_experts(att, hm, b0, out, s0=0):
        nonlocal tables
        nb, sp, _ = att.shape
        x1, h2, qp = _outproj(att, hm, x, mod, norm2_g[0], w_o, w_q, b0, s0)
        idx_t, g_t = _route(qp.reshape(nb * sp, -1), sub_keys)
        if tables is None:
            tables = _pack_tables(peer_u[0], peer_v[0])
        peer_out = _experts(h2.reshape(nb * sp, d // 2), idx_t.T, g_t.T, *tables)
        return _final(x1, peer_out.reshape(nb, sp, d), mod, final_g, b0, out, s, s0)

    sp = s // ROW_PIECES
    for b0 in range(SPLIT_ROWS):
        parts = []
        for p in range(ROW_PIECES):
            parts.append(_inproj(x, mod, norm1_g[0], w_qk, w_vt, w_m, w_gt, b0, 1, p * sp, sp))
            att_qk, att_vt, m_qkvo, gates = (jnp.concatenate(t, axis=a)
                                             for t, a in zip(zip(*parts), (1, 2, 1, 2)))
            att = _attn(att_qk, att_vt, bias_tiles, lam4, diff_sub_g[0], q_start=p * sp)
            hm = _mlstm(m_qkvo, gates, gate_bias, conv_w[0], conv_b[0], mlstm_norm_g[0])[:, p * sp:]
            out = mix_and_experts(att, hm, b0, out, p * sp)

    b0 = SPLIT_ROWS
    for nb in BATCH_CHUNKS:
        att_qk, att_vt, m_qkvo, gates = _inproj(x, mod, norm1_g[0], w_qk, w_vt, w_m, w_gt, b0, nb)
        att = _attn(att_qk, att_vt, bias_tiles, lam4, diff_sub_g[0])
        hm = _mlstm(m_qkvo, gates, gate_bias, conv_w[0], conv_b[0], mlstm_norm_g[0])
        out = mix_and_experts(att, hm, b0, out)
        b0 += nb
    return out
```
